```python
import math
import jax, jax.numpy as jnp
from jax import lax
import numpy as np

D_MODEL = 2048
BATCH = 2
SEQ = 16384
DEPTH = 2

MEM_LEN = 256
HALF = D_MODEL // 2

POOL_WINDOWS = (2, 4, 8, 16)
POOL_GROUPS = len(POOL_WINDOWS)
POOL_GDIM = HALF // POOL_GROUPS

GDN_HEADS = 8
GDN_DK = HALF // GDN_HEADS
GDN_DV = HALF // GDN_HEADS
GDN_CONV = 4
GDN_CHUNK = 64
NORM_EPS = 1e-6

RWKV_HEAD = 64
RWKV_HEADS = HALF // RWKV_HEAD
RWKV_DECAY_LORA = 64
RWKV_A_LORA = 64
RWKV_GATE_LORA = 160
RWKV_LNX_EPS = 64e-5
RWKV_SPLITS = (HALF, HALF, HALF, RWKV_DECAY_LORA, RWKV_A_LORA, RWKV_GATE_LORA)
RWKV_IN = sum(RWKV_SPLITS)

FOX_HEADS = 8
FOX_HD = HALF // FOX_HEADS
FOX_BLOCK = 128

EVEN_SPLITS = (HALF, HALF, HALF, HALF, HALF, GDN_HEADS, GDN_HEADS)
EVEN_IN = sum(EVEN_SPLITS)
ODD_SPLITS = (RWKV_IN, HALF, HALF, HALF, FOX_HEADS)
ODD_IN = sum(ODD_SPLITS)

XA_HEADS = 4
XA_HD = D_MODEL // XA_HEADS

D_FF = 5632
FFN_CONV = 3

N_EVEN = (DEPTH + 1) // 2
N_ODD = DEPTH // 2
DEEPNORM_ALPHA = float((2 * DEPTH) ** 0.25)
DEEPNORM_BETA = float((8 * DEPTH) ** -0.25)
LN_EPS = 1e-5

kernel_name = 'hybrid_pool_gdn_rwkv7_fox_trunk'


def split_sizes(t, sizes):
    offs = np.cumsum(sizes)[:-1].tolist()
    return jnp.split(t, offs, axis=-1)


def layer_norm(x, w, b):
    xf = x.astype(jnp.float32)
    mu = jnp.mean(xf, -1, keepdims=True)
    var = jnp.mean(jnp.square(xf - mu), -1, keepdims=True)
    return ((xf - mu) * lax.rsqrt(var + LN_EPS) * w + b).astype(x.dtype)


def l2norm(t):
    return t * lax.rsqrt(jnp.sum(t * t, -1, keepdims=True) + NORM_EPS)


def causal_dwconv(x, w):
    K, S = w.shape[0], x.shape[1]
    xp = jnp.pad(x, ((0, 0), (K - 1, 0), (0, 0)))
    return sum(xp[:, i:i + S] * w[i] for i in range(K))


def pool_mixer(u, pool_w, pool_scale):
    f32 = jnp.float32
    uf = u.astype(f32)
    S = uf.shape[1]
    cs = jnp.pad(jnp.cumsum(uf, axis=1), ((0, 0), (1, 0), (0, 0)))
    upper = cs[:, 1:]
    pos = jnp.arange(1, S + 1, dtype=f32)
    outs = []
    for g, win in enumerate(POOL_WINDOWS):
        sl = slice(g * POOL_GDIM, (g + 1) * POOL_GDIM)
        lower = jnp.pad(cs[:, :S + 1 - win, sl], ((0, 0), (win - 1, 0), (0, 0)))
        mean = (upper[..., sl] - lower) / jnp.minimum(pos, win)[None, :, None]
        outs.append(jnp.einsum('bsc,cd->bsd', mean - uf[..., sl], pool_w[g].astype(f32)))
    return (jnp.concatenate(outs, -1) * pool_scale).astype(u.dtype)


def gated_deltanet(q, k, v, z, a_logit, b_logit, conv_w, a_log, dt_bias, norm_w):
    f32 = jnp.float32
    B_, S, _ = q.shape
    H, dk, dv, C = GDN_HEADS, GDN_DK, GDN_DV, GDN_CHUNK
    N = S // C
    qkv = jax.nn.silu(causal_dwconv(jnp.concatenate([q, k, v], -1), conv_w).astype(f32))
    q, k, v = jnp.split(qkv, 3, axis=-1)
    q = l2norm(q.reshape(B_, S, H, dk)) * dk ** -0.5
    k = l2norm(k.reshape(B_, S, H, dk))
    v = v.reshape(B_, S, H, dv)
    g = -jnp.exp(a_log.astype(f32)) * jax.nn.softplus(a_logit.astype(f32) + dt_bias)
    beta = jax.nn.sigmoid(b_logit.astype(f32))
    to_chunks = lambda t: t.reshape(B_, N, C, H, -1).transpose(0, 3, 1, 2, 4)
    qc, kc, vc = to_chunks(q), to_chunks(k), to_chunks(v)
    gc = jnp.cumsum(to_chunks(g[..., None])[..., 0], axis=-1)
    bc = to_chunks(beta[..., None])
    causal = jnp.tril(jnp.ones((C, C), dtype=bool))
    strict = jnp.tril(jnp.ones((C, C), dtype=bool), -1)
    diff = gc[..., :, None] - gc[..., None, :]
    decay = jnp.where(causal, jnp.exp(jnp.where(causal, diff, 0.0)), 0.0)
    kb = kc * bc
    L = jnp.where(strict, jnp.einsum('bhnid,bhnjd->bhnij', kb, kc) * decay, 0.0)
    eye = jnp.eye(C, dtype=f32)
    T = lax.linalg.triangular_solve(eye + L, jnp.broadcast_to(eye, L.shape),
                                    left_side=True, lower=True, unit_diagonal=True)
    u = T @ (vc * bc)
    w = T @ (kb * jnp.exp(gc)[..., None])
    intra = jnp.where(causal, jnp.einsum('bhnid,bhnjd->bhnij', qc, kc) * decay, 0.0)
    g_last = gc[..., -1]
    q_dec = qc * jnp.exp(gc)[..., None]
    k_dec = kc * jnp.exp(g_last[..., None] - gc)[..., None]

    def chunk_step(state, xs):
        q_i, k_i, u_i, w_i, a_i, gl_i = xs
        v_new = u_i - w_i @ state
        o_i = q_i @ state + a_i @ v_new
        state = state * jnp.exp(gl_i)[..., None, None] + jnp.swapaxes(k_i, -1, -2) @ v_new
        return state, o_i

    xs = tuple(jnp.moveaxis(t, 2, 0) for t in (q_dec, k_dec, u, w, intra, g_last))
    _, o = lax.scan(chunk_step, jnp.zeros((B_, H, dk, dv), f32), xs)
    o = o.transpose(1, 0, 3, 2, 4).reshape(B_, S, H, dv)
    o = o * lax.rsqrt(jnp.mean(o * o, -1, keepdims=True) + NORM_EPS) * norm_w
    o = o * jax.nn.silu(z.astype(f32).reshape(B_, S, H, dv))
    return o.reshape(B_, S, HALF).astype(z.dtype)


def rwkv7_time_mix(c_blk, mu, w0, w2, a0, a2, g2, k_k, k_a, r_k, lnx_w, lnx_b):
    f32 = jnp.float32
    c_blk = c_blk.astype(f32)
    prev = jnp.pad(c_blk, ((0, 0), (1, 0), (0, 0)))[:, :-1]
    c_blk = c_blk + (prev - c_blk) * mu
    r, k, v, wl, al, gl = split_sizes(c_blk, RWKV_SPLITS)
    w_log = -jax.nn.softplus(-(w0 + jnp.tanh(wl) @ w2)) - 0.5
    decay = jnp.exp(-jnp.exp(w_log))
    a = jax.nn.sigmoid(a0 + al @ a2)
    g = jax.nn.sigmoid(gl) @ g2
    B_, S, _ = r.shape
    heads = lambda t: t.reshape(B_, S, RWKV_HEADS, RWKV_HEAD)
    kk = l2norm(heads(k * k_k))
    k = k * (1.0 + (a - 1.0) * k_a)
    r_h, k_h, v_h, w_h, a_h = heads(r), heads(k), heads(v), heads(decay), heads(a)

    def step(state, xs):
        r_t, w_t, k_t, v_t, aa_t, bb_t = xs
        sa = jnp.einsum('bhvk,bhk->bhv', state, aa_t)
        state = (state * w_t[:, :, None, :] + sa[..., None] * bb_t[:, :, None, :]
                 + v_t[..., None] * k_t[:, :, None, :])
        return state, jnp.einsum('bhvk,bhk->bhv', state, r_t)

    xs = tuple(jnp.moveaxis(t, 1, 0) for t in (r_h, w_h, k_h, v_h, -kk, kk * a_h))
    _, y = lax.scan(step, jnp.zeros((B_, RWKV_HEADS, RWKV_HEAD, RWKV_HEAD), f32), xs)
    y = jnp.moveaxis(y, 0, 1)
    ym = jnp.mean(y, -1, keepdims=True)
    yv = jnp.mean(jnp.square(y - ym), -1, keepdims=True)
    y = ((y - ym) * lax.rsqrt(yv + RWKV_LNX_EPS)).reshape(B_, S, HALF) * lnx_w + lnx_b
    bonus = jnp.sum(r_h * k_h * r_k, -1, keepdims=True) * v_h
    return ((y + bonus.reshape(B_, S, HALF)) * g)


def forgetting_attention(q, k, v, f_logit, b_f):
    f32 = jnp.float32
    B_, S, _ = q.shape
    H, hd, FB = FOX_HEADS, FOX_HD, FOX_BLOCK
    nb = S // FB
    heads = lambda t: t.astype(f32).reshape(B_, S, H, hd).transpose(0, 2, 1, 3)
    qh, kh, vh = heads(q) * hd ** -0.5, heads(k), heads(v)
    c = jnp.cumsum(jax.nn.log_sigmoid(f_logit.astype(f32) + b_f), axis=1).transpose(0, 2, 1)
    kpos = jnp.arange(S)

    def block(args):
        q_i, c_i, i = args
        s = jnp.einsum('bhqd,bhkd->bhqk', q_i, kh) + (c_i[..., :, None] - c[..., None, :])
        qpos = i * FB + jnp.arange(FB)
        s = jnp.where(kpos[None, :] <= qpos[:, None], s, -jnp.inf)
        return jnp.einsum('bhqk,bhkd->bhqd', jax.nn.softmax(s, axis=-1), vh)

    qb = qh.reshape(B_, H, nb, FB, hd).transpose(2, 0, 1, 3, 4)
    cb = c.reshape(B_, H, nb, FB).transpose(2, 0, 1, 3)
    o = lax.map(block, (qb, cb, jnp.arange(nb)))
    return o.transpose(1, 0, 3, 2, 4).reshape(B_, S, HALF)


def even_mixer(x, w_in, pool_w, pool_scale, conv_w, a_log, dt_bias, norm_w, w_out):
    h = x @ w_in
    u_pool, q, k, v, z, a_logit, b_logit = split_sizes(h, EVEN_SPLITS)
    y_a = pool_mixer(u_pool, pool_w, pool_scale)
    y_b = gated_deltanet(q, k, v, z, a_logit, b_logit, conv_w, a_log, dt_bias, norm_w)
    return jnp.concatenate([y_a, y_b.astype(y_a.dtype)], -1) @ w_out


def odd_mixer(x, w_in, mu, w0, w2, a0, a2, g2, k_k, k_a, r_k, lnx_w, lnx_b, b_f, w_out):
    h = x @ w_in
    c_blk, fq, fk, fv, f_logit = split_sizes(h, ODD_SPLITS)
    y_c = rwkv7_time_mix(c_blk, mu, w0, w2, a0, a2, g2, k_k, k_a, r_k, lnx_w, lnx_b)
    y_d = forgetting_attention(fq, fk, fv, f_logit, b_f)
    return jnp.concatenate([y_c, y_d], -1).astype(x.dtype) @ w_out


def memory_cross_attention(x, mem, w_q, w_kv, w_o):
    B_, S, _ = x.shape
    M = mem.shape[1]
    q = (x @ w_q).reshape(B_, S, XA_HEADS, XA_HD).astype(jnp.float32)
    k, v = jnp.split((mem @ w_kv).astype(jnp.float32), 2, axis=-1)
    k = k.reshape(B_, M, XA_HEADS, XA_HD)
    v = v.reshape(B_, M, XA_HEADS, XA_HD)
    p = jax.nn.softmax(jnp.einsum('bshd,bmhd->bhsm', q, k) * XA_HD ** -0.5, axis=-1)
    o = jnp.einsum('bhsm,bmhd->bshd', p, v).reshape(B_, S, D_MODEL).astype(x.dtype)
    return o @ w_o


def conv_glu_ffn(x, w_up, conv_w, w_down):
    h = causal_dwconv(x @ w_up, conv_w)
    gate, up = jnp.split(h, 2, axis=-1)
    return (jax.nn.silu(gate) * up) @ w_down


def setup_inputs(seed: int = 0) -> dict:
    key = jax.random.key(seed)
    ks = iter(jax.random.split(key, 64))
    f32 = jnp.float32
    nrm = lambda shape, scale: jax.random.normal(next(ks), shape, f32) * scale
    unif = lambda shape, lo, hi: jax.random.uniform(next(ks), shape, f32, lo, hi)
    gain = lambda shape: 1.0 + nrm(shape, 0.02)
    NE, NO, L = N_EVEN, N_ODD, DEPTH
    dt = jnp.exp(unif((NE, GDN_HEADS), math.log(1e-3), math.log(1e-1)))
    return {
        'x': nrm((BATCH, SEQ, D_MODEL), 1.0),
        'mem': nrm((BATCH, MEM_LEN, D_MODEL), 1.0),
        'ev_w_in': nrm((NE, D_MODEL, EVEN_IN), D_MODEL ** -0.5),
        'pool_w': nrm((NE, POOL_GROUPS, POOL_GDIM, POOL_GDIM), POOL_GDIM ** -0.5),
        'pool_scale': gain((NE, HALF)),
        'gdn_conv_w': nrm((NE, GDN_CONV, 3 * HALF), GDN_CONV ** -0.5),
        'gdn_a_log': jnp.log(unif((NE, GDN_HEADS), 1.0, 16.0)),
        'gdn_dt_bias': dt + jnp.log(-jnp.expm1(-dt)),
        'gdn_norm_w': gain((NE, GDN_DV)),
        'ev_w_out': nrm((NE, D_MODEL, D_MODEL), D_MODEL ** -0.5 * DEEPNORM_BETA),
        'od_w_in': nrm((NO, D_MODEL, ODD_IN), D_MODEL ** -0.5),
        'rwkv_mu': unif((NO, RWKV_IN), 0.0, 1.0),
        'rwkv_w0': unif((NO, HALF), -6.0, -1.0),
        'rwkv_w2': nrm((NO, RWKV_DECAY_LORA, HALF), RWKV_DECAY_LORA ** -0.5),
        'rwkv_a0': nrm((NO, HALF), 0.1),
        'rwkv_a2': nrm((NO, RWKV_A_LORA, HALF), RWKV_A_LORA ** -0.5),
        'rwkv_g2': nrm((NO, RWKV_GATE_LORA, HALF), RWKV_GATE_LORA ** -0.5),
        'rwkv_k_k': 0.85 + nrm((NO, HALF), 0.05),
        'rwkv_k_a': 1.0 + nrm((NO, HALF), 0.05),
        'rwkv_r_k': nrm((NO, RWKV_HEADS, RWKV_HEAD), 0.1),
        'rwkv_lnx_w': gain((NO, HALF)),
        'rwkv_lnx_b': nrm((NO, HALF), 0.02),
        'fox_b_f': unif((NO, FOX_HEADS), 1.0, 6.0),
        'od_w_out': nrm((NO, D_MODEL, D_MODEL), D_MODEL ** -0.5 * DEEPNORM_BETA),
        'ln_mix_w': gain((L, D_MODEL)),
        'ln_mix_b': nrm((L, D_MODEL), 0.02),
        'xa_w_q': nrm((L, D_MODEL, D_MODEL), D_MODEL ** -0.5),
        'xa_w_kv': nrm((L, D_MODEL, 2 * D_MODEL), D_MODEL ** -0.5),
        'xa_w_o': nrm((L, D_MODEL, D_MODEL), D_MODEL ** -0.5 * DEEPNORM_BETA),
        'ln_xa_w': gain((L, D_MODEL)),
        'ln_xa_b': nrm((L, D_MODEL), 0.02),
        'ffn_w_up': nrm((L, D_MODEL, 2 * D_FF), D_MODEL ** -0.5),
        'ffn_conv_w': nrm((L, FFN_CONV, 2 * D_FF), FFN_CONV ** -0.5),
        'ffn_w_down': nrm((L, D_FF, D_MODEL), D_FF ** -0.5 * DEEPNORM_BETA),
        'ln_ffn_w': gain((L, D_MODEL)),
        'ln_ffn_b': nrm((L, D_MODEL), 0.02),
    }


def reference(x, mem, ev_w_in, pool_w, pool_scale, gdn_conv_w, gdn_a_log, gdn_dt_bias, gdn_norm_w, ev_w_out,
              od_w_in, rwkv_mu, rwkv_w0, rwkv_w2, rwkv_a0, rwkv_a2, rwkv_g2, rwkv_k_k, rwkv_k_a, rwkv_r_k,
              rwkv_lnx_w, rwkv_lnx_b, fox_b_f, od_w_out,
              ln_mix_w, ln_mix_b, xa_w_q, xa_w_kv, xa_w_o, ln_xa_w, ln_xa_b,
              ffn_w_up, ffn_conv_w, ffn_w_down, ln_ffn_w, ln_ffn_b):
    for i in range(DEPTH):
        j = i // 2
        if i % 2 == 0:
            y = even_mixer(x, ev_w_in[j], pool_w[j], pool_scale[j], gdn_conv_w[j], gdn_a_log[j],
                           gdn_dt_bias[j], gdn_norm_w[j], ev_w_out[j])
        else:
            y = odd_mixer(x, od_w_in[j], rwkv_mu[j], rwkv_w0[j], rwkv_w2[j], rwkv_a0[j], rwkv_a2[j],
                          rwkv_g2[j], rwkv_k_k[j], rwkv_k_a[j], rwkv_r_k[j], rwkv_lnx_w[j], rwkv_lnx_b[j],
                          fox_b_f[j], od_w_out[j])
        x = layer_norm(DEEPNORM_ALPHA * x + y, ln_mix_w[i], ln_mix_b[i])
        x = layer_norm(DEEPNORM_ALPHA * x + memory_cross_attention(x, mem, xa_w_q[i], xa_w_kv[i], xa_w_o[i]),
                       ln_xa_w[i], ln_xa_b[i])
        x = layer_norm(DEEPNORM_ALPHA * x + conv_glu_ffn(x, ffn_w_up[i], ffn_conv_w[i], ffn_w_down[i]),
                       ln_ffn_w[i], ln_ffn_b[i])
    return x
```

```python
import functools
import math

import numpy as np
import jax
import jax.numpy as jnp
from jax import lax
from jax.experimental import pallas as pl
from jax.experimental.pallas import tpu as pltpu

F32 = jnp.float32
BF16 = jnp.bfloat16

D_MODEL = 2048
HALF = D_MODEL // 2
POOL_WINDOWS = (2, 4, 8, 16)
POOL_GDIM = HALF // len(POOL_WINDOWS)
GDN_HEADS = 8
GDN_DK = HALF // GDN_HEADS
GDN_CONV = 4
NORM_EPS = 1e-6
RWKV_HEAD = 64
RWKV_LNX_EPS = 64e-5
RWKV_LORA = (64, 64, 160)
RWKV_LORA_PAD = (128, 128, 256)
FOX_HEADS = 8
FOX_HD = HALF // FOX_HEADS
XA_HEADS = 4
XA_HD = D_MODEL // XA_HEADS
D_FF = 5632
FFN_CONV = 3
DEPTH = 2
DEEPNORM_ALPHA = float((2 * DEPTH) ** 0.25)
LN_EPS = 1e-5

CHUNK = 64
HIST = 8
VMEM_LIMIT_BYTES = 56 * 1024 * 1024


def _cparams(*sem):
    return pltpu.CompilerParams(dimension_semantics=sem, vmem_limit_bytes=VMEM_LIMIT_BYTES)


def _dot(a, b):
    return jnp.dot(a, b, preferred_element_type=F32)


def _dot_nt(a, b):
    return lax.dot_general(a, b, (((1,), (1,)), ((), ())), preferred_element_type=F32)


def _dot_tn(a, b):
    return lax.dot_general(a, b, (((0,), (0,)), ((), ())), preferred_element_type=F32)


def _bf(x):
    return x.astype(BF16)


def _split3(x):
    x1 = x.astype(BF16)
    r1 = x - x1.astype(F32)
    x2 = r1.astype(BF16)
    x3 = (r1 - x2.astype(F32)).astype(BF16)
    return x1, x2, x3


def _dot01(m01, x):
    x1, x2, x3 = _split3(x)
    return _dot(m01, x1) + _dot(m01, x2) + _dot(m01, x3)


def _dot_hi(a, b):
    a1 = a.astype(BF16)
    a2 = (a - a1.astype(F32)).astype(BF16)
    b1 = b.astype(BF16)
    b2 = (b - b1.astype(F32)).astype(BF16)
    return _dot(a1, b1) + _dot(a1, b2) + _dot(a2, b1)


def _unit_lower_inverse(n_mat, eye):
    c = n_mat.shape[0]
    t = eye + n_mat
    p = n_mat
    steps = int(math.log2(c)) - 1
    for _ in range(steps):
        p = _dot_hi(p, p)
        t = t + _dot_hi(t, p)
    return t


def _sigmoid(x):
    return 1.0 / (1.0 + jnp.exp(-x))


def _softplus(x):
    return jnp.maximum(x, 0.0) + jnp.log(1.0 + jnp.exp(-jnp.abs(x)))


def _tri_masks(c):
    row = lax.broadcasted_iota(jnp.int32, (c, c), 0)
    col = lax.broadcasted_iota(jnp.int32, (c, c), 1)
    return row >= col, row > col


def _stage_rows(x_ref, sc_ref, ts, first):
    @pl.when(first)
    def _():
        sc_ref[0:HIST, :] = jnp.zeros((HIST, sc_ref.shape[1]), F32)
    sc_ref[HIST:HIST + ts, :] = x_ref[0].astype(F32)


def _carry_rows(sc_ref, ts):
    sc_ref[0:HIST, :] = sc_ref[ts:ts + HIST, :]


def _mm_kernel(a_ref, w_ref, o_ref):
    o_ref[...] = _dot(a_ref[...], w_ref[...]).astype(o_ref.dtype)


def _mm(a, w, out_dtype, tm=1024, tn=512):
    m, k = a.shape
    n = w.shape[1]
    tm = min(tm, m)
    tn = min(tn, n)
    return pl.pallas_call(
        _mm_kernel,
        grid=(n // tn, m // tm),
        in_specs=[pl.BlockSpec((tm, k), lambda j, i: (i, 0)),
                  pl.BlockSpec((k, tn), lambda j, i: (0, j))],
        out_specs=pl.BlockSpec((tm, tn), lambda j, i: (i, j)),
        out_shape=jax.ShapeDtypeStruct((m, n), out_dtype),
        compiler_params=_cparams("arbitrary", "arbitrary"),
        name="mm",
    )(a, w)


def _mm_res_ln_kernel(*refs, n_in, tk_steps):
    a_refs = refs[:n_in]
    w_refs = refs[n_in:2 * n_in]
    res_ref, lnw_ref, lnb_ref, o32_ref, o16_ref, acc_ref = refs[2 * n_in:]
    kk = pl.program_id(1)

    @pl.when(kk == 0)
    def _():
        acc_ref[...] = jnp.zeros_like(acc_ref)

    acc = acc_ref[...]
    for a_ref, w_ref in zip(a_refs, w_refs):
        acc = acc + _dot(a_ref[...], w_ref[...])
    acc_ref[...] = acc

    @pl.when(kk == tk_steps - 1)
    def _():
        y = DEEPNORM_ALPHA * res_ref[...] + acc_ref[...]
        mu = jnp.mean(y, axis=-1, keepdims=True)
        d = y - mu
        var = jnp.mean(d * d, axis=-1, keepdims=True)
        out = d * lax.rsqrt(var + LN_EPS) * lnw_ref[...] + lnb_ref[...]
        o32_ref[...] = out
        o16_ref[...] = out.astype(BF16)


def _mm_res_ln(a_list, w_list, res, lnw, lnb, tm=512, tk=512):
    m = res.shape[0]
    n = res.shape[1]
    k = a_list[0].shape[1]
    tm = min(tm, m)
    tk = min(tk, k)
    n_in = len(a_list)
    steps = k // tk
    in_specs = ([pl.BlockSpec((tm, tk), lambda i, kk: (i, kk)) for _ in a_list]
                + [pl.BlockSpec((tk, n), lambda i, kk: (kk, 0)) for _ in w_list]
                + [pl.BlockSpec((tm, n), lambda i, kk: (i, 0)),
                   pl.BlockSpec((1, n), lambda i, kk: (0, 0)),
                   pl.BlockSpec((1, n), lambda i, kk: (0, 0))])
    return pl.pallas_call(
        functools.partial(_mm_res_ln_kernel, n_in=n_in, tk_steps=steps),
        grid=(m // tm, steps),
        in_specs=in_specs,
        out_specs=[pl.BlockSpec((tm, n), lambda i, kk: (i, 0)),
                   pl.BlockSpec((tm, n), lambda i, kk: (i, 0))],
        out_shape=[jax.ShapeDtypeStruct((m, n), F32), jax.ShapeDtypeStruct((m, n), BF16)],
        scratch_shapes=[pltpu.VMEM((tm, n), F32)],
        compiler_params=_cparams("arbitrary", "arbitrary"),
        name="mm_res_ln",
    )(*a_list, *w_list, res, lnw.reshape(1, n), lnb.reshape(1, n))


def _xattn_kernel(q_ref, k_ref, v_ref, o_ref):
    s = _dot_nt(q_ref[0], k_ref[0]) * (XA_HD ** -0.5)
    m = jnp.max(s, axis=-1, keepdims=True)
    p = jnp.exp(s - m)
    l = jnp.sum(p, axis=-1, keepdims=True)
    o = _dot(_bf(p / l), v_ref[0])
    o_ref[0] = o.astype(o_ref.dtype)


def _xattn(q, kv, ts=512):
    b, s, d = q.shape
    mlen = kv.shape[1]
    ts = min(ts, s)
    return pl.pallas_call(
        _xattn_kernel,
        grid=(b, s // ts, XA_HEADS),
        in_specs=[pl.BlockSpec((1, ts, XA_HD), lambda bi, i, h: (bi, i, h)),
                  pl.BlockSpec((1, mlen, XA_HD), lambda bi, i, h: (bi, 0, h)),
                  pl.BlockSpec((1, mlen, XA_HD), lambda bi, i, h: (bi, 0, XA_HEADS + h))],
        out_specs=pl.BlockSpec((1, ts, XA_HD), lambda bi, i, h: (bi, i, h)),
        out_shape=jax.ShapeDtypeStruct((b, s, d), BF16),
        compiler_params=_cparams("arbitrary", "arbitrary", "arbitrary"),
        name="xattn",
    )(q, kv, kv)


def _ffn_up_kernel(x_ref, wg_ref, wu_ref, cg_ref, cu_ref, o_ref, sg_ref, su_ref, *, ts):
    first = pl.program_id(2) == 0

    @pl.when(first)
    def _():
        sg_ref[0:HIST, :] = jnp.zeros((HIST, sg_ref.shape[1]), F32)
        su_ref[0:HIST, :] = jnp.zeros((HIST, su_ref.shape[1]), F32)

    x = x_ref[0]
    sg_ref[HIST:HIST + ts, :] = _dot(x, wg_ref[...])
    su_ref[HIST:HIST + ts, :] = _dot(x, wu_ref[...])

    def conv(sc_ref, c_ref):
        y = c_ref[FFN_CONV - 1:FFN_CONV, :] * sc_ref[HIST:HIST + ts, :]
        for back in range(1, FFN_CONV):
            tap = FFN_CONV - 1 - back
            y = y + c_ref[tap:tap + 1, :] * sc_ref[HIST - back:HIST - back + ts, :]
        return y

    g = conv(sg_ref, cg_ref)
    u = conv(su_ref, cu_ref)
    o_ref[0] = (g * _sigmoid(g) * u).astype(o_ref.dtype)
    _carry_rows(sg_ref, ts)
    _carry_rows(su_ref, ts)


def _ffn_up(xb, w_up, conv_w, ts=1024, tn=512):
    b, s, d = xb.shape
    ts = min(ts, s)
    nj = D_FF // tn
    return pl.pallas_call(
        functools.partial(_ffn_up_kernel, ts=ts),
        grid=(nj, b, s // ts),
        in_specs=[pl.BlockSpec((1, ts, d), lambda j, bi, i: (bi, i, 0)),
                  pl.BlockSpec((d, tn), lambda j, bi, i: (0, j)),
                  pl.BlockSpec((d, tn), lambda j, bi, i: (0, nj + j)),
                  pl.BlockSpec((FFN_CONV, tn), lambda j, bi, i: (0, j)),
                  pl.BlockSpec((FFN_CONV, tn), lambda j, bi, i: (0, nj + j))],
        out_specs=pl.BlockSpec((1, ts, tn), lambda j, bi, i: (bi, i, j)),
        out_shape=jax.ShapeDtypeStruct((b, s, D_FF), BF16),
        scratch_shapes=[pltpu.VMEM((ts + HIST, tn), F32), pltpu.VMEM((ts + HIST, tn), F32)],
        compiler_params=_cparams("arbitrary", "arbitrary", "arbitrary"),
        name="ffn_up",
    )(xb, w_up, w_up, conv_w, conv_w)


def _pool_kernel(u_ref, w_ref, sc_ref, o_ref, st_ref, *, ts):
    i = pl.program_id(1)

    @pl.when(i == 0)
    def _():
        st_ref[0:2 * HIST, :] = jnp.zeros((2 * HIST, st_ref.shape[1]), F32)

    st_ref[2 * HIST:2 * HIST + ts, :] = u_ref[0].astype(F32)
    pos = (i * ts + 1 + lax.broadcasted_iota(jnp.int32, (ts, 1), 0)).astype(F32)
    base = 2 * HIST
    for g, win in enumerate(POOL_WINDOWS):
        cols = slice(g * POOL_GDIM, (g + 1) * POOL_GDIM)
        cur = st_ref[base:base + ts, cols]
        acc = cur
        for back in range(1, win):
            acc = acc + st_ref[base - back:base - back + ts, cols]
        mean = acc / jnp.minimum(pos, float(win))
        y = _dot(_bf(mean - cur), w_ref[g])
        o_ref[0, :, cols] = (y * sc_ref[:, cols]).astype(o_ref.dtype)
    st_ref[0:2 * HIST, :] = st_ref[ts:ts + 2 * HIST, :]


def _pool_mixer(h_main, pool_w, pool_scale, ts=512):
    b, s, _ = h_main.shape
    ts = min(ts, s)
    return pl.pallas_call(
        functools.partial(_pool_kernel, ts=ts),
        grid=(b, s // ts),
        in_specs=[pl.BlockSpec((1, ts, HALF), lambda bi, i: (bi, i, 0)),
                  pl.BlockSpec((len(POOL_WINDOWS), POOL_GDIM, POOL_GDIM), lambda bi, i: (0, 0, 0)),
                  pl.BlockSpec((1, HALF), lambda bi, i: (0, 0))],
        out_specs=pl.BlockSpec((1, ts, HALF), lambda bi, i: (bi, i, 0)),
        out_shape=jax.ShapeDtypeStruct((b, s, HALF), BF16),
        scratch_shapes=[pltpu.VMEM((ts + 2 * HIST, HALF), F32)],
        compiler_params=_cparams("arbitrary", "arbitrary"),
        name="pool_mixer",
    )(h_main, pool_w, pool_scale.reshape(1, HALF))


def _gdn_kernel(q_ref, k_ref, v_ref, z_ref, g_ref, cq_ref, ck_ref, cv_ref, alog_ref, dtb_ref, nw_ref,
                o_ref, sq_ref, sk_ref, sv_ref, st_ref, *, ts):
    c = CHUNK
    h = pl.program_id(1)
    first = pl.program_id(2) == 0

    @pl.when(first)
    def _():
        st_ref[...] = jnp.zeros_like(st_ref)

    def conv_silu(x_ref, sc_ref, cw_ref):
        _stage_rows(x_ref, sc_ref, ts, first)
        y = cw_ref[GDN_CONV - 1:GDN_CONV, :] * sc_ref[HIST:HIST + ts, :]
        for back in range(1, GDN_CONV):
            tap = GDN_CONV - 1 - back
            y = y + cw_ref[tap:tap + 1, :] * sc_ref[HIST - back:HIST - back + ts, :]
        _carry_rows(sc_ref, ts)
        return y * _sigmoid(y)

    q = conv_silu(q_ref, sq_ref, cq_ref)
    k = conv_silu(k_ref, sk_ref, ck_ref)
    v = conv_silu(v_ref, sv_ref, cv_ref)
    q = q * lax.rsqrt(jnp.sum(q * q, axis=-1, keepdims=True) + NORM_EPS) * (GDN_DK ** -0.5)
    k = k * lax.rsqrt(jnp.sum(k * k, axis=-1, keepdims=True) + NORM_EPS)

    logits = g_ref[0]
    lane = lax.broadcasted_iota(jnp.int32, logits.shape, 1)
    g_all = -jnp.exp(alog_ref[...]) * _softplus(logits + dtb_ref[...])
    g_col = jnp.sum(jnp.where(lane == h, g_all, 0.0), axis=1, keepdims=True)
    b_col = jnp.sum(jnp.where(lane == h + GDN_HEADS, _sigmoid(logits), 0.0), axis=1, keepdims=True)

    causal, strict = _tri_masks(c)
    tril = causal.astype(BF16)
    eye = jnp.where(causal & (~strict), 1.0, 0.0).astype(F32)
    lane_c = lax.broadcasted_iota(jnp.int32, (c, GDN_DK), 1)
    z = z_ref[0].astype(F32)
    nw = nw_ref[...]

    for ci in range(ts // c):
        rows = slice(ci * c, (ci + 1) * c)
        qc, kc, vc = q[rows], k[rows], v[rows]
        beta = jnp.broadcast_to(b_col[rows], (c, GDN_DK))
        gc = _dot01(tril, jnp.broadcast_to(g_col[rows], (c, GDN_DK)))
        g1, g2, g3 = (p.astype(F32) for p in _split3(gc))
        lhs = jnp.where(lane_c == 0, g1, jnp.where(lane_c == 1, g2, jnp.where(lane_c == 2, g3,
              jnp.where(lane_c < 6, 1.0, 0.0))))
        rhs = jnp.where(lane_c < 3, 1.0, jnp.where(lane_c == 3, -g1, jnp.where(lane_c == 4, -g2,
              jnp.where(lane_c == 5, -g3, 0.0))))
        diff = _dot_nt(_bf(lhs), _bf(rhs))
        decay = jnp.where(causal, jnp.exp(jnp.where(causal, diff, 0.0)), 0.0)
        kb = kc * beta
        kcb = _bf(kc)
        lmat = jnp.where(strict, _dot_nt(_bf(kb), kcb) * decay, 0.0)
        tmat = _bf(_unit_lower_inverse(-lmat, eye))
        eg = jnp.exp(gc)
        g_last = gc[c - 1:c, :]
        u = _dot(tmat, _bf(vc * beta))
        w = _dot(tmat, _bf(kb * eg))
        intra = jnp.where(causal, _dot_nt(_bf(qc), kcb) * decay, 0.0)
        q_dec = _bf(qc * eg)
        k_dec = _bf(kc * jnp.exp(g_last - gc))
        state = st_ref[...]
        sb = _bf(state)
        v_new = u - _dot(_bf(w), sb)
        o = _dot(q_dec, sb) + _dot(_bf(intra), _bf(v_new))
        st_ref[...] = state * jnp.exp(g_last) + _dot_tn(k_dec, _bf(v_new))
        o = o * lax.rsqrt(jnp.mean(o * o, axis=-1, keepdims=True) + NORM_EPS) * nw
        zc = z[rows]
        o_ref[0, rows, :] = (o * (zc * _sigmoid(zc))).astype(o_ref.dtype)


def _gated_deltanet(h_main, logits, conv_w, a_log, dt_bias, norm_w, ts=256):
    b, s, _ = h_main.shape
    ts = min(ts, s)
    nh = GDN_HEADS
    hd = GDN_DK
    pad = lambda t: jnp.pad(t.astype(F32), (0, hd - t.shape[0])).reshape(1, hd)
    col = lambda grp: (lambda bi, h, i: (bi, i, grp * nh + h))
    cwspec = lambda grp: pl.BlockSpec((GDN_CONV, hd), lambda bi, h, i: (0, grp * nh + h))
    return pl.pallas_call(
        functools.partial(_gdn_kernel, ts=ts),
        grid=(b, nh, s // ts),
        in_specs=[pl.BlockSpec((1, ts, hd), col(1)), pl.BlockSpec((1, ts, hd), col(2)),
                  pl.BlockSpec((1, ts, hd), col(3)), pl.BlockSpec((1, ts, hd), col(4)),
                  pl.BlockSpec((1, ts, hd), lambda bi, h, i: (bi, i, 0)),
                  cwspec(0), cwspec(1), cwspec(2),
                  pl.BlockSpec((1, hd), lambda bi, h, i: (0, 0)),
                  pl.BlockSpec((1, hd), lambda bi, h, i: (0, 0)),
                  pl.BlockSpec((1, hd), lambda bi, h, i: (0, 0))],
        out_specs=pl.BlockSpec((1, ts, hd), lambda bi, h, i: (bi, i, h)),
        out_shape=jax.ShapeDtypeStruct((b, s, HALF), BF16),
        scratch_shapes=[pltpu.VMEM((ts + HIST, hd), F32), pltpu.VMEM((ts + HIST, hd), F32),
                        pltpu.VMEM((ts + HIST, hd), F32), pltpu.VMEM((hd, hd), F32)],
        compiler_params=_cparams("arbitrary", "arbitrary", "arbitrary"),
        name="gated_deltanet",
    )(h_main, h_main, h_main, h_main, logits, conv_w, conv_w, conv_w,
      pad(a_log), pad(dt_bias), norm_w.astype(F32).reshape(1, hd))


def _rwkv_kernel(r_ref, k_ref, v_ref, l_ref, mur_ref, muk_ref, muv_ref, mul_ref, w2_ref, a2_ref, g2_ref,
                 w0_ref, a0_ref, kk_ref, ka_ref, rk_ref, lnw_ref, lnb_ref,
                 o_ref, sr_ref, sk_ref, sv_ref, sl_ref, st_ref, *, ts):
    c = CHUNK
    hs = RWKV_HEAD
    first = pl.program_id(2) == 0

    @pl.when(first)
    def _():
        st_ref[...] = jnp.zeros_like(st_ref)

    def shift(x_ref, sc_ref, mu_ref):
        _stage_rows(x_ref, sc_ref, ts, first)
        cur = sc_ref[HIST:HIST + ts, :]
        prev = sc_ref[HIST - 1:HIST - 1 + ts, :]
        _carry_rows(sc_ref, ts)
        return cur + (prev - cur) * mu_ref[...]

    r = shift(r_ref, sr_ref, mur_ref)
    k = shift(k_ref, sk_ref, muk_ref)
    v = shift(v_ref, sv_ref, muv_ref)
    lo = shift(l_ref, sl_ref, mul_ref)
    p0, p1, p2 = RWKV_LORA_PAD
    wl, al, gl = lo[:, :p0], lo[:, p0:p0 + p1], lo[:, p0 + p1:p0 + p1 + p2]
    w_log = -_softplus(-(w0_ref[...] + _dot(_bf(jnp.tanh(wl)), w2_ref[...]))) - 0.5
    lw = -jnp.exp(w_log)
    a = _sigmoid(a0_ref[...] + _dot(_bf(al), a2_ref[...]))
    g = _dot(_bf(_sigmoid(gl)), g2_ref[...])
    kk_raw = k * kk_ref[...]
    km = k * (1.0 + (a - 1.0) * ka_ref[...])
    lane = lax.broadcasted_iota(jnp.int32, (ts, 2 * hs), 1)
    sq = kk_raw * kk_raw
    n0 = jnp.sum(jnp.where(lane < hs, sq, 0.0), axis=-1, keepdims=True)
    n1 = jnp.sum(jnp.where(lane >= hs, sq, 0.0), axis=-1, keepdims=True)
    kk = kk_raw * lax.rsqrt(jnp.where(lane < hs, n0, n1) + NORM_EPS)
    aa = -kk
    bb = kk * a
    rkk = r * km * rk_ref[...]
    b0 = jnp.sum(jnp.where(lane < hs, rkk, 0.0), axis=-1, keepdims=True)
    b1 = jnp.sum(jnp.where(lane >= hs, rkk, 0.0), axis=-1, keepdims=True)
    bonus = jnp.where(lane < hs, b0, b1) * v

    causal, strict = _tri_masks(c)
    tril = causal.astype(BF16)
    eye = jnp.where(causal & (~strict), 1.0, 0.0).astype(F32)
    lnw = lnw_ref[...]
    lnb = lnb_ref[...]

    for ci in range(ts // c):
        rows = slice(ci * c, (ci + 1) * c)
        lwc = lw[rows]
        cum = _dot01(tril, lwc)
        c_last = cum[c - 1:c, :]
        w_in = jnp.exp(cum)
        w_inv = jnp.exp(-cum)
        w_prev = jnp.exp(cum - lwc)
        w_tail = jnp.exp(c_last - cum)
        a_t = _bf(aa[rows] * w_prev)
        r_t = _bf(r[rows] * w_in)
        b_t = _bf(bb[rows] * w_inv)
        k_t = _bf(km[rows] * w_inv)
        b_h = _bf(bb[rows] * w_tail)
        k_h = _bf(km[rows] * w_tail)
        vc = _bf(v[rows])
        e_last = jnp.exp(c_last)
        outs = []
        for hd in range(2):
            cs = slice(hd * hs, (hd + 1) * hs)
            a_ab = jnp.where(strict, _dot_nt(a_t[:, cs], b_t[:, cs]), 0.0)
            a_ak = jnp.where(strict, _dot_nt(a_t[:, cs], k_t[:, cs]), 0.0)
            a_rb = jnp.where(causal, _dot_nt(r_t[:, cs], b_t[:, cs]), 0.0)
            a_rk = jnp.where(causal, _dot_nt(r_t[:, cs], k_t[:, cs]), 0.0)
            tmat = _bf(_unit_lower_inverse(a_ab, eye))
            pmat = _dot(tmat, a_t[:, cs])
            qmat = _dot(tmat, _bf(_dot(_bf(a_ak), vc[:, cs])))
            state = st_ref[hd]
            sb = _bf(state)
            u = _dot_nt(_bf(pmat), sb) + qmat
            ub = _bf(u)
            y = _dot_nt(r_t[:, cs], sb) + _dot(_bf(a_rb), ub) + _dot(_bf(a_rk), vc[:, cs])
            st_ref[hd] = state * e_last[:, cs] + _dot_tn(ub, b_h[:, cs]) + _dot_tn(vc[:, cs], k_h[:, cs])
            ym = jnp.mean(y, axis=-1, keepdims=True)
            yd = y - ym
            yv = jnp.mean(yd * yd, axis=-1, keepdims=True)
            outs.append(yd * lax.rsqrt(yv + RWKV_LNX_EPS))
        yn = jnp.concatenate(outs, axis=-1) * lnw + lnb
        o_ref[0, rows, :] = ((yn + bonus[rows]) * g[rows]).astype(o_ref.dtype)


def _rwkv7(hr, hl, mu_rkv, mu_l, w2p, a2p, g2p, w0, a0, k_k, k_a, r_k, lnx_w, lnx_b, ts=256):
    b, s, _ = hr.shape
    ts = min(ts, s)
    pw = 2 * RWKV_HEAD
    npair = HALF // pw
    lw = sum(RWKV_LORA_PAD)
    col = lambda grp: (lambda bi, hp, i: (bi, i, grp * npair + hp))
    vec = lambda grp: pl.BlockSpec((1, pw), lambda bi, hp, i: (0, grp * npair + hp))
    row1 = lambda t: t.astype(F32).reshape(1, -1)
    return pl.pallas_call(
        functools.partial(_rwkv_kernel, ts=ts),
        grid=(b, npair, s // ts),
        in_specs=[pl.BlockSpec((1, ts, pw), col(0)), pl.BlockSpec((1, ts, pw), col(1)),
                  pl.BlockSpec((1, ts, pw), col(2)),
                  pl.BlockSpec((1, ts, lw), lambda bi, hp, i: (bi, i, 0)),
                  vec(0), vec(1), vec(2),
                  pl.BlockSpec((1, lw), lambda bi, hp, i: (0, 0)),
                  pl.BlockSpec((RWKV_LORA_PAD[0], pw), lambda bi, hp, i: (0, hp)),
                  pl.BlockSpec((RWKV_LORA_PAD[1], pw), lambda bi, hp, i: (0, hp)),
                  pl.BlockSpec((RWKV_LORA_PAD[2], pw), lambda bi, hp, i: (0, hp)),
                  vec(0), vec(0), vec(0), vec(0), vec(0), vec(0), vec(0)],
        out_specs=pl.BlockSpec((1, ts, pw), lambda bi, hp, i: (bi, i, hp)),
        out_shape=jax.ShapeDtypeStruct((b, s, HALF), BF16),
        scratch_shapes=[pltpu.VMEM((ts + HIST, pw), F32), pltpu.VMEM((ts + HIST, pw), F32),
                        pltpu.VMEM((ts + HIST, pw), F32), pltpu.VMEM((ts + HIST, lw), F32),
                        pltpu.VMEM((2, RWKV_HEAD, RWKV_HEAD), F32)],
        compiler_params=_cparams("arbitrary", "arbitrary", "arbitrary"),
        name="rwkv7",
    )(hr, hr, hr, hl, row1(mu_rkv), row1(mu_rkv), row1(mu_rkv), row1(mu_l), w2p, a2p, g2p,
      row1(w0), row1(a0), row1(k_k), row1(k_a), row1(r_k), row1(lnx_w), row1(lnx_b))


def _fox_cumsum_kernel(f_ref, bf_ref, o_ref):
    x = f_ref[0] + bf_ref[...]
    ls = jnp.minimum(x, 0.0) - jnp.log(1.0 + jnp.exp(-jnp.abs(x)))
    nh, nr, nl = ls.shape
    li = lax.broadcasted_iota(jnp.int32, (nl, nl), 0)
    lj = lax.broadcasted_iota(jnp.int32, (nl, nl), 1)
    upper = (li <= lj).astype(BF16)
    ri = lax.broadcasted_iota(jnp.int32, (nr, nr), 0)
    rj = lax.broadcasted_iota(jnp.int32, (nr, nr), 1)
    below = (ri > rj).astype(BF16)
    for h in range(nh):
        within = _dot01_right(ls[h], upper)
        tot = jnp.broadcast_to(within[:, nl - 1:nl], (nr, nl))
        o_ref[0, h] = within + _dot01(below, tot)


def _dot01_right(x, m01):
    x1, x2, x3 = _split3(x)
    return _dot(x1, m01) + _dot(x2, m01) + _dot(x3, m01)


def _fox_cumsum(f_t, b_f):
    b, nh, s = f_t.shape
    nl = 128
    nr = s // nl
    out = pl.pallas_call(
        _fox_cumsum_kernel,
        grid=(b,),
        in_specs=[pl.BlockSpec((1, nh, nr, nl), lambda bi: (bi, 0, 0, 0)),
                  pl.BlockSpec((nh, 1, 1), lambda bi: (0, 0, 0))],
        out_specs=pl.BlockSpec((1, nh, nr, nl), lambda bi: (bi, 0, 0, 0)),
        out_shape=jax.ShapeDtypeStruct((b, nh, nr, nl), F32),
        compiler_params=_cparams("arbitrary"),
        name="fox_cumsum",
    )(f_t.reshape(b, nh, nr, nl), b_f.astype(F32).reshape(nh, 1, 1))
    return out.reshape(b, nh, 1, s)


def _fox_kernel(qi_ref, ki_ref, q_ref, k_ref, v_ref, cq_ref, ck_ref, o_ref, m_ref, l_ref, acc_ref, *, tq, tk):
    p = pl.program_id(2)
    qi = qi_ref[p]
    ki = ki_ref[p]
    last_k = ((qi + 1) * tq - 1) // tk

    @pl.when(ki == 0)
    def _():
        m_ref[...] = jnp.full_like(m_ref, -jnp.inf)
        l_ref[...] = jnp.zeros_like(l_ref)
        acc_ref[...] = jnp.zeros_like(acc_ref)

    bias = cq_ref[0, 0, :, 0:1] - ck_ref[0, 0]
    s = _dot_nt(q_ref[0], k_ref[0]) * (FOX_HD ** -0.5) + bias
    qpos = qi * tq + lax.broadcasted_iota(jnp.int32, (tq, tk), 0)
    kpos = ki * tk + lax.broadcasted_iota(jnp.int32, (tq, tk), 1)
    s = jnp.where(kpos <= qpos, s, -jnp.inf)
    m_old = m_ref[...]
    m_new = jnp.maximum(m_old, jnp.max(s, axis=-1, keepdims=True))
    corr = jnp.exp(m_old - m_new)
    pexp = jnp.exp(s - m_new)
    l_ref[...] = corr * l_ref[...] + jnp.sum(pexp, axis=-1, keepdims=True)
    acc_ref[...] = corr * acc_ref[...] + _dot(_bf(pexp), v_ref[0])
    m_ref[...] = m_new

    @pl.when(ki == last_k)
    def _():
        o_ref[0] = (acc_ref[...] / l_ref[...]).astype(o_ref.dtype)


def _fox_attention(hf, c, tq=1024, tk=1024):
    b, s, _ = hf.shape
    tq = min(tq, s)
    tk = min(tk, s)
    nh = FOX_HEADS
    pairs = [(qi, ki) for qi in range(s // tq) for ki in range(((qi + 1) * tq - 1) // tk + 1)]
    qi_arr = jnp.asarray(np.array([p[0] for p in pairs], np.int32))
    ki_arr = jnp.asarray(np.array([p[1] for p in pairs], np.int32))
    grid_spec = pltpu.PrefetchScalarGridSpec(
        num_scalar_prefetch=2,
        grid=(b, nh, len(pairs)),
        in_specs=[pl.BlockSpec((1, tq, FOX_HD), lambda bi, h, p, qa, ka: (bi, qa[p], h)),
                  pl.BlockSpec((1, tk, FOX_HD), lambda bi, h, p, qa, ka: (bi, ka[p], nh + h)),
                  pl.BlockSpec((1, tk, FOX_HD), lambda bi, h, p, qa, ka: (bi, ka[p], 2 * nh + h)),
                  pl.BlockSpec((1, 1, 1, tq), lambda bi, h, p, qa, ka: (bi, h, 0, qa[p])),
                  pl.BlockSpec((1, 1, 1, tk), lambda bi, h, p, qa, ka: (bi, h, 0, ka[p]))],
        out_specs=pl.BlockSpec((1, tq, FOX_HD), lambda bi, h, p, qa, ka: (bi, qa[p], h)),
        scratch_shapes=[pltpu.VMEM((tq, 1), F32), pltpu.VMEM((tq, 1), F32), pltpu.VMEM((tq, FOX_HD), F32)])
    return pl.pallas_call(
        functools.partial(_fox_kernel, tq=tq, tk=tk),
        grid_spec=grid_spec,
        out_shape=jax.ShapeDtypeStruct((b, s, HALF), BF16),
        compiler_params=_cparams("arbitrary", "arbitrary", "arbitrary"),
        name="fox_attention",
    )(qi_arr, ki_arr, hf, hf, hf, c, c)


def _pad_cols(w, width):
    return jnp.pad(w, ((0, 0), (0, width - w.shape[1])))


def _pad_rows(w, height):
    return jnp.pad(w, ((0, height - w.shape[0]), (0, 0)))


def _even_mixer(x32, xb, w_in, pool_w, pool_scale, conv_w, a_log, dt_bias, norm_w, w_out, ln_w, ln_b):
    b, s, d = x32.shape
    t = b * s
    n_main = 5 * HALF
    w_main = _bf(w_in[:, :n_main])
    w_gate = _bf(_pad_cols(w_in[:, n_main:], 128))
    xb2 = xb.reshape(t, d)
    h_main = _mm(xb2, w_main, F32).reshape(b, s, n_main)
    logits = _mm(xb2, w_gate, F32, tn=128).reshape(b, s, 128)
    y_a = _pool_mixer(h_main, _bf(pool_w), pool_scale)
    y_b = _gated_deltanet(h_main, logits, conv_w, a_log, dt_bias, norm_w)
    wo = _bf(w_out)
    return _mm_res_ln([y_a.reshape(t, HALF), y_b.reshape(t, HALF)], [wo[:HALF], wo[HALF:]],
                      x32.reshape(t, d), ln_w, ln_b)


def _odd_mixer(x32, xb, w_in, mu, w0, w2, a0, a2, g2, k_k, k_a, r_k, lnx_w, lnx_b, b_f, w_out, ln_w, ln_b):
    b, s, d = x32.shape
    t = b * s
    l0, l1, l2 = RWKV_LORA
    p0, p1, p2 = RWKV_LORA_PAD
    o_l = 3 * HALF
    o_f = o_l + l0 + l1 + l2
    w_rkv = _bf(w_in[:, :o_l])
    w_l = _bf(jnp.concatenate([_pad_cols(w_in[:, o_l:o_l + l0], p0),
                               _pad_cols(w_in[:, o_l + l0:o_l + l0 + l1], p1),
                               _pad_cols(w_in[:, o_l + l0 + l1:o_f], p2)], axis=1))
    w_fox = _bf(w_in[:, o_f:o_f + 3 * HALF])
    w_fl = _bf(_pad_cols(w_in[:, o_f + 3 * HALF:], 128))
    mu_l = jnp.concatenate([jnp.pad(mu[o_l:o_l + l0], (0, p0 - l0)),
                            jnp.pad(mu[o_l + l0:o_l + l0 + l1], (0, p1 - l1)),
                            jnp.pad(mu[o_l + l0 + l1:o_f], (0, p2 - l2))])
    xb2 = xb.reshape(t, d)
    hr = _mm(xb2, w_rkv, F32).reshape(b, s, o_l)
    hl = _mm(xb2, w_l, F32).reshape(b, s, p0 + p1 + p2)
    hf = _mm(xb2, w_fox, BF16).reshape(b, s, 3 * HALF)
    fl = _mm(xb2, w_fl, F32, tn=128).reshape(b, s, 128)
    y_c = _rwkv7(hr, hl, mu[:o_l], mu_l, _bf(_pad_rows(w2, p0)), _bf(_pad_rows(a2, p1)), _bf(_pad_rows(g2, p2)),
                 w0, a0, k_k, k_a, r_k, lnx_w, lnx_b)
    c = _fox_cumsum(jnp.transpose(fl[:, :, :FOX_HEADS], (0, 2, 1)), b_f)
    y_d = _fox_attention(hf, c)
    wo = _bf(w_out)
    return _mm_res_ln([y_c.reshape(t, HALF), y_d.reshape(t, HALF)], [wo[:HALF], wo[HALF:]],
                      x32.reshape(t, d), ln_w, ln_b)


def _cross_attention(x32, xb, mem_b, w_q, w_kv, w_o, ln_w, ln_b):
    t, d = x32.shape
    b, mlen, _ = mem_b.shape
    s = t // b
    q = _mm(xb, _bf(w_q), BF16).reshape(b, s, d)
    kv = _mm(mem_b.reshape(b * mlen, d), _bf(w_kv), BF16).reshape(b, mlen, 2 * d)
    o = _xattn(q, kv).reshape(t, d)
    return _mm_res_ln([o], [_bf(w_o)], x32, ln_w, ln_b)


def _conv_glu_ffn(x32, xb, b, w_up, conv_w, w_down, ln_w, ln_b):
    t, d = x32.shape
    hmid = _ffn_up(xb.reshape(b, t // b, d), _bf(w_up), conv_w)
    return _mm_res_ln([hmid.reshape(t, D_FF)], [_bf(w_down)], x32, ln_w, ln_b)


def kernel(x, mem, ev_w_in, pool_w, pool_scale, gdn_conv_w, gdn_a_log, gdn_dt_bias, gdn_norm_w, ev_w_out,
           od_w_in, rwkv_mu, rwkv_w0, rwkv_w2, rwkv_a0, rwkv_a2, rwkv_g2, rwkv_k_k, rwkv_k_a, rwkv_r_k,
           rwkv_lnx_w, rwkv_lnx_b, fox_b_f, od_w_out,
           ln_mix_w, ln_mix_b, xa_w_q, xa_w_kv, xa_w_o, ln_xa_w, ln_xa_b,
           ffn_w_up, ffn_conv_w, ffn_w_down, ln_ffn_w, ln_ffn_b):
    b, s, d = x.shape
    depth = ln_mix_w.shape[0]
    mem_b = _bf(mem)
    x32 = x
    xb = _bf(x)
    for i in range(depth):
        j = i // 2
        x3 = x32.reshape(b, s, d)
        xb3 = xb.reshape(b, s, d)
        if i % 2 == 0:
            x32, xb = _even_mixer(x3, xb3, ev_w_in[j], pool_w[j], pool_scale[j], gdn_conv_w[j], gdn_a_log[j],
                                  gdn_dt_bias[j], gdn_norm_w[j], ev_w_out[j], ln_mix_w[i], ln_mix_b[i])
        else:
            x32, xb = _odd_mixer(x3, xb3, od_w_in[j], rwkv_mu[j], rwkv_w0[j], rwkv_w2[j], rwkv_a0[j], rwkv_a2[j],
                                 rwkv_g2[j], rwkv_k_k[j], rwkv_k_a[j], rwkv_r_k[j].reshape(-1), rwkv_lnx_w[j],
                                 rwkv_lnx_b[j], fox_b_f[j], od_w_out[j], ln_mix_w[i], ln_mix_b[i])
        x32, xb = _cross_attention(x32, xb, mem_b, xa_w_q[i], xa_w_kv[i], xa_w_o[i], ln_xa_w[i], ln_xa_b[i])
        x32, xb = _conv_glu_ffn(x32, xb, b, ffn_w_up[i], ffn_conv_w[i], ffn_w_down[i], ln_ffn_w[i], ln_ffn_b[i])
    return x32.reshape(b, s, d)
```

```python
import functools
import math

import numpy as np
import jax
import jax.numpy as jnp
from jax import lax
from jax.experimental import pallas as pl
from jax.experimental.pallas import tpu as pltpu

F32 = jnp.float32
BF16 = jnp.bfloat16

D_MODEL = 2048
HALF = D_MODEL // 2
POOL_WINDOWS = (2, 4, 8, 16)
POOL_GDIM = HALF // len(POOL_WINDOWS)
GDN_HEADS = 8
GDN_DK = HALF // GDN_HEADS
GDN_CONV = 4
NORM_EPS = 1e-6
RWKV_HEAD = 64
RWKV_LNX_EPS = 64e-5
RWKV_LORA = (64, 64, 160)
RWKV_LORA_PAD = (128, 128, 256)
FOX_HEADS = 8
FOX_HD = HALF // FOX_HEADS
XA_HEADS = 4
XA_HD = D_MODEL // XA_HEADS
D_FF = 5632
FFN_CONV = 3
DEPTH = 2
DEEPNORM_ALPHA = float((2 * DEPTH) ** 0.25)
LN_EPS = 1e-5

CHUNK = 64
HIST = 8
VMEM_LIMIT_BYTES = 56 * 1024 * 1024


def _cparams(*sem):
    return pltpu.CompilerParams(dimension_semantics=sem, vmem_limit_bytes=VMEM_LIMIT_BYTES)


def _dot(a, b):
    return jnp.dot(a, b, preferred_element_type=F32)


def _dot_nt(a, b):
    return lax.dot_general(a, b, (((1,), (1,)), ((), ())), preferred_element_type=F32)


def _dot_tn(a, b):
    return lax.dot_general(a, b, (((0,), (0,)), ((), ())), preferred_element_type=F32)


def _bf(x):
    return x.astype(BF16)


def _split3(x):
    x1 = x.astype(BF16)
    r1 = x - x1.astype(F32)
    x2 = r1.astype(BF16)
    x3 = (r1 - x2.astype(F32)).astype(BF16)
    return x1, x2, x3


def _dot01(m01, x):
    x1, x2, x3 = _split3(x)
    return _dot(m01, x1) + _dot(m01, x2) + _dot(m01, x3)


def _unit_lower_inverses(n_mats, eye):
    c = eye.shape[0]
    levels = int(math.log2(c))
    xs = [eye + n for n in n_mats]
    nbs = [_bf(n) for n in n_mats]
    ps = [_dot(nb, nb) for nb in nbs]
    for level in range(1, levels):
        pbs = [_bf(p) for p in ps]
        if level == levels - 1:
            xs = [x + _dot(_bf(x), pb) for x, pb in zip(xs, pbs)]
        else:
            prods = [_dot(jnp.concatenate([pb, _bf(x)], axis=0), pb) for x, pb in zip(xs, pbs)]
            xs = [x + pr[c:] for x, pr in zip(xs, prods)]
            ps = [pr[:c] for pr in prods]
    return xs


def _sigmoid(x):
    return 1.0 / (1.0 + jnp.exp(-x))


def _softplus(x):
    return jnp.maximum(x, 0.0) + jnp.log(1.0 + jnp.exp(-jnp.abs(x)))


def _tri_masks(c):
    row = lax.broadcasted_iota(jnp.int32, (c, c), 0)
    col = lax.broadcasted_iota(jnp.int32, (c, c), 1)
    return row >= col, row > col


def _stage_rows(x_ref, sc_ref, ts, first):
    @pl.when(first)
    def _():
        sc_ref[0:HIST, :] = jnp.zeros((HIST, sc_ref.shape[1]), F32)
    sc_ref[HIST:HIST + ts, :] = x_ref[0].astype(F32)


def _carry_rows(sc_ref, ts):
    sc_ref[0:HIST, :] = sc_ref[ts:ts + HIST, :]


def _mm_kernel(a_ref, w_ref, o_ref):
    o_ref[...] = _dot(a_ref[...], w_ref[...]).astype(o_ref.dtype)


def _mm(a, w, out_dtype, tm=1024, tn=512):
    m, k = a.shape
    n = w.shape[1]
    tm = min(tm, m)
    tn = min(tn, n)
    return pl.pallas_call(
        _mm_kernel,
        grid=(n // tn, m // tm),
        in_specs=[pl.BlockSpec((tm, k), lambda j, i: (i, 0)),
                  pl.BlockSpec((k, tn), lambda j, i: (0, j))],
        out_specs=pl.BlockSpec((tm, tn), lambda j, i: (i, j)),
        out_shape=jax.ShapeDtypeStruct((m, n), out_dtype),
        compiler_params=_cparams("arbitrary", "arbitrary"),
        name="mm",
    )(a, w)


def _mm_res_ln_kernel(*refs, n_in, tk_steps):
    a_refs = refs[:n_in]
    w_refs = refs[n_in:2 * n_in]
    res_ref, lnw_ref, lnb_ref, o32_ref, o16_ref, acc_ref = refs[2 * n_in:]
    kk = pl.program_id(1)

    @pl.when(kk == 0)
    def _():
        acc_ref[...] = jnp.zeros_like(acc_ref)

    acc = acc_ref[...]
    for a_ref, w_ref in zip(a_refs, w_refs):
        acc = acc + _dot(a_ref[...], w_ref[...])
    acc_ref[...] = acc

    @pl.when(kk == tk_steps - 1)
    def _():
        y = DEEPNORM_ALPHA * res_ref[...] + acc_ref[...]
        mu = jnp.mean(y, axis=-1, keepdims=True)
        d = y - mu
        var = jnp.mean(d * d, axis=-1, keepdims=True)
        out = d * lax.rsqrt(var + LN_EPS) * lnw_ref[...] + lnb_ref[...]
        o32_ref[...] = out
        o16_ref[...] = out.astype(BF16)


def _mm_res_ln(a_list, w_list, res, lnw, lnb, tm=512, tk=512):
    m = res.shape[0]
    n = res.shape[1]
    k = a_list[0].shape[1]
    tm = min(tm, m)
    tk = min(tk, k)
    n_in = len(a_list)
    steps = k // tk
    in_specs = ([pl.BlockSpec((tm, tk), lambda i, kk: (i, kk)) for _ in a_list]
                + [pl.BlockSpec((tk, n), lambda i, kk: (kk, 0)) for _ in w_list]
                + [pl.BlockSpec((tm, n), lambda i, kk: (i, 0)),
                   pl.BlockSpec((1, n), lambda i, kk: (0, 0)),
                   pl.BlockSpec((1, n), lambda i, kk: (0, 0))])
    return pl.pallas_call(
        functools.partial(_mm_res_ln_kernel, n_in=n_in, tk_steps=steps),
        grid=(m // tm, steps),
        in_specs=in_specs,
        out_specs=[pl.BlockSpec((tm, n), lambda i, kk: (i, 0)),
                   pl.BlockSpec((tm, n), lambda i, kk: (i, 0))],
        out_shape=[jax.ShapeDtypeStruct((m, n), F32), jax.ShapeDtypeStruct((m, n), BF16)],
        scratch_shapes=[pltpu.VMEM((tm, n), F32)],
        compiler_params=_cparams("arbitrary", "arbitrary"),
        name="mm_res_ln",
    )(*a_list, *w_list, res, lnw.reshape(1, n), lnb.reshape(1, n))


def _xattn_kernel(q_ref, k_ref, v_ref, o_ref):
    s = _dot_nt(q_ref[0], k_ref[0]) * (XA_HD ** -0.5)
    m = jnp.max(s, axis=-1, keepdims=True)
    p = jnp.exp(s - m)
    l = jnp.sum(p, axis=-1, keepdims=True)
    o = _dot(_bf(p / l), v_ref[0])
    o_ref[0] = o.astype(o_ref.dtype)


def _xattn(q, kv, ts=512):
    b, s, d = q.shape
    mlen = kv.shape[1]
    ts = min(ts, s)
    return pl.pallas_call(
        _xattn_kernel,
        grid=(b, s // ts, XA_HEADS),
        in_specs=[pl.BlockSpec((1, ts, XA_HD), lambda bi, i, h: (bi, i, h)),
                  pl.BlockSpec((1, mlen, XA_HD), lambda bi, i, h: (bi, 0, h)),
                  pl.BlockSpec((1, mlen, XA_HD), lambda bi, i, h: (bi, 0, XA_HEADS + h))],
        out_specs=pl.BlockSpec((1, ts, XA_HD), lambda bi, i, h: (bi, i, h)),
        out_shape=jax.ShapeDtypeStruct((b, s, d), BF16),
        compiler_params=_cparams("arbitrary", "arbitrary", "arbitrary"),
        name="xattn",
    )(q, kv, kv)


def _ffn_up_kernel(x_ref, wg_ref, wu_ref, cg_ref, cu_ref, o_ref, sg_ref, su_ref, *, ts):
    first = pl.program_id(2) == 0

    @pl.when(first)
    def _():
        sg_ref[0:HIST, :] = jnp.zeros((HIST, sg_ref.shape[1]), F32)
        su_ref[0:HIST, :] = jnp.zeros((HIST, su_ref.shape[1]), F32)

    x = x_ref[0]
    sg_ref[HIST:HIST + ts, :] = _dot(x, wg_ref[...])
    su_ref[HIST:HIST + ts, :] = _dot(x, wu_ref[...])

    def conv(sc_ref, c_ref):
        y = c_ref[FFN_CONV - 1:FFN_CONV, :] * sc_ref[HIST:HIST + ts, :]
        for back in range(1, FFN_CONV):
            tap = FFN_CONV - 1 - back
            y = y + c_ref[tap:tap + 1, :] * sc_ref[HIST - back:HIST - back + ts, :]
        return y

    g = conv(sg_ref, cg_ref)
    u = conv(su_ref, cu_ref)
    o_ref[0] = (g * _sigmoid(g) * u).astype(o_ref.dtype)
    _carry_rows(sg_ref, ts)
    _carry_rows(su_ref, ts)


def _ffn_up(xb, w_up, conv_w, ts=1024, tn=512):
    b, s, d = xb.shape
    ts = min(ts, s)
    nj = D_FF // tn
    return pl.pallas_call(
        functools.partial(_ffn_up_kernel, ts=ts),
        grid=(nj, b, s // ts),
        in_specs=[pl.BlockSpec((1, ts, d), lambda j, bi, i: (bi, i, 0)),
                  pl.BlockSpec((d, tn), lambda j, bi, i: (0, j)),
                  pl.BlockSpec((d, tn), lambda j, bi, i: (0, nj + j)),
                  pl.BlockSpec((FFN_CONV, tn), lambda j, bi, i: (0, j)),
                  pl.BlockSpec((FFN_CONV, tn), lambda j, bi, i: (0, nj + j))],
        out_specs=pl.BlockSpec((1, ts, tn), lambda j, bi, i: (bi, i, j)),
        out_shape=jax.ShapeDtypeStruct((b, s, D_FF), BF16),
        scratch_shapes=[pltpu.VMEM((ts + HIST, tn), F32), pltpu.VMEM((ts + HIST, tn), F32)],
        compiler_params=_cparams("arbitrary", "arbitrary", "arbitrary"),
        name="ffn_up",
    )(xb, w_up, w_up, conv_w, conv_w)


def _pool_kernel(u_ref, w_ref, sc_ref, o_ref, st_ref, *, ts):
    i = pl.program_id(1)

    @pl.when(i == 0)
    def _():
        st_ref[0:2 * HIST, :] = jnp.zeros((2 * HIST, st_ref.shape[1]), F32)

    st_ref[2 * HIST:2 * HIST + ts, :] = u_ref[0].astype(F32)
    pos = (i * ts + 1 + lax.broadcasted_iota(jnp.int32, (ts, 1), 0)).astype(F32)
    base = 2 * HIST
    for g, win in enumerate(POOL_WINDOWS):
        cols = slice(g * POOL_GDIM, (g + 1) * POOL_GDIM)
        cur = st_ref[base:base + ts, cols]
        acc = cur
        for back in range(1, win):
            acc = acc + st_ref[base - back:base - back + ts, cols]
        mean = acc / jnp.minimum(pos, float(win))
        y = _dot(_bf(mean - cur), w_ref[g])
        o_ref[0, :, cols] = (y * sc_ref[:, cols]).astype(o_ref.dtype)
    st_ref[0:2 * HIST, :] = st_ref[ts:ts + 2 * HIST, :]


def _pool_mixer(h_main, pool_w, pool_scale, ts=512):
    b, s, _ = h_main.shape
    ts = min(ts, s)
    return pl.pallas_call(
        functools.partial(_pool_kernel, ts=ts),
        grid=(b, s // ts),
        in_specs=[pl.BlockSpec((1, ts, HALF), lambda bi, i: (bi, i, 0)),
                  pl.BlockSpec((len(POOL_WINDOWS), POOL_GDIM, POOL_GDIM), lambda bi, i: (0, 0, 0)),
                  pl.BlockSpec((1, HALF), lambda bi, i: (0, 0))],
        out_specs=pl.BlockSpec((1, ts, HALF), lambda bi, i: (bi, i, 0)),
        out_shape=jax.ShapeDtypeStruct((b, s, HALF), BF16),
        scratch_shapes=[pltpu.VMEM((ts + 2 * HIST, HALF), F32)],
        compiler_params=_cparams("arbitrary", "arbitrary"),
        name="pool_mixer",
    )(h_main, pool_w, pool_scale.reshape(1, HALF))


def _gdn_kernel(q_ref, k_ref, v_ref, z_ref, g_ref, cq_ref, ck_ref, cv_ref, alog_ref, dtb_ref, nw_ref,
                o_ref, sq_ref, sk_ref, sv_ref, st_ref, *, ts):
    c = CHUNK
    h = pl.program_id(1)
    first = pl.program_id(2) == 0

    @pl.when(first)
    def _():
        st_ref[...] = jnp.zeros_like(st_ref)

    def conv_silu(x_ref, sc_ref, cw_ref):
        _stage_rows(x_ref, sc_ref, ts, first)
        y = cw_ref[GDN_CONV - 1:GDN_CONV, :] * sc_ref[HIST:HIST + ts, :]
        for back in range(1, GDN_CONV):
            tap = GDN_CONV - 1 - back
            y = y + cw_ref[tap:tap + 1, :] * sc_ref[HIST - back:HIST - back + ts, :]
        _carry_rows(sc_ref, ts)
        return y * _sigmoid(y)

    q = conv_silu(q_ref, sq_ref, cq_ref)
    k = conv_silu(k_ref, sk_ref, ck_ref)
    v = conv_silu(v_ref, sv_ref, cv_ref)
    q = q * lax.rsqrt(jnp.sum(q * q, axis=-1, keepdims=True) + NORM_EPS) * (GDN_DK ** -0.5)
    k = k * lax.rsqrt(jnp.sum(k * k, axis=-1, keepdims=True) + NORM_EPS)

    logits = g_ref[0]
    lane = lax.broadcasted_iota(jnp.int32, logits.shape, 1)
    g_all = -jnp.exp(alog_ref[...]) * _softplus(logits + dtb_ref[...])
    g_col = jnp.sum(jnp.where(lane == h, g_all, 0.0), axis=1, keepdims=True)
    b_col = jnp.sum(jnp.where(lane == h + GDN_HEADS, _sigmoid(logits), 0.0), axis=1, keepdims=True)

    causal, strict = _tri_masks(c)
    tril = causal.astype(BF16)
    eye = jnp.where(causal & (~strict), 1.0, 0.0).astype(F32)
    lane_c = lax.broadcasted_iota(jnp.int32, (c, GDN_DK), 1)
    z = z_ref[0].astype(F32)
    nw = nw_ref[...]

    rows = [slice(ci * c, (ci + 1) * c) for ci in range(ts // c)]
    nck = len(rows)
    betas = [jnp.broadcast_to(b_col[r], (c, GDN_DK)) for r in rows]
    gcs = [_dot01(tril, jnp.broadcast_to(g_col[r], (c, GDN_DK))) for r in rows]
    diffs = []
    for gc in gcs:
        g1, g2, g3 = (p.astype(F32) for p in _split3(gc))
        lhs = jnp.where(lane_c == 0, g1, jnp.where(lane_c == 1, g2, jnp.where(lane_c == 2, g3,
              jnp.where(lane_c < 6, 1.0, 0.0))))
        rhs = jnp.where(lane_c < 3, 1.0, jnp.where(lane_c == 3, -g1, jnp.where(lane_c == 4, -g2,
              jnp.where(lane_c == 5, -g3, 0.0))))
        diffs.append(_dot_nt(_bf(lhs), _bf(rhs)))
    decays = [jnp.where(causal, jnp.exp(jnp.where(causal, d, 0.0)), 0.0) for d in diffs]
    kbs = [k[r] * bt for r, bt in zip(rows, betas)]
    kcbs = [_bf(k[r]) for r in rows]
    lmats = [jnp.where(strict, _dot_nt(_bf(kb), kcb) * dc, 0.0) for kb, kcb, dc in zip(kbs, kcbs, decays)]
    intras = [_bf(jnp.where(causal, _dot_nt(_bf(q[r]), kcb) * dc, 0.0)) for r, kcb, dc in zip(rows, kcbs, decays)]
    tmats = [_bf(t) for t in _unit_lower_inverses([-l for l in lmats], eye)]
    egs = [jnp.exp(gc) for gc in gcs]
    g_lasts = [gc[c - 1:c, :] for gc in gcs]
    us = [_dot(t, _bf(v[r] * bt)) for t, r, bt in zip(tmats, rows, betas)]
    ws = [_bf(_dot(t, _bf(kb * eg))) for t, kb, eg in zip(tmats, kbs, egs)]
    q_decs = [_bf(q[r] * eg) for r, eg in zip(rows, egs)]
    k_decs = [_bf(k[r] * jnp.exp(gl - gc)) for r, gl, gc in zip(rows, g_lasts, gcs)]
    e_lasts = [jnp.exp(gl) for gl in g_lasts]

    state = st_ref[...]
    outs = []
    for ci in range(nck):
        sb = _bf(state)
        v_new = _bf(us[ci] - _dot(ws[ci], sb))
        outs.append(_dot(q_decs[ci], sb) + _dot(intras[ci], v_new))
        state = state * e_lasts[ci] + _dot_tn(k_decs[ci], v_new)
    st_ref[...] = state
    for ci in range(nck):
        o = outs[ci]
        o = o * lax.rsqrt(jnp.mean(o * o, axis=-1, keepdims=True) + NORM_EPS) * nw
        zc = z[rows[ci]]
        o_ref[0, rows[ci], :] = (o * (zc * _sigmoid(zc))).astype(o_ref.dtype)


def _gated_deltanet(h_main, logits, conv_w, a_log, dt_bias, norm_w, ts=512):
    b, s, _ = h_main.shape
    ts = min(ts, s)
    nh = GDN_HEADS
    hd = GDN_DK
    pad = lambda t: jnp.pad(t.astype(F32), (0, hd - t.shape[0])).reshape(1, hd)
    col = lambda grp: (lambda bi, h, i: (bi, i, grp * nh + h))
    cwspec = lambda grp: pl.BlockSpec((GDN_CONV, hd), lambda bi, h, i: (0, grp * nh + h))
    return pl.pallas_call(
        functools.partial(_gdn_kernel, ts=ts),
        grid=(b, nh, s // ts),
        in_specs=[pl.BlockSpec((1, ts, hd), col(1)), pl.BlockSpec((1, ts, hd), col(2)),
                  pl.BlockSpec((1, ts, hd), col(3)), pl.BlockSpec((1, ts, hd), col(4)),
                  pl.BlockSpec((1, ts, hd), lambda bi, h, i: (bi, i, 0)),
                  cwspec(0), cwspec(1), cwspec(2),
                  pl.BlockSpec((1, hd), lambda bi, h, i: (0, 0)),
                  pl.BlockSpec((1, hd), lambda bi, h, i: (0, 0)),
                  pl.BlockSpec((1, hd), lambda bi, h, i: (0, 0))],
        out_specs=pl.BlockSpec((1, ts, hd), lambda bi, h, i: (bi, i, h)),
        out_shape=jax.ShapeDtypeStruct((b, s, HALF), BF16),
        scratch_shapes=[pltpu.VMEM((ts + HIST, hd), F32), pltpu.VMEM((ts + HIST, hd), F32),
                        pltpu.VMEM((ts + HIST, hd), F32), pltpu.VMEM((hd, hd), F32)],
        compiler_params=_cparams("arbitrary", "arbitrary", "arbitrary"),
        name="gated_deltanet",
    )(h_main, h_main, h_main, h_main, logits, conv_w, conv_w, conv_w,
      pad(a_log), pad(dt_bias), norm_w.astype(F32).reshape(1, hd))


def _rwkv_kernel(r_ref, k_ref, v_ref, l_ref, mur_ref, muk_ref, muv_ref, mul_ref, w2_ref, a2_ref, g2_ref,
                 w0_ref, a0_ref, kk_ref, ka_ref, rk_ref, lnw_ref, lnb_ref,
                 o_ref, sr_ref, sk_ref, sv_ref, sl_ref, st_ref, *, ts):
    c = CHUNK
    hs = RWKV_HEAD
    first = pl.program_id(2) == 0

    @pl.when(first)
    def _():
        st_ref[...] = jnp.zeros_like(st_ref)

    def shift(x_ref, sc_ref, mu_ref):
        _stage_rows(x_ref, sc_ref, ts, first)
        cur = sc_ref[HIST:HIST + ts, :]
        prev = sc_ref[HIST - 1:HIST - 1 + ts, :]
        _carry_rows(sc_ref, ts)
        return cur + (prev - cur) * mu_ref[...]

    r = shift(r_ref, sr_ref, mur_ref)
    k = shift(k_ref, sk_ref, muk_ref)
    v = shift(v_ref, sv_ref, muv_ref)
    lo = shift(l_ref, sl_ref, mul_ref)
    p0, p1, p2 = RWKV_LORA_PAD
    wl, al, gl = lo[:, :p0], lo[:, p0:p0 + p1], lo[:, p0 + p1:p0 + p1 + p2]
    w_log = -_softplus(-(w0_ref[...] + _dot(_bf(jnp.tanh(wl)), w2_ref[...]))) - 0.5
    lw = -jnp.exp(w_log)
    a = _sigmoid(a0_ref[...] + _dot(_bf(al), a2_ref[...]))
    g = _dot(_bf(_sigmoid(gl)), g2_ref[...])
    kk_raw = k * kk_ref[...]
    km = k * (1.0 + (a - 1.0) * ka_ref[...])
    lane = lax.broadcasted_iota(jnp.int32, (ts, 2 * hs), 1)
    sq = kk_raw * kk_raw
    n0 = jnp.sum(jnp.where(lane < hs, sq, 0.0), axis=-1, keepdims=True)
    n1 = jnp.sum(jnp.where(lane >= hs, sq, 0.0), axis=-1, keepdims=True)
    kk = kk_raw * lax.rsqrt(jnp.where(lane < hs, n0, n1) + NORM_EPS)
    aa = -kk
    bb = kk * a
    rkk = r * km * rk_ref[...]
    b0 = jnp.sum(jnp.where(lane < hs, rkk, 0.0), axis=-1, keepdims=True)
    b1 = jnp.sum(jnp.where(lane >= hs, rkk, 0.0), axis=-1, keepdims=True)
    bonus = jnp.where(lane < hs, b0, b1) * v

    causal, strict = _tri_masks(c)
    tril = causal.astype(BF16)
    eye = jnp.where(causal & (~strict), 1.0, 0.0).astype(F32)
    lnw = lnw_ref[...]
    lnb = lnb_ref[...]

    rows = [slice(ci * c, (ci + 1) * c) for ci in range(ts // c)]
    nck = len(rows)
    heads = [slice(hd * hs, (hd + 1) * hs) for hd in range(2)]
    cums = [_dot01(tril, lw[rw]) for rw in rows]
    c_lasts = [cum[c - 1:c, :] for cum in cums]
    w_invs = [jnp.exp(-cum) for cum in cums]
    w_tails = [jnp.exp(cl - cum) for cl, cum in zip(c_lasts, cums)]
    a_ts = [_bf(aa[rw] * jnp.exp(cum - lw[rw])) for rw, cum in zip(rows, cums)]
    r_ts = [_bf(r[rw] * jnp.exp(cum)) for rw, cum in zip(rows, cums)]
    b_ts = [_bf(bb[rw] * wi) for rw, wi in zip(rows, w_invs)]
    k_ts = [_bf(km[rw] * wi) for rw, wi in zip(rows, w_invs)]
    b_hs = [_bf(bb[rw] * wt) for rw, wt in zip(rows, w_tails)]
    k_hs = [_bf(km[rw] * wt) for rw, wt in zip(rows, w_tails)]
    vcs = [_bf(v[rw]) for rw in rows]
    e_lasts = [jnp.exp(cl) for cl in c_lasts]
    prob = [(ci, cs) for ci in range(nck) for cs in heads]
    a_abs = [jnp.where(strict, _dot_nt(a_ts[ci][:, cs], b_ts[ci][:, cs]), 0.0) for ci, cs in prob]
    a_aks = [_bf(jnp.where(strict, _dot_nt(a_ts[ci][:, cs], k_ts[ci][:, cs]), 0.0)) for ci, cs in prob]
    a_rbs = [_bf(jnp.where(causal, _dot_nt(r_ts[ci][:, cs], b_ts[ci][:, cs]), 0.0)) for ci, cs in prob]
    a_rks = [_bf(jnp.where(causal, _dot_nt(r_ts[ci][:, cs], k_ts[ci][:, cs]), 0.0)) for ci, cs in prob]
    tmats = [_bf(t) for t in _unit_lower_inverses(a_abs, eye)]
    avs = [_bf(_dot(ak, vcs[ci][:, cs])) for ak, (ci, cs) in zip(a_aks, prob)]
    pmats = [_bf(_dot(t, a_ts[ci][:, cs])) for t, (ci, cs) in zip(tmats, prob)]
    qmats = [_dot(t, av) for t, av in zip(tmats, avs)]
    rkvs = [_dot(ark, vcs[ci][:, cs]) for ark, (ci, cs) in zip(a_rks, prob)]
    vks = [_dot_tn(vcs[ci][:, cs], k_hs[ci][:, cs]) for ci, cs in prob]

    states = [st_ref[hd] for hd in range(2)]
    ys = []
    for ci in range(nck):
        sbs = [_bf(s) for s in states]
        ubs = [_bf(_dot_nt(pmats[2 * ci + hd], sbs[hd]) + qmats[2 * ci + hd]) for hd in range(2)]
        ys.append([_dot_nt(r_ts[ci][:, heads[hd]], sbs[hd]) + _dot(a_rbs[2 * ci + hd], ubs[hd]) + rkvs[2 * ci + hd]
                   for hd in range(2)])
        states = [states[hd] * e_lasts[ci][:, heads[hd]] + _dot_tn(ubs[hd], b_hs[ci][:, heads[hd]]) + vks[2 * ci + hd]
                  for hd in range(2)]
    for hd in range(2):
        st_ref[hd] = states[hd]
    for ci in range(nck):
        outs = []
        for y in ys[ci]:
            ym = jnp.mean(y, axis=-1, keepdims=True)
            yd = y - ym
            yv = jnp.mean(yd * yd, axis=-1, keepdims=True)
            outs.append(yd * lax.rsqrt(yv + RWKV_LNX_EPS))
        yn = jnp.concatenate(outs, axis=-1) * lnw + lnb
        o_ref[0, rows[ci], :] = ((yn + bonus[rows[ci]]) * g[rows[ci]]).astype(o_ref.dtype)


def _rwkv7(hr, hl, mu_rkv, mu_l, w2p, a2p, g2p, w0, a0, k_k, k_a, r_k, lnx_w, lnx_b, ts=256):
    b, s, _ = hr.shape
    ts = min(ts, s)
    pw = 2 * RWKV_HEAD
    npair = HALF // pw
    lw = sum(RWKV_LORA_PAD)
    col = lambda grp: (lambda bi, hp, i: (bi, i, grp * npair + hp))
    vec = lambda grp: pl.BlockSpec((1, pw), lambda bi, hp, i: (0, grp * npair + hp))
    row1 = lambda t: t.astype(F32).reshape(1, -1)
    return pl.pallas_call(
        functools.partial(_rwkv_kernel, ts=ts),
        grid=(b, npair, s // ts),
        in_specs=[pl.BlockSpec((1, ts, pw), col(0)), pl.BlockSpec((1, ts, pw), col(1)),
                  pl.BlockSpec((1, ts, pw), col(2)),
                  pl.BlockSpec((1, ts, lw), lambda bi, hp, i: (bi, i, 0)),
                  vec(0), vec(1), vec(2),
                  pl.BlockSpec((1, lw), lambda bi, hp, i: (0, 0)),
                  pl.BlockSpec((RWKV_LORA_PAD[0], pw), lambda bi, hp, i: (0, hp)),
                  pl.BlockSpec((RWKV_LORA_PAD[1], pw), lambda bi, hp, i: (0, hp)),
                  pl.BlockSpec((RWKV_LORA_PAD[2], pw), lambda bi, hp, i: (0, hp)),
                  vec(0), vec(0), vec(0), vec(0), vec(0), vec(0), vec(0)],
        out_specs=pl.BlockSpec((1, ts, pw), lambda bi, hp, i: (bi, i, hp)),
        out_shape=jax.ShapeDtypeStruct((b, s, HALF), BF16),
        scratch_shapes=[pltpu.VMEM((ts + HIST, pw), F32), pltpu.VMEM((ts + HIST, pw), F32),
                        pltpu.VMEM((ts + HIST, pw), F32), pltpu.VMEM((ts + HIST, lw), F32),
                        pltpu.VMEM((2, RWKV_HEAD, RWKV_HEAD), F32)],
        compiler_params=_cparams("arbitrary", "arbitrary", "arbitrary"),
        name="rwkv7",
    )(hr, hr, hr, hl, row1(mu_rkv), row1(mu_rkv), row1(mu_rkv), row1(mu_l), w2p, a2p, g2p,
      row1(w0), row1(a0), row1(k_k), row1(k_a), row1(r_k), row1(lnx_w), row1(lnx_b))


def _fox_cumsum_kernel(f_ref, bf_ref, o_ref):
    x = f_ref[0] + bf_ref[...]
    ls = jnp.minimum(x, 0.0) - jnp.log(1.0 + jnp.exp(-jnp.abs(x)))
    nh, nr, nl = ls.shape
    li = lax.broadcasted_iota(jnp.int32, (nl, nl), 0)
    lj = lax.broadcasted_iota(jnp.int32, (nl, nl), 1)
    upper = (li <= lj).astype(BF16)
    ri = lax.broadcasted_iota(jnp.int32, (nr, nr), 0)
    rj = lax.broadcasted_iota(jnp.int32, (nr, nr), 1)
    below = (ri > rj).astype(BF16)
    for h in range(nh):
        within = _dot01_right(ls[h], upper)
        tot = jnp.broadcast_to(within[:, nl - 1:nl], (nr, nl))
        o_ref[0, h] = within + _dot01(below, tot)


def _dot01_right(x, m01):
    x1, x2, x3 = _split3(x)
    return _dot(x1, m01) + _dot(x2, m01) + _dot(x3, m01)


def _fox_cumsum(f_t, b_f):
    b, nh, s = f_t.shape
    nl = 128
    nr = s // nl
    out = pl.pallas_call(
        _fox_cumsum_kernel,
        grid=(b,),
        in_specs=[pl.BlockSpec((1, nh, nr, nl), lambda bi: (bi, 0, 0, 0)),
                  pl.BlockSpec((nh, 1, 1), lambda bi: (0, 0, 0))],
        out_specs=pl.BlockSpec((1, nh, nr, nl), lambda bi: (bi, 0, 0, 0)),
        out_shape=jax.ShapeDtypeStruct((b, nh, nr, nl), F32),
        compiler_params=_cparams("arbitrary"),
        name="fox_cumsum",
    )(f_t.reshape(b, nh, nr, nl), b_f.astype(F32).reshape(nh, 1, 1))
    return out.reshape(b, nh, 1, s)


def _fox_kernel(qi_ref, ki_ref, q_ref, k_ref, v_ref, cq_ref, ck_ref, o_ref, qs_ref, m_ref, acc_ref, *, tq, tk, tr):
    p = pl.program_id(2)
    qi = qi_ref[p]
    ki = ki_ref[p]
    last_k = ((qi + 1) * tq - 1) // tk
    log2e = 1.0 / math.log(2.0)

    @pl.when(ki == 0)
    def _():
        qs_ref[...] = (q_ref[0].astype(F32) * (FOX_HD ** -0.5 * log2e)).astype(BF16)
        m_ref[...] = jnp.full_like(m_ref, -jnp.inf)
        acc_ref[...] = jnp.zeros_like(acc_ref)

    def step(masked):
        bias = (cq_ref[0, 0, :, 0:1] - ck_ref[0, 0]) * log2e
        v_aug = jnp.concatenate([v_ref[0], jnp.ones((tk, FOX_HD), BF16)], axis=1)
        kb = k_ref[0]
        for r0 in range(0, tq, tr):
            rows = slice(r0, r0 + tr)
            s = _dot_nt(qs_ref[rows, :], kb) + bias
            if masked:
                qpos = qi * tq + r0 + lax.broadcasted_iota(jnp.int32, (tr, tk), 0)
                kpos = ki * tk + lax.broadcasted_iota(jnp.int32, (tr, tk), 1)
                s = jnp.where(kpos <= qpos, s, -jnp.inf)
            m_old = m_ref[rows, :]
            m_new = jnp.maximum(m_old, jnp.max(s, axis=-1, keepdims=True))
            corr = jnp.exp2(m_old - m_new)
            pexp = _bf(jnp.exp2(s - m_new))
            acc_ref[rows, :] = corr * acc_ref[rows, :] + _dot(pexp, v_aug)
            m_ref[rows, :] = m_new

    needs_mask = (ki + 1) * tk - 1 > qi * tq

    @pl.when(needs_mask)
    def _():
        step(True)

    @pl.when(jnp.logical_not(needs_mask))
    def _():
        step(False)

    @pl.when(ki == last_k)
    def _():
        acc = acc_ref[...]
        o_ref[0] = (acc[:, :FOX_HD] / acc[:, FOX_HD:FOX_HD + 1]).astype(o_ref.dtype)


def _fox_attention(hf, c, tq=1024, tk=1024, tr=128):
    b, s, _ = hf.shape
    tq = min(tq, s)
    tk = min(tk, s)
    nh = FOX_HEADS
    pairs = [(qi, ki) for qi in range(s // tq) for ki in range(((qi + 1) * tq - 1) // tk + 1)]
    qi_arr = jnp.asarray(np.array([p[0] for p in pairs], np.int32))
    ki_arr = jnp.asarray(np.array([p[1] for p in pairs], np.int32))
    grid_spec = pltpu.PrefetchScalarGridSpec(
        num_scalar_prefetch=2,
        grid=(b, nh, len(pairs)),
        in_specs=[pl.BlockSpec((1, tq, FOX_HD), lambda bi, h, p, qa, ka: (bi, qa[p], h)),
                  pl.BlockSpec((1, tk, FOX_HD), lambda bi, h, p, qa, ka: (bi, ka[p], nh + h)),
                  pl.BlockSpec((1, tk, FOX_HD), lambda bi, h, p, qa, ka: (bi, ka[p], 2 * nh + h)),
                  pl.BlockSpec((1, 1, 1, tq), lambda bi, h, p, qa, ka: (bi, h, 0, qa[p])),
                  pl.BlockSpec((1, 1, 1, tk), lambda bi, h, p, qa, ka: (bi, h, 0, ka[p]))],
        out_specs=pl.BlockSpec((1, tq, FOX_HD), lambda bi, h, p, qa, ka: (bi, qa[p], h)),
        scratch_shapes=[pltpu.VMEM((tq, FOX_HD), BF16), pltpu.VMEM((tq, 1), F32),
                        pltpu.VMEM((tq, 2 * FOX_HD), F32)])
    return pl.pallas_call(
        functools.partial(_fox_kernel, tq=tq, tk=tk, tr=min(tr, tq)),
        grid_spec=grid_spec,
        out_shape=jax.ShapeDtypeStruct((b, s, HALF), BF16),
        compiler_params=_cparams("arbitrary", "arbitrary", "arbitrary"),
        name="fox_attention",
    )(qi_arr, ki_arr, hf, hf, hf, c, c)


def _pad_cols(w, width):
    return jnp.pad(w, ((0, 0), (0, width - w.shape[1])))


def _pad_rows(w, height):
    return jnp.pad(w, ((0, height - w.shape[0]), (0, 0)))


def _even_mixer(x32, xb, w_in, pool_w, pool_scale, conv_w, a_log, dt_bias, norm_w, w_out, ln_w, ln_b):
    b, s, d = x32.shape
    t = b * s
    n_main = 5 * HALF
    w_main = _bf(w_in[:, :n_main])
    w_gate = _bf(_pad_cols(w_in[:, n_main:], 128))
    xb2 = xb.reshape(t, d)
    h_main = _mm(xb2, w_main, F32).reshape(b, s, n_main)
    logits = _mm(xb2, w_gate, F32, tn=128).reshape(b, s, 128)
    y_a = _pool_mixer(h_main, _bf(pool_w), pool_scale)
    y_b = _gated_deltanet(h_main, logits, conv_w, a_log, dt_bias, norm_w)
    wo = _bf(w_out)
    return _mm_res_ln([y_a.reshape(t, HALF), y_b.reshape(t, HALF)], [wo[:HALF], wo[HALF:]],
                      x32.reshape(t, d), ln_w, ln_b)


def _odd_mixer(x32, xb, w_in, mu, w0, w2, a0, a2, g2, k_k, k_a, r_k, lnx_w, lnx_b, b_f, w_out, ln_w, ln_b):
    b, s, d = x32.shape
    t = b * s
    l0, l1, l2 = RWKV_LORA
    p0, p1, p2 = RWKV_LORA_PAD
    o_l = 3 * HALF
    o_f = o_l + l0 + l1 + l2
    w_rkv = _bf(w_in[:, :o_l])
    w_l = _bf(jnp.concatenate([_pad_cols(w_in[:, o_l:o_l + l0], p0),
                               _pad_cols(w_in[:, o_l + l0:o_l + l0 + l1], p1),
                               _pad_cols(w_in[:, o_l + l0 + l1:o_f], p2)], axis=1))
    w_fox = _bf(w_in[:, o_f:o_f + 3 * HALF])
    w_fl = _bf(_pad_cols(w_in[:, o_f + 3 * HALF:], 128))
    mu_l = jnp.concatenate([jnp.pad(mu[o_l:o_l + l0], (0, p0 - l0)),
                            jnp.pad(mu[o_l + l0:o_l + l0 + l1], (0, p1 - l1)),
                            jnp.pad(mu[o_l + l0 + l1:o_f], (0, p2 - l2))])
    xb2 = xb.reshape(t, d)
    hr = _mm(xb2, w_rkv, F32).reshape(b, s, o_l)
    hl = _mm(xb2, w_l, F32).reshape(b, s, p0 + p1 + p2)
    hf = _mm(xb2, w_fox, BF16).reshape(b, s, 3 * HALF)
    fl = _mm(xb2, w_fl, F32, tn=128).reshape(b, s, 128)
    y_c = _rwkv7(hr, hl, mu[:o_l], mu_l, _bf(_pad_rows(w2, p0)), _bf(_pad_rows(a2, p1)), _bf(_pad_rows(g2, p2)),
                 w0, a0, k_k, k_a, r_k, lnx_w, lnx_b)
    c = _fox_cumsum(jnp.transpose(fl[:, :, :FOX_HEADS], (0, 2, 1)), b_f)
    y_d = _fox_attention(hf, c)
    wo = _bf(w_out)
    return _mm_res_ln([y_c.reshape(t, HALF), y_d.reshape(t, HALF)], [wo[:HALF], wo[HALF:]],
                      x32.reshape(t, d), ln_w, ln_b)


def _cross_attention(x32, xb, mem_b, w_q, w_kv, w_o, ln_w, ln_b):
    t, d = x32.shape
    b, mlen, _ = mem_b.shape
    s = t // b
    q = _mm(xb, _bf(w_q), BF16).reshape(b, s, d)
    kv = _mm(mem_b.reshape(b * mlen, d), _bf(w_kv), BF16).reshape(b, mlen, 2 * d)
    o = _xattn(q, kv).reshape(t, d)
    return _mm_res_ln([o], [_bf(w_o)], x32, ln_w, ln_b)


def _conv_glu_ffn(x32, xb, b, w_up, conv_w, w_down, ln_w, ln_b):
    t, d = x32.shape
    hmid = _ffn_up(xb.reshape(b, t // b, d), _bf(w_up), conv_w)
    return _mm_res_ln([hmid.reshape(t, D_FF)], [_bf(w_down)], x32, ln_w, ln_b)


def kernel(x, mem, ev_w_in, pool_w, pool_scale, gdn_conv_w, gdn_a_log, gdn_dt_bias, gdn_norm_w, ev_w_out,
           od_w_in, rwkv_mu, rwkv_w0, rwkv_w2, rwkv_a0, rwkv_a2, rwkv_g2, rwkv_k_k, rwkv_k_a, rwkv_r_k,
           rwkv_lnx_w, rwkv_lnx_b, fox_b_f, od_w_out,
           ln_mix_w, ln_mix_b, xa_w_q, xa_w_kv, xa_w_o, ln_xa_w, ln_xa_b,
           ffn_w_up, ffn_conv_w, ffn_w_down, ln_ffn_w, ln_ffn_b):
    b, s, d = x.shape
    depth = ln_mix_w.shape[0]
    mem_b = _bf(mem)
    x32 = x
    xb = _bf(x)
    for i in range(depth):
        j = i // 2
        x3 = x32.reshape(b, s, d)
        xb3 = xb.reshape(b, s, d)
        if i % 2 == 0:
            x32, xb = _even_mixer(x3, xb3, ev_w_in[j], pool_w[j], pool_scale[j], gdn_conv_w[j], gdn_a_log[j],
                                  gdn_dt_bias[j], gdn_norm_w[j], ev_w_out[j], ln_mix_w[i], ln_mix_b[i])
        else:
            x32, xb = _odd_mixer(x3, xb3, od_w_in[j], rwkv_mu[j], rwkv_w0[j], rwkv_w2[j], rwkv_a0[j], rwkv_a2[j],
                                 rwkv_g2[j], rwkv_k_k[j], rwkv_k_a[j], rwkv_r_k[j].reshape(-1), rwkv_lnx_w[j],
                                 rwkv_lnx_b[j], fox_b_f[j], od_w_out[j], ln_mix_w[i], ln_mix_b[i])
        x32, xb = _cross_attention(x32, xb, mem_b, xa_w_q[i], xa_w_kv[i], xa_w_o[i], ln_xa_w[i], ln_xa_b[i])
        x32, xb = _conv_glu_ffn(x32, xb, b, ffn_w_up[i], ffn_conv_w[i], ffn_w_down[i], ln_ffn_w[i], ln_ffn_b[i])
    return x32.reshape(b, s, d)
```

```python
import functools
import math

import numpy as np
import jax
import jax.numpy as jnp
from jax import lax
from jax.experimental import pallas as pl
from jax.experimental.pallas import tpu as pltpu

F32 = jnp.float32
BF16 = jnp.bfloat16

D_MODEL = 2048
HALF = D_MODEL // 2
POOL_WINDOWS = (2, 4, 8, 16)
POOL_GDIM = HALF // len(POOL_WINDOWS)
GDN_HEADS = 8
GDN_DK = HALF // GDN_HEADS
GDN_CONV = 4
NORM_EPS = 1e-6
RWKV_HEAD = 64
RWKV_LNX_EPS = 64e-5
RWKV_LORA = (64, 64, 160)
RWKV_LORA_PAD = (128, 128, 256)
FOX_HEADS = 8
FOX_HD = HALF // FOX_HEADS
XA_HEADS = 4
XA_HD = D_MODEL // XA_HEADS
D_FF = 5632
FFN_CONV = 3
DEPTH = 2
DEEPNORM_ALPHA = float((2 * DEPTH) ** 0.25)
LN_EPS = 1e-5

CHUNK = 64
HIST = 8
VMEM_LIMIT_BYTES = 56 * 1024 * 1024


def _cparams(*sem):
    return pltpu.CompilerParams(dimension_semantics=sem, vmem_limit_bytes=VMEM_LIMIT_BYTES)


def _dot(a, b):
    return jnp.dot(a, b, preferred_element_type=F32)


def _dot_nt(a, b):
    return lax.dot_general(a, b, (((1,), (1,)), ((), ())), preferred_element_type=F32)


def _dot_tn(a, b):
    return lax.dot_general(a, b, (((0,), (0,)), ((), ())), preferred_element_type=F32)


def _bf(x):
    return x.astype(BF16)


def _split3(x):
    x1 = x.astype(BF16)
    r1 = x - x1.astype(F32)
    x2 = r1.astype(BF16)
    x3 = (r1 - x2.astype(F32)).astype(BF16)
    return x1, x2, x3


def _dot01(m01, x):
    x1, x2, x3 = _split3(x)
    return _dot(m01, x1) + _dot(m01, x2) + _dot(m01, x3)


def _unit_lower_inverses(n_mats, eye):
    c = eye.shape[0]
    levels = int(math.log2(c))
    xs = [eye + n for n in n_mats]
    nbs = [_bf(n) for n in n_mats]
    ps = [_dot(nb, nb) for nb in nbs]
    for level in range(1, levels):
        pbs = [_bf(p) for p in ps]
        if level == levels - 1:
            xs = [x + _dot(_bf(x), pb) for x, pb in zip(xs, pbs)]
        else:
            prods = [_dot(jnp.concatenate([pb, _bf(x)], axis=0), pb) for x, pb in zip(xs, pbs)]
            xs = [x + pr[c:] for x, pr in zip(xs, prods)]
            ps = [pr[:c] for pr in prods]
    return xs


def _sigmoid(x):
    return 1.0 / (1.0 + jnp.exp(-x))


def _softplus(x):
    return jnp.maximum(x, 0.0) + jnp.log(1.0 + jnp.exp(-jnp.abs(x)))


def _tri_masks(c):
    row = lax.broadcasted_iota(jnp.int32, (c, c), 0)
    col = lax.broadcasted_iota(jnp.int32, (c, c), 1)
    return row >= col, row > col


def _stage_rows(x_ref, sc_ref, ts, first):
    @pl.when(first)
    def _():
        sc_ref[0:HIST, :] = jnp.zeros((HIST, sc_ref.shape[1]), F32)
    sc_ref[HIST:HIST + ts, :] = x_ref[0].astype(F32)


def _carry_rows(sc_ref, ts):
    sc_ref[0:HIST, :] = sc_ref[ts:ts + HIST, :]


def _mm_kernel(a_ref, w_ref, o_ref):
    o_ref[...] = _dot(a_ref[...], w_ref[...]).astype(o_ref.dtype)


def _mm(a, w, out_dtype, tm=1024, tn=1024):
    m, k = a.shape
    n = w.shape[1]
    tm = min(tm, m)
    tn = min(tn, n)
    return pl.pallas_call(
        _mm_kernel,
        grid=(n // tn, m // tm),
        in_specs=[pl.BlockSpec((tm, k), lambda j, i: (i, 0)),
                  pl.BlockSpec((k, tn), lambda j, i: (0, j))],
        out_specs=pl.BlockSpec((tm, tn), lambda j, i: (i, j)),
        out_shape=jax.ShapeDtypeStruct((m, n), out_dtype),
        compiler_params=_cparams("arbitrary", "arbitrary"),
        name="mm",
    )(a, w)


def _mm_res_ln_kernel(*refs, n_in, tk_steps, tr):
    a_refs = refs[:n_in]
    w_refs = refs[n_in:2 * n_in]
    res_ref, lnw_ref, lnb_ref, o32_ref, o16_ref = refs[2 * n_in:2 * n_in + 5]
    acc_ref = refs[2 * n_in + 5] if tk_steps > 1 else None
    kk = pl.program_id(1)
    tm = res_ref.shape[0]

    def partial_sum(rows):
        tot = None
        for a_ref, w_ref in zip(a_refs, w_refs):
            d = _dot(a_ref[rows, :], w_ref[...])
            tot = d if tot is None else tot + d
        return tot

    if tk_steps > 1:
        @pl.when(kk == 0)
        def _():
            acc_ref[...] = partial_sum(slice(None))

        @pl.when((kk > 0) & (kk < tk_steps - 1))
        def _():
            acc_ref[...] += partial_sum(slice(None))

    @pl.when(kk == tk_steps - 1)
    def _():
        for r0 in range(0, tm, tr):
            rows = slice(r0, r0 + tr)
            y = DEEPNORM_ALPHA * res_ref[rows, :] + partial_sum(rows)
            if tk_steps > 1:
                y = y + acc_ref[rows, :]
            mu = jnp.mean(y, axis=-1, keepdims=True)
            d = y - mu
            var = jnp.mean(d * d, axis=-1, keepdims=True)
            out = d * lax.rsqrt(var + LN_EPS) * lnw_ref[...] + lnb_ref[...]
            o32_ref[rows, :] = out
            o16_ref[rows, :] = out.astype(BF16)


def _mm_res_ln(a_list, w_list, res, lnw, lnb, tm=512, tk=None, tr=128):
    m = res.shape[0]
    n = res.shape[1]
    k = a_list[0].shape[1]
    tm = min(tm, m)
    tk = k if tk is None else min(tk, k)
    tr = min(tr, tm)
    n_in = len(a_list)
    steps = k // tk
    in_specs = ([pl.BlockSpec((tm, tk), lambda i, kk: (i, kk)) for _ in a_list]
                + [pl.BlockSpec((tk, n), lambda i, kk: (kk, 0)) for _ in w_list]
                + [pl.BlockSpec((tm, n), lambda i, kk: (i, 0)),
                   pl.BlockSpec((1, n), lambda i, kk: (0, 0)),
                   pl.BlockSpec((1, n), lambda i, kk: (0, 0))])
    return pl.pallas_call(
        functools.partial(_mm_res_ln_kernel, n_in=n_in, tk_steps=steps, tr=tr),
        grid=(m // tm, steps),
        in_specs=in_specs,
        out_specs=[pl.BlockSpec((tm, n), lambda i, kk: (i, 0)),
                   pl.BlockSpec((tm, n), lambda i, kk: (i, 0))],
        out_shape=[jax.ShapeDtypeStruct((m, n), F32), jax.ShapeDtypeStruct((m, n), BF16)],
        scratch_shapes=[pltpu.VMEM((tm, n), F32)] if steps > 1 else [],
        compiler_params=_cparams("arbitrary", "arbitrary"),
        name="mm_res_ln",
    )(*a_list, *w_list, res, lnw.reshape(1, n), lnb.reshape(1, n))


def _xattn_kernel(q_ref, k_ref, v_ref, o_ref):
    s = _dot_nt(q_ref[0], k_ref[0]) * (XA_HD ** -0.5)
    m = jnp.max(s, axis=-1, keepdims=True)
    p = jnp.exp(s - m)
    l = jnp.sum(p, axis=-1, keepdims=True)
    o = _dot(_bf(p / l), v_ref[0])
    o_ref[0] = o.astype(o_ref.dtype)


def _xattn(q, kv, ts=512):
    b, s, d = q.shape
    mlen = kv.shape[1]
    ts = min(ts, s)
    return pl.pallas_call(
        _xattn_kernel,
        grid=(b, s // ts, XA_HEADS),
        in_specs=[pl.BlockSpec((1, ts, XA_HD), lambda bi, i, h: (bi, i, h)),
                  pl.BlockSpec((1, mlen, XA_HD), lambda bi, i, h: (bi, 0, h)),
                  pl.BlockSpec((1, mlen, XA_HD), lambda bi, i, h: (bi, 0, XA_HEADS + h))],
        out_specs=pl.BlockSpec((1, ts, XA_HD), lambda bi, i, h: (bi, i, h)),
        out_shape=jax.ShapeDtypeStruct((b, s, d), BF16),
        compiler_params=_cparams("arbitrary", "arbitrary", "arbitrary"),
        name="xattn",
    )(q, kv, kv)


def _ffn_up_kernel(x_ref, wg_ref, wu_ref, cg_ref, cu_ref, o_ref, sg_ref, su_ref, *, ts):
    first = pl.program_id(2) == 0

    @pl.when(first)
    def _():
        sg_ref[0:HIST, :] = jnp.zeros((HIST, sg_ref.shape[1]), F32)
        su_ref[0:HIST, :] = jnp.zeros((HIST, su_ref.shape[1]), F32)

    x = x_ref[0]
    sg_ref[HIST:HIST + ts, :] = _dot(x, wg_ref[...])
    su_ref[HIST:HIST + ts, :] = _dot(x, wu_ref[...])

    def conv(sc_ref, c_ref):
        y = c_ref[FFN_CONV - 1:FFN_CONV, :] * sc_ref[HIST:HIST + ts, :]
        for back in range(1, FFN_CONV):
            tap = FFN_CONV - 1 - back
            y = y + c_ref[tap:tap + 1, :] * sc_ref[HIST - back:HIST - back + ts, :]
        return y

    g = conv(sg_ref, cg_ref)
    u = conv(su_ref, cu_ref)
    o_ref[0] = (g * _sigmoid(g) * u).astype(o_ref.dtype)
    _carry_rows(sg_ref, ts)
    _carry_rows(su_ref, ts)


def _ffn_up(xb, w_up, conv_w, ts=1024, tn=512):
    b, s, d = xb.shape
    ts = min(ts, s)
    nj = D_FF // tn
    return pl.pallas_call(
        functools.partial(_ffn_up_kernel, ts=ts),
        grid=(nj, b, s // ts),
        in_specs=[pl.BlockSpec((1, ts, d), lambda j, bi, i: (bi, i, 0)),
                  pl.BlockSpec((d, tn), lambda j, bi, i: (0, j)),
                  pl.BlockSpec((d, tn), lambda j, bi, i: (0, nj + j)),
                  pl.BlockSpec((FFN_CONV, tn), lambda j, bi, i: (0, j)),
                  pl.BlockSpec((FFN_CONV, tn), lambda j, bi, i: (0, nj + j))],
        out_specs=pl.BlockSpec((1, ts, tn), lambda j, bi, i: (bi, i, j)),
        out_shape=jax.ShapeDtypeStruct((b, s, D_FF), BF16),
        scratch_shapes=[pltpu.VMEM((ts + HIST, tn), F32), pltpu.VMEM((ts + HIST, tn), F32)],
        compiler_params=_cparams("arbitrary", "arbitrary", "arbitrary"),
        name="ffn_up",
    )(xb, w_up, w_up, conv_w, conv_w)


def _pool_kernel(u_ref, w_ref, sc_ref, o_ref, st_ref, *, ts):
    i = pl.program_id(1)

    @pl.when(i == 0)
    def _():
        st_ref[0:2 * HIST, :] = jnp.zeros((2 * HIST, st_ref.shape[1]), F32)

    st_ref[2 * HIST:2 * HIST + ts, :] = u_ref[0].astype(F32)
    pos = (i * ts + 1 + lax.broadcasted_iota(jnp.int32, (ts, 1), 0)).astype(F32)
    base = 2 * HIST
    for g, win in enumerate(POOL_WINDOWS):
        cols = slice(g * POOL_GDIM, (g + 1) * POOL_GDIM)
        cur = st_ref[base:base + ts, cols]
        acc = cur
        for back in range(1, win):
            acc = acc + st_ref[base - back:base - back + ts, cols]
        mean = acc / jnp.minimum(pos, float(win))
        y = _dot(_bf(mean - cur), w_ref[g])
        o_ref[0, :, cols] = (y * sc_ref[:, cols]).astype(o_ref.dtype)
    st_ref[0:2 * HIST, :] = st_ref[ts:ts + 2 * HIST, :]


def _pool_mixer(h_main, pool_w, pool_scale, ts=512):
    b, s, _ = h_main.shape
    ts = min(ts, s)
    return pl.pallas_call(
        functools.partial(_pool_kernel, ts=ts),
        grid=(b, s // ts),
        in_specs=[pl.BlockSpec((1, ts, HALF), lambda bi, i: (bi, i, 0)),
                  pl.BlockSpec((len(POOL_WINDOWS), POOL_GDIM, POOL_GDIM), lambda bi, i: (0, 0, 0)),
                  pl.BlockSpec((1, HALF), lambda bi, i: (0, 0))],
        out_specs=pl.BlockSpec((1, ts, HALF), lambda bi, i: (bi, i, 0)),
        out_shape=jax.ShapeDtypeStruct((b, s, HALF), BF16),
        scratch_shapes=[pltpu.VMEM((ts + 2 * HIST, HALF), F32)],
        compiler_params=_cparams("arbitrary", "arbitrary"),
        name="pool_mixer",
    )(h_main, pool_w, pool_scale.reshape(1, HALF))


def _gdn_kernel(q_ref, k_ref, v_ref, z_ref, g_ref, cq_ref, ck_ref, cv_ref, alog_ref, dtb_ref, nw_ref,
                o_ref, sq_ref, sk_ref, sv_ref, st_ref, *, ts, hps):
    c = CHUNK
    h0 = pl.program_id(1)
    first = pl.program_id(2) == 0

    @pl.when(first)
    def _():
        st_ref[...] = jnp.zeros_like(st_ref)

    def conv_silu(x_ref, sc_ref, cw_ref):
        _stage_rows(x_ref, sc_ref, ts, first)
        y = cw_ref[GDN_CONV - 1:GDN_CONV, :] * sc_ref[HIST:HIST + ts, :]
        for back in range(1, GDN_CONV):
            tap = GDN_CONV - 1 - back
            y = y + cw_ref[tap:tap + 1, :] * sc_ref[HIST - back:HIST - back + ts, :]
        _carry_rows(sc_ref, ts)
        return y * _sigmoid(y)

    q_all = conv_silu(q_ref, sq_ref, cq_ref)
    k_all = conv_silu(k_ref, sk_ref, ck_ref)
    v_all = conv_silu(v_ref, sv_ref, cv_ref)
    z_all = z_ref[0].astype(F32)

    logits = g_ref[0]
    lane = lax.broadcasted_iota(jnp.int32, logits.shape, 1)
    g_all = -jnp.exp(alog_ref[...]) * _softplus(logits + dtb_ref[...])
    sig_all = _sigmoid(logits)

    causal, strict = _tri_masks(c)
    tril = causal.astype(BF16)
    eye = jnp.where(causal & (~strict), 1.0, 0.0).astype(F32)
    lane_c = lax.broadcasted_iota(jnp.int32, (c, GDN_DK), 1)
    nw = nw_ref[...]
    chunk_rows = [slice(ci * c, (ci + 1) * c) for ci in range(ts // c)]
    nck = len(chunk_rows)

    q, k, v, betas, g_cols = [], [], [], [], []
    for hd in range(hps):
        hl = slice(hd * GDN_DK, (hd + 1) * GDN_DK)
        qh, kh = q_all[:, hl], k_all[:, hl]
        qh = qh * lax.rsqrt(jnp.sum(qh * qh, axis=-1, keepdims=True) + NORM_EPS) * (GDN_DK ** -0.5)
        kh = kh * lax.rsqrt(jnp.sum(kh * kh, axis=-1, keepdims=True) + NORM_EPS)
        head = h0 * hps + hd
        g_col = jnp.sum(jnp.where(lane == head, g_all, 0.0), axis=1, keepdims=True)
        b_col = jnp.sum(jnp.where(lane == head + GDN_HEADS, sig_all, 0.0), axis=1, keepdims=True)
        for r in chunk_rows:
            q.append(qh[r])
            k.append(kh[r])
            v.append(v_all[r, hl])
            betas.append(jnp.broadcast_to(b_col[r], (c, GDN_DK)))
            g_cols.append(jnp.broadcast_to(g_col[r], (c, GDN_DK)))
    nprob = len(q)
    rows = list(range(nprob))
    gcs = [_dot01(tril, gcol) for gcol in g_cols]
    diffs = []
    for gc in gcs:
        g1, g2, g3 = (p.astype(F32) for p in _split3(gc))
        lhs = jnp.where(lane_c == 0, g1, jnp.where(lane_c == 1, g2, jnp.where(lane_c == 2, g3,
              jnp.where(lane_c < 6, 1.0, 0.0))))
        rhs = jnp.where(lane_c < 3, 1.0, jnp.where(lane_c == 3, -g1, jnp.where(lane_c == 4, -g2,
              jnp.where(lane_c == 5, -g3, 0.0))))
        diffs.append(_dot_nt(_bf(lhs), _bf(rhs)))
    decays = [jnp.where(causal, jnp.exp(jnp.where(causal, d, 0.0)), 0.0) for d in diffs]
    kbs = [k[r] * bt for r, bt in zip(rows, betas)]
    kcbs = [_bf(k[r]) for r in rows]
    lmats = [jnp.where(strict, _dot_nt(_bf(kb), kcb) * dc, 0.0) for kb, kcb, dc in zip(kbs, kcbs, decays)]
    intras = [_bf(jnp.where(causal, _dot_nt(_bf(q[r]), kcb) * dc, 0.0)) for r, kcb, dc in zip(rows, kcbs, decays)]
    tmats = [_bf(t) for t in _unit_lower_inverses([-l for l in lmats], eye)]
    egs = [jnp.exp(gc) for gc in gcs]
    g_lasts = [gc[c - 1:c, :] for gc in gcs]
    us = [_dot(t, _bf(v[r] * bt)) for t, r, bt in zip(tmats, rows, betas)]
    ws = [_bf(_dot(t, _bf(kb * eg))) for t, kb, eg in zip(tmats, kbs, egs)]
    q_decs = [_bf(q[r] * eg) for r, eg in zip(rows, egs)]
    k_decs = [_bf(k[r] * jnp.exp(gl - gc)) for r, gl, gc in zip(rows, g_lasts, gcs)]
    e_lasts = [jnp.exp(gl) for gl in g_lasts]
    mmats = [_bf(_dot_tn(kd, w)) for kd, w in zip(k_decs, ws)]
    nmats = [_dot_tn(kd, _bf(u)) for kd, u in zip(k_decs, us)]

    states = [st_ref[hd] for hd in range(hps)]
    sb_hist = [[None] * nck for _ in range(hps)]
    for ci in range(nck):
        for hd in range(hps):
            p = hd * nck + ci
            sb = _bf(states[hd])
            sb_hist[hd][ci] = sb
            states[hd] = states[hd] * e_lasts[p] - _dot(mmats[p], sb) + nmats[p]
    for hd in range(hps):
        st_ref[hd] = states[hd]
    for hd in range(hps):
        hl = slice(hd * GDN_DK, (hd + 1) * GDN_DK)
        for ci in range(nck):
            p = hd * nck + ci
            sb = sb_hist[hd][ci]
            v_new = _bf(us[p] - _dot(ws[p], sb))
            o = _dot(q_decs[p], sb) + _dot(intras[p], v_new)
            o = o * lax.rsqrt(jnp.mean(o * o, axis=-1, keepdims=True) + NORM_EPS) * nw
            zc = z_all[chunk_rows[ci], hl]
            o_ref[0, chunk_rows[ci], hl] = (o * (zc * _sigmoid(zc))).astype(o_ref.dtype)


def _gated_deltanet(h_main, logits, conv_w, a_log, dt_bias, norm_w, ts=256, hps=2):
    b, s, _ = h_main.shape
    ts = min(ts, s)
    hd = GDN_DK
    gw = hps * hd
    ng = HALF // gw
    pad = lambda t: jnp.pad(t.astype(F32), (0, hd - t.shape[0])).reshape(1, hd)
    col = lambda grp: (lambda bi, h, i: (bi, i, grp * ng + h))
    cwspec = lambda grp: pl.BlockSpec((GDN_CONV, gw), lambda bi, h, i: (0, grp * ng + h))
    return pl.pallas_call(
        functools.partial(_gdn_kernel, ts=ts, hps=hps),
        grid=(b, ng, s // ts),
        in_specs=[pl.BlockSpec((1, ts, gw), col(1)), pl.BlockSpec((1, ts, gw), col(2)),
                  pl.BlockSpec((1, ts, gw), col(3)), pl.BlockSpec((1, ts, gw), col(4)),
                  pl.BlockSpec((1, ts, hd), lambda bi, h, i: (bi, i, 0)),
                  cwspec(0), cwspec(1), cwspec(2),
                  pl.BlockSpec((1, hd), lambda bi, h, i: (0, 0)),
                  pl.BlockSpec((1, hd), lambda bi, h, i: (0, 0)),
                  pl.BlockSpec((1, hd), lambda bi, h, i: (0, 0))],
        out_specs=pl.BlockSpec((1, ts, gw), lambda bi, h, i: (bi, i, h)),
        out_shape=jax.ShapeDtypeStruct((b, s, HALF), BF16),
        scratch_shapes=[pltpu.VMEM((ts + HIST, gw), F32), pltpu.VMEM((ts + HIST, gw), F32),
                        pltpu.VMEM((ts + HIST, gw), F32), pltpu.VMEM((hps, hd, hd), F32)],
        compiler_params=_cparams("arbitrary", "arbitrary", "arbitrary"),
        name="gated_deltanet",
    )(h_main, h_main, h_main, h_main, logits, conv_w, conv_w, conv_w,
      pad(a_log), pad(dt_bias), norm_w.astype(F32).reshape(1, hd))


def _rwkv_kernel(r_ref, k_ref, v_ref, l_ref, mur_ref, muk_ref, muv_ref, mul_ref, w2_ref, a2_ref, g2_ref,
                 w0_ref, a0_ref, kk_ref, ka_ref, rk_ref, lnw_ref, lnb_ref,
                 o_ref, sr_ref, sk_ref, sv_ref, sl_ref, st_ref, *, ts, hps):
    c = CHUNK
    hs = RWKV_HEAD
    first = pl.program_id(2) == 0

    @pl.when(first)
    def _():
        st_ref[...] = jnp.zeros_like(st_ref)

    def shift(x_ref, sc_ref, mu_ref):
        _stage_rows(x_ref, sc_ref, ts, first)
        cur = sc_ref[HIST:HIST + ts, :]
        prev = sc_ref[HIST - 1:HIST - 1 + ts, :]
        _carry_rows(sc_ref, ts)
        return cur + (prev - cur) * mu_ref[...]

    r = shift(r_ref, sr_ref, mur_ref)
    k = shift(k_ref, sk_ref, muk_ref)
    v = shift(v_ref, sv_ref, muv_ref)
    lo = shift(l_ref, sl_ref, mul_ref)
    p0, p1, p2 = RWKV_LORA_PAD
    wl, al, gl = lo[:, :p0], lo[:, p0:p0 + p1], lo[:, p0 + p1:p0 + p1 + p2]
    w_log = -_softplus(-(w0_ref[...] + _dot(_bf(jnp.tanh(wl)), w2_ref[...]))) - 0.5
    lw = -jnp.exp(w_log)
    a = _sigmoid(a0_ref[...] + _dot(_bf(al), a2_ref[...]))
    g = _dot(_bf(_sigmoid(gl)), g2_ref[...])
    kk_raw = k * kk_ref[...]
    km = k * (1.0 + (a - 1.0) * ka_ref[...])
    head_of_lane = lax.broadcasted_iota(jnp.int32, (ts, hps * hs), 1) // hs

    def per_head_sum(x):
        out = jnp.zeros_like(x)
        for hd in range(hps):
            sel = head_of_lane == hd
            out = jnp.where(sel, jnp.sum(jnp.where(sel, x, 0.0), axis=-1, keepdims=True), out)
        return out

    kk = kk_raw * lax.rsqrt(per_head_sum(kk_raw * kk_raw) + NORM_EPS)
    aa = -kk
    bb = kk * a
    bonus = per_head_sum(r * km * rk_ref[...]) * v

    causal, strict = _tri_masks(c)
    tril = causal.astype(BF16)
    eye = jnp.where(causal & (~strict), 1.0, 0.0).astype(F32)
    lnw = lnw_ref[...]
    lnb = lnb_ref[...]

    rows = [slice(ci * c, (ci + 1) * c) for ci in range(ts // c)]
    nck = len(rows)
    heads = [slice(hd * hs, (hd + 1) * hs) for hd in range(hps)]
    cums = [_dot01(tril, lw[rw]) for rw in rows]
    c_lasts = [cum[c - 1:c, :] for cum in cums]
    w_invs = [jnp.exp(-cum) for cum in cums]
    w_tails = [jnp.exp(cl - cum) for cl, cum in zip(c_lasts, cums)]
    a_ts = [_bf(aa[rw] * jnp.exp(cum - lw[rw])) for rw, cum in zip(rows, cums)]
    r_ts = [_bf(r[rw] * jnp.exp(cum)) for rw, cum in zip(rows, cums)]
    b_ts = [_bf(bb[rw] * wi) for rw, wi in zip(rows, w_invs)]
    k_ts = [_bf(km[rw] * wi) for rw, wi in zip(rows, w_invs)]
    b_hs = [_bf(bb[rw] * wt) for rw, wt in zip(rows, w_tails)]
    k_hs = [_bf(km[rw] * wt) for rw, wt in zip(rows, w_tails)]
    vcs = [_bf(v[rw]) for rw in rows]
    e_lasts = [jnp.exp(cl) for cl in c_lasts]
    prob = [(ci, cs) for ci in range(nck) for cs in heads]
    a_abs = [jnp.where(strict, _dot_nt(a_ts[ci][:, cs], b_ts[ci][:, cs]), 0.0) for ci, cs in prob]
    a_aks = [_bf(jnp.where(strict, _dot_nt(a_ts[ci][:, cs], k_ts[ci][:, cs]), 0.0)) for ci, cs in prob]
    a_rbs = [_bf(jnp.where(causal, _dot_nt(r_ts[ci][:, cs], b_ts[ci][:, cs]), 0.0)) for ci, cs in prob]
    a_rks = [_bf(jnp.where(causal, _dot_nt(r_ts[ci][:, cs], k_ts[ci][:, cs]), 0.0)) for ci, cs in prob]
    tmats = [_bf(t) for t in _unit_lower_inverses(a_abs, eye)]
    avs = [_bf(_dot(ak, vcs[ci][:, cs])) for ak, (ci, cs) in zip(a_aks, prob)]
    pmats = [_bf(_dot(t, a_ts[ci][:, cs])) for t, (ci, cs) in zip(tmats, prob)]
    qmats = [_dot(t, av) for t, av in zip(tmats, avs)]
    rkvs = [_dot(ark, vcs[ci][:, cs]) for ark, (ci, cs) in zip(a_rks, prob)]
    mmats = [_bf(_dot_tn(pm, b_hs[ci][:, cs])) for pm, (ci, cs) in zip(pmats, prob)]
    nmats = [_dot_tn(_bf(qm), b_hs[ci][:, cs]) + _dot_tn(vcs[ci][:, cs], k_hs[ci][:, cs])
             for qm, (ci, cs) in zip(qmats, prob)]

    states = [st_ref[hd] for hd in range(hps)]
    sb_hist = []
    for ci in range(nck):
        sbs = [_bf(s) for s in states]
        sb_hist.append(sbs)
        states = [states[hd] * e_lasts[ci][:, heads[hd]] + _dot(sbs[hd], mmats[hps * ci + hd]) + nmats[hps * ci + hd]
                  for hd in range(hps)]
    for hd in range(hps):
        st_ref[hd] = states[hd]
    ys = []
    for ci in range(nck):
        sbs = sb_hist[ci]
        ubs = [_bf(_dot_nt(pmats[hps * ci + hd], sbs[hd]) + qmats[hps * ci + hd]) for hd in range(hps)]
        ys.append([_dot_nt(r_ts[ci][:, heads[hd]], sbs[hd]) + _dot(a_rbs[hps * ci + hd], ubs[hd])
                   + rkvs[hps * ci + hd] for hd in range(hps)])
    for ci in range(nck):
        outs = []
        for y in ys[ci]:
            ym = jnp.mean(y, axis=-1, keepdims=True)
            yd = y - ym
            yv = jnp.mean(yd * yd, axis=-1, keepdims=True)
            outs.append(yd * lax.rsqrt(yv + RWKV_LNX_EPS))
        yn = jnp.concatenate(outs, axis=-1) * lnw + lnb
        o_ref[0, rows[ci], :] = ((yn + bonus[rows[ci]]) * g[rows[ci]]).astype(o_ref.dtype)


def _rwkv7(hr, hl, mu_rkv, mu_l, w2p, a2p, g2p, w0, a0, k_k, k_a, r_k, lnx_w, lnx_b, ts=256, hps=4):
    b, s, _ = hr.shape
    ts = min(ts, s)
    pw = hps * RWKV_HEAD
    npair = HALF // pw
    lw = sum(RWKV_LORA_PAD)
    col = lambda grp: (lambda bi, hp, i: (bi, i, grp * npair + hp))
    vec = lambda grp: pl.BlockSpec((1, pw), lambda bi, hp, i: (0, grp * npair + hp))
    row1 = lambda t: t.astype(F32).reshape(1, -1)
    return pl.pallas_call(
        functools.partial(_rwkv_kernel, ts=ts, hps=hps),
        grid=(b, npair, s // ts),
        in_specs=[pl.BlockSpec((1, ts, pw), col(0)), pl.BlockSpec((1, ts, pw), col(1)),
                  pl.BlockSpec((1, ts, pw), col(2)),
                  pl.BlockSpec((1, ts, lw), lambda bi, hp, i: (bi, i, 0)),
                  vec(0), vec(1), vec(2),
                  pl.BlockSpec((1, lw), lambda bi, hp, i: (0, 0)),
                  pl.BlockSpec((RWKV_LORA_PAD[0], pw), lambda bi, hp, i: (0, hp)),
                  pl.BlockSpec((RWKV_LORA_PAD[1], pw), lambda bi, hp, i: (0, hp)),
                  pl.BlockSpec((RWKV_LORA_PAD[2], pw), lambda bi, hp, i: (0, hp)),
                  vec(0), vec(0), vec(0), vec(0), vec(0), vec(0), vec(0)],
        out_specs=pl.BlockSpec((1, ts, pw), lambda bi, hp, i: (bi, i, hp)),
        out_shape=jax.ShapeDtypeStruct((b, s, HALF), BF16),
        scratch_shapes=[pltpu.VMEM((ts + HIST, pw), F32), pltpu.VMEM((ts + HIST, pw), F32),
                        pltpu.VMEM((ts + HIST, pw), F32), pltpu.VMEM((ts + HIST, lw), F32),
                        pltpu.VMEM((hps, RWKV_HEAD, RWKV_HEAD), F32)],
        compiler_params=_cparams("arbitrary", "arbitrary", "arbitrary"),
        name="rwkv7",
    )(hr, hr, hr, hl, row1(mu_rkv), row1(mu_rkv), row1(mu_rkv), row1(mu_l), w2p, a2p, g2p,
      row1(w0), row1(a0), row1(k_k), row1(k_a), row1(r_k), row1(lnx_w), row1(lnx_b))


def _fox_cumsum_kernel(f_ref, bf_ref, o_ref):
    x = f_ref[0] + bf_ref[...]
    ls = jnp.minimum(x, 0.0) - jnp.log(1.0 + jnp.exp(-jnp.abs(x)))
    nh, nr, nl = ls.shape
    li = lax.broadcasted_iota(jnp.int32, (nl, nl), 0)
    lj = lax.broadcasted_iota(jnp.int32, (nl, nl), 1)
    upper = (li <= lj).astype(BF16)
    ri = lax.broadcasted_iota(jnp.int32, (nr, nr), 0)
    rj = lax.broadcasted_iota(jnp.int32, (nr, nr), 1)
    below = (ri > rj).astype(BF16)
    for h in range(nh):
        within = _dot01_right(ls[h], upper)
        tot = jnp.broadcast_to(within[:, nl - 1:nl], (nr, nl))
        o_ref[0, h] = within + _dot01(below, tot)


def _dot01_right(x, m01):
    x1, x2, x3 = _split3(x)
    return _dot(x1, m01) + _dot(x2, m01) + _dot(x3, m01)


def _fox_cumsum(f_t, b_f):
    b, nh, s = f_t.shape
    nl = 128
    nr = s // nl
    out = pl.pallas_call(
        _fox_cumsum_kernel,
        grid=(b,),
        in_specs=[pl.BlockSpec((1, nh, nr, nl), lambda bi: (bi, 0, 0, 0)),
                  pl.BlockSpec((nh, 1, 1), lambda bi: (0, 0, 0))],
        out_specs=pl.BlockSpec((1, nh, nr, nl), lambda bi: (bi, 0, 0, 0)),
        out_shape=jax.ShapeDtypeStruct((b, nh, nr, nl), F32),
        compiler_params=_cparams("arbitrary"),
        name="fox_cumsum",
    )(f_t.reshape(b, nh, nr, nl), b_f.astype(F32).reshape(nh, 1, 1))
    return out.reshape(b, nh, 1, s)


def _fox_kernel(qi_ref, ki_ref, q_ref, k_ref, v_ref, cq_ref, ck_ref, o_ref, qs_ref, m_ref, acc_ref, *, tq, tk, tr):
    p = pl.program_id(2)
    qi = qi_ref[p]
    ki = ki_ref[p]
    last_k = ((qi + 1) * tq - 1) // tk
    log2e = 1.0 / math.log(2.0)

    @pl.when(ki == 0)
    def _():
        qs_ref[...] = (q_ref[0].astype(F32) * (FOX_HD ** -0.5 * log2e)).astype(BF16)
        m_ref[...] = jnp.full_like(m_ref, -jnp.inf)
        acc_ref[...] = jnp.zeros_like(acc_ref)

    def step(masked):
        bias = (cq_ref[0, 0, :, 0:1] - ck_ref[0, 0]) * log2e
        v_aug = jnp.concatenate([v_ref[0], jnp.ones((tk, FOX_HD), BF16)], axis=1)
        kb = k_ref[0]
        for r0 in range(0, tq, tr):
            rows = slice(r0, r0 + tr)
            s = _dot_nt(qs_ref[rows, :], kb) + bias
            if masked:
                qpos = qi * tq + r0 + lax.broadcasted_iota(jnp.int32, (tr, tk), 0)
                kpos = ki * tk + lax.broadcasted_iota(jnp.int32, (tr, tk), 1)
                s = jnp.where(kpos <= qpos, s, -jnp.inf)
            m_old = m_ref[rows, :]
            m_new = jnp.maximum(m_old, jnp.max(s, axis=-1, keepdims=True))
            corr = jnp.exp2(m_old - m_new)
            pexp = _bf(jnp.exp2(s - m_new))
            acc_ref[rows, :] = corr * acc_ref[rows, :] + _dot(pexp, v_aug)
            m_ref[rows, :] = m_new

    needs_mask = (ki + 1) * tk - 1 > qi * tq

    @pl.when(needs_mask)
    def _():
        step(True)

    @pl.when(jnp.logical_not(needs_mask))
    def _():
        step(False)

    @pl.when(ki == last_k)
    def _():
        acc = acc_ref[...]
        o_ref[0] = (acc[:, :FOX_HD] / acc[:, FOX_HD:FOX_HD + 1]).astype(o_ref.dtype)


def _fox_attention(hf, c, tq=1024, tk=1024, tr=128):
    b, s, _ = hf.shape
    tq = min(tq, s)
    tk = min(tk, s)
    nh = FOX_HEADS
    pairs = [(qi, ki) for qi in range(s // tq) for ki in range(((qi + 1) * tq - 1) // tk + 1)]
    qi_arr = jnp.asarray(np.array([p[0] for p in pairs], np.int32))
    ki_arr = jnp.asarray(np.array([p[1] for p in pairs], np.int32))
    grid_spec = pltpu.PrefetchScalarGridSpec(
        num_scalar_prefetch=2,
        grid=(b, nh, len(pairs)),
        in_specs=[pl.BlockSpec((1, tq, FOX_HD), lambda bi, h, p, qa, ka: (bi, qa[p], h)),
                  pl.BlockSpec((1, tk, FOX_HD), lambda bi, h, p, qa, ka: (bi, ka[p], nh + h)),
                  pl.BlockSpec((1, tk, FOX_HD), lambda bi, h, p, qa, ka: (bi, ka[p], 2 * nh + h)),
                  pl.BlockSpec((1, 1, 1, tq), lambda bi, h, p, qa, ka: (bi, h, 0, qa[p])),
                  pl.BlockSpec((1, 1, 1, tk), lambda bi, h, p, qa, ka: (bi, h, 0, ka[p]))],
        out_specs=pl.BlockSpec((1, tq, FOX_HD), lambda bi, h, p, qa, ka: (bi, qa[p], h)),
        scratch_shapes=[pltpu.VMEM((tq, FOX_HD), BF16), pltpu.VMEM((tq, 1), F32),
                        pltpu.VMEM((tq, 2 * FOX_HD), F32)])
    return pl.pallas_call(
        functools.partial(_fox_kernel, tq=tq, tk=tk, tr=min(tr, tq)),
        grid_spec=grid_spec,
        out_shape=jax.ShapeDtypeStruct((b, s, HALF), BF16),
        compiler_params=_cparams("arbitrary", "arbitrary", "arbitrary"),
        name="fox_attention",
    )(qi_arr, ki_arr, hf, hf, hf, c, c)


def _pad_cols(w, width):
    return jnp.pad(w, ((0, 0), (0, width - w.shape[1])))


def _pad_rows(w, height):
    return jnp.pad(w, ((0, height - w.shape[0]), (0, 0)))


def _even_mixer(x32, xb, w_in, pool_w, pool_scale, conv_w, a_log, dt_bias, norm_w, w_out, ln_w, ln_b):
    b, s, d = x32.shape
    t = b * s
    n_main = 5 * HALF
    w_main = _bf(w_in[:, :n_main])
    w_gate = _bf(_pad_cols(w_in[:, n_main:], 128))
    xb2 = xb.reshape(t, d)
    h_main = _mm(xb2, w_main, F32).reshape(b, s, n_main)
    logits = _mm(xb2, w_gate, F32, tn=128).reshape(b, s, 128)
    y_a = _pool_mixer(h_main, _bf(pool_w), pool_scale)
    y_b = _gated_deltanet(h_main, logits, conv_w, a_log, dt_bias, norm_w)
    wo = _bf(w_out)
    return _mm_res_ln([y_a.reshape(t, HALF), y_b.reshape(t, HALF)], [wo[:HALF], wo[HALF:]],
                      x32.reshape(t, d), ln_w, ln_b)


def _odd_mixer(x32, xb, w_in, mu, w0, w2, a0, a2, g2, k_k, k_a, r_k, lnx_w, lnx_b, b_f, w_out, ln_w, ln_b):
    b, s, d = x32.shape
    t = b * s
    l0, l1, l2 = RWKV_LORA
    p0, p1, p2 = RWKV_LORA_PAD
    o_l = 3 * HALF
    o_f = o_l + l0 + l1 + l2
    w_rkv = _bf(w_in[:, :o_l])
    w_l = _bf(jnp.concatenate([_pad_cols(w_in[:, o_l:o_l + l0], p0),
                               _pad_cols(w_in[:, o_l + l0:o_l + l0 + l1], p1),
                               _pad_cols(w_in[:, o_l + l0 + l1:o_f], p2)], axis=1))
    w_fox = _bf(w_in[:, o_f:o_f + 3 * HALF])
    w_fl = _bf(_pad_cols(w_in[:, o_f + 3 * HALF:], 128))
    mu_l = jnp.concatenate([jnp.pad(mu[o_l:o_l + l0], (0, p0 - l0)),
                            jnp.pad(mu[o_l + l0:o_l + l0 + l1], (0, p1 - l1)),
                            jnp.pad(mu[o_l + l0 + l1:o_f], (0, p2 - l2))])
    xb2 = xb.reshape(t, d)
    hr = _mm(xb2, w_rkv, F32).reshape(b, s, o_l)
    hl = _mm(xb2, w_l, F32).reshape(b, s, p0 + p1 + p2)
    hf = _mm(xb2, w_fox, BF16).reshape(b, s, 3 * HALF)
    fl = _mm(xb2, w_fl, F32, tn=128).reshape(b, s, 128)
    y_c = _rwkv7(hr, hl, mu[:o_l], mu_l, _bf(_pad_rows(w2, p0)), _bf(_pad_rows(a2, p1)), _bf(_pad_rows(g2, p2)),
                 w0, a0, k_k, k_a, r_k, lnx_w, lnx_b)
    c = _fox_cumsum(jnp.transpose(fl[:, :, :FOX_HEADS], (0, 2, 1)), b_f)
    y_d = _fox_attention(hf, c)
    wo = _bf(w_out)
    return _mm_res_ln([y_c.reshape(t, HALF), y_d.reshape(t, HALF)], [wo[:HALF], wo[HALF:]],
                      x32.reshape(t, d), ln_w, ln_b)


def _cross_attention(x32, xb, mem_b, w_q, w_kv, w_o, ln_w, ln_b):
    t, d = x32.shape
    b, mlen, _ = mem_b.shape
    s = t // b
    q = _mm(xb, _bf(w_q), BF16).reshape(b, s, d)
    kv = _mm(mem_b.reshape(b * mlen, d), _bf(w_kv), BF16).reshape(b, mlen, 2 * d)
    o = _xattn(q, kv).reshape(t, d)
    return _mm_res_ln([o], [_bf(w_o)], x32, ln_w, ln_b)


def _conv_glu_ffn(x32, xb, b, w_up, conv_w, w_down, ln_w, ln_b):
    t, d = x32.shape
    hmid = _ffn_up(xb.reshape(b, t // b, d), _bf(w_up), conv_w)
    return _mm_res_ln([hmid.reshape(t, D_FF)], [_bf(w_down)], x32, ln_w, ln_b, tk=D_FF // 4)


def kernel(x, mem, ev_w_in, pool_w, pool_scale, gdn_conv_w, gdn_a_log, gdn_dt_bias, gdn_norm_w, ev_w_out,
           od_w_in, rwkv_mu, rwkv_w0, rwkv_w2, rwkv_a0, rwkv_a2, rwkv_g2, rwkv_k_k, rwkv_k_a, rwkv_r_k,
           rwkv_lnx_w, rwkv_lnx_b, fox_b_f, od_w_out,
           ln_mix_w, ln_mix_b, xa_w_q, xa_w_kv, xa_w_o, ln_xa_w, ln_xa_b,
           ffn_w_up, ffn_conv_w, ffn_w_down, ln_ffn_w, ln_ffn_b):
    b, s, d = x.shape
    depth = ln_mix_w.shape[0]
    mem_b = _bf(mem)
    x32 = x
    xb = _bf(x)
    for i in range(depth):
        j = i // 2
        x3 = x32.reshape(b, s, d)
        xb3 = xb.reshape(b, s, d)
        if i % 2 == 0:
            x32, xb = _even_mixer(x3, xb3, ev_w_in[j], pool_w[j], pool_scale[j], gdn_conv_w[j], gdn_a_log[j],
                                  gdn_dt_bias[j], gdn_norm_w[j], ev_w_out[j], ln_mix_w[i], ln_mix_b[i])
        else:
            x32, xb = _odd_mixer(x3, xb3, od_w_in[j], rwkv_mu[j], rwkv_w0[j], rwkv_w2[j], rwkv_a0[j], rwkv_a2[j],
                                 rwkv_g2[j], rwkv_k_k[j], rwkv_k_a[j], rwkv_r_k[j].reshape(-1), rwkv_lnx_w[j],
                                 rwkv_lnx_b[j], fox_b_f[j], od_w_out[j], ln_mix_w[i], ln_mix_b[i])
        x32, xb = _cross_attention(x32, xb, mem_b, xa_w_q[i], xa_w_kv[i], xa_w_o[i], ln_xa_w[i], ln_xa_b[i])
        x32, xb = _conv_glu_ffn(x32, xb, b, ffn_w_up[i], ffn_conv_w[i], ffn_w_down[i], ln_ffn_w[i], ln_ffn_b[i])
    return x32.reshape(b, s, d)
```

```python
import functools
import math

import numpy as np
import jax
import jax.numpy as jnp
from jax import lax
from jax.experimental import pallas as pl
from jax.experimental.pallas import tpu as pltpu

F32 = jnp.float32
BF16 = jnp.bfloat16

D_MODEL = 2048
HALF = D_MODEL // 2
POOL_WINDOWS = (2, 4, 8, 16)
POOL_GDIM = HALF // len(POOL_WINDOWS)
GDN_HEADS = 8
GDN_DK = HALF // GDN_HEADS
GDN_CONV = 4
NORM_EPS = 1e-6
RWKV_HEAD = 64
RWKV_LNX_EPS = 64e-5
RWKV_LORA = (64, 64, 160)
RWKV_LORA_PAD = (128, 128, 256)
FOX_HEADS = 8
FOX_HD = HALF // FOX_HEADS
XA_HEADS = 4
XA_HD = D_MODEL // XA_HEADS
D_FF = 5632
FFN_CONV = 3
DEPTH = 2
DEEPNORM_ALPHA = float((2 * DEPTH) ** 0.25)
LN_EPS = 1e-5

CHUNK = 64
HIST = 8
VMEM_LIMIT_BYTES = 56 * 1024 * 1024


def _cparams(*sem):
    return pltpu.CompilerParams(dimension_semantics=sem, vmem_limit_bytes=VMEM_LIMIT_BYTES)


def _dot(a, b):
    return jnp.dot(a, b, preferred_element_type=F32)


def _dot_nt(a, b):
    return lax.dot_general(a, b, (((1,), (1,)), ((), ())), preferred_element_type=F32)


def _dot_tn(a, b):
    return lax.dot_general(a, b, (((0,), (0,)), ((), ())), preferred_element_type=F32)


def _bf(x):
    return x.astype(BF16)


def _split3(x):
    x1 = x.astype(BF16)
    r1 = x - x1.astype(F32)
    x2 = r1.astype(BF16)
    x3 = (r1 - x2.astype(F32)).astype(BF16)
    return x1, x2, x3


def _dot01(m01, x):
    x1, x2, x3 = _split3(x)
    return _dot(m01, x1) + _dot(m01, x2) + _dot(m01, x3)


def _unit_lower_inverses(n_mats, eye, fillers=()):
    c = eye.shape[0]
    levels = int(math.log2(c))
    fillers = list(fillers)
    xs = [eye + n for n in n_mats]
    nbs = [_bf(n) for n in n_mats]
    ps = [_dot(nb, nb) for nb in nbs]
    for level in range(1, levels):
        pbs = [_bf(p) for p in ps]
        if level == levels - 1:
            prods = [_dot(_bf(x), pb) for x, pb in zip(xs, pbs)]
        else:
            prods = [_dot(jnp.concatenate([pb, _bf(x)], axis=0), pb) for x, pb in zip(xs, pbs)]
        if fillers:
            fillers.pop(0)()
        if level == levels - 1:
            xs = [x + pr for x, pr in zip(xs, prods)]
        else:
            xs = [x + pr[c:] for x, pr in zip(xs, prods)]
            ps = [pr[:c] for pr in prods]
    for filler in fillers:
        filler()
    return xs


def _sigmoid(x):
    return 1.0 / (1.0 + jnp.exp(-x))


def _softplus(x):
    return jnp.maximum(x, 0.0) + jnp.log(1.0 + jnp.exp(-jnp.abs(x)))


def _tri_masks(c):
    row = lax.broadcasted_iota(jnp.int32, (c, c), 0)
    col = lax.broadcasted_iota(jnp.int32, (c, c), 1)
    return row >= col, row > col


def _stage_rows(x_ref, sc_ref, ts, first):
    @pl.when(first)
    def _():
        sc_ref[0:HIST, :] = jnp.zeros((HIST, sc_ref.shape[1]), F32)
    sc_ref[HIST:HIST + ts, :] = x_ref[0].astype(F32)


def _carry_rows(sc_ref, ts):
    sc_ref[0:HIST, :] = sc_ref[ts:ts + HIST, :]


def _mm_kernel(a_ref, w_ref, o_ref):
    o_ref[...] = _dot(a_ref[...], w_ref[...]).astype(o_ref.dtype)


def _mm(a, w, out_dtype, tm=1024, tn=1024):
    m, k = a.shape
    n = w.shape[1]
    tm = min(tm, m)
    tn = min(tn, n)
    return pl.pallas_call(
        _mm_kernel,
        grid=(n // tn, m // tm),
        in_specs=[pl.BlockSpec((tm, k), lambda j, i: (i, 0)),
                  pl.BlockSpec((k, tn), lambda j, i: (0, j))],
        out_specs=pl.BlockSpec((tm, tn), lambda j, i: (i, j)),
        out_shape=jax.ShapeDtypeStruct((m, n), out_dtype),
        compiler_params=_cparams("arbitrary", "arbitrary"),
        name="mm",
    )(a, w)


def _mm_res_ln_kernel(*refs, n_in, tk_steps, tr):
    a_refs = refs[:n_in]
    w_refs = refs[n_in:2 * n_in]
    res_ref, lnw_ref, lnb_ref, o32_ref, o16_ref = refs[2 * n_in:2 * n_in + 5]
    acc_ref = refs[2 * n_in + 5] if tk_steps > 1 else None
    kk = pl.program_id(1)
    tm = res_ref.shape[0]

    def partial_sum(rows):
        tot = None
        for a_ref, w_ref in zip(a_refs, w_refs):
            d = _dot(a_ref[rows, :], w_ref[...])
            tot = d if tot is None else tot + d
        return tot

    if tk_steps > 1:
        @pl.when(kk == 0)
        def _():
            acc_ref[...] = partial_sum(slice(None))

        @pl.when((kk > 0) & (kk < tk_steps - 1))
        def _():
            acc_ref[...] += partial_sum(slice(None))

    @pl.when(kk == tk_steps - 1)
    def _():
        for r0 in range(0, tm, tr):
            rows = slice(r0, r0 + tr)
            y = DEEPNORM_ALPHA * res_ref[rows, :] + partial_sum(rows)
            if tk_steps > 1:
                y = y + acc_ref[rows, :]
            mu = jnp.mean(y, axis=-1, keepdims=True)
            d = y - mu
            var = jnp.mean(d * d, axis=-1, keepdims=True)
            out = d * lax.rsqrt(var + LN_EPS) * lnw_ref[...] + lnb_ref[...]
            o32_ref[rows, :] = out
            o16_ref[rows, :] = out.astype(BF16)


def _mm_res_ln(a_list, w_list, res, lnw, lnb, tm=512, tk=None, tr=128):
    m = res.shape[0]
    n = res.shape[1]
    k = a_list[0].shape[1]
    tm = min(tm, m)
    tk = k if tk is None else min(tk, k)
    tr = min(tr, tm)
    n_in = len(a_list)
    steps = k // tk
    in_specs = ([pl.BlockSpec((tm, tk), lambda i, kk: (i, kk)) for _ in a_list]
                + [pl.BlockSpec((tk, n), lambda i, kk: (kk, 0)) for _ in w_list]
                + [pl.BlockSpec((tm, n), lambda i, kk: (i, 0)),
                   pl.BlockSpec((1, n), lambda i, kk: (0, 0)),
                   pl.BlockSpec((1, n), lambda i, kk: (0, 0))])
    return pl.pallas_call(
        functools.partial(_mm_res_ln_kernel, n_in=n_in, tk_steps=steps, tr=tr),
        grid=(m // tm, steps),
        in_specs=in_specs,
        out_specs=[pl.BlockSpec((tm, n), lambda i, kk: (i, 0)),
                   pl.BlockSpec((tm, n), lambda i, kk: (i, 0))],
        out_shape=[jax.ShapeDtypeStruct((m, n), F32), jax.ShapeDtypeStruct((m, n), BF16)],
        scratch_shapes=[pltpu.VMEM((tm, n), F32)] if steps > 1 else [],
        compiler_params=_cparams("arbitrary", "arbitrary"),
        name="mm_res_ln",
    )(*a_list, *w_list, res, lnw.reshape(1, n), lnb.reshape(1, n))


def _xattn_kernel(q_ref, k_ref, v_ref, o_ref, *, tr):
    kb = k_ref[0]
    vb = v_ref[0]
    for r0 in range(0, q_ref.shape[1], tr):
        s = _dot_nt(q_ref[0, r0:r0 + tr, :], kb) * (XA_HD ** -0.5)
        m = jnp.max(s, axis=-1, keepdims=True)
        p = jnp.exp(s - m)
        l = jnp.sum(p, axis=-1, keepdims=True)
        o = _dot(_bf(p / l), vb)
        o_ref[0, r0:r0 + tr, :] = o.astype(o_ref.dtype)


def _xattn(q, kv, ts=2048):
    b, s, d = q.shape
    mlen = kv.shape[1]
    ts = min(ts, s)
    return pl.pallas_call(
        functools.partial(_xattn_kernel, tr=min(256, ts)),
        grid=(b, s // ts, XA_HEADS),
        in_specs=[pl.BlockSpec((1, ts, XA_HD), lambda bi, i, h: (bi, i, h)),
                  pl.BlockSpec((1, mlen, XA_HD), lambda bi, i, h: (bi, 0, h)),
                  pl.BlockSpec((1, mlen, XA_HD), lambda bi, i, h: (bi, 0, XA_HEADS + h))],
        out_specs=pl.BlockSpec((1, ts, XA_HD), lambda bi, i, h: (bi, i, h)),
        out_shape=jax.ShapeDtypeStruct((b, s, d), BF16),
        compiler_params=_cparams("arbitrary", "arbitrary", "arbitrary"),
        name="xattn",
    )(q, kv, kv)


def _ffn_up_kernel(x_ref, wg_ref, wu_ref, cg_ref, cu_ref, o_ref, sg_ref, su_ref, *, ts, tr):
    first = pl.program_id(2) == 0

    @pl.when(first)
    def _():
        sg_ref[0:HIST, :] = jnp.zeros((HIST, sg_ref.shape[1]), F32)
        su_ref[0:HIST, :] = jnp.zeros((HIST, su_ref.shape[1]), F32)

    def conv(sc_ref, c_ref, r0):
        y = c_ref[FFN_CONV - 1:FFN_CONV, :] * sc_ref[HIST + r0:HIST + r0 + tr, :]
        for back in range(1, FFN_CONV):
            tap = FFN_CONV - 1 - back
            y = y + c_ref[tap:tap + 1, :] * sc_ref[HIST + r0 - back:HIST + r0 - back + tr, :]
        return y

    for r0 in range(0, ts, tr):
        x = x_ref[0, r0:r0 + tr, :]
        sg_ref[HIST + r0:HIST + r0 + tr, :] = _dot(x, wg_ref[...])
        su_ref[HIST + r0:HIST + r0 + tr, :] = _dot(x, wu_ref[...])
        g = conv(sg_ref, cg_ref, r0)
        u = conv(su_ref, cu_ref, r0)
        o_ref[0, r0:r0 + tr, :] = (g * _sigmoid(g) * u).astype(o_ref.dtype)
    _carry_rows(sg_ref, ts)
    _carry_rows(su_ref, ts)


def _ffn_up(xb, w_up, conv_w, ts=1024, tn=512, tr=1024):
    b, s, d = xb.shape
    ts = min(ts, s)
    nj = D_FF // tn
    return pl.pallas_call(
        functools.partial(_ffn_up_kernel, ts=ts, tr=min(tr, ts)),
        grid=(nj, b, s // ts),
        in_specs=[pl.BlockSpec((1, ts, d), lambda j, bi, i: (bi, i, 0)),
                  pl.BlockSpec((d, tn), lambda j, bi, i: (0, j)),
                  pl.BlockSpec((d, tn), lambda j, bi, i: (0, nj + j)),
                  pl.BlockSpec((FFN_CONV, tn), lambda j, bi, i: (0, j)),
                  pl.BlockSpec((FFN_CONV, tn), lambda j, bi, i: (0, nj + j))],
        out_specs=pl.BlockSpec((1, ts, tn), lambda j, bi, i: (bi, i, j)),
        out_shape=jax.ShapeDtypeStruct((b, s, D_FF), BF16),
        scratch_shapes=[pltpu.VMEM((ts + HIST, tn), F32), pltpu.VMEM((ts + HIST, tn), F32)],
        compiler_params=_cparams("arbitrary", "arbitrary", "arbitrary"),
        name="ffn_up",
    )(xb, w_up, w_up, conv_w, conv_w)


def _pool_kernel(u_ref, w_ref, sc_ref, o_ref, st_ref, *, ts):
    i = pl.program_id(1)

    @pl.when(i == 0)
    def _():
        st_ref[0:2 * HIST, :] = jnp.zeros((2 * HIST, st_ref.shape[1]), F32)

    st_ref[2 * HIST:2 * HIST + ts, :] = u_ref[0].astype(F32)
    pos = (i * ts + 1 + lax.broadcasted_iota(jnp.int32, (ts, 1), 0)).astype(F32)
    base = 2 * HIST
    for g, win in enumerate(POOL_WINDOWS):
        cols = slice(g * POOL_GDIM, (g + 1) * POOL_GDIM)
        cur = st_ref[base:base + ts, cols]
        acc = cur
        for back in range(1, win):
            acc = acc + st_ref[base - back:base - back + ts, cols]
        mean = acc / jnp.minimum(pos, float(win))
        y = _dot(_bf(mean - cur), w_ref[g])
        o_ref[0, :, cols] = (y * sc_ref[:, cols]).astype(o_ref.dtype)
    st_ref[0:2 * HIST, :] = st_ref[ts:ts + 2 * HIST, :]


def _pool_mixer(h_main, pool_w, pool_scale, ts=512):
    b, s, _ = h_main.shape
    ts = min(ts, s)
    return pl.pallas_call(
        functools.partial(_pool_kernel, ts=ts),
        grid=(b, s // ts),
        in_specs=[pl.BlockSpec((1, ts, HALF), lambda bi, i: (bi, i, 0)),
                  pl.BlockSpec((len(POOL_WINDOWS), POOL_GDIM, POOL_GDIM), lambda bi, i: (0, 0, 0)),
                  pl.BlockSpec((1, HALF), lambda bi, i: (0, 0))],
        out_specs=pl.BlockSpec((1, ts, HALF), lambda bi, i: (bi, i, 0)),
        out_shape=jax.ShapeDtypeStruct((b, s, HALF), BF16),
        scratch_shapes=[pltpu.VMEM((ts + 2 * HIST, HALF), F32)],
        compiler_params=_cparams("arbitrary", "arbitrary"),
        name="pool_mixer",
    )(h_main, pool_w, pool_scale.reshape(1, HALF))


def _gdn_kernel(q_ref, k_ref, v_ref, z_ref, g_ref, cq_ref, ck_ref, cv_ref, alog_ref, dtb_ref, nw_ref,
                o_ref, sq_ref, sk_ref, sv_ref, st_ref, *, ts, hps):
    c = CHUNK
    h0 = pl.program_id(1)
    first = pl.program_id(2) == 0

    @pl.when(first)
    def _():
        st_ref[...] = jnp.zeros_like(st_ref)

    def conv_silu(x_ref, sc_ref, cw_ref):
        _stage_rows(x_ref, sc_ref, ts, first)
        y = cw_ref[GDN_CONV - 1:GDN_CONV, :] * sc_ref[HIST:HIST + ts, :]
        for back in range(1, GDN_CONV):
            tap = GDN_CONV - 1 - back
            y = y + cw_ref[tap:tap + 1, :] * sc_ref[HIST - back:HIST - back + ts, :]
        _carry_rows(sc_ref, ts)
        return y * _sigmoid(y)

    q_all = conv_silu(q_ref, sq_ref, cq_ref)
    k_all = conv_silu(k_ref, sk_ref, ck_ref)
    v_all = conv_silu(v_ref, sv_ref, cv_ref)
    z_all = z_ref[0].astype(F32)

    logits = g_ref[0]
    lane = lax.broadcasted_iota(jnp.int32, logits.shape, 1)
    g_all = -jnp.exp(alog_ref[...]) * _softplus(logits + dtb_ref[...])
    sig_all = _sigmoid(logits)

    causal, strict = _tri_masks(c)
    tril = causal.astype(BF16)
    eye = jnp.where(causal & (~strict), 1.0, 0.0).astype(F32)
    lane_c = lax.broadcasted_iota(jnp.int32, (c, GDN_DK), 1)
    nw = nw_ref[...]
    chunk_rows = [slice(ci * c, (ci + 1) * c) for ci in range(ts // c)]
    nck = len(chunk_rows)

    q, k, v, betas, g_cols = [], [], [], [], []
    for hd in range(hps):
        hl = slice(hd * GDN_DK, (hd + 1) * GDN_DK)
        qh, kh = q_all[:, hl], k_all[:, hl]
        qh = qh * lax.rsqrt(jnp.sum(qh * qh, axis=-1, keepdims=True) + NORM_EPS) * (GDN_DK ** -0.5)
        kh = kh * lax.rsqrt(jnp.sum(kh * kh, axis=-1, keepdims=True) + NORM_EPS)
        head = h0 * hps + hd
        g_col = jnp.sum(jnp.where(lane == head, g_all, 0.0), axis=1, keepdims=True)
        b_col = jnp.sum(jnp.where(lane == head + GDN_HEADS, sig_all, 0.0), axis=1, keepdims=True)
        for r in chunk_rows:
            q.append(qh[r])
            k.append(kh[r])
            v.append(v_all[r, hl])
            betas.append(jnp.broadcast_to(b_col[r], (c, GDN_DK)))
            g_cols.append(jnp.broadcast_to(g_col[r], (c, GDN_DK)))
    nprob = len(q)
    rows = list(range(nprob))
    gcs = [_dot01(tril, gcol) for gcol in g_cols]
    diffs = []
    for gc in gcs:
        g1, g2, g3 = (p.astype(F32) for p in _split3(gc))
        lhs = jnp.where(lane_c == 0, g1, jnp.where(lane_c == 1, g2, jnp.where(lane_c == 2, g3,
              jnp.where(lane_c < 6, 1.0, 0.0))))
        rhs = jnp.where(lane_c < 3, 1.0, jnp.where(lane_c == 3, -g1, jnp.where(lane_c == 4, -g2,
              jnp.where(lane_c == 5, -g3, 0.0))))
        diffs.append(_dot_nt(_bf(lhs), _bf(rhs)))
    decays = [jnp.where(causal, jnp.exp(jnp.where(causal, d, 0.0)), 0.0) for d in diffs]
    kbs = [k[r] * bt for r, bt in zip(rows, betas)]
    kcbs = [_bf(k[r]) for r in rows]
    lmats = [jnp.where(strict, _dot_nt(_bf(kb), kcb) * dc, 0.0) for kb, kcb, dc in zip(kbs, kcbs, decays)]
    intras = [_bf(jnp.where(causal, _dot_nt(_bf(q[r]), kcb) * dc, 0.0)) for r, kcb, dc in zip(rows, kcbs, decays)]
    tmats = [_bf(t) for t in _unit_lower_inverses([-l for l in lmats], eye)]
    egs = [jnp.exp(gc) for gc in gcs]
    g_lasts = [gc[c - 1:c, :] for gc in gcs]
    us = [_dot(t, _bf(v[r] * bt)) for t, r, bt in zip(tmats, rows, betas)]
    ws = [_bf(_dot(t, _bf(kb * eg))) for t, kb, eg in zip(tmats, kbs, egs)]
    q_decs = [_bf(q[r] * eg) for r, eg in zip(rows, egs)]
    k_decs = [_bf(k[r] * jnp.exp(gl - gc)) for r, gl, gc in zip(rows, g_lasts, gcs)]
    e_lasts = [jnp.exp(gl) for gl in g_lasts]
    mmats = [_bf(_dot_tn(kd, w)) for kd, w in zip(k_decs, ws)]
    nmats = [_dot_tn(kd, _bf(u)) for kd, u in zip(k_decs, us)]

    states = [st_ref[hd] for hd in range(hps)]
    sb_hist = [[None] * nck for _ in range(hps)]
    for ci in range(nck):
        for hd in range(hps):
            p = hd * nck + ci
            sb = _bf(states[hd])
            sb_hist[hd][ci] = sb
            states[hd] = states[hd] * e_lasts[p] - _dot(mmats[p], sb) + nmats[p]
    for hd in range(hps):
        st_ref[hd] = states[hd]
    for hd in range(hps):
        hl = slice(hd * GDN_DK, (hd + 1) * GDN_DK)
        for ci in range(nck):
            p = hd * nck + ci
            sb = sb_hist[hd][ci]
            v_new = _bf(us[p] - _dot(ws[p], sb))
            o = _dot(q_decs[p], sb) + _dot(intras[p], v_new)
            o = o * lax.rsqrt(jnp.mean(o * o, axis=-1, keepdims=True) + NORM_EPS) * nw
            zc = z_all[chunk_rows[ci], hl]
            o_ref[0, chunk_rows[ci], hl] = (o * (zc * _sigmoid(zc))).astype(o_ref.dtype)


def _gated_deltanet(h_main, logits, conv_w, a_log, dt_bias, norm_w, ts=256, hps=2):
    b, s, _ = h_main.shape
    ts = min(ts, s)
    hd = GDN_DK
    gw = hps * hd
    ng = HALF // gw
    pad = lambda t: jnp.pad(t.astype(F32), (0, hd - t.shape[0])).reshape(1, hd)
    col = lambda grp: (lambda bi, h, i: (bi, i, grp * ng + h))
    cwspec = lambda grp: pl.BlockSpec((GDN_CONV, gw), lambda bi, h, i: (0, grp * ng + h))
    return pl.pallas_call(
        functools.partial(_gdn_kernel, ts=ts, hps=hps),
        grid=(b, ng, s // ts),
        in_specs=[pl.BlockSpec((1, ts, gw), col(1)), pl.BlockSpec((1, ts, gw), col(2)),
                  pl.BlockSpec((1, ts, gw), col(3)), pl.BlockSpec((1, ts, gw), col(4)),
                  pl.BlockSpec((1, ts, hd), lambda bi, h, i: (bi, i, 0)),
                  cwspec(0), cwspec(1), cwspec(2),
                  pl.BlockSpec((1, hd), lambda bi, h, i: (0, 0)),
                  pl.BlockSpec((1, hd), lambda bi, h, i: (0, 0)),
                  pl.BlockSpec((1, hd), lambda bi, h, i: (0, 0))],
        out_specs=pl.BlockSpec((1, ts, gw), lambda bi, h, i: (bi, i, h)),
        out_shape=jax.ShapeDtypeStruct((b, s, HALF), BF16),
        scratch_shapes=[pltpu.VMEM((ts + HIST, gw), F32), pltpu.VMEM((ts + HIST, gw), F32),
                        pltpu.VMEM((ts + HIST, gw), F32), pltpu.VMEM((hps, hd, hd), F32)],
        compiler_params=_cparams("arbitrary", "arbitrary", "arbitrary"),
        name="gated_deltanet",
    )(h_main, h_main, h_main, h_main, logits, conv_w, conv_w, conv_w,
      pad(a_log), pad(dt_bias), norm_w.astype(F32).reshape(1, hd))


def _rwkv_kernel(r_ref, k_ref, v_ref, l_ref, mur_ref, muk_ref, muv_ref, mul_ref, w2_ref, a2_ref, g2_ref,
                 w0_ref, a0_ref, kk_ref, ka_ref, rk_ref, lnw_ref, lnb_ref,
                 o_ref, sr_ref, sk_ref, sv_ref, sl_ref, st_ref, *, ts, hps):
    c = CHUNK
    hs = RWKV_HEAD
    first = pl.program_id(2) == 0

    @pl.when(first)
    def _():
        st_ref[...] = jnp.zeros_like(st_ref)

    def shift(x_ref, sc_ref, mu_ref):
        _stage_rows(x_ref, sc_ref, ts, first)
        cur = sc_ref[HIST:HIST + ts, :]
        prev = sc_ref[HIST - 1:HIST - 1 + ts, :]
        _carry_rows(sc_ref, ts)
        return cur + (prev - cur) * mu_ref[...]

    r = shift(r_ref, sr_ref, mur_ref)
    k = shift(k_ref, sk_ref, muk_ref)
    v = shift(v_ref, sv_ref, muv_ref)
    lo = shift(l_ref, sl_ref, mul_ref)
    p0, p1, p2 = RWKV_LORA_PAD
    wl, al, gl = lo[:, :p0], lo[:, p0:p0 + p1], lo[:, p0 + p1:p0 + p1 + p2]
    w_log = -_softplus(-(w0_ref[...] + _dot(_bf(jnp.tanh(wl)), w2_ref[...]))) - 0.5
    lw = -jnp.exp(w_log)
    a = _sigmoid(a0_ref[...] + _dot(_bf(al), a2_ref[...]))
    g = _dot(_bf(_sigmoid(gl)), g2_ref[...])
    kk_raw = k * kk_ref[...]
    km = k * (1.0 + (a - 1.0) * ka_ref[...])
    head_of_lane = lax.broadcasted_iota(jnp.int32, (ts, hps * hs), 1) // hs

    def per_head_sum(x):
        out = jnp.zeros_like(x)
        for hd in range(hps):
            sel = head_of_lane == hd
            out = jnp.where(sel, jnp.sum(jnp.where(sel, x, 0.0), axis=-1, keepdims=True), out)
        return out

    kk = kk_raw * lax.rsqrt(per_head_sum(kk_raw * kk_raw) + NORM_EPS)
    aa = -kk
    bb = kk * a
    bonus = per_head_sum(r * km * rk_ref[...]) * v

    causal, strict = _tri_masks(c)
    tril = causal.astype(BF16)
    eye = jnp.where(causal & (~strict), 1.0, 0.0).astype(F32)
    lnw = lnw_ref[...]
    lnb = lnb_ref[...]

    rows = [slice(ci * c, (ci + 1) * c) for ci in range(ts // c)]
    nck = len(rows)
    heads = [slice(hd * hs, (hd + 1) * hs) for hd in range(hps)]
    cums = [_dot01(tril, lw[rw]) for rw in rows]
    c_lasts = [cum[c - 1:c, :] for cum in cums]
    w_invs = [jnp.exp(-cum) for cum in cums]
    w_tails = [jnp.exp(cl - cum) for cl, cum in zip(c_lasts, cums)]
    a_ts = [_bf(aa[rw] * jnp.exp(cum - lw[rw])) for rw, cum in zip(rows, cums)]
    r_ts = [_bf(r[rw] * jnp.exp(cum)) for rw, cum in zip(rows, cums)]
    b_ts = [_bf(bb[rw] * wi) for rw, wi in zip(rows, w_invs)]
    k_ts = [_bf(km[rw] * wi) for rw, wi in zip(rows, w_invs)]
    b_hs = [_bf(bb[rw] * wt) for rw, wt in zip(rows, w_tails)]
    k_hs = [_bf(km[rw] * wt) for rw, wt in zip(rows, w_tails)]
    vcs = [_bf(v[rw]) for rw in rows]
    e_lasts = [jnp.exp(cl) for cl in c_lasts]
    prob = [(ci, cs) for ci in range(nck) for cs in heads]
    a_abs = [jnp.where(strict, _dot_nt(a_ts[ci][:, cs], b_ts[ci][:, cs]), 0.0) for ci, cs in prob]
    a_aks = [_bf(jnp.where(strict, _dot_nt(a_ts[ci][:, cs], k_ts[ci][:, cs]), 0.0)) for ci, cs in prob]
    a_rbs = [_bf(jnp.where(causal, _dot_nt(r_ts[ci][:, cs], b_ts[ci][:, cs]), 0.0)) for ci, cs in prob]
    a_rks = [_bf(jnp.where(causal, _dot_nt(r_ts[ci][:, cs], k_ts[ci][:, cs]), 0.0)) for ci, cs in prob]
    tmats = [_bf(t) for t in _unit_lower_inverses(a_abs, eye)]
    avs = [_bf(_dot(ak, vcs[ci][:, cs])) for ak, (ci, cs) in zip(a_aks, prob)]
    pmats = [_bf(_dot(t, a_ts[ci][:, cs])) for t, (ci, cs) in zip(tmats, prob)]
    qmats = [_dot(t, av) for t, av in zip(tmats, avs)]
    rkvs = [_dot(ark, vcs[ci][:, cs]) for ark, (ci, cs) in zip(a_rks, prob)]
    mmats = [_bf(_dot_tn(pm, b_hs[ci][:, cs])) for pm, (ci, cs) in zip(pmats, prob)]
    nmats = [_dot_tn(_bf(qm), b_hs[ci][:, cs]) + _dot_tn(vcs[ci][:, cs], k_hs[ci][:, cs])
             for qm, (ci, cs) in zip(qmats, prob)]

    states = [st_ref[hd] for hd in range(hps)]
    sb_hist = []
    for ci in range(nck):
        sbs = [_bf(s) for s in states]
        sb_hist.append(sbs)
        states = [states[hd] * e_lasts[ci][:, heads[hd]] + _dot(sbs[hd], mmats[hps * ci + hd]) + nmats[hps * ci + hd]
                  for hd in range(hps)]
    for hd in range(hps):
        st_ref[hd] = states[hd]
    ys = []
    for ci in range(nck):
        sbs = sb_hist[ci]
        ubs = [_bf(_dot_nt(pmats[hps * ci + hd], sbs[hd]) + qmats[hps * ci + hd]) for hd in range(hps)]
        ys.append([_dot_nt(r_ts[ci][:, heads[hd]], sbs[hd]) + _dot(a_rbs[hps * ci + hd], ubs[hd])
                   + rkvs[hps * ci + hd] for hd in range(hps)])
    for ci in range(nck):
        outs = []
        for y in ys[ci]:
            ym = jnp.mean(y, axis=-1, keepdims=True)
            yd = y - ym
            yv = jnp.mean(yd * yd, axis=-1, keepdims=True)
            outs.append(yd * lax.rsqrt(yv + RWKV_LNX_EPS))
        yn = jnp.concatenate(outs, axis=-1) * lnw + lnb
        o_ref[0, rows[ci], :] = ((yn + bonus[rows[ci]]) * g[rows[ci]]).astype(o_ref.dtype)


def _rwkv7(hr, hl, mu_rkv, mu_l, w2p, a2p, g2p, w0, a0, k_k, k_a, r_k, lnx_w, lnx_b, ts=256, hps=4):
    b, s, _ = hr.shape
    ts = min(ts, s)
    pw = hps * RWKV_HEAD
    npair = HALF // pw
    lw = sum(RWKV_LORA_PAD)
    col = lambda grp: (lambda bi, hp, i: (bi, i, grp * npair + hp))
    vec = lambda grp: pl.BlockSpec((1, pw), lambda bi, hp, i: (0, grp * npair + hp))
    row1 = lambda t: t.astype(F32).reshape(1, -1)
    return pl.pallas_call(
        functools.partial(_rwkv_kernel, ts=ts, hps=hps),
        grid=(b, npair, s // ts),
        in_specs=[pl.BlockSpec((1, ts, pw), col(0)), pl.BlockSpec((1, ts, pw), col(1)),
                  pl.BlockSpec((1, ts, pw), col(2)),
                  pl.BlockSpec((1, ts, lw), lambda bi, hp, i: (bi, i, 0)),
                  vec(0), vec(1), vec(2),
                  pl.BlockSpec((1, lw), lambda bi, hp, i: (0, 0)),
                  pl.BlockSpec((RWKV_LORA_PAD[0], pw), lambda bi, hp, i: (0, hp)),
                  pl.BlockSpec((RWKV_LORA_PAD[1], pw), lambda bi, hp, i: (0, hp)),
                  pl.BlockSpec((RWKV_LORA_PAD[2], pw), lambda bi, hp, i: (0, hp)),
                  vec(0), vec(0), vec(0), vec(0), vec(0), vec(0), vec(0)],
        out_specs=pl.BlockSpec((1, ts, pw), lambda bi, hp, i: (bi, i, hp)),
        out_shape=jax.ShapeDtypeStruct((b, s, HALF), BF16),
        scratch_shapes=[pltpu.VMEM((ts + HIST, pw), F32), pltpu.VMEM((ts + HIST, pw), F32),
                        pltpu.VMEM((ts + HIST, pw), F32), pltpu.VMEM((ts + HIST, lw), F32),
                        pltpu.VMEM((hps, RWKV_HEAD, RWKV_HEAD), F32)],
        compiler_params=_cparams("arbitrary", "arbitrary", "arbitrary"),
        name="rwkv7",
    )(hr, hr, hr, hl, row1(mu_rkv), row1(mu_rkv), row1(mu_rkv), row1(mu_l), w2p, a2p, g2p,
      row1(w0), row1(a0), row1(k_k), row1(k_a), row1(r_k), row1(lnx_w), row1(lnx_b))


def _fox_cumsum_kernel(f_ref, bf_ref, o_ref):
    x = f_ref[0] + bf_ref[...]
    ls = jnp.minimum(x, 0.0) - jnp.log(1.0 + jnp.exp(-jnp.abs(x)))
    nh, nr, nl = ls.shape
    li = lax.broadcasted_iota(jnp.int32, (nl, nl), 0)
    lj = lax.broadcasted_iota(jnp.int32, (nl, nl), 1)
    upper = (li <= lj).astype(BF16)
    ri = lax.broadcasted_iota(jnp.int32, (nr, nr), 0)
    rj = lax.broadcasted_iota(jnp.int32, (nr, nr), 1)
    below = (ri > rj).astype(BF16)
    for h in range(nh):
        within = _dot01_right(ls[h], upper)
        tot = jnp.broadcast_to(within[:, nl - 1:nl], (nr, nl))
        o_ref[0, h] = within + _dot01(below, tot)


def _dot01_right(x, m01):
    x1, x2, x3 = _split3(x)
    return _dot(x1, m01) + _dot(x2, m01) + _dot(x3, m01)


def _fox_cumsum(f_t, b_f):
    b, nh, s = f_t.shape
    nl = 128
    nr = s // nl
    out = pl.pallas_call(
        _fox_cumsum_kernel,
        grid=(b,),
        in_specs=[pl.BlockSpec((1, nh, nr, nl), lambda bi: (bi, 0, 0, 0)),
                  pl.BlockSpec((nh, 1, 1), lambda bi: (0, 0, 0))],
        out_specs=pl.BlockSpec((1, nh, nr, nl), lambda bi: (bi, 0, 0, 0)),
        out_shape=jax.ShapeDtypeStruct((b, nh, nr, nl), F32),
        compiler_params=_cparams("arbitrary"),
        name="fox_cumsum",
    )(f_t.reshape(b, nh, nr, nl), b_f.astype(F32).reshape(nh, 1, 1))
    return out.reshape(b, nh, 1, s)


def _fox_kernel(qi_ref, ki_ref, q_ref, k_ref, v_ref, cq_ref, ck_ref, o_ref, qs_ref, m_ref, acc_ref, *, tq, tk, tr):
    p = pl.program_id(2)
    qi = qi_ref[p]
    ki = ki_ref[p]
    last_k = ((qi + 1) * tq - 1) // tk
    log2e = 1.0 / math.log(2.0)

    @pl.when(ki == 0)
    def _():
        qs_ref[...] = (q_ref[0].astype(F32) * (FOX_HD ** -0.5 * log2e)).astype(BF16)
        m_ref[...] = jnp.full_like(m_ref, -jnp.inf)
        acc_ref[...] = jnp.zeros_like(acc_ref)

    def step(masked):
        bias = (cq_ref[0, 0, :, 0:1] - ck_ref[0, 0]) * log2e
        v_aug = jnp.concatenate([v_ref[0], jnp.ones((tk, FOX_HD), BF16)], axis=1)
        kb = k_ref[0]
        for r0 in range(0, tq, tr):
            rows = slice(r0, r0 + tr)
            s = _dot_nt(qs_ref[rows, :], kb) + bias
            if masked:
                qpos = qi * tq + r0 + lax.broadcasted_iota(jnp.int32, (tr, tk), 0)
                kpos = ki * tk + lax.broadcasted_iota(jnp.int32, (tr, tk), 1)
                s = jnp.where(kpos <= qpos, s, -jnp.inf)
            m_old = m_ref[rows, :]
            m_new = jnp.maximum(m_old, jnp.max(s, axis=-1, keepdims=True))
            corr = jnp.exp2(m_old - m_new)
            pexp = _bf(jnp.exp2(s - m_new))
            acc_ref[rows, :] = corr * acc_ref[rows, :] + _dot(pexp, v_aug)
            m_ref[rows, :] = m_new

    needs_mask = (ki + 1) * tk - 1 > qi * tq

    @pl.when(needs_mask)
    def _():
        step(True)

    @pl.when(jnp.logical_not(needs_mask))
    def _():
        step(False)

    @pl.when(ki == last_k)
    def _():
        acc = acc_ref[...]
        o_ref[0] = (acc[:, :FOX_HD] / acc[:, FOX_HD:FOX_HD + 1]).astype(o_ref.dtype)


def _fox_attention(hf, c, tq=2048, tk=1024, tr=128):
    b, s, _ = hf.shape
    tq = min(tq, s)
    tk = min(tk, s)
    nh = FOX_HEADS
    pairs = [(qi, ki) for qi in range(s // tq) for ki in range(((qi + 1) * tq - 1) // tk + 1)]
    qi_arr = jnp.asarray(np.array([p[0] for p in pairs], np.int32))
    ki_arr = jnp.asarray(np.array([p[1] for p in pairs], np.int32))
    grid_spec = pltpu.PrefetchScalarGridSpec(
        num_scalar_prefetch=2,
        grid=(b, nh, len(pairs)),
        in_specs=[pl.BlockSpec((1, tq, FOX_HD), lambda bi, h, p, qa, ka: (bi, qa[p], h)),
                  pl.BlockSpec((1, tk, FOX_HD), lambda bi, h, p, qa, ka: (bi, ka[p], nh + h)),
                  pl.BlockSpec((1, tk, FOX_HD), lambda bi, h, p, qa, ka: (bi, ka[p], 2 * nh + h)),
                  pl.BlockSpec((1, 1, 1, tq), lambda bi, h, p, qa, ka: (bi, h, 0, qa[p])),
                  pl.BlockSpec((1, 1, 1, tk), lambda bi, h, p, qa, ka: (bi, h, 0, ka[p]))],
        out_specs=pl.BlockSpec((1, tq, FOX_HD), lambda bi, h, p, qa, ka: (bi, qa[p], h)),
        scratch_shapes=[pltpu.VMEM((tq, FOX_HD), BF16), pltpu.VMEM((tq, 1), F32),
                        pltpu.VMEM((tq, 2 * FOX_HD), F32)])
    return pl.pallas_call(
        functools.partial(_fox_kernel, tq=tq, tk=tk, tr=min(tr, tq)),
        grid_spec=grid_spec,
        out_shape=jax.ShapeDtypeStruct((b, s, HALF), BF16),
        compiler_params=_cparams("arbitrary", "arbitrary", "arbitrary"),
        name="fox_attention",
    )(qi_arr, ki_arr, hf, hf, hf, c, c)


def _pad_cols(w, width):
    return jnp.pad(w, ((0, 0), (0, width - w.shape[1])))


def _pad_rows(w, height):
    return jnp.pad(w, ((0, height - w.shape[0]), (0, 0)))


def _even_mixer(x32, xb, w_in, pool_w, pool_scale, conv_w, a_log, dt_bias, norm_w, w_out, ln_w, ln_b):
    b, s, d = x32.shape
    t = b * s
    n_main = 5 * HALF
    w_main = _bf(w_in[:, :n_main])
    w_gate = _bf(_pad_cols(w_in[:, n_main:], 128))
    xb2 = xb.reshape(t, d)
    h_main = _mm(xb2, w_main, F32).reshape(b, s, n_main)
    logits = _mm(xb2, w_gate, F32, tn=128).reshape(b, s, 128)
    y_a = _pool_mixer(h_main, _bf(pool_w), pool_scale)
    y_b = _gated_deltanet(h_main, logits, conv_w, a_log, dt_bias, norm_w)
    wo = _bf(w_out)
    return _mm_res_ln([y_a.reshape(t, HALF), y_b.reshape(t, HALF)], [wo[:HALF], wo[HALF:]],
                      x32.reshape(t, d), ln_w, ln_b)


def _odd_mixer(x32, xb, w_in, mu, w0, w2, a0, a2, g2, k_k, k_a, r_k, lnx_w, lnx_b, b_f, w_out, ln_w, ln_b):
    b, s, d = x32.shape
    t = b * s
    l0, l1, l2 = RWKV_LORA
    p0, p1, p2 = RWKV_LORA_PAD
    o_l = 3 * HALF
    o_f = o_l + l0 + l1 + l2
    w_rkv = _bf(w_in[:, :o_l])
    w_l = _bf(jnp.concatenate([_pad_cols(w_in[:, o_l:o_l + l0], p0),
                               _pad_cols(w_in[:, o_l + l0:o_l + l0 + l1], p1),
                               _pad_cols(w_in[:, o_l + l0 + l1:o_f], p2)], axis=1))
    w_fox = _bf(w_in[:, o_f:o_f + 3 * HALF])
    w_fl = _bf(_pad_cols(w_in[:, o_f + 3 * HALF:], 128))
    mu_l = jnp.concatenate([jnp.pad(mu[o_l:o_l + l0], (0, p0 - l0)),
                            jnp.pad(mu[o_l + l0:o_l + l0 + l1], (0, p1 - l1)),
                            jnp.pad(mu[o_l + l0 + l1:o_f], (0, p2 - l2))])
    xb2 = xb.reshape(t, d)
    hr = _mm(xb2, w_rkv, F32).reshape(b, s, o_l)
    hl = _mm(xb2, w_l, F32).reshape(b, s, p0 + p1 + p2)
    hf = _mm(xb2, w_fox, BF16).reshape(b, s, 3 * HALF)
    fl = _mm(xb2, w_fl, F32, tn=128).reshape(b, s, 128)
    y_c = _rwkv7(hr, hl, mu[:o_l], mu_l, _bf(_pad_rows(w2, p0)), _bf(_pad_rows(a2, p1)), _bf(_pad_rows(g2, p2)),
                 w0, a0, k_k, k_a, r_k, lnx_w, lnx_b)
    c = _fox_cumsum(jnp.transpose(fl[:, :, :FOX_HEADS], (0, 2, 1)), b_f)
    y_d = _fox_attention(hf, c)
    wo = _bf(w_out)
    return _mm_res_ln([y_c.reshape(t, HALF), y_d.reshape(t, HALF)], [wo[:HALF], wo[HALF:]],
                      x32.reshape(t, d), ln_w, ln_b)


def _cross_attention(x32, xb, mem_b, w_q, w_kv, w_o, ln_w, ln_b):
    t, d = x32.shape
    b, mlen, _ = mem_b.shape
    s = t // b
    q = _mm(xb, _bf(w_q), BF16).reshape(b, s, d)
    kv = _mm(mem_b.reshape(b * mlen, d), _bf(w_kv), BF16).reshape(b, mlen, 2 * d)
    o = _xattn(q, kv).reshape(t, d)
    return _mm_res_ln([o], [_bf(w_o)], x32, ln_w, ln_b)


def _conv_glu_ffn(x32, xb, b, w_up, conv_w, w_down, ln_w, ln_b):
    t, d = x32.shape
    hmid = _ffn_up(xb.reshape(b, t // b, d), _bf(w_up), conv_w)
    return _mm_res_ln([hmid.reshape(t, D_FF)], [_bf(w_down)], x32, ln_w, ln_b, tk=D_FF // 4)


def kernel(x, mem, ev_w_in, pool_w, pool_scale, gdn_conv_w, gdn_a_log, gdn_dt_bias, gdn_norm_w, ev_w_out,
           od_w_in, rwkv_mu, rwkv_w0, rwkv_w2, rwkv_a0, rwkv_a2, rwkv_g2, rwkv_k_k, rwkv_k_a, rwkv_r_k,
           rwkv_lnx_w, rwkv_lnx_b, fox_b_f, od_w_out,
           ln_mix_w, ln_mix_b, xa_w_q, xa_w_kv, xa_w_o, ln_xa_w, ln_xa_b,
           ffn_w_up, ffn_conv_w, ffn_w_down, ln_ffn_w, ln_ffn_b):
    b, s, d = x.shape
    depth = ln_mix_w.shape[0]
    mem_b = _bf(mem)
    x32 = x
    xb = _bf(x)
    for i in range(depth):
        j = i // 2
        x3 = x32.reshape(b, s, d)
        xb3 = xb.reshape(b, s, d)
        if i % 2 == 0:
            x32, xb = _even_mixer(x3, xb3, ev_w_in[j], pool_w[j], pool_scale[j], gdn_conv_w[j], gdn_a_log[j],
                                  gdn_dt_bias[j], gdn_norm_w[j], ev_w_out[j], ln_mix_w[i], ln_mix_b[i])
        else:
            x32, xb = _odd_mixer(x3, xb3, od_w_in[j], rwkv_mu[j], rwkv_w0[j], rwkv_w2[j], rwkv_a0[j], rwkv_a2[j],
                                 rwkv_g2[j], rwkv_k_k[j], rwkv_k_a[j], rwkv_r_k[j].reshape(-1), rwkv_lnx_w[j],
                                 rwkv_lnx_b[j], fox_b_f[j], od_w_out[j], ln_mix_w[i], ln_mix_b[i])
        x32, xb = _cross_attention(x32, xb, mem_b, xa_w_q[i], xa_w_kv[i], xa_w_o[i], ln_xa_w[i], ln_xa_b[i])
        x32, xb = _conv_glu_ffn(x32, xb, b, ffn_w_up[i], ffn_conv_w[i], ffn_w_down[i], ln_ffn_w[i], ln_ffn_b[i])
    return x32.reshape(b, s, d)
```

```python
import functools
import math

import numpy as np
import jax
import jax.numpy as jnp
from jax import lax
from jax.experimental import pallas as pl
from jax.experimental.pallas import tpu as pltpu

F32 = jnp.float32
BF16 = jnp.bfloat16

D_MODEL = 2048
HALF = D_MODEL // 2
POOL_WINDOWS = (2, 4, 8, 16)
POOL_GDIM = HALF // len(POOL_WINDOWS)
GDN_HEADS = 8
GDN_DK = HALF // GDN_HEADS
GDN_CONV = 4
NORM_EPS = 1e-6
RWKV_HEAD = 64
RWKV_LNX_EPS = 64e-5
RWKV_LORA = (64, 64, 160)
RWKV_LORA_PAD = (128, 128, 256)
FOX_HEADS = 8
FOX_HD = HALF // FOX_HEADS
XA_HEADS = 4
XA_HD = D_MODEL // XA_HEADS
D_FF = 5632
FFN_CONV = 3
DEPTH = 2
DEEPNORM_ALPHA = float((2 * DEPTH) ** 0.25)
LN_EPS = 1e-5

CHUNK = 64
HIST = 8
VMEM_LIMIT_BYTES = 56 * 1024 * 1024


def _cparams(*sem):
    return pltpu.CompilerParams(dimension_semantics=sem, vmem_limit_bytes=VMEM_LIMIT_BYTES)


def _dot(a, b):
    return jnp.dot(a, b, preferred_element_type=F32)


def _dot_nt(a, b):
    return lax.dot_general(a, b, (((1,), (1,)), ((), ())), preferred_element_type=F32)


def _dot_tn(a, b):
    return lax.dot_general(a, b, (((0,), (0,)), ((), ())), preferred_element_type=F32)


def _bf(x):
    return x.astype(BF16)


def _split3(x):
    x1 = x.astype(BF16)
    r1 = x - x1.astype(F32)
    x2 = r1.astype(BF16)
    x3 = (r1 - x2.astype(F32)).astype(BF16)
    return x1, x2, x3


def _dot01(m01, x):
    x1, x2, x3 = _split3(x)
    return _dot(m01, x1) + _dot(m01, x2) + _dot(m01, x3)


def _unit_lower_inverses(n_mats, eye, fillers=()):
    c = eye.shape[0]
    levels = int(math.log2(c))
    fillers = list(fillers)
    xs = [eye + n for n in n_mats]
    nbs = [_bf(n) for n in n_mats]
    ps = [_dot(nb, nb) for nb in nbs]
    for level in range(1, levels):
        pbs = [_bf(p) for p in ps]
        if level == levels - 1:
            prods = [_dot(_bf(x), pb) for x, pb in zip(xs, pbs)]
        else:
            prods = [_dot(jnp.concatenate([pb, _bf(x)], axis=0), pb) for x, pb in zip(xs, pbs)]
        if fillers:
            fillers.pop(0)()
        if level == levels - 1:
            xs = [x + pr for x, pr in zip(xs, prods)]
        else:
            xs = [x + pr[c:] for x, pr in zip(xs, prods)]
            ps = [pr[:c] for pr in prods]
    for filler in fillers:
        filler()
    return xs


def _sigmoid(x):
    return 1.0 / (1.0 + jnp.exp(-x))


def _softplus(x):
    return jnp.maximum(x, 0.0) + jnp.log(1.0 + jnp.exp(-jnp.abs(x)))


def _tri_masks(c):
    row = lax.broadcasted_iota(jnp.int32, (c, c), 0)
    col = lax.broadcasted_iota(jnp.int32, (c, c), 1)
    return row >= col, row > col


def _stage_rows(x_ref, sc_ref, ts, first):
    @pl.when(first)
    def _():
        sc_ref[0:HIST, :] = jnp.zeros((HIST, sc_ref.shape[1]), F32)
    sc_ref[HIST:HIST + ts, :] = x_ref[0].astype(F32)


def _carry_rows(sc_ref, ts):
    sc_ref[0:HIST, :] = sc_ref[ts:ts + HIST, :]


def _mm_kernel(a_ref, w_ref, o_ref):
    o_ref[...] = _dot(a_ref[...], w_ref[...]).astype(o_ref.dtype)


def _mm(a, w, out_dtype, tm=1024, tn=1024):
    m, k = a.shape
    n = w.shape[1]
    tm = min(tm, m)
    tn = min(tn, n)
    return pl.pallas_call(
        _mm_kernel,
        grid=(n // tn, m // tm),
        in_specs=[pl.BlockSpec((tm, k), lambda j, i: (i, 0)),
                  pl.BlockSpec((k, tn), lambda j, i: (0, j))],
        out_specs=pl.BlockSpec((tm, tn), lambda j, i: (i, j)),
        out_shape=jax.ShapeDtypeStruct((m, n), out_dtype),
        compiler_params=_cparams("arbitrary", "arbitrary"),
        name="mm",
    )(a, w)


def _mm_res_ln_kernel(*refs, n_in, tk_steps, tr):
    a_refs = refs[:n_in]
    w_refs = refs[n_in:2 * n_in]
    res_ref, lnw_ref, lnb_ref, o32_ref, o16_ref = refs[2 * n_in:2 * n_in + 5]
    acc_ref = refs[2 * n_in + 5] if tk_steps > 1 else None
    kk = pl.program_id(1)
    tm = res_ref.shape[0]

    def partial_sum(rows):
        tot = None
        for a_ref, w_ref in zip(a_refs, w_refs):
            d = _dot(a_ref[rows, :], w_ref[...])
            tot = d if tot is None else tot + d
        return tot

    if tk_steps > 1:
        @pl.when(kk == 0)
        def _():
            acc_ref[...] = partial_sum(slice(None))

        @pl.when((kk > 0) & (kk < tk_steps - 1))
        def _():
            acc_ref[...] += partial_sum(slice(None))

    @pl.when(kk == tk_steps - 1)
    def _():
        for r0 in range(0, tm, tr):
            rows = slice(r0, r0 + tr)
            y = DEEPNORM_ALPHA * res_ref[rows, :] + partial_sum(rows)
            if tk_steps > 1:
                y = y + acc_ref[rows, :]
            mu = jnp.mean(y, axis=-1, keepdims=True)
            d = y - mu
            var = jnp.mean(d * d, axis=-1, keepdims=True)
            out = d * lax.rsqrt(var + LN_EPS) * lnw_ref[...] + lnb_ref[...]
            o32_ref[rows, :] = out
            o16_ref[rows, :] = out.astype(BF16)


def _mm_res_ln(a_list, w_list, res, lnw, lnb, tm=512, tk=None, tr=128):
    m = res.shape[0]
    n = res.shape[1]
    k = a_list[0].shape[1]
    tm = min(tm, m)
    tk = k if tk is None else min(tk, k)
    tr = min(tr, tm)
    n_in = len(a_list)
    steps = k // tk
    in_specs = ([pl.BlockSpec((tm, tk), lambda i, kk: (i, kk)) for _ in a_list]
                + [pl.BlockSpec((tk, n), lambda i, kk: (kk, 0)) for _ in w_list]
                + [pl.BlockSpec((tm, n), lambda i, kk: (i, 0)),
                   pl.BlockSpec((1, n), lambda i, kk: (0, 0)),
                   pl.BlockSpec((1, n), lambda i, kk: (0, 0))])
    return pl.pallas_call(
        functools.partial(_mm_res_ln_kernel, n_in=n_in, tk_steps=steps, tr=tr),
        grid=(m // tm, steps),
        in_specs=in_specs,
        out_specs=[pl.BlockSpec((tm, n), lambda i, kk: (i, 0)),
                   pl.BlockSpec((tm, n), lambda i, kk: (i, 0))],
        out_shape=[jax.ShapeDtypeStruct((m, n), F32), jax.ShapeDtypeStruct((m, n), BF16)],
        scratch_shapes=[pltpu.VMEM((tm, n), F32)] if steps > 1 else [],
        compiler_params=_cparams("arbitrary", "arbitrary"),
        name="mm_res_ln",
    )(*a_list, *w_list, res, lnw.reshape(1, n), lnb.reshape(1, n))


def _xattn_kernel(q_ref, k_ref, v_ref, o_ref, *, tr):
    kb = k_ref[0]
    vb = v_ref[0]
    for r0 in range(0, q_ref.shape[1], tr):
        s = _dot_nt(q_ref[0, r0:r0 + tr, :], kb) * (XA_HD ** -0.5)
        m = jnp.max(s, axis=-1, keepdims=True)
        p = jnp.exp(s - m)
        l = jnp.sum(p, axis=-1, keepdims=True)
        o = _dot(_bf(p / l), vb)
        o_ref[0, r0:r0 + tr, :] = o.astype(o_ref.dtype)


def _xattn(q, kv, ts=2048):
    b, s, d = q.shape
    mlen = kv.shape[1]
    ts = min(ts, s)
    return pl.pallas_call(
        functools.partial(_xattn_kernel, tr=min(256, ts)),
        grid=(b, s // ts, XA_HEADS),
        in_specs=[pl.BlockSpec((1, ts, XA_HD), lambda bi, i, h: (bi, i, h)),
                  pl.BlockSpec((1, mlen, XA_HD), lambda bi, i, h: (bi, 0, h)),
                  pl.BlockSpec((1, mlen, XA_HD), lambda bi, i, h: (bi, 0, XA_HEADS + h))],
        out_specs=pl.BlockSpec((1, ts, XA_HD), lambda bi, i, h: (bi, i, h)),
        out_shape=jax.ShapeDtypeStruct((b, s, d), BF16),
        compiler_params=_cparams("arbitrary", "arbitrary", "arbitrary"),
        name="xattn",
    )(q, kv, kv)


def _mm_taps_kernel(x_ref, w_ref, t_ref, o_ref, sc_ref, *, ts, ntap, act):
    first = pl.program_id(2) == 0

    @pl.when(first)
    def _():
        sc_ref[0:HIST, :] = jnp.zeros((HIST, sc_ref.shape[1]), F32)

    sc_ref[HIST:HIST + ts, :] = _dot(x_ref[0], w_ref[...])
    y = t_ref[ntap - 1:ntap, :] * sc_ref[HIST:HIST + ts, :]
    for back in range(1, ntap):
        tap = ntap - 1 - back
        y = y + t_ref[tap:tap + 1, :] * sc_ref[HIST - back:HIST - back + ts, :]
    if act == "silu":
        y = y * _sigmoid(y)
    elif act == "lora":
        p0, p1, _ = RWKV_LORA_PAD
        y = jnp.concatenate([jnp.tanh(y[:, :p0]), y[:, p0:p0 + p1], _sigmoid(y[:, p0 + p1:])], axis=1)
    o_ref[0] = y.astype(o_ref.dtype)
    _carry_rows(sc_ref, ts)


def _mm_taps(xb, w, taps, out_dtype, act=None, ts=1024, tn=1024):
    b, s, d = xb.shape
    n = w.shape[1]
    ts = min(ts, s)
    tn = min(tn, n)
    ntap = taps.shape[0]
    return pl.pallas_call(
        functools.partial(_mm_taps_kernel, ts=ts, ntap=ntap, act=act),
        grid=(n // tn, b, s // ts),
        in_specs=[pl.BlockSpec((1, ts, d), lambda j, bi, i: (bi, i, 0)),
                  pl.BlockSpec((d, tn), lambda j, bi, i: (0, j)),
                  pl.BlockSpec((ntap, tn), lambda j, bi, i: (0, j))],
        out_specs=pl.BlockSpec((1, ts, tn), lambda j, bi, i: (bi, i, j)),
        out_shape=jax.ShapeDtypeStruct((b, s, n), out_dtype),
        scratch_shapes=[pltpu.VMEM((ts + HIST, tn), F32)],
        compiler_params=_cparams("arbitrary", "arbitrary", "arbitrary"),
        name="mm_taps",
    )(xb, w, taps.astype(F32))


def _ffn_up_kernel(x_ref, wg_ref, wu_ref, cg_ref, cu_ref, o_ref, sg_ref, su_ref, *, ts, tr):
    first = pl.program_id(2) == 0

    @pl.when(first)
    def _():
        sg_ref[0:HIST, :] = jnp.zeros((HIST, sg_ref.shape[1]), F32)
        su_ref[0:HIST, :] = jnp.zeros((HIST, su_ref.shape[1]), F32)

    def conv(sc_ref, c_ref, r0):
        y = c_ref[FFN_CONV - 1:FFN_CONV, :] * sc_ref[HIST + r0:HIST + r0 + tr, :]
        for back in range(1, FFN_CONV):
            tap = FFN_CONV - 1 - back
            y = y + c_ref[tap:tap + 1, :] * sc_ref[HIST + r0 - back:HIST + r0 - back + tr, :]
        return y

    for r0 in range(0, ts, tr):
        x = x_ref[0, r0:r0 + tr, :]
        sg_ref[HIST + r0:HIST + r0 + tr, :] = _dot(x, wg_ref[...])
        su_ref[HIST + r0:HIST + r0 + tr, :] = _dot(x, wu_ref[...])
        g = conv(sg_ref, cg_ref, r0)
        u = conv(su_ref, cu_ref, r0)
        o_ref[0, r0:r0 + tr, :] = (g * _sigmoid(g) * u).astype(o_ref.dtype)
    _carry_rows(sg_ref, ts)
    _carry_rows(su_ref, ts)


def _ffn_up(xb, w_up, conv_w, ts=1024, tn=512, tr=1024):
    b, s, d = xb.shape
    ts = min(ts, s)
    nj = D_FF // tn
    return pl.pallas_call(
        functools.partial(_ffn_up_kernel, ts=ts, tr=min(tr, ts)),
        grid=(nj, b, s // ts),
        in_specs=[pl.BlockSpec((1, ts, d), lambda j, bi, i: (bi, i, 0)),
                  pl.BlockSpec((d, tn), lambda j, bi, i: (0, j)),
                  pl.BlockSpec((d, tn), lambda j, bi, i: (0, nj + j)),
                  pl.BlockSpec((FFN_CONV, tn), lambda j, bi, i: (0, j)),
                  pl.BlockSpec((FFN_CONV, tn), lambda j, bi, i: (0, nj + j))],
        out_specs=pl.BlockSpec((1, ts, tn), lambda j, bi, i: (bi, i, j)),
        out_shape=jax.ShapeDtypeStruct((b, s, D_FF), BF16),
        scratch_shapes=[pltpu.VMEM((ts + HIST, tn), F32), pltpu.VMEM((ts + HIST, tn), F32)],
        compiler_params=_cparams("arbitrary", "arbitrary", "arbitrary"),
        name="ffn_up",
    )(xb, w_up, w_up, conv_w, conv_w)


def _pool_kernel(u_ref, w_ref, sc_ref, o_ref, st_ref, *, ts):
    i = pl.program_id(1)

    @pl.when(i == 0)
    def _():
        st_ref[0:2 * HIST, :] = jnp.zeros((2 * HIST, st_ref.shape[1]), F32)

    st_ref[2 * HIST:2 * HIST + ts, :] = u_ref[0].astype(F32)
    pos = (i * ts + 1 + lax.broadcasted_iota(jnp.int32, (ts, 1), 0)).astype(F32)
    base = 2 * HIST
    for g, win in enumerate(POOL_WINDOWS):
        cols = slice(g * POOL_GDIM, (g + 1) * POOL_GDIM)
        cur = st_ref[base:base + ts, cols]
        acc = cur
        for back in range(1, win):
            acc = acc + st_ref[base - back:base - back + ts, cols]
        mean = acc / jnp.minimum(pos, float(win))
        y = _dot(_bf(mean - cur), w_ref[g])
        o_ref[0, :, cols] = (y * sc_ref[:, cols]).astype(o_ref.dtype)
    st_ref[0:2 * HIST, :] = st_ref[ts:ts + 2 * HIST, :]


def _pool_mixer(h_main, pool_w, pool_scale, ts=512):
    b, s, _ = h_main.shape
    ts = min(ts, s)
    return pl.pallas_call(
        functools.partial(_pool_kernel, ts=ts),
        grid=(b, s // ts),
        in_specs=[pl.BlockSpec((1, ts, HALF), lambda bi, i: (bi, i, 0)),
                  pl.BlockSpec((len(POOL_WINDOWS), POOL_GDIM, POOL_GDIM), lambda bi, i: (0, 0, 0)),
                  pl.BlockSpec((1, HALF), lambda bi, i: (0, 0))],
        out_specs=pl.BlockSpec((1, ts, HALF), lambda bi, i: (bi, i, 0)),
        out_shape=jax.ShapeDtypeStruct((b, s, HALF), BF16),
        scratch_shapes=[pltpu.VMEM((ts + 2 * HIST, HALF), F32)],
        compiler_params=_cparams("arbitrary", "arbitrary"),
        name="pool_mixer",
    )(h_main, pool_w, pool_scale.reshape(1, HALF))


def _gdn_kernel(q_ref, k_ref, v_ref, z_ref, g_ref, alog_ref, dtb_ref, nw_ref, o_ref, st_ref, *, ts, hps):
    c = CHUNK
    h0 = pl.program_id(1)

    @pl.when(pl.program_id(2) == 0)
    def _():
        st_ref[...] = jnp.zeros_like(st_ref)

    q_all = q_ref[0]
    k_all = k_ref[0]
    v_all = v_ref[0]
    z_all = z_ref[0].astype(F32)

    logits = g_ref[0]
    lane = lax.broadcasted_iota(jnp.int32, logits.shape, 1)
    g_all = -jnp.exp(alog_ref[...]) * _softplus(logits + dtb_ref[...])
    sig_all = _sigmoid(logits)

    causal, strict = _tri_masks(c)
    tril = causal.astype(BF16)
    eye = jnp.where(causal & (~strict), 1.0, 0.0).astype(F32)
    lane_c = lax.broadcasted_iota(jnp.int32, (c, GDN_DK), 1)
    nw = nw_ref[...]
    chunk_rows = [slice(ci * c, (ci + 1) * c) for ci in range(ts // c)]
    nck = len(chunk_rows)

    q, k, v, betas, g_cols = [], [], [], [], []
    for hd in range(hps):
        hl = slice(hd * GDN_DK, (hd + 1) * GDN_DK)
        qh, kh = q_all[:, hl], k_all[:, hl]
        qh = qh * lax.rsqrt(jnp.sum(qh * qh, axis=-1, keepdims=True) + NORM_EPS) * (GDN_DK ** -0.5)
        kh = kh * lax.rsqrt(jnp.sum(kh * kh, axis=-1, keepdims=True) + NORM_EPS)
        head = h0 * hps + hd
        g_col = jnp.sum(jnp.where(lane == head, g_all, 0.0), axis=1, keepdims=True)
        b_col = jnp.sum(jnp.where(lane == head + GDN_HEADS, sig_all, 0.0), axis=1, keepdims=True)
        for r in chunk_rows:
            q.append(qh[r])
            k.append(kh[r])
            v.append(v_all[r, hl])
            betas.append(jnp.broadcast_to(b_col[r], (c, GDN_DK)))
            g_cols.append(jnp.broadcast_to(g_col[r], (c, GDN_DK)))
    nprob = len(q)
    rows = list(range(nprob))
    gcs = [_dot01(tril, gcol) for gcol in g_cols]
    diffs = []
    for gc in gcs:
        g1, g2, g3 = (p.astype(F32) for p in _split3(gc))
        lhs = jnp.where(lane_c == 0, g1, jnp.where(lane_c == 1, g2, jnp.where(lane_c == 2, g3,
              jnp.where(lane_c < 6, 1.0, 0.0))))
        rhs = jnp.where(lane_c < 3, 1.0, jnp.where(lane_c == 3, -g1, jnp.where(lane_c == 4, -g2,
              jnp.where(lane_c == 5, -g3, 0.0))))
        diffs.append(_dot_nt(_bf(lhs), _bf(rhs)))
    decays = [jnp.where(causal, jnp.exp(jnp.where(causal, d, 0.0)), 0.0) for d in diffs]
    kbs = [k[r] * bt for r, bt in zip(rows, betas)]
    kcbs = [_bf(k[r]) for r in rows]
    lmats = [jnp.where(strict, _dot_nt(_bf(kb), kcb) * dc, 0.0) for kb, kcb, dc in zip(kbs, kcbs, decays)]
    intras = [_bf(jnp.where(causal, _dot_nt(_bf(q[r]), kcb) * dc, 0.0)) for r, kcb, dc in zip(rows, kcbs, decays)]
    tmats = [_bf(t) for t in _unit_lower_inverses([-l for l in lmats], eye)]
    egs = [jnp.exp(gc) for gc in gcs]
    g_lasts = [gc[c - 1:c, :] for gc in gcs]
    uws = [_dot(t, jnp.concatenate([_bf(v[r] * bt), _bf(kb * eg)], axis=1))
           for t, r, bt, kb, eg in zip(tmats, rows, betas, kbs, egs)]
    us = [uw[:, :GDN_DK] for uw in uws]
    ws = [_bf(uw[:, GDN_DK:]) for uw in uws]
    q_decs = [_bf(q[r] * eg) for r, eg in zip(rows, egs)]
    k_decs = [_bf(k[r] * jnp.exp(gl - gc)) for r, gl, gc in zip(rows, g_lasts, gcs)]
    e_lasts = [jnp.exp(gl) for gl in g_lasts]
    mns = [_dot_tn(kd, jnp.concatenate([w, _bf(u)], axis=1)) for kd, w, u in zip(k_decs, ws, us)]
    mmats = [_bf(mn[:, :GDN_DK]) for mn in mns]
    nmats = [mn[:, GDN_DK:] for mn in mns]

    states = [st_ref[hd] for hd in range(hps)]
    sb_hist = [[None] * nck for _ in range(hps)]
    for ci in range(nck):
        for hd in range(hps):
            p = hd * nck + ci
            sb = _bf(states[hd])
            sb_hist[hd][ci] = sb
            states[hd] = states[hd] * e_lasts[p] - _dot(mmats[p], sb) + nmats[p]
    for hd in range(hps):
        st_ref[hd] = states[hd]
    for hd in range(hps):
        hl = slice(hd * GDN_DK, (hd + 1) * GDN_DK)
        for ci in range(nck):
            p = hd * nck + ci
            sb = sb_hist[hd][ci]
            v_new = _bf(us[p] - _dot(ws[p], sb))
            o = _dot(q_decs[p], sb) + _dot(intras[p], v_new)
            o = o * lax.rsqrt(jnp.mean(o * o, axis=-1, keepdims=True) + NORM_EPS) * nw
            zc = z_all[chunk_rows[ci], hl]
            o_ref[0, chunk_rows[ci], hl] = (o * (zc * _sigmoid(zc))).astype(o_ref.dtype)


def _gated_deltanet(qkv, z, logits, a_log, dt_bias, norm_w, ts=256, hps=8):
    b, s, _ = qkv.shape
    ts = min(ts, s)
    hd = GDN_DK
    gw = hps * hd
    ng = HALF // gw
    pad = lambda t: jnp.pad(t.astype(F32), (0, hd - t.shape[0])).reshape(1, hd)
    col = lambda grp: (lambda bi, h, i: (bi, i, grp * ng + h))
    return pl.pallas_call(
        functools.partial(_gdn_kernel, ts=ts, hps=hps),
        grid=(b, ng, s // ts),
        in_specs=[pl.BlockSpec((1, ts, gw), col(0)), pl.BlockSpec((1, ts, gw), col(1)),
                  pl.BlockSpec((1, ts, gw), col(2)), pl.BlockSpec((1, ts, gw), col(0)),
                  pl.BlockSpec((1, ts, hd), lambda bi, h, i: (bi, i, 0)),
                  pl.BlockSpec((1, hd), lambda bi, h, i: (0, 0)),
                  pl.BlockSpec((1, hd), lambda bi, h, i: (0, 0)),
                  pl.BlockSpec((1, hd), lambda bi, h, i: (0, 0))],
        out_specs=pl.BlockSpec((1, ts, gw), lambda bi, h, i: (bi, i, h)),
        out_shape=jax.ShapeDtypeStruct((b, s, HALF), BF16),
        scratch_shapes=[pltpu.VMEM((hps, hd, hd), F32)],
        compiler_params=_cparams("arbitrary", "arbitrary", "arbitrary"),
        name="gated_deltanet",
    )(qkv, qkv, qkv, z, logits, pad(a_log), pad(dt_bias), norm_w.astype(F32).reshape(1, hd))


def _rwkv_kernel(r_ref, k_ref, v_ref, l_ref, w2_ref, a2_ref, g2_ref,
                 w0_ref, a0_ref, kk_ref, ka_ref, rk_ref, lnw_ref, lnb_ref, o_ref, st_ref, *, ts, hps):
    c = CHUNK
    hs = RWKV_HEAD

    @pl.when(pl.program_id(2) == 0)
    def _():
        st_ref[...] = jnp.zeros_like(st_ref)

    r = r_ref[0]
    k = k_ref[0]
    v = v_ref[0]
    lo = l_ref[0]
    p0, p1, p2 = RWKV_LORA_PAD
    w_log = -_softplus(-(w0_ref[...] + _dot(lo[:, :p0], w2_ref[...]))) - 0.5
    lw = -jnp.exp(w_log)
    a = _sigmoid(a0_ref[...] + _dot(lo[:, p0:p0 + p1], a2_ref[...]))
    g = _dot(lo[:, p0 + p1:p0 + p1 + p2], g2_ref[...])
    kk_raw = k * kk_ref[...]
    km = k * (1.0 + (a - 1.0) * ka_ref[...])
    head_of_lane = lax.broadcasted_iota(jnp.int32, (ts, hps * hs), 1) // hs

    def per_head_sum(x):
        out = jnp.zeros_like(x)
        for hd in range(hps):
            sel = head_of_lane == hd
            out = jnp.where(sel, jnp.sum(jnp.where(sel, x, 0.0), axis=-1, keepdims=True), out)
        return out

    kk = kk_raw * lax.rsqrt(per_head_sum(kk_raw * kk_raw) + NORM_EPS)
    aa = -kk
    bb = kk * a
    bonus = per_head_sum(r * km * rk_ref[...]) * v

    causal, strict = _tri_masks(c)
    tril = causal.astype(BF16)
    eye = jnp.where(causal & (~strict), 1.0, 0.0).astype(F32)
    lnw = lnw_ref[...]
    lnb = lnb_ref[...]

    rows = [slice(ci * c, (ci + 1) * c) for ci in range(ts // c)]
    nck = len(rows)
    heads = [slice(hd * hs, (hd + 1) * hs) for hd in range(hps)]
    cums = [_dot01(tril, lw[rw]) for rw in rows]
    c_lasts = [cum[c - 1:c, :] for cum in cums]
    w_invs = [jnp.exp(-cum) for cum in cums]
    w_tails = [jnp.exp(cl - cum) for cl, cum in zip(c_lasts, cums)]
    a_ts = [_bf(aa[rw] * jnp.exp(cum - lw[rw])) for rw, cum in zip(rows, cums)]
    r_ts = [_bf(r[rw] * jnp.exp(cum)) for rw, cum in zip(rows, cums)]
    b_ts = [_bf(bb[rw] * wi) for rw, wi in zip(rows, w_invs)]
    k_ts = [_bf(km[rw] * wi) for rw, wi in zip(rows, w_invs)]
    b_hs = [_bf(bb[rw] * wt) for rw, wt in zip(rows, w_tails)]
    k_hs = [_bf(km[rw] * wt) for rw, wt in zip(rows, w_tails)]
    vcs = [_bf(v[rw]) for rw in rows]
    e_lasts = [jnp.exp(cl) for cl in c_lasts]
    prob = [(ci, cs) for ci in range(nck) for cs in heads]
    row2 = lax.broadcasted_iota(jnp.int32, (c, 2 * c), 0)
    col2 = lax.broadcasted_iota(jnp.int32, (c, 2 * c), 1) % c
    bks = [jnp.concatenate([b_ts[ci][:, cs], k_ts[ci][:, cs]], axis=0) for ci, cs in prob]
    a_abks = [jnp.where(row2 > col2, _dot_nt(a_ts[ci][:, cs], bk), 0.0) for bk, (ci, cs) in zip(bks, prob)]
    a_rbks = [_bf(jnp.where(row2 >= col2, _dot_nt(r_ts[ci][:, cs], bk), 0.0)) for bk, (ci, cs) in zip(bks, prob)]
    tmats = [_bf(t) for t in _unit_lower_inverses([m[:, :c] for m in a_abks], eye)]
    zero_c = jnp.zeros((c, hs), BF16)
    avs = [_bf(_dot(_bf(abk), jnp.concatenate([zero_c, vcs[ci][:, cs]], axis=0)))
           for abk, (ci, cs) in zip(a_abks, prob)]
    pmats = [_bf(_dot(t, a_ts[ci][:, cs])) for t, (ci, cs) in zip(tmats, prob)]
    qmats = [_dot(t, av) for t, av in zip(tmats, avs)]
    mmats = [_bf(_dot_tn(pm, b_hs[ci][:, cs])) for pm, (ci, cs) in zip(pmats, prob)]
    nmats = [_dot_tn(jnp.concatenate([_bf(qm), vcs[ci][:, cs]], axis=0),
                     jnp.concatenate([b_hs[ci][:, cs], k_hs[ci][:, cs]], axis=0))
             for qm, (ci, cs) in zip(qmats, prob)]

    states = [st_ref[hd] for hd in range(hps)]
    sb_hist = []
    for ci in range(nck):
        sbs = [_bf(s) for s in states]
        sb_hist.append(sbs)
        states = [states[hd] * e_lasts[ci][:, heads[hd]] + _dot(sbs[hd], mmats[hps * ci + hd]) + nmats[hps * ci + hd]
                  for hd in range(hps)]
    for hd in range(hps):
        st_ref[hd] = states[hd]
    ys = []
    for ci in range(nck):
        sbs = sb_hist[ci]
        ubs = [_bf(_dot_nt(pmats[hps * ci + hd], sbs[hd]) + qmats[hps * ci + hd]) for hd in range(hps)]
        ys.append([_dot_nt(r_ts[ci][:, heads[hd]], sbs[hd])
                   + _dot(a_rbks[hps * ci + hd], jnp.concatenate([ubs[hd], vcs[ci][:, heads[hd]]], axis=0))
                   for hd in range(hps)])
    for ci in range(nck):
        outs = []
        for y in ys[ci]:
            ym = jnp.mean(y, axis=-1, keepdims=True)
            yd = y - ym
            yv = jnp.mean(yd * yd, axis=-1, keepdims=True)
            outs.append(yd * lax.rsqrt(yv + RWKV_LNX_EPS))
        yn = jnp.concatenate(outs, axis=-1) * lnw + lnb
        o_ref[0, rows[ci], :] = ((yn + bonus[rows[ci]]) * g[rows[ci]]).astype(o_ref.dtype)


def _rwkv7(hr, hl, w2p, a2p, g2p, w0, a0, k_k, k_a, r_k, lnx_w, lnx_b, ts=128, hps=16):
    b, s, _ = hr.shape
    ts = min(ts, s)
    pw = hps * RWKV_HEAD
    npair = HALF // pw
    lw = sum(RWKV_LORA_PAD)
    col = lambda grp: (lambda bi, hp, i: (bi, i, grp * npair + hp))
    vec = lambda grp: pl.BlockSpec((1, pw), lambda bi, hp, i: (0, grp * npair + hp))
    row1 = lambda t: t.astype(F32).reshape(1, -1)
    return pl.pallas_call(
        functools.partial(_rwkv_kernel, ts=ts, hps=hps),
        grid=(b, npair, s // ts),
        in_specs=[pl.BlockSpec((1, ts, pw), col(0)), pl.BlockSpec((1, ts, pw), col(1)),
                  pl.BlockSpec((1, ts, pw), col(2)),
                  pl.BlockSpec((1, ts, lw), lambda bi, hp, i: (bi, i, 0)),
                  pl.BlockSpec((RWKV_LORA_PAD[0], pw), lambda bi, hp, i: (0, hp)),
                  pl.BlockSpec((RWKV_LORA_PAD[1], pw), lambda bi, hp, i: (0, hp)),
                  pl.BlockSpec((RWKV_LORA_PAD[2], pw), lambda bi, hp, i: (0, hp)),
                  vec(0), vec(0), vec(0), vec(0), vec(0), vec(0), vec(0)],
        out_specs=pl.BlockSpec((1, ts, pw), lambda bi, hp, i: (bi, i, hp)),
        out_shape=jax.ShapeDtypeStruct((b, s, HALF), BF16),
        scratch_shapes=[pltpu.VMEM((hps, RWKV_HEAD, RWKV_HEAD), F32)],
        compiler_params=_cparams("arbitrary", "arbitrary", "arbitrary"),
        name="rwkv7",
    )(hr, hr, hr, hl, w2p, a2p, g2p,
      row1(w0), row1(a0), row1(k_k), row1(k_a), row1(r_k), row1(lnx_w), row1(lnx_b))


def _fox_cumsum_kernel(f_ref, bf_ref, o_ref):
    x = f_ref[0] + bf_ref[...]
    ls = jnp.minimum(x, 0.0) - jnp.log(1.0 + jnp.exp(-jnp.abs(x)))
    nh, nr, nl = ls.shape
    li = lax.broadcasted_iota(jnp.int32, (nl, nl), 0)
    lj = lax.broadcasted_iota(jnp.int32, (nl, nl), 1)
    upper = (li <= lj).astype(BF16)
    ri = lax.broadcasted_iota(jnp.int32, (nr, nr), 0)
    rj = lax.broadcasted_iota(jnp.int32, (nr, nr), 1)
    below = (ri > rj).astype(BF16)
    for h in range(nh):
        within = _dot01_right(ls[h], upper)
        tot = jnp.broadcast_to(within[:, nl - 1:nl], (nr, nl))
        o_ref[0, h] = within + _dot01(below, tot)


def _dot01_right(x, m01):
    x1, x2, x3 = _split3(x)
    return _dot(x1, m01) + _dot(x2, m01) + _dot(x3, m01)


def _fox_cumsum(f_t, b_f):
    b, nh, s = f_t.shape
    nl = 128
    nr = s // nl
    out = pl.pallas_call(
        _fox_cumsum_kernel,
        grid=(b,),
        in_specs=[pl.BlockSpec((1, nh, nr, nl), lambda bi: (bi, 0, 0, 0)),
                  pl.BlockSpec((nh, 1, 1), lambda bi: (0, 0, 0))],
        out_specs=pl.BlockSpec((1, nh, nr, nl), lambda bi: (bi, 0, 0, 0)),
        out_shape=jax.ShapeDtypeStruct((b, nh, nr, nl), F32),
        compiler_params=_cparams("arbitrary"),
        name="fox_cumsum",
    )(f_t.reshape(b, nh, nr, nl), b_f.astype(F32).reshape(nh, 1, 1))
    return out.reshape(b, nh, 1, s)


def _fox_kernel(qi_ref, ki_ref, q_ref, k_ref, v_ref, cq_ref, ck_ref, o_ref, qs_ref, m_ref, acc_ref, *, tq, tk, tr):
    p = pl.program_id(2)
    qi = qi_ref[p]
    ki = ki_ref[p]
    last_k = ((qi + 1) * tq - 1) // tk
    log2e = 1.0 / math.log(2.0)

    @pl.when(ki == 0)
    def _():
        qs_ref[...] = (q_ref[0].astype(F32) * (FOX_HD ** -0.5 * log2e)).astype(BF16)
        m_ref[...] = jnp.full_like(m_ref, -jnp.inf)
        acc_ref[...] = jnp.zeros_like(acc_ref)

    def step(masked):
        bias = (cq_ref[0, 0, :, 0:1] - ck_ref[0, 0]) * log2e
        v_aug = jnp.concatenate([v_ref[0], jnp.ones((tk, FOX_HD), BF16)], axis=1)
        kb = k_ref[0]
        for r0 in range(0, tq, tr):
            rows = slice(r0, r0 + tr)
            s = _dot_nt(qs_ref[rows, :], kb) + bias
            if masked:
                qpos = qi * tq + r0 + lax.broadcasted_iota(jnp.int32, (tr, tk), 0)
                kpos = ki * tk + lax.broadcasted_iota(jnp.int32, (tr, tk), 1)
                s = jnp.where(kpos <= qpos, s, -jnp.inf)
            m_old = m_ref[rows, :]
            m_new = jnp.maximum(m_old, jnp.max(s, axis=-1, keepdims=True))
            corr = jnp.exp2(m_old - m_new)
            pexp = _bf(jnp.exp2(s - m_new))
            acc_ref[rows, :] = corr * acc_ref[rows, :] + _dot(pexp, v_aug)
            m_ref[rows, :] = m_new

    needs_mask = (ki + 1) * tk - 1 > qi * tq

    @pl.when(needs_mask)
    def _():
        step(True)

    @pl.when(jnp.logical_not(needs_mask))
    def _():
        step(False)

    @pl.when(ki == last_k)
    def _():
        acc = acc_ref[...]
        o_ref[0] = (acc[:, :FOX_HD] / acc[:, FOX_HD:FOX_HD + 1]).astype(o_ref.dtype)


def _fox_attention(hf, c, tq=2048, tk=1024, tr=128):
    b, s, _ = hf.shape
    tq = min(tq, s)
    tk = min(tk, s)
    nh = FOX_HEADS
    pairs = [(qi, ki) for qi in range(s // tq) for ki in range(((qi + 1) * tq - 1) // tk + 1)]
    qi_arr = jnp.asarray(np.array([p[0] for p in pairs], np.int32))
    ki_arr = jnp.asarray(np.array([p[1] for p in pairs], np.int32))
    grid_spec = pltpu.PrefetchScalarGridSpec(
        num_scalar_prefetch=2,
        grid=(b, nh, len(pairs)),
        in_specs=[pl.BlockSpec((1, tq, FOX_HD), lambda bi, h, p, qa, ka: (bi, qa[p], h)),
                  pl.BlockSpec((1, tk, FOX_HD), lambda bi, h, p, qa, ka: (bi, ka[p], nh + h)),
                  pl.BlockSpec((1, tk, FOX_HD), lambda bi, h, p, qa, ka: (bi, ka[p], 2 * nh + h)),
                  pl.BlockSpec((1, 1, 1, tq), lambda bi, h, p, qa, ka: (bi, h, 0, qa[p])),
                  pl.BlockSpec((1, 1, 1, tk), lambda bi, h, p, qa, ka: (bi, h, 0, ka[p]))],
        out_specs=pl.BlockSpec((1, tq, FOX_HD), lambda bi, h, p, qa, ka: (bi, qa[p], h)),
        scratch_shapes=[pltpu.VMEM((tq, FOX_HD), BF16), pltpu.VMEM((tq, 1), F32),
                        pltpu.VMEM((tq, 2 * FOX_HD), F32)])
    return pl.pallas_call(
        functools.partial(_fox_kernel, tq=tq, tk=tk, tr=min(tr, tq)),
        grid_spec=grid_spec,
        out_shape=jax.ShapeDtypeStruct((b, s, HALF), BF16),
        compiler_params=_cparams("arbitrary", "arbitrary", "arbitrary"),
        name="fox_attention",
    )(qi_arr, ki_arr, hf, hf, hf, c, c)


def _pad_cols(w, width):
    return jnp.pad(w, ((0, 0), (0, width - w.shape[1])))


def _pad_rows(w, height):
    return jnp.pad(w, ((0, height - w.shape[0]), (0, 0)))


def _even_mixer(x32, xb, w_in, pool_w, pool_scale, conv_w, a_log, dt_bias, norm_w, w_out, ln_w, ln_b):
    b, s, d = x32.shape
    t = b * s
    n_main = 5 * HALF
    wb = _bf(w_in[:, :n_main])
    w_gate = _bf(_pad_cols(w_in[:, n_main:], 128))
    xb2 = xb.reshape(t, d)
    u_pool = _mm(xb2, wb[:, :HALF], F32).reshape(b, s, HALF)
    qkv = _mm_taps(xb, wb[:, HALF:4 * HALF], conv_w, F32, act="silu")
    z = _mm(xb2, wb[:, 4 * HALF:], BF16).reshape(b, s, HALF)
    logits = _mm(xb2, w_gate, F32, tn=128).reshape(b, s, 128)
    y_a = _pool_mixer(u_pool, _bf(pool_w), pool_scale)
    y_b = _gated_deltanet(qkv, z, logits, a_log, dt_bias, norm_w)
    wo = _bf(w_out)
    return _mm_res_ln([y_a.reshape(t, HALF), y_b.reshape(t, HALF)], [wo[:HALF], wo[HALF:]],
                      x32.reshape(t, d), ln_w, ln_b)


def _odd_mixer(x32, xb, w_in, mu, w0, w2, a0, a2, g2, k_k, k_a, r_k, lnx_w, lnx_b, b_f, w_out, ln_w, ln_b):
    b, s, d = x32.shape
    t = b * s
    l0, l1, l2 = RWKV_LORA
    p0, p1, p2 = RWKV_LORA_PAD
    o_l = 3 * HALF
    o_f = o_l + l0 + l1 + l2
    w_rkv = _bf(w_in[:, :o_l])
    w_l = _bf(jnp.concatenate([_pad_cols(w_in[:, o_l:o_l + l0], p0),
                               _pad_cols(w_in[:, o_l + l0:o_l + l0 + l1], p1),
                               _pad_cols(w_in[:, o_l + l0 + l1:o_f], p2)], axis=1))
    w_fox = _bf(w_in[:, o_f:o_f + 3 * HALF])
    w_fl = _bf(_pad_cols(w_in[:, o_f + 3 * HALF:], 128))
    mu_l = jnp.concatenate([jnp.pad(mu[o_l:o_l + l0], (0, p0 - l0)),
                            jnp.pad(mu[o_l + l0:o_l + l0 + l1], (0, p1 - l1)),
                            jnp.pad(mu[o_l + l0 + l1:o_f], (0, p2 - l2))])
    xb2 = xb.reshape(t, d)
    lerp = lambda m: jnp.stack([m, 1.0 - m])
    hr = _mm_taps(xb, w_rkv, lerp(mu[:o_l]), F32)
    hl = _mm_taps(xb, w_l, lerp(mu_l), BF16, act="lora", tn=p0 + p1 + p2)
    hf = _mm(xb2, w_fox, BF16).reshape(b, s, 3 * HALF)
    fl = _mm(xb2, w_fl, F32, tn=128).reshape(b, s, 128)
    y_c = _rwkv7(hr, hl, _bf(_pad_rows(w2, p0)), _bf(_pad_rows(a2, p1)), _bf(_pad_rows(g2, p2)),
                 w0, a0, k_k, k_a, r_k, lnx_w, lnx_b)
    c = _fox_cumsum(jnp.transpose(fl[:, :, :FOX_HEADS], (0, 2, 1)), b_f)
    y_d = _fox_attention(hf, c)
    wo = _bf(w_out)
    return _mm_res_ln([y_c.reshape(t, HALF), y_d.reshape(t, HALF)], [wo[:HALF], wo[HALF:]],
                      x32.reshape(t, d), ln_w, ln_b)


def _cross_attention(x32, xb, mem_b, w_q, w_kv, w_o, ln_w, ln_b):
    t, d = x32.shape
    b, mlen, _ = mem_b.shape
    s = t // b
    q = _mm(xb, _bf(w_q), BF16).reshape(b, s, d)
    kv = _mm(mem_b.reshape(b * mlen, d), _bf(w_kv), BF16).reshape(b, mlen, 2 * d)
    o = _xattn(q, kv).reshape(t, d)
    return _mm_res_ln([o], [_bf(w_o)], x32, ln_w, ln_b)


def _conv_glu_ffn(x32, xb, b, w_up, conv_w, w_down, ln_w, ln_b):
    t, d = x32.shape
    hmid = _ffn_up(xb.reshape(b, t // b, d), _bf(w_up), conv_w)
    return _mm_res_ln([hmid.reshape(t, D_FF)], [_bf(w_down)], x32, ln_w, ln_b, tk=D_FF // 4)


def kernel(x, mem, ev_w_in, pool_w, pool_scale, gdn_conv_w, gdn_a_log, gdn_dt_bias, gdn_norm_w, ev_w_out,
           od_w_in, rwkv_mu, rwkv_w0, rwkv_w2, rwkv_a0, rwkv_a2, rwkv_g2, rwkv_k_k, rwkv_k_a, rwkv_r_k,
           rwkv_lnx_w, rwkv_lnx_b, fox_b_f, od_w_out,
           ln_mix_w, ln_mix_b, xa_w_q, xa_w_kv, xa_w_o, ln_xa_w, ln_xa_b,
           ffn_w_up, ffn_conv_w, ffn_w_down, ln_ffn_w, ln_ffn_b):
    b, s, d = x.shape
    depth = ln_mix_w.shape[0]
    mem_b = _bf(mem)
    x32 = x
    xb = _bf(x)
    for i in range(depth):
        j = i // 2
        x3 = x32.reshape(b, s, d)
        xb3 = xb.reshape(b, s, d)
        if i % 2 == 0:
            x32, xb = _even_mixer(x3, xb3, ev_w_in[j], pool_w[j], pool_scale[j], gdn_conv_w[j], gdn_a_log[j],
                                  gdn_dt_bias[j], gdn_norm_w[j], ev_w_out[j], ln_mix_w[i], ln_mix_b[i])
        else:
            x32, xb = _odd_mixer(x3, xb3, od_w_in[j], rwkv_mu[j], rwkv_w0[j], rwkv_w2[j], rwkv_a0[j], rwkv_a2[j],
                                 rwkv_g2[j], rwkv_k_k[j], rwkv_k_a[j], rwkv_r_k[j].reshape(-1), rwkv_lnx_w[j],
                                 rwkv_lnx_b[j], fox_b_f[j], od_w_out[j], ln_mix_w[i], ln_mix_b[i])
        x32, xb = _cross_attention(x32, xb, mem_b, xa_w_q[i], xa_w_kv[i], xa_w_o[i], ln_xa_w[i], ln_xa_b[i])
        x32, xb = _conv_glu_ffn(x32, xb, b, ffn_w_up[i], ffn_conv_w[i], ffn_w_down[i], ln_ffn_w[i], ln_ffn_b[i])
    return x32.reshape(b, s, d)
```

```python
import functools
import math

import numpy as np
import jax
import jax.numpy as jnp
from jax import lax
from jax.experimental import pallas as pl
from jax.experimental.pallas import tpu as pltpu

F32 = jnp.float32
BF16 = jnp.bfloat16

D_MODEL = 2048
HALF = D_MODEL // 2
POOL_WINDOWS = (2, 4, 8, 16)
POOL_GDIM = HALF // len(POOL_WINDOWS)
GDN_HEADS = 8
GDN_DK = HALF // GDN_HEADS
GDN_CONV = 4
NORM_EPS = 1e-6
RWKV_HEAD = 64
RWKV_LNX_EPS = 64e-5
RWKV_LORA = (64, 64, 160)
RWKV_LORA_PAD = (128, 128, 256)
FOX_HEADS = 8
FOX_HD = HALF // FOX_HEADS
XA_HEADS = 4
XA_HD = D_MODEL // XA_HEADS
D_FF = 5632
FFN_CONV = 3
DEPTH = 2
DEEPNORM_ALPHA = float((2 * DEPTH) ** 0.25)
LN_EPS = 1e-5

CHUNK = 64
HIST = 8
VMEM_LIMIT_BYTES = 56 * 1024 * 1024


def _cparams(*sem):
    return pltpu.CompilerParams(dimension_semantics=sem, vmem_limit_bytes=VMEM_LIMIT_BYTES)


def _dot(a, b):
    return jnp.dot(a, b, preferred_element_type=F32)


def _dot_nt(a, b):
    return lax.dot_general(a, b, (((1,), (1,)), ((), ())), preferred_element_type=F32)


def _dot_tn(a, b):
    return lax.dot_general(a, b, (((0,), (0,)), ((), ())), preferred_element_type=F32)


def _bf(x):
    return x.astype(BF16)


def _split3(x):
    x1 = x.astype(BF16)
    r1 = x - x1.astype(F32)
    x2 = r1.astype(BF16)
    x3 = (r1 - x2.astype(F32)).astype(BF16)
    return x1, x2, x3


def _dot01(m01, x):
    x1, x2, x3 = _split3(x)
    return _dot(m01, x1) + _dot(m01, x2) + _dot(m01, x3)


def _unit_lower_inverses(n_mats, eye, fillers=()):
    c = eye.shape[0]
    levels = int(math.log2(c))
    fillers = list(fillers)
    xs = [eye + n for n in n_mats]
    nbs = [_bf(n) for n in n_mats]
    ps = [_dot(nb, nb) for nb in nbs]
    for level in range(1, levels):
        pbs = [_bf(p) for p in ps]
        if level == levels - 1:
            prods = [_dot(_bf(x), pb) for x, pb in zip(xs, pbs)]
        else:
            prods = [_dot(jnp.concatenate([pb, _bf(x)], axis=0), pb) for x, pb in zip(xs, pbs)]
        if fillers:
            fillers.pop(0)()
        if level == levels - 1:
            xs = [x + pr for x, pr in zip(xs, prods)]
        else:
            xs = [x + pr[c:] for x, pr in zip(xs, prods)]
            ps = [pr[:c] for pr in prods]
    for filler in fillers:
        filler()
    return xs


def _sigmoid(x):
    return 1.0 / (1.0 + jnp.exp(-x))


def _softplus(x):
    return jnp.maximum(x, 0.0) + jnp.log(1.0 + jnp.exp(-jnp.abs(x)))


def _tri_masks(c):
    row = lax.broadcasted_iota(jnp.int32, (c, c), 0)
    col = lax.broadcasted_iota(jnp.int32, (c, c), 1)
    return row >= col, row > col


def _stage_rows(x_ref, sc_ref, ts, first):
    @pl.when(first)
    def _():
        sc_ref[0:HIST, :] = jnp.zeros((HIST, sc_ref.shape[1]), F32)
    sc_ref[HIST:HIST + ts, :] = x_ref[0].astype(F32)


def _carry_rows(sc_ref, ts):
    sc_ref[0:HIST, :] = sc_ref[ts:ts + HIST, :]


def _mm_kernel(a_ref, w_ref, o_ref):
    o_ref[...] = _dot(a_ref[...], w_ref[...]).astype(o_ref.dtype)


def _mm(a, w, out_dtype, tm=1024, tn=1024):
    m, k = a.shape
    n = w.shape[1]
    tm = min(tm, m)
    tn = min(tn, n)
    return pl.pallas_call(
        _mm_kernel,
        grid=(n // tn, m // tm),
        in_specs=[pl.BlockSpec((tm, k), lambda j, i: (i, 0)),
                  pl.BlockSpec((k, tn), lambda j, i: (0, j))],
        out_specs=pl.BlockSpec((tm, tn), lambda j, i: (i, j)),
        out_shape=jax.ShapeDtypeStruct((m, n), out_dtype),
        compiler_params=_cparams("arbitrary", "arbitrary"),
        name="mm",
    )(a, w)


def _mm_res_ln_kernel(*refs, n_in, tk_steps, tr):
    a_refs = refs[:n_in]
    w_refs = refs[n_in:2 * n_in]
    res_ref, lnw_ref, lnb_ref, o32_ref, o16_ref = refs[2 * n_in:2 * n_in + 5]
    acc_ref = refs[2 * n_in + 5] if tk_steps > 1 else None
    kk = pl.program_id(1)
    tm = res_ref.shape[0]

    def partial_sum(rows):
        tot = None
        for a_ref, w_ref in zip(a_refs, w_refs):
            d = _dot(a_ref[rows, :], w_ref[...])
            tot = d if tot is None else tot + d
        return tot

    if tk_steps > 1:
        @pl.when(kk == 0)
        def _():
            acc_ref[...] = partial_sum(slice(None))

        @pl.when((kk > 0) & (kk < tk_steps - 1))
        def _():
            acc_ref[...] += partial_sum(slice(None))

    @pl.when(kk == tk_steps - 1)
    def _():
        for r0 in range(0, tm, tr):
            rows = slice(r0, r0 + tr)
            y = DEEPNORM_ALPHA * res_ref[rows, :] + partial_sum(rows)
            if tk_steps > 1:
                y = y + acc_ref[rows, :]
            mu = jnp.mean(y, axis=-1, keepdims=True)
            d = y - mu
            var = jnp.mean(d * d, axis=-1, keepdims=True)
            out = d * lax.rsqrt(var + LN_EPS) * lnw_ref[...] + lnb_ref[...]
            o32_ref[rows, :] = out
            o16_ref[rows, :] = out.astype(BF16)


def _mm_res_ln(a_list, w_list, res, lnw, lnb, tm=512, tk=None, tr=128):
    m = res.shape[0]
    n = res.shape[1]
    k = a_list[0].shape[1]
    tm = min(tm, m)
    tk = k if tk is None else min(tk, k)
    tr = min(tr, tm)
    n_in = len(a_list)
    steps = k // tk
    in_specs = ([pl.BlockSpec((tm, tk), lambda i, kk: (i, kk)) for _ in a_list]
                + [pl.BlockSpec((tk, n), lambda i, kk: (kk, 0)) for _ in w_list]
                + [pl.BlockSpec((tm, n), lambda i, kk: (i, 0)),
                   pl.BlockSpec((1, n), lambda i, kk: (0, 0)),
                   pl.BlockSpec((1, n), lambda i, kk: (0, 0))])
    return pl.pallas_call(
        functools.partial(_mm_res_ln_kernel, n_in=n_in, tk_steps=steps, tr=tr),
        grid=(m // tm, steps),
        in_specs=in_specs,
        out_specs=[pl.BlockSpec((tm, n), lambda i, kk: (i, 0)),
                   pl.BlockSpec((tm, n), lambda i, kk: (i, 0))],
        out_shape=[jax.ShapeDtypeStruct((m, n), F32), jax.ShapeDtypeStruct((m, n), BF16)],
        scratch_shapes=[pltpu.VMEM((tm, n), F32)] if steps > 1 else [],
        compiler_params=_cparams("arbitrary", "arbitrary"),
        name="mm_res_ln",
    )(*a_list, *w_list, res, lnw.reshape(1, n), lnb.reshape(1, n))


def _xattn_kernel(q_ref, k_ref, v_ref, o_ref, *, tr):
    kb = k_ref[0]
    vb = v_ref[0]
    for r0 in range(0, q_ref.shape[1], tr):
        s = _dot_nt(q_ref[0, r0:r0 + tr, :], kb) * (XA_HD ** -0.5)
        m = jnp.max(s, axis=-1, keepdims=True)
        p = jnp.exp(s - m)
        l = jnp.sum(p, axis=-1, keepdims=True)
        o = _dot(_bf(p / l), vb)
        o_ref[0, r0:r0 + tr, :] = o.astype(o_ref.dtype)


def _xattn(q, kv, ts=2048):
    b, s, d = q.shape
    mlen = kv.shape[1]
    ts = min(ts, s)
    return pl.pallas_call(
        functools.partial(_xattn_kernel, tr=min(256, ts)),
        grid=(b, s // ts, XA_HEADS),
        in_specs=[pl.BlockSpec((1, ts, XA_HD), lambda bi, i, h: (bi, i, h)),
                  pl.BlockSpec((1, mlen, XA_HD), lambda bi, i, h: (bi, 0, h)),
                  pl.BlockSpec((1, mlen, XA_HD), lambda bi, i, h: (bi, 0, XA_HEADS + h))],
        out_specs=pl.BlockSpec((1, ts, XA_HD), lambda bi, i, h: (bi, i, h)),
        out_shape=jax.ShapeDtypeStruct((b, s, d), BF16),
        compiler_params=_cparams("arbitrary", "arbitrary", "arbitrary"),
        name="xattn",
    )(q, kv, kv)


def _mm_taps_kernel(x_ref, w_ref, t_ref, o_ref, sc_ref, *, ts, ntap, act):
    first = pl.program_id(2) == 0

    @pl.when(first)
    def _():
        sc_ref[0:HIST, :] = jnp.zeros((HIST, sc_ref.shape[1]), F32)

    sc_ref[HIST:HIST + ts, :] = _dot(x_ref[0], w_ref[...])
    y = t_ref[ntap - 1:ntap, :] * sc_ref[HIST:HIST + ts, :]
    for back in range(1, ntap):
        tap = ntap - 1 - back
        y = y + t_ref[tap:tap + 1, :] * sc_ref[HIST - back:HIST - back + ts, :]
    if act == "silu":
        y = y * _sigmoid(y)
    elif act == "lora":
        p0, p1, _ = RWKV_LORA_PAD
        y = jnp.concatenate([jnp.tanh(y[:, :p0]), y[:, p0:p0 + p1], _sigmoid(y[:, p0 + p1:])], axis=1)
    o_ref[0] = y.astype(o_ref.dtype)
    _carry_rows(sc_ref, ts)


def _mm_taps(xb, w, taps, out_dtype, act=None, ts=1024, tn=1024):
    b, s, d = xb.shape
    n = w.shape[1]
    ts = min(ts, s)
    tn = min(tn, n)
    ntap = taps.shape[0]
    return pl.pallas_call(
        functools.partial(_mm_taps_kernel, ts=ts, ntap=ntap, act=act),
        grid=(n // tn, b, s // ts),
        in_specs=[pl.BlockSpec((1, ts, d), lambda j, bi, i: (bi, i, 0)),
                  pl.BlockSpec((d, tn), lambda j, bi, i: (0, j)),
                  pl.BlockSpec((ntap, tn), lambda j, bi, i: (0, j))],
        out_specs=pl.BlockSpec((1, ts, tn), lambda j, bi, i: (bi, i, j)),
        out_shape=jax.ShapeDtypeStruct((b, s, n), out_dtype),
        scratch_shapes=[pltpu.VMEM((ts + HIST, tn), F32)],
        compiler_params=_cparams("arbitrary", "arbitrary", "arbitrary"),
        name="mm_taps",
    )(xb, w, taps.astype(F32))


def _ffn_up_kernel(x_ref, wg_ref, wu_ref, cg_ref, cu_ref, o_ref, sg_ref, su_ref, *, ts, tr):
    first = pl.program_id(2) == 0

    @pl.when(first)
    def _():
        sg_ref[0:HIST, :] = jnp.zeros((HIST, sg_ref.shape[1]), F32)
        su_ref[0:HIST, :] = jnp.zeros((HIST, su_ref.shape[1]), F32)

    def conv(sc_ref, c_ref, r0):
        y = c_ref[FFN_CONV - 1:FFN_CONV, :] * sc_ref[HIST + r0:HIST + r0 + tr, :]
        for back in range(1, FFN_CONV):
            tap = FFN_CONV - 1 - back
            y = y + c_ref[tap:tap + 1, :] * sc_ref[HIST + r0 - back:HIST + r0 - back + tr, :]
        return y

    for r0 in range(0, ts, tr):
        x = x_ref[0, r0:r0 + tr, :]
        sg_ref[HIST + r0:HIST + r0 + tr, :] = _dot(x, wg_ref[...])
        g = conv(sg_ref, cg_ref, r0)
        g = g * _sigmoid(g)
        su_ref[HIST + r0:HIST + r0 + tr, :] = _dot(x, wu_ref[...])
        u = conv(su_ref, cu_ref, r0)
        o_ref[0, r0:r0 + tr, :] = (g * u).astype(o_ref.dtype)
    _carry_rows(sg_ref, ts)
    _carry_rows(su_ref, ts)


def _ffn_up(xb, w_up, conv_w, ts=1024, tn=512, tr=1024):
    b, s, d = xb.shape
    ts = min(ts, s)
    nj = D_FF // tn
    return pl.pallas_call(
        functools.partial(_ffn_up_kernel, ts=ts, tr=min(tr, ts)),
        grid=(nj, b, s // ts),
        in_specs=[pl.BlockSpec((1, ts, d), lambda j, bi, i: (bi, i, 0)),
                  pl.BlockSpec((d, tn), lambda j, bi, i: (0, j)),
                  pl.BlockSpec((d, tn), lambda j, bi, i: (0, nj + j)),
                  pl.BlockSpec((FFN_CONV, tn), lambda j, bi, i: (0, j)),
                  pl.BlockSpec((FFN_CONV, tn), lambda j, bi, i: (0, nj + j))],
        out_specs=pl.BlockSpec((1, ts, tn), lambda j, bi, i: (bi, i, j)),
        out_shape=jax.ShapeDtypeStruct((b, s, D_FF), BF16),
        scratch_shapes=[pltpu.VMEM((ts + HIST, tn), F32), pltpu.VMEM((ts + HIST, tn), F32)],
        compiler_params=_cparams("arbitrary", "arbitrary", "arbitrary"),
        name="ffn_up",
    )(xb, w_up, w_up, conv_w, conv_w)


def _pool_kernel(u_ref, w_ref, sc_ref, o_ref, st_ref, *, ts):
    i = pl.program_id(1)

    @pl.when(i == 0)
    def _():
        st_ref[0:2 * HIST, :] = jnp.zeros((2 * HIST, st_ref.shape[1]), F32)

    st_ref[2 * HIST:2 * HIST + ts, :] = u_ref[0].astype(F32)
    pos = (i * ts + 1 + lax.broadcasted_iota(jnp.int32, (ts, 1), 0)).astype(F32)
    base = 2 * HIST
    for g, win in enumerate(POOL_WINDOWS):
        cols = slice(g * POOL_GDIM, (g + 1) * POOL_GDIM)
        cur = st_ref[base:base + ts, cols]
        acc = cur
        for back in range(1, win):
            acc = acc + st_ref[base - back:base - back + ts, cols]
        mean = acc / jnp.minimum(pos, float(win))
        y = _dot(_bf(mean - cur), w_ref[g])
        o_ref[0, :, cols] = (y * sc_ref[:, cols]).astype(o_ref.dtype)
    st_ref[0:2 * HIST, :] = st_ref[ts:ts + 2 * HIST, :]


def _pool_mixer(h_main, pool_w, pool_scale, ts=512):
    b, s, _ = h_main.shape
    ts = min(ts, s)
    return pl.pallas_call(
        functools.partial(_pool_kernel, ts=ts),
        grid=(b, s // ts),
        in_specs=[pl.BlockSpec((1, ts, HALF), lambda bi, i: (bi, i, 0)),
                  pl.BlockSpec((len(POOL_WINDOWS), POOL_GDIM, POOL_GDIM), lambda bi, i: (0, 0, 0)),
                  pl.BlockSpec((1, HALF), lambda bi, i: (0, 0))],
        out_specs=pl.BlockSpec((1, ts, HALF), lambda bi, i: (bi, i, 0)),
        out_shape=jax.ShapeDtypeStruct((b, s, HALF), BF16),
        scratch_shapes=[pltpu.VMEM((ts + 2 * HIST, HALF), F32)],
        compiler_params=_cparams("arbitrary", "arbitrary"),
        name="pool_mixer",
    )(h_main, pool_w, pool_scale.reshape(1, HALF))


def _gdn_kernel(q_ref, k_ref, v_ref, z_ref, g_ref, alog_ref, dtb_ref, nw_ref, o_ref, st_ref, *, ts, hps):
    c = CHUNK
    h0 = pl.program_id(1)

    @pl.when(pl.program_id(2) == 0)
    def _():
        st_ref[...] = jnp.zeros_like(st_ref)

    q_all = q_ref[0]
    k_all = k_ref[0]
    v_all = v_ref[0]
    z_all = z_ref[0].astype(F32)

    logits = g_ref[0]
    lane = lax.broadcasted_iota(jnp.int32, logits.shape, 1)
    g_all = -jnp.exp(alog_ref[...]) * _softplus(logits + dtb_ref[...])
    sig_all = _sigmoid(logits)

    causal, strict = _tri_masks(c)
    tril = causal.astype(BF16)
    eye = jnp.where(causal & (~strict), 1.0, 0.0).astype(F32)
    lane_c = lax.broadcasted_iota(jnp.int32, (c, GDN_DK), 1)
    nw = nw_ref[...]
    chunk_rows = [slice(ci * c, (ci + 1) * c) for ci in range(ts // c)]
    nck = len(chunk_rows)

    q, k, v, betas, g_cols = [], [], [], [], []
    for hd in range(hps):
        hl = slice(hd * GDN_DK, (hd + 1) * GDN_DK)
        qh, kh = q_all[:, hl], k_all[:, hl]
        qh = qh * lax.rsqrt(jnp.sum(qh * qh, axis=-1, keepdims=True) + NORM_EPS) * (GDN_DK ** -0.5)
        kh = kh * lax.rsqrt(jnp.sum(kh * kh, axis=-1, keepdims=True) + NORM_EPS)
        head = h0 * hps + hd
        g_col = jnp.sum(jnp.where(lane == head, g_all, 0.0), axis=1, keepdims=True)
        b_col = jnp.sum(jnp.where(lane == head + GDN_HEADS, sig_all, 0.0), axis=1, keepdims=True)
        for r in chunk_rows:
            q.append(qh[r])
            k.append(kh[r])
            v.append(v_all[r, hl])
            betas.append(jnp.broadcast_to(b_col[r], (c, GDN_DK)))
            g_cols.append(jnp.broadcast_to(g_col[r], (c, GDN_DK)))
    nprob = len(q)
    rows = list(range(nprob))
    gcs = [_dot01(tril, gcol) for gcol in g_cols]
    diffs = []
    for gc in gcs:
        g1, g2, g3 = (p.astype(F32) for p in _split3(gc))
        lhs = jnp.where(lane_c == 0, g1, jnp.where(lane_c == 1, g2, jnp.where(lane_c == 2, g3,
              jnp.where(lane_c < 6, 1.0, 0.0))))
        rhs = jnp.where(lane_c < 3, 1.0, jnp.where(lane_c == 3, -g1, jnp.where(lane_c == 4, -g2,
              jnp.where(lane_c == 5, -g3, 0.0))))
        diffs.append(_dot_nt(_bf(lhs), _bf(rhs)))
    decays = [jnp.where(causal, jnp.exp(jnp.where(causal, d, 0.0)), 0.0) for d in diffs]
    kbs = [k[r] * bt for r, bt in zip(rows, betas)]
    kcbs = [_bf(k[r]) for r in rows]
    lmats = [jnp.where(strict, _dot_nt(_bf(kb), kcb) * dc, 0.0) for kb, kcb, dc in zip(kbs, kcbs, decays)]
    intras = [_bf(jnp.where(causal, _dot_nt(_bf(q[r]), kcb) * dc, 0.0)) for r, kcb, dc in zip(rows, kcbs, decays)]
    tmats = [_bf(t) for t in _unit_lower_inverses([-l for l in lmats], eye)]
    egs = [jnp.exp(gc) for gc in gcs]
    g_lasts = [gc[c - 1:c, :] for gc in gcs]
    uws = [_dot(t, jnp.concatenate([_bf(v[r] * bt), _bf(kb * eg)], axis=1))
           for t, r, bt, kb, eg in zip(tmats, rows, betas, kbs, egs)]
    us = [uw[:, :GDN_DK] for uw in uws]
    ws = [_bf(uw[:, GDN_DK:]) for uw in uws]
    q_decs = [_bf(q[r] * eg) for r, eg in zip(rows, egs)]
    k_decs = [_bf(k[r] * jnp.exp(gl - gc)) for r, gl, gc in zip(rows, g_lasts, gcs)]
    e_lasts = [jnp.exp(gl) for gl in g_lasts]
    mns = [_dot_tn(kd, jnp.concatenate([w, _bf(u)], axis=1)) for kd, w, u in zip(k_decs, ws, us)]
    mmats = [_bf(mn[:, :GDN_DK]) for mn in mns]
    nmats = [mn[:, GDN_DK:] for mn in mns]

    states = [st_ref[hd] for hd in range(hps)]
    sb_hist = [[None] * nck for _ in range(hps)]
    for ci in range(nck):
        for hd in range(hps):
            p = hd * nck + ci
            sb = _bf(states[hd])
            sb_hist[hd][ci] = sb
            states[hd] = states[hd] * e_lasts[p] - _dot(mmats[p], sb) + nmats[p]
    for hd in range(hps):
        st_ref[hd] = states[hd]
    for hd in range(hps):
        hl = slice(hd * GDN_DK, (hd + 1) * GDN_DK)
        for ci in range(nck):
            p = hd * nck + ci
            sb = sb_hist[hd][ci]
            v_new = _bf(us[p] - _dot(ws[p], sb))
            o = _dot(q_decs[p], sb) + _dot(intras[p], v_new)
            o = o * lax.rsqrt(jnp.mean(o * o, axis=-1, keepdims=True) + NORM_EPS) * nw
            zc = z_all[chunk_rows[ci], hl]
            o_ref[0, chunk_rows[ci], hl] = (o * (zc * _sigmoid(zc))).astype(o_ref.dtype)


def _gated_deltanet(qkv, z, logits, a_log, dt_bias, norm_w, ts=256, hps=8):
    b, s, _ = qkv.shape
    ts = min(ts, s)
    hd = GDN_DK
    gw = hps * hd
    ng = HALF // gw
    pad = lambda t: jnp.pad(t.astype(F32), (0, hd - t.shape[0])).reshape(1, hd)
    col = lambda grp: (lambda bi, h, i: (bi, i, grp * ng + h))
    return pl.pallas_call(
        functools.partial(_gdn_kernel, ts=ts, hps=hps),
        grid=(b, ng, s // ts),
        in_specs=[pl.BlockSpec((1, ts, gw), col(0)), pl.BlockSpec((1, ts, gw), col(1)),
                  pl.BlockSpec((1, ts, gw), col(2)), pl.BlockSpec((1, ts, gw), col(0)),
                  pl.BlockSpec((1, ts, hd), lambda bi, h, i: (bi, i, 0)),
                  pl.BlockSpec((1, hd), lambda bi, h, i: (0, 0)),
                  pl.BlockSpec((1, hd), lambda bi, h, i: (0, 0)),
                  pl.BlockSpec((1, hd), lambda bi, h, i: (0, 0))],
        out_specs=pl.BlockSpec((1, ts, gw), lambda bi, h, i: (bi, i, h)),
        out_shape=jax.ShapeDtypeStruct((b, s, HALF), BF16),
        scratch_shapes=[pltpu.VMEM((hps, hd, hd), F32)],
        compiler_params=_cparams("arbitrary", "arbitrary", "arbitrary"),
        name="gated_deltanet",
    )(qkv, qkv, qkv, z, logits, pad(a_log), pad(dt_bias), norm_w.astype(F32).reshape(1, hd))


def _rwkv_kernel(r_ref, k_ref, v_ref, l_ref, w2_ref, a2_ref, g2_ref,
                 w0_ref, a0_ref, kk_ref, ka_ref, rk_ref, lnw_ref, lnb_ref, o_ref, st_ref, *, ts, hps):
    c = CHUNK
    hs = RWKV_HEAD

    @pl.when(pl.program_id(2) == 0)
    def _():
        st_ref[...] = jnp.zeros_like(st_ref)

    r = r_ref[0]
    k = k_ref[0]
    v = v_ref[0]
    lo = l_ref[0]
    p0, p1, p2 = RWKV_LORA_PAD
    w_log = -_softplus(-(w0_ref[...] + _dot(lo[:, :p0], w2_ref[...]))) - 0.5
    lw = -jnp.exp(w_log)
    a = _sigmoid(a0_ref[...] + _dot(lo[:, p0:p0 + p1], a2_ref[...]))
    g = _dot(lo[:, p0 + p1:p0 + p1 + p2], g2_ref[...])
    kk_raw = k * kk_ref[...]
    km = k * (1.0 + (a - 1.0) * ka_ref[...])
    head_of_lane = lax.broadcasted_iota(jnp.int32, (ts, hps * hs), 1) // hs

    def per_head_sum(x):
        out = jnp.zeros_like(x)
        for hd in range(hps):
            sel = head_of_lane == hd
            out = jnp.where(sel, jnp.sum(jnp.where(sel, x, 0.0), axis=-1, keepdims=True), out)
        return out

    kk = kk_raw * lax.rsqrt(per_head_sum(kk_raw * kk_raw) + NORM_EPS)
    aa = -kk
    bb = kk * a
    bonus = per_head_sum(r * km * rk_ref[...]) * v

    causal, strict = _tri_masks(c)
    tril = causal.astype(BF16)
    eye = jnp.where(causal & (~strict), 1.0, 0.0).astype(F32)
    lnw = lnw_ref[...]
    lnb = lnb_ref[...]

    rows = [slice(ci * c, (ci + 1) * c) for ci in range(ts // c)]
    nck = len(rows)
    heads = [slice(hd * hs, (hd + 1) * hs) for hd in range(hps)]
    cums = [_dot01(tril, lw[rw]) for rw in rows]
    c_lasts = [cum[c - 1:c, :] for cum in cums]
    w_invs = [jnp.exp(-cum) for cum in cums]
    w_tails = [jnp.exp(cl - cum) for cl, cum in zip(c_lasts, cums)]
    a_ts = [_bf(aa[rw] * jnp.exp(cum - lw[rw])) for rw, cum in zip(rows, cums)]
    r_ts = [_bf(r[rw] * jnp.exp(cum)) for rw, cum in zip(rows, cums)]
    b_ts = [_bf(bb[rw] * wi) for rw, wi in zip(rows, w_invs)]
    k_ts = [_bf(km[rw] * wi) for rw, wi in zip(rows, w_invs)]
    b_hs = [_bf(bb[rw] * wt) for rw, wt in zip(rows, w_tails)]
    k_hs = [_bf(km[rw] * wt) for rw, wt in zip(rows, w_tails)]
    vcs = [_bf(v[rw]) for rw in rows]
    e_lasts = [jnp.exp(cl) for cl in c_lasts]
    prob = [(ci, cs) for ci in range(nck) for cs in heads]
    row2 = lax.broadcasted_iota(jnp.int32, (c, 2 * c), 0)
    col2 = lax.broadcasted_iota(jnp.int32, (c, 2 * c), 1) % c
    bks = [jnp.concatenate([b_ts[ci][:, cs], k_ts[ci][:, cs]], axis=0) for ci, cs in prob]
    a_abks = [jnp.where(row2 > col2, _dot_nt(a_ts[ci][:, cs], bk), 0.0) for bk, (ci, cs) in zip(bks, prob)]
    a_rbks = [_bf(jnp.where(row2 >= col2, _dot_nt(r_ts[ci][:, cs], bk), 0.0)) for bk, (ci, cs) in zip(bks, prob)]
    tmats = [_bf(t) for t in _unit_lower_inverses([m[:, :c] for m in a_abks], eye)]
    zero_c = jnp.zeros((c, hs), BF16)
    avs = [_bf(_dot(_bf(abk), jnp.concatenate([zero_c, vcs[ci][:, cs]], axis=0)))
           for abk, (ci, cs) in zip(a_abks, prob)]
    pmats = [_bf(_dot(t, a_ts[ci][:, cs])) for t, (ci, cs) in zip(tmats, prob)]
    qmats = [_dot(t, av) for t, av in zip(tmats, avs)]
    mmats = [_bf(_dot_tn(pm, b_hs[ci][:, cs])) for pm, (ci, cs) in zip(pmats, prob)]
    nmats = [_dot_tn(jnp.concatenate([_bf(qm), vcs[ci][:, cs]], axis=0),
                     jnp.concatenate([b_hs[ci][:, cs], k_hs[ci][:, cs]], axis=0))
             for qm, (ci, cs) in zip(qmats, prob)]

    states = [st_ref[hd] for hd in range(hps)]
    sb_hist = []
    for ci in range(nck):
        sbs = [_bf(s) for s in states]
        sb_hist.append(sbs)
        states = [states[hd] * e_lasts[ci][:, heads[hd]] + _dot(sbs[hd], mmats[hps * ci + hd]) + nmats[hps * ci + hd]
                  for hd in range(hps)]
    for hd in range(hps):
        st_ref[hd] = states[hd]
    ys = []
    for ci in range(nck):
        sbs = sb_hist[ci]
        ubs = [_bf(_dot_nt(pmats[hps * ci + hd], sbs[hd]) + qmats[hps * ci + hd]) for hd in range(hps)]
        ys.append([_dot_nt(r_ts[ci][:, heads[hd]], sbs[hd])
                   + _dot(a_rbks[hps * ci + hd], jnp.concatenate([ubs[hd], vcs[ci][:, heads[hd]]], axis=0))
                   for hd in range(hps)])
    for ci in range(nck):
        outs = []
        for y in ys[ci]:
            ym = jnp.mean(y, axis=-1, keepdims=True)
            yd = y - ym
            yv = jnp.mean(yd * yd, axis=-1, keepdims=True)
            outs.append(yd * lax.rsqrt(yv + RWKV_LNX_EPS))
        yn = jnp.concatenate(outs, axis=-1) * lnw + lnb
        o_ref[0, rows[ci], :] = ((yn + bonus[rows[ci]]) * g[rows[ci]]).astype(o_ref.dtype)


def _rwkv7(hr, hl, w2p, a2p, g2p, w0, a0, k_k, k_a, r_k, lnx_w, lnx_b, ts=128, hps=16):
    b, s, _ = hr.shape
    ts = min(ts, s)
    pw = hps * RWKV_HEAD
    npair = HALF // pw
    lw = sum(RWKV_LORA_PAD)
    col = lambda grp: (lambda bi, hp, i: (bi, i, grp * npair + hp))
    vec = lambda grp: pl.BlockSpec((1, pw), lambda bi, hp, i: (0, grp * npair + hp))
    row1 = lambda t: t.astype(F32).reshape(1, -1)
    return pl.pallas_call(
        functools.partial(_rwkv_kernel, ts=ts, hps=hps),
        grid=(b, npair, s // ts),
        in_specs=[pl.BlockSpec((1, ts, pw), col(0)), pl.BlockSpec((1, ts, pw), col(1)),
                  pl.BlockSpec((1, ts, pw), col(2)),
                  pl.BlockSpec((1, ts, lw), lambda bi, hp, i: (bi, i, 0)),
                  pl.BlockSpec((RWKV_LORA_PAD[0], pw), lambda bi, hp, i: (0, hp)),
                  pl.BlockSpec((RWKV_LORA_PAD[1], pw), lambda bi, hp, i: (0, hp)),
                  pl.BlockSpec((RWKV_LORA_PAD[2], pw), lambda bi, hp, i: (0, hp)),
                  vec(0), vec(0), vec(0), vec(0), vec(0), vec(0), vec(0)],
        out_specs=pl.BlockSpec((1, ts, pw), lambda bi, hp, i: (bi, i, hp)),
        out_shape=jax.ShapeDtypeStruct((b, s, HALF), BF16),
        scratch_shapes=[pltpu.VMEM((hps, RWKV_HEAD, RWKV_HEAD), F32)],
        compiler_params=_cparams("arbitrary", "arbitrary", "arbitrary"),
        name="rwkv7",
    )(hr, hr, hr, hl, w2p, a2p, g2p,
      row1(w0), row1(a0), row1(k_k), row1(k_a), row1(r_k), row1(lnx_w), row1(lnx_b))


def _fox_cumsum_kernel(f_ref, bf_ref, o_ref):
    x = f_ref[0] + bf_ref[...]
    ls = jnp.minimum(x, 0.0) - jnp.log(1.0 + jnp.exp(-jnp.abs(x)))
    nh, nr, nl = ls.shape
    li = lax.broadcasted_iota(jnp.int32, (nl, nl), 0)
    lj = lax.broadcasted_iota(jnp.int32, (nl, nl), 1)
    upper = (li <= lj).astype(BF16)
    ri = lax.broadcasted_iota(jnp.int32, (nr, nr), 0)
    rj = lax.broadcasted_iota(jnp.int32, (nr, nr), 1)
    below = (ri > rj).astype(BF16)
    for h in range(nh):
        within = _dot01_right(ls[h], upper)
        tot = jnp.broadcast_to(within[:, nl - 1:nl], (nr, nl))
        o_ref[0, h] = within + _dot01(below, tot)


def _dot01_right(x, m01):
    x1, x2, x3 = _split3(x)
    return _dot(x1, m01) + _dot(x2, m01) + _dot(x3, m01)


def _fox_cumsum(f_t, b_f):
    b, nh, s = f_t.shape
    nl = 128
    nr = s // nl
    out = pl.pallas_call(
        _fox_cumsum_kernel,
        grid=(b,),
        in_specs=[pl.BlockSpec((1, nh, nr, nl), lambda bi: (bi, 0, 0, 0)),
                  pl.BlockSpec((nh, 1, 1), lambda bi: (0, 0, 0))],
        out_specs=pl.BlockSpec((1, nh, nr, nl), lambda bi: (bi, 0, 0, 0)),
        out_shape=jax.ShapeDtypeStruct((b, nh, nr, nl), F32),
        compiler_params=_cparams("arbitrary"),
        name="fox_cumsum",
    )(f_t.reshape(b, nh, nr, nl), b_f.astype(F32).reshape(nh, 1, 1))
    return out.reshape(b, nh, 1, s)


def _fox_kernel(qi_ref, ki_ref, q_ref, k_ref, v_ref, cq_ref, ck_ref, o_ref, qs_ref, m_ref, acc_ref, *, tq, tk, tr):
    p = pl.program_id(2)
    qi = qi_ref[p]
    ki = ki_ref[p]
    last_k = ((qi + 1) * tq - 1) // tk
    log2e = 1.0 / math.log(2.0)

    @pl.when(ki == 0)
    def _():
        qs_ref[...] = (q_ref[0].astype(F32) * (FOX_HD ** -0.5 * log2e)).astype(BF16)
        m_ref[...] = jnp.full_like(m_ref, -jnp.inf)
        acc_ref[...] = jnp.zeros_like(acc_ref)

    def step(masked):
        bias = (cq_ref[0, 0, :, 0:1] - ck_ref[0, 0]) * log2e
        v_aug = jnp.concatenate([v_ref[0], jnp.ones((tk, FOX_HD), BF16)], axis=1)
        kb = k_ref[0]
        for r0 in range(0, tq, tr):
            rows = slice(r0, r0 + tr)
            s = _dot_nt(qs_ref[rows, :], kb) + bias
            if masked:
                qpos = qi * tq + r0 + lax.broadcasted_iota(jnp.int32, (tr, tk), 0)
                kpos = ki * tk + lax.broadcasted_iota(jnp.int32, (tr, tk), 1)
                s = jnp.where(kpos <= qpos, s, -jnp.inf)
            m_old = m_ref[rows, :]
            m_new = jnp.maximum(m_old, jnp.max(s, axis=-1, keepdims=True))
            corr = jnp.exp2(m_old - m_new)
            pexp = _bf(jnp.exp2(s - m_new))
            acc_ref[rows, :] = corr * acc_ref[rows, :] + _dot(pexp, v_aug)
            m_ref[rows, :] = m_new

    needs_mask = (ki + 1) * tk - 1 > qi * tq

    @pl.when(needs_mask)
    def _():
        step(True)

    @pl.when(jnp.logical_not(needs_mask))
    def _():
        step(False)

    @pl.when(ki == last_k)
    def _():
        acc = acc_ref[...]
        o_ref[0] = (acc[:, :FOX_HD] / acc[:, FOX_HD:FOX_HD + 1]).astype(o_ref.dtype)


def _fox_attention(hf, c, tq=2048, tk=2048, tr=128):
    b, s, _ = hf.shape
    tq = min(tq, s)
    tk = min(tk, s)
    nh = FOX_HEADS
    pairs = [(qi, ki) for qi in range(s // tq) for ki in range(((qi + 1) * tq - 1) // tk + 1)]
    qi_arr = jnp.asarray(np.array([p[0] for p in pairs], np.int32))
    ki_arr = jnp.asarray(np.array([p[1] for p in pairs], np.int32))
    grid_spec = pltpu.PrefetchScalarGridSpec(
        num_scalar_prefetch=2,
        grid=(b, nh, len(pairs)),
        in_specs=[pl.BlockSpec((1, tq, FOX_HD), lambda bi, h, p, qa, ka: (bi, qa[p], h)),
                  pl.BlockSpec((1, tk, FOX_HD), lambda bi, h, p, qa, ka: (bi, ka[p], nh + h)),
                  pl.BlockSpec((1, tk, FOX_HD), lambda bi, h, p, qa, ka: (bi, ka[p], 2 * nh + h)),
                  pl.BlockSpec((1, 1, 1, tq), lambda bi, h, p, qa, ka: (bi, h, 0, qa[p])),
                  pl.BlockSpec((1, 1, 1, tk), lambda bi, h, p, qa, ka: (bi, h, 0, ka[p]))],
        out_specs=pl.BlockSpec((1, tq, FOX_HD), lambda bi, h, p, qa, ka: (bi, qa[p], h)),
        scratch_shapes=[pltpu.VMEM((tq, FOX_HD), BF16), pltpu.VMEM((tq, 1), F32),
                        pltpu.VMEM((tq, 2 * FOX_HD), F32)])
    return pl.pallas_call(
        functools.partial(_fox_kernel, tq=tq, tk=tk, tr=min(tr, tq)),
        grid_spec=grid_spec,
        out_shape=jax.ShapeDtypeStruct((b, s, HALF), BF16),
        compiler_params=_cparams("arbitrary", "arbitrary", "arbitrary"),
        name="fox_attention",
    )(qi_arr, ki_arr, hf, hf, hf, c, c)


def _pad_cols(w, width):
    return jnp.pad(w, ((0, 0), (0, width - w.shape[1])))


def _pad_rows(w, height):
    return jnp.pad(w, ((0, height - w.shape[0]), (0, 0)))


def _even_mixer(x32, xb, w_in, pool_w, pool_scale, conv_w, a_log, dt_bias, norm_w, w_out, ln_w, ln_b):
    b, s, d = x32.shape
    t = b * s
    n_main = 5 * HALF
    wb = _bf(w_in[:, :n_main])
    w_gate = _bf(_pad_cols(w_in[:, n_main:], 128))
    xb2 = xb.reshape(t, d)
    u_pool = _mm(xb2, wb[:, :HALF], F32).reshape(b, s, HALF)
    qkv = _mm_taps(xb, wb[:, HALF:4 * HALF], conv_w, F32, act="silu")
    z = _mm(xb2, wb[:, 4 * HALF:], BF16).reshape(b, s, HALF)
    logits = _mm(xb2, w_gate, F32, tn=128).reshape(b, s, 128)
    y_a = _pool_mixer(u_pool, _bf(pool_w), pool_scale)
    y_b = _gated_deltanet(qkv, z, logits, a_log, dt_bias, norm_w)
    wo = _bf(w_out)
    return _mm_res_ln([y_a.reshape(t, HALF), y_b.reshape(t, HALF)], [wo[:HALF], wo[HALF:]],
                      x32.reshape(t, d), ln_w, ln_b)


def _odd_mixer(x32, xb, w_in, mu, w0, w2, a0, a2, g2, k_k, k_a, r_k, lnx_w, lnx_b, b_f, w_out, ln_w, ln_b):
    b, s, d = x32.shape
    t = b * s
    l0, l1, l2 = RWKV_LORA
    p0, p1, p2 = RWKV_LORA_PAD
    o_l = 3 * HALF
    o_f = o_l + l0 + l1 + l2
    w_rkv = _bf(w_in[:, :o_l])
    w_l = _bf(jnp.concatenate([_pad_cols(w_in[:, o_l:o_l + l0], p0),
                               _pad_cols(w_in[:, o_l + l0:o_l + l0 + l1], p1),
                               _pad_cols(w_in[:, o_l + l0 + l1:o_f], p2)], axis=1))
    w_fox = _bf(w_in[:, o_f:o_f + 3 * HALF])
    w_fl = _bf(_pad_cols(w_in[:, o_f + 3 * HALF:], 128))
    mu_l = jnp.concatenate([jnp.pad(mu[o_l:o_l + l0], (0, p0 - l0)),
                            jnp.pad(mu[o_l + l0:o_l + l0 + l1], (0, p1 - l1)),
                            jnp.pad(mu[o_l + l0 + l1:o_f], (0, p2 - l2))])
    xb2 = xb.reshape(t, d)
    lerp = lambda m: jnp.stack([m, 1.0 - m])
    hr = _mm_taps(xb, w_rkv, lerp(mu[:o_l]), F32)
    hl = _mm_taps(xb, w_l, lerp(mu_l), BF16, act="lora", tn=p0 + p1 + p2)
    hf = _mm(xb2, w_fox, BF16).reshape(b, s, 3 * HALF)
    fl = _mm(xb2, w_fl, F32, tn=128).reshape(b, s, 128)
    y_c = _rwkv7(hr, hl, _bf(_pad_rows(w2, p0)), _bf(_pad_rows(a2, p1)), _bf(_pad_rows(g2, p2)),
                 w0, a0, k_k, k_a, r_k, lnx_w, lnx_b)
    c = _fox_cumsum(jnp.transpose(fl[:, :, :FOX_HEADS], (0, 2, 1)), b_f)
    y_d = _fox_attention(hf, c)
    wo = _bf(w_out)
    return _mm_res_ln([y_c.reshape(t, HALF), y_d.reshape(t, HALF)], [wo[:HALF], wo[HALF:]],
                      x32.reshape(t, d), ln_w, ln_b)


def _cross_attention(x32, xb, mem_b, w_q, w_kv, w_o, ln_w, ln_b):
    t, d = x32.shape
    b, mlen, _ = mem_b.shape
    s = t // b
    q = _mm(xb, _bf(w_q), BF16).reshape(b, s, d)
    kv = _mm(mem_b.reshape(b * mlen, d), _bf(w_kv), BF16).reshape(b, mlen, 2 * d)
    o = _xattn(q, kv).reshape(t, d)
    return _mm_res_ln([o], [_bf(w_o)], x32, ln_w, ln_b)


def _conv_glu_ffn(x32, xb, b, w_up, conv_w, w_down, ln_w, ln_b):
    t, d = x32.shape
    hmid = _ffn_up(xb.reshape(b, t // b, d), _bf(w_up), conv_w)
    return _mm_res_ln([hmid.reshape(t, D_FF)], [_bf(w_down)], x32, ln_w, ln_b, tk=D_FF // 2)


def kernel(x, mem, ev_w_in, pool_w, pool_scale, gdn_conv_w, gdn_a_log, gdn_dt_bias, gdn_norm_w, ev_w_out,
           od_w_in, rwkv_mu, rwkv_w0, rwkv_w2, rwkv_a0, rwkv_a2, rwkv_g2, rwkv_k_k, rwkv_k_a, rwkv_r_k,
           rwkv_lnx_w, rwkv_lnx_b, fox_b_f, od_w_out,
           ln_mix_w, ln_mix_b, xa_w_q, xa_w_kv, xa_w_o, ln_xa_w, ln_xa_b,
           ffn_w_up, ffn_conv_w, ffn_w_down, ln_ffn_w, ln_ffn_b):
    b, s, d = x.shape
    depth = ln_mix_w.shape[0]
    mem_b = _bf(mem)
    x32 = x
    xb = _bf(x)
    for i in range(depth):
        j = i // 2
        x3 = x32.reshape(b, s, d)
        xb3 = xb.reshape(b, s, d)
        if i % 2 == 0:
            x32, xb = _even_mixer(x3, xb3, ev_w_in[j], pool_w[j], pool_scale[j], gdn_conv_w[j], gdn_a_log[j],
                                  gdn_dt_bias[j], gdn_norm_w[j], ev_w_out[j], ln_mix_w[i], ln_mix_b[i])
        else:
            x32, xb = _odd_mixer(x3, xb3, od_w_in[j], rwkv_mu[j], rwkv_w0[j], rwkv_w2[j], rwkv_a0[j], rwkv_a2[j],
                                 rwkv_g2[j], rwkv_k_k[j], rwkv_k_a[j], rwkv_r_k[j].reshape(-1), rwkv_lnx_w[j],
                                 rwkv_lnx_b[j], fox_b_f[j], od_w_out[j], ln_mix_w[i], ln_mix_b[i])
        x32, xb = _cross_attention(x32, xb, mem_b, xa_w_q[i], xa_w_kv[i], xa_w_o[i], ln_xa_w[i], ln_xa_b[i])
        x32, xb = _conv_glu_ffn(x32, xb, b, ffn_w_up[i], ffn_conv_w[i], ffn_w_down[i], ln_ffn_w[i], ln_ffn_b[i])
    return x32.reshape(b, s, d)
```

```python
import functools
import math

import numpy as np
import jax
import jax.numpy as jnp
from jax import lax
from jax.experimental import pallas as pl
from jax.experimental.pallas import tpu as pltpu

F32 = jnp.float32
BF16 = jnp.bfloat16

D_MODEL = 2048
HALF = D_MODEL // 2
POOL_WINDOWS = (2, 4, 8, 16)
POOL_GDIM = HALF // len(POOL_WINDOWS)
GDN_HEADS = 8
GDN_DK = HALF // GDN_HEADS
GDN_CONV = 4
NORM_EPS = 1e-6
RWKV_HEAD = 64
RWKV_LNX_EPS = 64e-5
RWKV_LORA = (64, 64, 160)
RWKV_LORA_PAD = (128, 128, 256)
FOX_HEADS = 8
FOX_HD = HALF // FOX_HEADS
XA_HEADS = 4
XA_HD = D_MODEL // XA_HEADS
D_FF = 5632
FFN_CONV = 3
DEPTH = 2
DEEPNORM_ALPHA = float((2 * DEPTH) ** 0.25)
LN_EPS = 1e-5

CHUNK = 64
HIST = 8
VMEM_LIMIT_BYTES = 56 * 1024 * 1024


def _cparams(*sem):
    return pltpu.CompilerParams(dimension_semantics=sem, vmem_limit_bytes=VMEM_LIMIT_BYTES)


def _dot(a, b):
    return jnp.dot(a, b, preferred_element_type=F32)


def _dot_nt(a, b):
    return lax.dot_general(a, b, (((1,), (1,)), ((), ())), preferred_element_type=F32)


def _dot_tn(a, b):
    return lax.dot_general(a, b, (((0,), (0,)), ((), ())), preferred_element_type=F32)


def _bf(x):
    return x.astype(BF16)


def _split3(x):
    x1 = x.astype(BF16)
    r1 = x - x1.astype(F32)
    x2 = r1.astype(BF16)
    x3 = (r1 - x2.astype(F32)).astype(BF16)
    return x1, x2, x3


def _dot01(m01, x):
    x1, x2, x3 = _split3(x)
    return _dot(m01, x1) + _dot(m01, x2) + _dot(m01, x3)


def _unit_lower_inverses(n_mats, eye, fillers=()):
    c = eye.shape[0]
    levels = int(math.log2(c))
    fillers = list(fillers)
    xs = [eye + n for n in n_mats]
    nbs = [_bf(n) for n in n_mats]
    ps = [_dot(nb, nb) for nb in nbs]
    for level in range(1, levels):
        pbs = [_bf(p) for p in ps]
        if level == levels - 1:
            prods = [_dot(_bf(x), pb) for x, pb in zip(xs, pbs)]
        else:
            prods = [_dot(jnp.concatenate([pb, _bf(x)], axis=0), pb) for x, pb in zip(xs, pbs)]
        if fillers:
            fillers.pop(0)()
        if level == levels - 1:
            xs = [x + pr for x, pr in zip(xs, prods)]
        else:
            xs = [x + pr[c:] for x, pr in zip(xs, prods)]
            ps = [pr[:c] for pr in prods]
    for filler in fillers:
        filler()
    return xs


def _sigmoid(x):
    return 1.0 / (1.0 + jnp.exp(-x))


def _softplus(x):
    return jnp.maximum(x, 0.0) + jnp.log(1.0 + jnp.exp(-jnp.abs(x)))


def _tri_masks(c):
    row = lax.broadcasted_iota(jnp.int32, (c, c), 0)
    col = lax.broadcasted_iota(jnp.int32, (c, c), 1)
    return row >= col, row > col


def _stage_rows(x_ref, sc_ref, ts, first):
    @pl.when(first)
    def _():
        sc_ref[0:HIST, :] = jnp.zeros((HIST, sc_ref.shape[1]), F32)
    sc_ref[HIST:HIST + ts, :] = x_ref[0].astype(F32)


def _carry_rows(sc_ref, ts):
    sc_ref[0:HIST, :] = sc_ref[ts:ts + HIST, :]


def _weight_tile(w_ref, wb_ref, first):
    if wb_ref is None:
        return w_ref
    @pl.when(first)
    def _():
        wb_ref[...] = w_ref[...].astype(BF16)
    return wb_ref


def _weight_cols(w, n, col0, tn):
    n = w.shape[1] - col0 if n is None else n
    tn = min(tn, n)
    assert n % tn == 0 and col0 % tn == 0, (n, col0, tn)
    scratch = [] if w.dtype == BF16 else [pltpu.VMEM((w.shape[0], tn), BF16)]
    return n, tn, col0 // tn, scratch


def _mm_kernel(a_ref, w_ref, o_ref, wb_ref=None):
    w = _weight_tile(w_ref, wb_ref, pl.program_id(1) == 0)
    o_ref[...] = _dot(a_ref[...], w[...]).astype(o_ref.dtype)


def _mm(a, w, out_dtype, tm=1024, tn=1024, col0=0, n=None):
    m, k = a.shape
    n, tn, joff, scratch = _weight_cols(w, n, col0, tn)
    tm = min(tm, m)
    return pl.pallas_call(
        _mm_kernel,
        grid=(n // tn, m // tm),
        in_specs=[pl.BlockSpec((tm, k), lambda j, i: (i, 0)),
                  pl.BlockSpec((k, tn), lambda j, i: (0, joff + j))],
        out_specs=pl.BlockSpec((tm, tn), lambda j, i: (i, j)),
        out_shape=jax.ShapeDtypeStruct((m, n), out_dtype),
        scratch_shapes=scratch,
        compiler_params=_cparams("arbitrary", "arbitrary"),
        name="mm",
    )(a, w)


def _mm_res_ln_kernel(*refs, n_in, tk_steps, tr):
    a_refs = refs[:n_in]
    w_refs = refs[n_in:2 * n_in]
    res_ref, lnw_ref, lnb_ref, o32_ref, o16_ref = refs[2 * n_in:2 * n_in + 5]
    acc_ref = refs[2 * n_in + 5] if tk_steps > 1 else None
    kk = pl.program_id(1)
    tm = res_ref.shape[0]

    def partial_sum(rows):
        tot = None
        for a_ref, w_ref in zip(a_refs, w_refs):
            d = _dot(a_ref[rows, :], w_ref[...])
            tot = d if tot is None else tot + d
        return tot

    if tk_steps > 1:
        @pl.when(kk == 0)
        def _():
            acc_ref[...] = partial_sum(slice(None))

        @pl.when((kk > 0) & (kk < tk_steps - 1))
        def _():
            acc_ref[...] += partial_sum(slice(None))

    @pl.when(kk == tk_steps - 1)
    def _():
        for r0 in range(0, tm, tr):
            rows = slice(r0, r0 + tr)
            y = DEEPNORM_ALPHA * res_ref[rows, :] + partial_sum(rows)
            if tk_steps > 1:
                y = y + acc_ref[rows, :]
            mu = jnp.mean(y, axis=-1, keepdims=True)
            d = y - mu
            var = jnp.mean(d * d, axis=-1, keepdims=True)
            out = d * lax.rsqrt(var + LN_EPS) * lnw_ref[...] + lnb_ref[...]
            o32_ref[rows, :] = out
            o16_ref[rows, :] = out.astype(BF16)


def _mm_res_ln(a_list, w_list, res, lnw, lnb, tm=512, tk=None, tr=128):
    m = res.shape[0]
    n = res.shape[1]
    k = a_list[0].shape[1]
    tm = min(tm, m)
    tk = k if tk is None else min(tk, k)
    tr = min(tr, tm)
    n_in = len(a_list)
    steps = k // tk
    in_specs = ([pl.BlockSpec((tm, tk), lambda i, kk: (i, kk)) for _ in a_list]
                + [pl.BlockSpec((tk, n), lambda i, kk: (kk, 0)) for _ in w_list]
                + [pl.BlockSpec((tm, n), lambda i, kk: (i, 0)),
                   pl.BlockSpec((1, n), lambda i, kk: (0, 0)),
                   pl.BlockSpec((1, n), lambda i, kk: (0, 0))])
    return pl.pallas_call(
        functools.partial(_mm_res_ln_kernel, n_in=n_in, tk_steps=steps, tr=tr),
        grid=(m // tm, steps),
        in_specs=in_specs,
        out_specs=[pl.BlockSpec((tm, n), lambda i, kk: (i, 0)),
                   pl.BlockSpec((tm, n), lambda i, kk: (i, 0))],
        out_shape=[jax.ShapeDtypeStruct((m, n), F32), jax.ShapeDtypeStruct((m, n), BF16)],
        scratch_shapes=[pltpu.VMEM((tm, n), F32)] if steps > 1 else [],
        compiler_params=_cparams("arbitrary", "arbitrary"),
        name="mm_res_ln",
    )(*a_list, *w_list, res, lnw.reshape(1, n), lnb.reshape(1, n))


def _xattn_kernel(q_ref, k_ref, v_ref, o_ref, *, tr):
    kb = k_ref[0]
    vb = v_ref[0]
    for r0 in range(0, q_ref.shape[1], tr):
        s = _dot_nt(q_ref[0, r0:r0 + tr, :], kb) * (XA_HD ** -0.5)
        m = jnp.max(s, axis=-1, keepdims=True)
        p = jnp.exp(s - m)
        l = jnp.sum(p, axis=-1, keepdims=True)
        o = _dot(_bf(p / l), vb)
        o_ref[0, r0:r0 + tr, :] = o.astype(o_ref.dtype)


def _xattn(q, kv, ts=2048):
    b, s, d = q.shape
    mlen = kv.shape[1]
    ts = min(ts, s)
    return pl.pallas_call(
        functools.partial(_xattn_kernel, tr=min(256, ts)),
        grid=(b, s // ts, XA_HEADS),
        in_specs=[pl.BlockSpec((1, ts, XA_HD), lambda bi, i, h: (bi, i, h)),
                  pl.BlockSpec((1, mlen, XA_HD), lambda bi, i, h: (bi, 0, h)),
                  pl.BlockSpec((1, mlen, XA_HD), lambda bi, i, h: (bi, 0, XA_HEADS + h))],
        out_specs=pl.BlockSpec((1, ts, XA_HD), lambda bi, i, h: (bi, i, h)),
        out_shape=jax.ShapeDtypeStruct((b, s, d), BF16),
        compiler_params=_cparams("arbitrary", "arbitrary", "arbitrary"),
        name="xattn",
    )(q, kv, kv)


def _mm_taps_kernel(x_ref, w_ref, t_ref, o_ref, sc_ref, wb_ref=None, *, ts, ntap, act):
    first = pl.program_id(2) == 0

    @pl.when(first)
    def _():
        sc_ref[0:HIST, :] = jnp.zeros((HIST, sc_ref.shape[1]), F32)

    w = _weight_tile(w_ref, wb_ref, first & (pl.program_id(1) == 0))
    sc_ref[HIST:HIST + ts, :] = _dot(x_ref[0], w[...])
    y = t_ref[ntap - 1:ntap, :] * sc_ref[HIST:HIST + ts, :]
    for back in range(1, ntap):
        tap = ntap - 1 - back
        y = y + t_ref[tap:tap + 1, :] * sc_ref[HIST - back:HIST - back + ts, :]
    if act == "silu":
        y = y * _sigmoid(y)
    elif act == "lora":
        p0, p1, _ = RWKV_LORA_PAD
        y = jnp.concatenate([jnp.tanh(y[:, :p0]), y[:, p0:p0 + p1], _sigmoid(y[:, p0 + p1:])], axis=1)
    o_ref[0] = y.astype(o_ref.dtype)
    _carry_rows(sc_ref, ts)


def _mm_taps(xb, w, taps, out_dtype, act=None, ts=1024, tn=1024, col0=0, n=None):
    b, s, d = xb.shape
    n, tn, joff, cast_scratch = _weight_cols(w, n, col0, tn)
    ts = min(ts, s)
    ntap = taps.shape[0]
    return pl.pallas_call(
        functools.partial(_mm_taps_kernel, ts=ts, ntap=ntap, act=act),
        grid=(n // tn, b, s // ts),
        in_specs=[pl.BlockSpec((1, ts, d), lambda j, bi, i: (bi, i, 0)),
                  pl.BlockSpec((d, tn), lambda j, bi, i: (0, joff + j)),
                  pl.BlockSpec((ntap, tn), lambda j, bi, i: (0, j))],
        out_specs=pl.BlockSpec((1, ts, tn), lambda j, bi, i: (bi, i, j)),
        out_shape=jax.ShapeDtypeStruct((b, s, n), out_dtype),
        scratch_shapes=[pltpu.VMEM((ts + HIST, tn), F32)] + cast_scratch,
        compiler_params=_cparams("arbitrary", "arbitrary", "arbitrary"),
        name="mm_taps",
    )(xb, w, taps.astype(F32))


def _ffn_up_kernel(x_ref, wg_ref, wu_ref, cg_ref, cu_ref, o_ref, sg_ref, su_ref, wgb_ref=None, wub_ref=None,
                   *, ts, tr):
    first = pl.program_id(2) == 0

    @pl.when(first)
    def _():
        sg_ref[0:HIST, :] = jnp.zeros((HIST, sg_ref.shape[1]), F32)
        su_ref[0:HIST, :] = jnp.zeros((HIST, su_ref.shape[1]), F32)

    new_tile = first & (pl.program_id(1) == 0)
    wg_ref = _weight_tile(wg_ref, wgb_ref, new_tile)
    wu_ref = _weight_tile(wu_ref, wub_ref, new_tile)

    def conv(sc_ref, c_ref, r0):
        y = c_ref[FFN_CONV - 1:FFN_CONV, :] * sc_ref[HIST + r0:HIST + r0 + tr, :]
        for back in range(1, FFN_CONV):
            tap = FFN_CONV - 1 - back
            y = y + c_ref[tap:tap + 1, :] * sc_ref[HIST + r0 - back:HIST + r0 - back + tr, :]
        return y

    for r0 in range(0, ts, tr):
        x = x_ref[0, r0:r0 + tr, :]
        sg_ref[HIST + r0:HIST + r0 + tr, :] = _dot(x, wg_ref[...])
        g = conv(sg_ref, cg_ref, r0)
        g = g * _sigmoid(g)
        su_ref[HIST + r0:HIST + r0 + tr, :] = _dot(x, wu_ref[...])
        u = conv(su_ref, cu_ref, r0)
        o_ref[0, r0:r0 + tr, :] = (g * u).astype(o_ref.dtype)
    _carry_rows(sg_ref, ts)
    _carry_rows(su_ref, ts)


def _ffn_up(xb, w_up, conv_w, ts=1024, tn=512, tr=1024):
    b, s, d = xb.shape
    ts = min(ts, s)
    nj = D_FF // tn
    cast_scratch = [] if w_up.dtype == BF16 else [pltpu.VMEM((d, tn), BF16), pltpu.VMEM((d, tn), BF16)]
    return pl.pallas_call(
        functools.partial(_ffn_up_kernel, ts=ts, tr=min(tr, ts)),
        grid=(nj, b, s // ts),
        in_specs=[pl.BlockSpec((1, ts, d), lambda j, bi, i: (bi, i, 0)),
                  pl.BlockSpec((d, tn), lambda j, bi, i: (0, j)),
                  pl.BlockSpec((d, tn), lambda j, bi, i: (0, nj + j)),
                  pl.BlockSpec((FFN_CONV, tn), lambda j, bi, i: (0, j)),
                  pl.BlockSpec((FFN_CONV, tn), lambda j, bi, i: (0, nj + j))],
        out_specs=pl.BlockSpec((1, ts, tn), lambda j, bi, i: (bi, i, j)),
        out_shape=jax.ShapeDtypeStruct((b, s, D_FF), BF16),
        scratch_shapes=[pltpu.VMEM((ts + HIST, tn), F32), pltpu.VMEM((ts + HIST, tn), F32)] + cast_scratch,
        compiler_params=_cparams("arbitrary", "arbitrary", "arbitrary"),
        name="ffn_up",
    )(xb, w_up, w_up, conv_w, conv_w)


def _pool_kernel(u_ref, w_ref, sc_ref, o_ref, st_ref, *, ts):
    i = pl.program_id(1)

    @pl.when(i == 0)
    def _():
        st_ref[0:2 * HIST, :] = jnp.zeros((2 * HIST, st_ref.shape[1]), F32)

    st_ref[2 * HIST:2 * HIST + ts, :] = u_ref[0].astype(F32)
    pos = (i * ts + 1 + lax.broadcasted_iota(jnp.int32, (ts, 1), 0)).astype(F32)
    base = 2 * HIST
    for g, win in enumerate(POOL_WINDOWS):
        cols = slice(g * POOL_GDIM, (g + 1) * POOL_GDIM)
        cur = st_ref[base:base + ts, cols]
        acc = cur
        for back in range(1, win):
            acc = acc + st_ref[base - back:base - back + ts, cols]
        mean = acc / jnp.minimum(pos, float(win))
        y = _dot(_bf(mean - cur), w_ref[g])
        o_ref[0, :, cols] = (y * sc_ref[:, cols]).astype(o_ref.dtype)
    st_ref[0:2 * HIST, :] = st_ref[ts:ts + 2 * HIST, :]


def _pool_mixer(h_main, pool_w, pool_scale, ts=512):
    b, s, _ = h_main.shape
    ts = min(ts, s)
    return pl.pallas_call(
        functools.partial(_pool_kernel, ts=ts),
        grid=(b, s // ts),
        in_specs=[pl.BlockSpec((1, ts, HALF), lambda bi, i: (bi, i, 0)),
                  pl.BlockSpec((len(POOL_WINDOWS), POOL_GDIM, POOL_GDIM), lambda bi, i: (0, 0, 0)),
                  pl.BlockSpec((1, HALF), lambda bi, i: (0, 0))],
        out_specs=pl.BlockSpec((1, ts, HALF), lambda bi, i: (bi, i, 0)),
        out_shape=jax.ShapeDtypeStruct((b, s, HALF), BF16),
        scratch_shapes=[pltpu.VMEM((ts + 2 * HIST, HALF), F32)],
        compiler_params=_cparams("arbitrary", "arbitrary"),
        name="pool_mixer",
    )(h_main, pool_w, pool_scale.reshape(1, HALF))


def _gdn_kernel(q_ref, k_ref, v_ref, z_ref, g_ref, alog_ref, dtb_ref, nw_ref, o_ref, st_ref, *, ts, hps):
    c = CHUNK
    h0 = pl.program_id(1)

    @pl.when(pl.program_id(2) == 0)
    def _():
        st_ref[...] = jnp.zeros_like(st_ref)

    q_all = q_ref[0]
    k_all = k_ref[0]
    v_all = v_ref[0]
    z_all = z_ref[0].astype(F32)

    logits = g_ref[0]
    lane = lax.broadcasted_iota(jnp.int32, logits.shape, 1)
    g_all = -jnp.exp(alog_ref[...]) * _softplus(logits + dtb_ref[...])
    sig_all = _sigmoid(logits)

    causal, strict = _tri_masks(c)
    tril = causal.astype(BF16)
    eye = jnp.where(causal & (~strict), 1.0, 0.0).astype(F32)
    lane_c = lax.broadcasted_iota(jnp.int32, (c, GDN_DK), 1)
    nw = nw_ref[...]
    chunk_rows = [slice(ci * c, (ci + 1) * c) for ci in range(ts // c)]
    nck = len(chunk_rows)

    q, k, v, betas, g_cols = [], [], [], [], []
    for hd in range(hps):
        hl = slice(hd * GDN_DK, (hd + 1) * GDN_DK)
        qh, kh = q_all[:, hl], k_all[:, hl]
        qh = qh * lax.rsqrt(jnp.sum(qh * qh, axis=-1, keepdims=True) + NORM_EPS) * (GDN_DK ** -0.5)
        kh = kh * lax.rsqrt(jnp.sum(kh * kh, axis=-1, keepdims=True) + NORM_EPS)
        head = h0 * hps + hd
        g_col = jnp.sum(jnp.where(lane == head, g_all, 0.0), axis=1, keepdims=True)
        b_col = jnp.sum(jnp.where(lane == head + GDN_HEADS, sig_all, 0.0), axis=1, keepdims=True)
        for r in chunk_rows:
            q.append(qh[r])
            k.append(kh[r])
            v.append(v_all[r, hl])
            betas.append(jnp.broadcast_to(b_col[r], (c, GDN_DK)))
            g_cols.append(jnp.broadcast_to(g_col[r], (c, GDN_DK)))
    nprob = len(q)
    rows = list(range(nprob))
    gcs = [_dot01(tril, gcol) for gcol in g_cols]
    diffs = []
    for gc in gcs:
        g1, g2, g3 = (p.astype(F32) for p in _split3(gc))
        lhs = jnp.where(lane_c == 0, g1, jnp.where(lane_c == 1, g2, jnp.where(lane_c == 2, g3,
              jnp.where(lane_c < 6, 1.0, 0.0))))
        rhs = jnp.where(lane_c < 3, 1.0, jnp.where(lane_c == 3, -g1, jnp.where(lane_c == 4, -g2,
              jnp.where(lane_c == 5, -g3, 0.0))))
        diffs.append(_dot_nt(_bf(lhs), _bf(rhs)))
    decays = [jnp.where(causal, jnp.exp(jnp.where(causal, d, 0.0)), 0.0) for d in diffs]
    kbs = [k[r] * bt for r, bt in zip(rows, betas)]
    kcbs = [_bf(k[r]) for r in rows]
    lmats = [jnp.where(strict, _dot_nt(_bf(kb), kcb) * dc, 0.0) for kb, kcb, dc in zip(kbs, kcbs, decays)]
    intras = [_bf(jnp.where(causal, _dot_nt(_bf(q[r]), kcb) * dc, 0.0)) for r, kcb, dc in zip(rows, kcbs, decays)]
    tmats = [_bf(t) for t in _unit_lower_inverses([-l for l in lmats], eye)]
    egs = [jnp.exp(gc) for gc in gcs]
    g_lasts = [gc[c - 1:c, :] for gc in gcs]
    uws = [_dot(t, jnp.concatenate([_bf(v[r] * bt), _bf(kb * eg)], axis=1))
           for t, r, bt, kb, eg in zip(tmats, rows, betas, kbs, egs)]
    us = [uw[:, :GDN_DK] for uw in uws]
    ws = [_bf(uw[:, GDN_DK:]) for uw in uws]
    q_decs = [_bf(q[r] * eg) for r, eg in zip(rows, egs)]
    k_decs = [_bf(k[r] * jnp.exp(gl - gc)) for r, gl, gc in zip(rows, g_lasts, gcs)]
    e_lasts = [jnp.exp(gl) for gl in g_lasts]
    mns = [_dot_tn(kd, jnp.concatenate([w, _bf(u)], axis=1)) for kd, w, u in zip(k_decs, ws, us)]
    mmats = [_bf(mn[:, :GDN_DK]) for mn in mns]
    nmats = [mn[:, GDN_DK:] for mn in mns]

    states = [st_ref[hd] for hd in range(hps)]
    sb_hist = [[None] * nck for _ in range(hps)]
    for ci in range(nck):
        for hd in range(hps):
            p = hd * nck + ci
            sb = _bf(states[hd])
            sb_hist[hd][ci] = sb
            states[hd] = states[hd] * e_lasts[p] - _dot(mmats[p], sb) + nmats[p]
    for hd in range(hps):
        st_ref[hd] = states[hd]
    for hd in range(hps):
        hl = slice(hd * GDN_DK, (hd + 1) * GDN_DK)
        for ci in range(nck):
            p = hd * nck + ci
            sb = sb_hist[hd][ci]
            v_new = _bf(us[p] - _dot(ws[p], sb))
            o = _dot(q_decs[p], sb) + _dot(intras[p], v_new)
            o = o * lax.rsqrt(jnp.mean(o * o, axis=-1, keepdims=True) + NORM_EPS) * nw
            zc = z_all[chunk_rows[ci], hl]
            o_ref[0, chunk_rows[ci], hl] = (o * (zc * _sigmoid(zc))).astype(o_ref.dtype)


def _gated_deltanet(qkv, z, logits, a_log, dt_bias, norm_w, ts=256, hps=8):
    b, s, _ = qkv.shape
    ts = min(ts, s)
    hd = GDN_DK
    gw = hps * hd
    ng = HALF // gw
    pad = lambda t: jnp.pad(t.astype(F32), (0, hd - t.shape[0])).reshape(1, hd)
    col = lambda grp: (lambda bi, h, i: (bi, i, grp * ng + h))
    return pl.pallas_call(
        functools.partial(_gdn_kernel, ts=ts, hps=hps),
        grid=(b, ng, s // ts),
        in_specs=[pl.BlockSpec((1, ts, gw), col(0)), pl.BlockSpec((1, ts, gw), col(1)),
                  pl.BlockSpec((1, ts, gw), col(2)), pl.BlockSpec((1, ts, gw), col(0)),
                  pl.BlockSpec((1, ts, hd), lambda bi, h, i: (bi, i, 0)),
                  pl.BlockSpec((1, hd), lambda bi, h, i: (0, 0)),
                  pl.BlockSpec((1, hd), lambda bi, h, i: (0, 0)),
                  pl.BlockSpec((1, hd), lambda bi, h, i: (0, 0))],
        out_specs=pl.BlockSpec((1, ts, gw), lambda bi, h, i: (bi, i, h)),
        out_shape=jax.ShapeDtypeStruct((b, s, HALF), BF16),
        scratch_shapes=[pltpu.VMEM((hps, hd, hd), F32)],
        compiler_params=_cparams("arbitrary", "arbitrary", "arbitrary"),
        name="gated_deltanet",
    )(qkv, qkv, qkv, z, logits, pad(a_log), pad(dt_bias), norm_w.astype(F32).reshape(1, hd))


def _rwkv_kernel(r_ref, k_ref, v_ref, l_ref, w2_ref, a2_ref, g2_ref,
                 w0_ref, a0_ref, kk_ref, ka_ref, rk_ref, lnw_ref, lnb_ref, o_ref, st_ref, *, ts, hps):
    c = CHUNK
    hs = RWKV_HEAD

    @pl.when(pl.program_id(2) == 0)
    def _():
        st_ref[...] = jnp.zeros_like(st_ref)

    r = r_ref[0]
    k = k_ref[0]
    v = v_ref[0]
    lo = l_ref[0]
    p0, p1, p2 = RWKV_LORA_PAD
    w_log = -_softplus(-(w0_ref[...] + _dot(lo[:, :p0], w2_ref[...]))) - 0.5
    lw = -jnp.exp(w_log)
    a = _sigmoid(a0_ref[...] + _dot(lo[:, p0:p0 + p1], a2_ref[...]))
    g = _dot(lo[:, p0 + p1:p0 + p1 + p2], g2_ref[...])
    kk_raw = k * kk_ref[...]
    km = k * (1.0 + (a - 1.0) * ka_ref[...])
    head_of_lane = lax.broadcasted_iota(jnp.int32, (ts, hps * hs), 1) // hs

    def per_head_sum(x):
        out = jnp.zeros_like(x)
        for hd in range(hps):
            sel = head_of_lane == hd
            out = jnp.where(sel, jnp.sum(jnp.where(sel, x, 0.0), axis=-1, keepdims=True), out)
        return out

    kk = kk_raw * lax.rsqrt(per_head_sum(kk_raw * kk_raw) + NORM_EPS)
    aa = -kk
    bb = kk * a
    bonus = per_head_sum(r * km * rk_ref[...]) * v

    causal, strict = _tri_masks(c)
    tril = causal.astype(BF16)
    eye = jnp.where(causal & (~strict), 1.0, 0.0).astype(F32)
    lnw = lnw_ref[...]
    lnb = lnb_ref[...]

    rows = [slice(ci * c, (ci + 1) * c) for ci in range(ts // c)]
    nck = len(rows)
    heads = [slice(hd * hs, (hd + 1) * hs) for hd in range(hps)]
    cums = [_dot01(tril, lw[rw]) for rw in rows]
    c_lasts = [cum[c - 1:c, :] for cum in cums]
    w_invs = [jnp.exp(-cum) for cum in cums]
    w_tails = [jnp.exp(cl - cum) for cl, cum in zip(c_lasts, cums)]
    a_ts = [_bf(aa[rw] * jnp.exp(cum - lw[rw])) for rw, cum in zip(rows, cums)]
    r_ts = [_bf(r[rw] * jnp.exp(cum)) for rw, cum in zip(rows, cums)]
    b_ts = [_bf(bb[rw] * wi) for rw, wi in zip(rows, w_invs)]
    k_ts = [_bf(km[rw] * wi) for rw, wi in zip(rows, w_invs)]
    b_hs = [_bf(bb[rw] * wt) for rw, wt in zip(rows, w_tails)]
    k_hs = [_bf(km[rw] * wt) for rw, wt in zip(rows, w_tails)]
    vcs = [_bf(v[rw]) for rw in rows]
    e_lasts = [jnp.exp(cl) for cl in c_lasts]
    prob = [(ci, cs) for ci in range(nck) for cs in heads]
    row2 = lax.broadcasted_iota(jnp.int32, (c, 2 * c), 0)
    col2 = lax.broadcasted_iota(jnp.int32, (c, 2 * c), 1) % c
    bks = [jnp.concatenate([b_ts[ci][:, cs], k_ts[ci][:, cs]], axis=0) for ci, cs in prob]
    a_abks = [jnp.where(row2 > col2, _dot_nt(a_ts[ci][:, cs], bk), 0.0) for bk, (ci, cs) in zip(bks, prob)]
    a_rbks = [_bf(jnp.where(row2 >= col2, _dot_nt(r_ts[ci][:, cs], bk), 0.0)) for bk, (ci, cs) in zip(bks, prob)]
    tmats = [_bf(t) for t in _unit_lower_inverses([m[:, :c] for m in a_abks], eye)]
    zero_c = jnp.zeros((c, hs), BF16)
    avs = [_bf(_dot(_bf(abk), jnp.concatenate([zero_c, vcs[ci][:, cs]], axis=0)))
           for abk, (ci, cs) in zip(a_abks, prob)]
    pmats = [_bf(_dot(t, a_ts[ci][:, cs])) for t, (ci, cs) in zip(tmats, prob)]
    qmats = [_dot(t, av) for t, av in zip(tmats, avs)]
    mmats = [_bf(_dot_tn(pm, b_hs[ci][:, cs])) for pm, (ci, cs) in zip(pmats, prob)]
    nmats = [_dot_tn(jnp.concatenate([_bf(qm), vcs[ci][:, cs]], axis=0),
                     jnp.concatenate([b_hs[ci][:, cs], k_hs[ci][:, cs]], axis=0))
             for qm, (ci, cs) in zip(qmats, prob)]

    states = [st_ref[hd] for hd in range(hps)]
    sb_hist = []
    for ci in range(nck):
        sbs = [_bf(s) for s in states]
        sb_hist.append(sbs)
        states = [states[hd] * e_lasts[ci][:, heads[hd]] + _dot(sbs[hd], mmats[hps * ci + hd]) + nmats[hps * ci + hd]
                  for hd in range(hps)]
    for hd in range(hps):
        st_ref[hd] = states[hd]
    ys = []
    for ci in range(nck):
        sbs = sb_hist[ci]
        ubs = [_bf(_dot_nt(pmats[hps * ci + hd], sbs[hd]) + qmats[hps * ci + hd]) for hd in range(hps)]
        ys.append([_dot_nt(r_ts[ci][:, heads[hd]], sbs[hd])
                   + _dot(a_rbks[hps * ci + hd], jnp.concatenate([ubs[hd], vcs[ci][:, heads[hd]]], axis=0))
                   for hd in range(hps)])
    for ci in range(nck):
        outs = []
        for y in ys[ci]:
            ym = jnp.mean(y, axis=-1, keepdims=True)
            yd = y - ym
            yv = jnp.mean(yd * yd, axis=-1, keepdims=True)
            outs.append(yd * lax.rsqrt(yv + RWKV_LNX_EPS))
        yn = jnp.concatenate(outs, axis=-1) * lnw + lnb
        o_ref[0, rows[ci], :] = ((yn + bonus[rows[ci]]) * g[rows[ci]]).astype(o_ref.dtype)


def _rwkv7(hr, hl, w2p, a2p, g2p, w0, a0, k_k, k_a, r_k, lnx_w, lnx_b, ts=128, hps=16):
    b, s, _ = hr.shape
    ts = min(ts, s)
    pw = hps * RWKV_HEAD
    npair = HALF // pw
    lw = sum(RWKV_LORA_PAD)
    col = lambda grp: (lambda bi, hp, i: (bi, i, grp * npair + hp))
    vec = lambda grp: pl.BlockSpec((1, pw), lambda bi, hp, i: (0, grp * npair + hp))
    row1 = lambda t: t.astype(F32).reshape(1, -1)
    return pl.pallas_call(
        functools.partial(_rwkv_kernel, ts=ts, hps=hps),
        grid=(b, npair, s // ts),
        in_specs=[pl.BlockSpec((1, ts, pw), col(0)), pl.BlockSpec((1, ts, pw), col(1)),
                  pl.BlockSpec((1, ts, pw), col(2)),
                  pl.BlockSpec((1, ts, lw), lambda bi, hp, i: (bi, i, 0)),
                  pl.BlockSpec((RWKV_LORA_PAD[0], pw), lambda bi, hp, i: (0, hp)),
                  pl.BlockSpec((RWKV_LORA_PAD[1], pw), lambda bi, hp, i: (0, hp)),
                  pl.BlockSpec((RWKV_LORA_PAD[2], pw), lambda bi, hp, i: (0, hp)),
                  vec(0), vec(0), vec(0), vec(0), vec(0), vec(0), vec(0)],
        out_specs=pl.BlockSpec((1, ts, pw), lambda bi, hp, i: (bi, i, hp)),
        out_shape=jax.ShapeDtypeStruct((b, s, HALF), BF16),
        scratch_shapes=[pltpu.VMEM((hps, RWKV_HEAD, RWKV_HEAD), F32)],
        compiler_params=_cparams("arbitrary", "arbitrary", "arbitrary"),
        name="rwkv7",
    )(hr, hr, hr, hl, w2p, a2p, g2p,
      row1(w0), row1(a0), row1(k_k), row1(k_a), row1(r_k), row1(lnx_w), row1(lnx_b))


def _fox_cumsum_kernel(f_ref, bf_ref, o_ref):
    x = f_ref[0] + bf_ref[...]
    ls = jnp.minimum(x, 0.0) - jnp.log(1.0 + jnp.exp(-jnp.abs(x)))
    nh, nr, nl = ls.shape
    li = lax.broadcasted_iota(jnp.int32, (nl, nl), 0)
    lj = lax.broadcasted_iota(jnp.int32, (nl, nl), 1)
    upper = (li <= lj).astype(BF16)
    ri = lax.broadcasted_iota(jnp.int32, (nr, nr), 0)
    rj = lax.broadcasted_iota(jnp.int32, (nr, nr), 1)
    below = (ri > rj).astype(BF16)
    for h in range(nh):
        within = _dot01_right(ls[h], upper)
        tot = jnp.broadcast_to(within[:, nl - 1:nl], (nr, nl))
        o_ref[0, h] = within + _dot01(below, tot)


def _dot01_right(x, m01):
    x1, x2, x3 = _split3(x)
    return _dot(x1, m01) + _dot(x2, m01) + _dot(x3, m01)


def _fox_cumsum(f_t, b_f):
    b, nh, s = f_t.shape
    nl = 128
    nr = s // nl
    out = pl.pallas_call(
        _fox_cumsum_kernel,
        grid=(b,),
        in_specs=[pl.BlockSpec((1, nh, nr, nl), lambda bi: (bi, 0, 0, 0)),
                  pl.BlockSpec((nh, 1, 1), lambda bi: (0, 0, 0))],
        out_specs=pl.BlockSpec((1, nh, nr, nl), lambda bi: (bi, 0, 0, 0)),
        out_shape=jax.ShapeDtypeStruct((b, nh, nr, nl), F32),
        compiler_params=_cparams("arbitrary"),
        name="fox_cumsum",
    )(f_t.reshape(b, nh, nr, nl), b_f.astype(F32).reshape(nh, 1, 1))
    return out.reshape(b, nh, 1, s)


def _fox_kernel(qi_ref, ki_ref, q_ref, k_ref, v_ref, cq_ref, ck_ref, o_ref, qs_ref, m_ref, acc_ref, *, tq, tk, tr):
    p = pl.program_id(2)
    qi = qi_ref[p]
    ki = ki_ref[p]
    last_k = ((qi + 1) * tq - 1) // tk
    log2e = 1.0 / math.log(2.0)

    @pl.when(ki == 0)
    def _():
        qs_ref[...] = (q_ref[0].astype(F32) * (FOX_HD ** -0.5 * log2e)).astype(BF16)
        m_ref[...] = jnp.full_like(m_ref, -jnp.inf)
        acc_ref[...] = jnp.zeros_like(acc_ref)

    def step(masked):
        bias = (cq_ref[0, 0, :, 0:1] - ck_ref[0, 0]) * log2e
        v_aug = jnp.concatenate([v_ref[0], jnp.ones((tk, FOX_HD), BF16)], axis=1)
        kb = k_ref[0]
        for r0 in range(0, tq, tr):
            rows = slice(r0, r0 + tr)
            s = _dot_nt(qs_ref[rows, :], kb) + bias
            if masked:
                qpos = qi * tq + r0 + lax.broadcasted_iota(jnp.int32, (tr, tk), 0)
                kpos = ki * tk + lax.broadcasted_iota(jnp.int32, (tr, tk), 1)
                s = jnp.where(kpos <= qpos, s, -jnp.inf)
            m_old = m_ref[rows, :]
            m_new = jnp.maximum(m_old, jnp.max(s, axis=-1, keepdims=True))
            corr = jnp.exp2(m_old - m_new)
            pexp = _bf(jnp.exp2(s - m_new))
            acc_ref[rows, :] = corr * acc_ref[rows, :] + _dot(pexp, v_aug)
            m_ref[rows, :] = m_new

    needs_mask = (ki + 1) * tk - 1 > qi * tq

    @pl.when(needs_mask)
    def _():
        step(True)

    @pl.when(jnp.logical_not(needs_mask))
    def _():
        step(False)

    @pl.when(ki == last_k)
    def _():
        acc = acc_ref[...]
        o_ref[0] = (acc[:, :FOX_HD] / acc[:, FOX_HD:FOX_HD + 1]).astype(o_ref.dtype)


def _fox_attention(hf, c, tq=2048, tk=2048, tr=128):
    b, s, _ = hf.shape
    tq = min(tq, s)
    tk = min(tk, s)
    nh = FOX_HEADS
    pairs = [(qi, ki) for qi in range(s // tq) for ki in range(((qi + 1) * tq - 1) // tk + 1)]
    qi_arr = jnp.asarray(np.array([p[0] for p in pairs], np.int32))
    ki_arr = jnp.asarray(np.array([p[1] for p in pairs], np.int32))
    grid_spec = pltpu.PrefetchScalarGridSpec(
        num_scalar_prefetch=2,
        grid=(b, nh, len(pairs)),
        in_specs=[pl.BlockSpec((1, tq, FOX_HD), lambda bi, h, p, qa, ka: (bi, qa[p], h)),
                  pl.BlockSpec((1, tk, FOX_HD), lambda bi, h, p, qa, ka: (bi, ka[p], nh + h)),
                  pl.BlockSpec((1, tk, FOX_HD), lambda bi, h, p, qa, ka: (bi, ka[p], 2 * nh + h)),
                  pl.BlockSpec((1, 1, 1, tq), lambda bi, h, p, qa, ka: (bi, h, 0, qa[p])),
                  pl.BlockSpec((1, 1, 1, tk), lambda bi, h, p, qa, ka: (bi, h, 0, ka[p]))],
        out_specs=pl.BlockSpec((1, tq, FOX_HD), lambda bi, h, p, qa, ka: (bi, qa[p], h)),
        scratch_shapes=[pltpu.VMEM((tq, FOX_HD), BF16), pltpu.VMEM((tq, 1), F32),
                        pltpu.VMEM((tq, 2 * FOX_HD), F32)])
    return pl.pallas_call(
        functools.partial(_fox_kernel, tq=tq, tk=tk, tr=min(tr, tq)),
        grid_spec=grid_spec,
        out_shape=jax.ShapeDtypeStruct((b, s, HALF), BF16),
        compiler_params=_cparams("arbitrary", "arbitrary", "arbitrary"),
        name="fox_attention",
    )(qi_arr, ki_arr, hf, hf, hf, c, c)


def _pad_cols(w, width):
    return jnp.pad(w, ((0, 0), (0, width - w.shape[1])))


def _pad_rows(w, height):
    return jnp.pad(w, ((0, height - w.shape[0]), (0, 0)))


def _even_mixer(x32, xb, w_in, pool_w, pool_scale, conv_w, a_log, dt_bias, norm_w, w_out, ln_w, ln_b):
    b, s, d = x32.shape
    t = b * s
    n_main = 5 * HALF
    w_gate = _bf(_pad_cols(w_in[:, n_main:], 128))
    xb2 = xb.reshape(t, d)
    u_pool = _mm(xb2, w_in, F32, col0=0, n=HALF).reshape(b, s, HALF)
    qkv = _mm_taps(xb, w_in, conv_w, F32, act="silu", col0=HALF, n=3 * HALF)
    z = _mm(xb2, w_in, BF16, col0=4 * HALF, n=HALF).reshape(b, s, HALF)
    logits = _mm(xb2, w_gate, F32, tn=128).reshape(b, s, 128)
    y_a = _pool_mixer(u_pool, _bf(pool_w), pool_scale)
    y_b = _gated_deltanet(qkv, z, logits, a_log, dt_bias, norm_w)
    wo = _bf(w_out)
    return _mm_res_ln([y_a.reshape(t, HALF), y_b.reshape(t, HALF)], [wo[:HALF], wo[HALF:]],
                      x32.reshape(t, d), ln_w, ln_b)


def _odd_mixer(x32, xb, w_in, mu, w0, w2, a0, a2, g2, k_k, k_a, r_k, lnx_w, lnx_b, b_f, w_out, ln_w, ln_b):
    b, s, d = x32.shape
    t = b * s
    l0, l1, l2 = RWKV_LORA
    p0, p1, p2 = RWKV_LORA_PAD
    o_l = 3 * HALF
    o_f = o_l + l0 + l1 + l2
    w_l = _bf(jnp.concatenate([_pad_cols(w_in[:, o_l:o_l + l0], p0),
                               _pad_cols(w_in[:, o_l + l0:o_l + l0 + l1], p1),
                               _pad_cols(w_in[:, o_l + l0 + l1:o_f], p2)], axis=1))
    w_fox = _bf(w_in[:, o_f:o_f + 3 * HALF])
    w_fl = _bf(_pad_cols(w_in[:, o_f + 3 * HALF:], 128))
    mu_l = jnp.concatenate([jnp.pad(mu[o_l:o_l + l0], (0, p0 - l0)),
                            jnp.pad(mu[o_l + l0:o_l + l0 + l1], (0, p1 - l1)),
                            jnp.pad(mu[o_l + l0 + l1:o_f], (0, p2 - l2))])
    xb2 = xb.reshape(t, d)
    lerp = lambda m: jnp.stack([m, 1.0 - m])
    hr = _mm_taps(xb, w_in, lerp(mu[:o_l]), F32, col0=0, n=o_l)
    hl = _mm_taps(xb, w_l, lerp(mu_l), BF16, act="lora", tn=p0 + p1 + p2)
    hf = _mm(xb2, w_fox, BF16).reshape(b, s, 3 * HALF)
    fl = _mm(xb2, w_fl, F32, tn=128).reshape(b, s, 128)
    y_c = _rwkv7(hr, hl, _bf(_pad_rows(w2, p0)), _bf(_pad_rows(a2, p1)), _bf(_pad_rows(g2, p2)),
                 w0, a0, k_k, k_a, r_k, lnx_w, lnx_b)
    c = _fox_cumsum(jnp.transpose(fl[:, :, :FOX_HEADS], (0, 2, 1)), b_f)
    y_d = _fox_attention(hf, c)
    wo = _bf(w_out)
    return _mm_res_ln([y_c.reshape(t, HALF), y_d.reshape(t, HALF)], [wo[:HALF], wo[HALF:]],
                      x32.reshape(t, d), ln_w, ln_b)


def _cross_attention(x32, xb, mem_b, w_q, w_kv, w_o, ln_w, ln_b):
    t, d = x32.shape
    b, mlen, _ = mem_b.shape
    s = t // b
    q = _mm(xb, w_q, BF16).reshape(b, s, d)
    kv = _mm(mem_b.reshape(b * mlen, d), w_kv, BF16).reshape(b, mlen, 2 * d)
    o = _xattn(q, kv).reshape(t, d)
    return _mm_res_ln([o], [_bf(w_o)], x32, ln_w, ln_b)


def _conv_glu_ffn(x32, xb, b, w_up, conv_w, w_down, ln_w, ln_b):
    t, d = x32.shape
    hmid = _ffn_up(xb.reshape(b, t // b, d), w_up, conv_w)
    return _mm_res_ln([hmid.reshape(t, D_FF)], [_bf(w_down)], x32, ln_w, ln_b, tk=D_FF // 2)


def kernel(x, mem, ev_w_in, pool_w, pool_scale, gdn_conv_w, gdn_a_log, gdn_dt_bias, gdn_norm_w, ev_w_out,
           od_w_in, rwkv_mu, rwkv_w0, rwkv_w2, rwkv_a0, rwkv_a2, rwkv_g2, rwkv_k_k, rwkv_k_a, rwkv_r_k,
           rwkv_lnx_w, rwkv_lnx_b, fox_b_f, od_w_out,
           ln_mix_w, ln_mix_b, xa_w_q, xa_w_kv, xa_w_o, ln_xa_w, ln_xa_b,
           ffn_w_up, ffn_conv_w, ffn_w_down, ln_ffn_w, ln_ffn_b):
    b, s, d = x.shape
    depth = ln_mix_w.shape[0]
    mem_b = _bf(mem)
    x32 = x
    xb = _bf(x)
    for i in range(depth):
        j = i // 2
        x3 = x32.reshape(b, s, d)
        xb3 = xb.reshape(b, s, d)
        if i % 2 == 0:
            x32, xb = _even_mixer(x3, xb3, ev_w_in[j], pool_w[j], pool_scale[j], gdn_conv_w[j], gdn_a_log[j],
                                  gdn_dt_bias[j], gdn_norm_w[j], ev_w_out[j], ln_mix_w[i], ln_mix_b[i])
        else:
            x32, xb = _odd_mixer(x3, xb3, od_w_in[j], rwkv_mu[j], rwkv_w0[j], rwkv_w2[j], rwkv_a0[j], rwkv_a2[j],
                                 rwkv_g2[j], rwkv_k_k[j], rwkv_k_a[j], rwkv_r_k[j].reshape(-1), rwkv_lnx_w[j],
                                 rwkv_lnx_b[j], fox_b_f[j], od_w_out[j], ln_mix_w[i], ln_mix_b[i])
        x32, xb = _cross_attention(x32, xb, mem_b, xa_w_q[i], xa_w_kv[i], xa_w_o[i], ln_xa_w[i], ln_xa_b[i])
        x32, xb = _conv_glu_ffn(x32, xb, b, ffn_w_up[i], ffn_conv_w[i], ffn_w_down[i], ln_ffn_w[i], ln_ffn_b[i])
    return x32.reshape(b, s, d)
```

```python
import functools
import math

import numpy as np
import jax
import jax.numpy as jnp
from jax import lax
from jax.experimental import pallas as pl
from jax.experimental.pallas import tpu as pltpu

F32 = jnp.float32
BF16 = jnp.bfloat16

D_MODEL = 2048
HALF = D_MODEL // 2
POOL_WINDOWS = (2, 4, 8, 16)
POOL_GDIM = HALF // len(POOL_WINDOWS)
GDN_HEADS = 8
GDN_DK = HALF // GDN_HEADS
GDN_CONV = 4
NORM_EPS = 1e-6
RWKV_HEAD = 64
RWKV_LNX_EPS = 64e-5
RWKV_LORA = (64, 64, 160)
RWKV_LORA_PAD = (128, 128, 256)
FOX_HEADS = 8
FOX_HD = HALF // FOX_HEADS
XA_HEADS = 4
XA_HD = D_MODEL // XA_HEADS
D_FF = 5632
FFN_CONV = 3
DEPTH = 2
DEEPNORM_ALPHA = float((2 * DEPTH) ** 0.25)
LN_EPS = 1e-5

CHUNK = 64
HIST = 8
VMEM_LIMIT_BYTES = 56 * 1024 * 1024


def _cparams(*sem):
    return pltpu.CompilerParams(dimension_semantics=sem, vmem_limit_bytes=VMEM_LIMIT_BYTES)


def _dot(a, b):
    return jnp.dot(a, b, preferred_element_type=F32)


def _dot_nt(a, b):
    return lax.dot_general(a, b, (((1,), (1,)), ((), ())), preferred_element_type=F32)


def _dot_tn(a, b):
    return lax.dot_general(a, b, (((0,), (0,)), ((), ())), preferred_element_type=F32)


def _bf(x):
    return x.astype(BF16)


def _split3(x):
    x1 = x.astype(BF16)
    r1 = x - x1.astype(F32)
    x2 = r1.astype(BF16)
    x3 = (r1 - x2.astype(F32)).astype(BF16)
    return x1, x2, x3


def _dot01(m01, x):
    x1, x2, x3 = _split3(x)
    return _dot(m01, x1) + _dot(m01, x2) + _dot(m01, x3)


def _unit_lower_inverses(n_mats, eye, fillers=()):
    c = eye.shape[0]
    levels = int(math.log2(c))
    fillers = list(fillers)
    xs = [eye + n for n in n_mats]
    nbs = [_bf(n) for n in n_mats]
    ps = [_dot(nb, nb) for nb in nbs]
    for level in range(1, levels):
        pbs = [_bf(p) for p in ps]
        if level == levels - 1:
            prods = [_dot(_bf(x), pb) for x, pb in zip(xs, pbs)]
        else:
            prods = [_dot(jnp.concatenate([pb, _bf(x)], axis=0), pb) for x, pb in zip(xs, pbs)]
        if fillers:
            fillers.pop(0)()
        if level == levels - 1:
            xs = [x + pr for x, pr in zip(xs, prods)]
        else:
            xs = [x + pr[c:] for x, pr in zip(xs, prods)]
            ps = [pr[:c] for pr in prods]
    for filler in fillers:
        filler()
    return xs


def _sigmoid(x):
    return 1.0 / (1.0 + jnp.exp(-x))


def _softplus(x):
    return jnp.maximum(x, 0.0) + jnp.log(1.0 + jnp.exp(-jnp.abs(x)))


def _tri_masks(c):
    row = lax.broadcasted_iota(jnp.int32, (c, c), 0)
    col = lax.broadcasted_iota(jnp.int32, (c, c), 1)
    return row >= col, row > col


def _stage_rows(x_ref, sc_ref, ts, first):
    @pl.when(first)
    def _():
        sc_ref[0:HIST, :] = jnp.zeros((HIST, sc_ref.shape[1]), F32)
    sc_ref[HIST:HIST + ts, :] = x_ref[0].astype(F32)


def _carry_rows(sc_ref, ts):
    sc_ref[0:HIST, :] = sc_ref[ts:ts + HIST, :]


def _weight_tile(w_ref, wb_ref, first):
    if wb_ref is None:
        return w_ref
    @pl.when(first)
    def _():
        wb_ref[...] = w_ref[...].astype(BF16)
    return wb_ref


def _weight_cols(w, n, col0, tn):
    n = w.shape[-1] - col0 if n is None else n
    tn = min(tn, n)
    assert n % tn == 0 and col0 % tn == 0, (n, col0, tn)
    scratch = [] if w.dtype == BF16 else [pltpu.VMEM((w.shape[-2], tn), BF16)]
    return n, tn, col0 // tn, scratch


def _weight_spec(w, tn, layer, col_of):
    k = w.shape[-2]
    if w.ndim == 2:
        return pl.BlockSpec((k, tn), lambda *g: (0, col_of(*g)))
    return pl.BlockSpec((None, k, tn), lambda *g: (layer, 0, col_of(*g)))


def _mm_kernel(a_ref, w_ref, o_ref, wb_ref=None):
    w = _weight_tile(w_ref, wb_ref, pl.program_id(1) == 0)
    o_ref[...] = _dot(a_ref[...], w[...]).astype(o_ref.dtype)


def _mm(a, w, out_dtype, tm=1024, tn=1024, col0=0, n=None, layer=0):
    m, k = a.shape
    n, tn, joff, scratch = _weight_cols(w, n, col0, tn)
    tm = min(tm, m)
    return pl.pallas_call(
        _mm_kernel,
        grid=(n // tn, m // tm),
        in_specs=[pl.BlockSpec((tm, k), lambda j, i: (i, 0)),
                  _weight_spec(w, tn, layer, lambda j, i: joff + j)],
        out_specs=pl.BlockSpec((tm, tn), lambda j, i: (i, j)),
        out_shape=jax.ShapeDtypeStruct((m, n), out_dtype),
        scratch_shapes=scratch,
        compiler_params=_cparams("arbitrary", "arbitrary"),
        name="mm",
    )(a, w)


def _mm_res_ln_kernel(*refs, n_in, tk_steps, tr):
    a_refs = refs[:n_in]
    w_refs = refs[n_in:2 * n_in]
    res_ref, lnw_ref, lnb_ref, o32_ref, o16_ref = refs[2 * n_in:2 * n_in + 5]
    acc_ref = refs[2 * n_in + 5] if tk_steps > 1 else None
    kk = pl.program_id(1)
    tm = res_ref.shape[0]

    def partial_sum(rows):
        tot = None
        for a_ref, w_ref in zip(a_refs, w_refs):
            d = _dot(a_ref[rows, :], w_ref[...])
            tot = d if tot is None else tot + d
        return tot

    if tk_steps > 1:
        @pl.when(kk == 0)
        def _():
            acc_ref[...] = partial_sum(slice(None))

        @pl.when((kk > 0) & (kk < tk_steps - 1))
        def _():
            acc_ref[...] += partial_sum(slice(None))

    @pl.when(kk == tk_steps - 1)
    def _():
        for r0 in range(0, tm, tr):
            rows = slice(r0, r0 + tr)
            y = DEEPNORM_ALPHA * res_ref[rows, :] + partial_sum(rows)
            if tk_steps > 1:
                y = y + acc_ref[rows, :]
            mu = jnp.mean(y, axis=-1, keepdims=True)
            d = y - mu
            var = jnp.mean(d * d, axis=-1, keepdims=True)
            out = d * lax.rsqrt(var + LN_EPS) * lnw_ref[...] + lnb_ref[...]
            o32_ref[rows, :] = out
            o16_ref[rows, :] = out.astype(BF16)


def _mm_res_ln(a_list, w_list, res, lnw, lnb, tm=512, tk=None, tr=128):
    m = res.shape[0]
    n = res.shape[1]
    k = a_list[0].shape[1]
    tm = min(tm, m)
    tk = k if tk is None else min(tk, k)
    tr = min(tr, tm)
    n_in = len(a_list)
    steps = k // tk
    in_specs = ([pl.BlockSpec((tm, tk), lambda i, kk: (i, kk)) for _ in a_list]
                + [pl.BlockSpec((tk, n), lambda i, kk: (kk, 0)) for _ in w_list]
                + [pl.BlockSpec((tm, n), lambda i, kk: (i, 0)),
                   pl.BlockSpec((1, n), lambda i, kk: (0, 0)),
                   pl.BlockSpec((1, n), lambda i, kk: (0, 0))])
    return pl.pallas_call(
        functools.partial(_mm_res_ln_kernel, n_in=n_in, tk_steps=steps, tr=tr),
        grid=(m // tm, steps),
        in_specs=in_specs,
        out_specs=[pl.BlockSpec((tm, n), lambda i, kk: (i, 0)),
                   pl.BlockSpec((tm, n), lambda i, kk: (i, 0))],
        out_shape=[jax.ShapeDtypeStruct((m, n), F32), jax.ShapeDtypeStruct((m, n), BF16)],
        scratch_shapes=[pltpu.VMEM((tm, n), F32)] if steps > 1 else [],
        compiler_params=_cparams("arbitrary", "arbitrary"),
        name="mm_res_ln",
    )(*a_list, *w_list, res, lnw.reshape(1, n), lnb.reshape(1, n))


def _xattn_kernel(q_ref, k_ref, v_ref, o_ref, *, tr):
    kb = k_ref[0]
    vb = v_ref[0]
    for r0 in range(0, q_ref.shape[1], tr):
        s = _dot_nt(q_ref[0, r0:r0 + tr, :], kb) * (XA_HD ** -0.5)
        m = jnp.max(s, axis=-1, keepdims=True)
        p = jnp.exp(s - m)
        l = jnp.sum(p, axis=-1, keepdims=True)
        o = _dot(_bf(p / l), vb)
        o_ref[0, r0:r0 + tr, :] = o.astype(o_ref.dtype)


def _xattn(q, kv, ts=2048):
    b, s, d = q.shape
    mlen = kv.shape[1]
    ts = min(ts, s)
    return pl.pallas_call(
        functools.partial(_xattn_kernel, tr=min(256, ts)),
        grid=(b, s // ts, XA_HEADS),
        in_specs=[pl.BlockSpec((1, ts, XA_HD), lambda bi, i, h: (bi, i, h)),
                  pl.BlockSpec((1, mlen, XA_HD), lambda bi, i, h: (bi, 0, h)),
                  pl.BlockSpec((1, mlen, XA_HD), lambda bi, i, h: (bi, 0, XA_HEADS + h))],
        out_specs=pl.BlockSpec((1, ts, XA_HD), lambda bi, i, h: (bi, i, h)),
        out_shape=jax.ShapeDtypeStruct((b, s, d), BF16),
        compiler_params=_cparams("arbitrary", "arbitrary", "arbitrary"),
        name="xattn",
    )(q, kv, kv)


def _mm_taps_kernel(x_ref, w_ref, t_ref, o_ref, sc_ref, wb_ref=None, *, ts, ntap, act):
    first = pl.program_id(2) == 0

    @pl.when(first)
    def _():
        sc_ref[0:HIST, :] = jnp.zeros((HIST, sc_ref.shape[1]), F32)

    w = _weight_tile(w_ref, wb_ref, first & (pl.program_id(1) == 0))
    sc_ref[HIST:HIST + ts, :] = _dot(x_ref[0], w[...])
    y = t_ref[ntap - 1:ntap, :] * sc_ref[HIST:HIST + ts, :]
    for back in range(1, ntap):
        tap = ntap - 1 - back
        y = y + t_ref[tap:tap + 1, :] * sc_ref[HIST - back:HIST - back + ts, :]
    if act == "silu":
        y = y * _sigmoid(y)
    elif act == "lora":
        p0, p1, _ = RWKV_LORA_PAD
        y = jnp.concatenate([jnp.tanh(y[:, :p0]), y[:, p0:p0 + p1], _sigmoid(y[:, p0 + p1:])], axis=1)
    o_ref[0] = y.astype(o_ref.dtype)
    _carry_rows(sc_ref, ts)


def _mm_taps(xb, w, taps, out_dtype, act=None, ts=1024, tn=1024, col0=0, n=None, layer=0):
    b, s, d = xb.shape
    n, tn, joff, cast_scratch = _weight_cols(w, n, col0, tn)
    ts = min(ts, s)
    ntap = taps.shape[0]
    return pl.pallas_call(
        functools.partial(_mm_taps_kernel, ts=ts, ntap=ntap, act=act),
        grid=(n // tn, b, s // ts),
        in_specs=[pl.BlockSpec((1, ts, d), lambda j, bi, i: (bi, i, 0)),
                  _weight_spec(w, tn, layer, lambda j, bi, i: joff + j),
                  pl.BlockSpec((ntap, tn), lambda j, bi, i: (0, j))],
        out_specs=pl.BlockSpec((1, ts, tn), lambda j, bi, i: (bi, i, j)),
        out_shape=jax.ShapeDtypeStruct((b, s, n), out_dtype),
        scratch_shapes=[pltpu.VMEM((ts + HIST, tn), F32)] + cast_scratch,
        compiler_params=_cparams("arbitrary", "arbitrary", "arbitrary"),
        name="mm_taps",
    )(xb, w, taps.astype(F32))


def _ffn_up_kernel(x_ref, wg_ref, wu_ref, cg_ref, cu_ref, o_ref, sg_ref, su_ref, wgb_ref=None, wub_ref=None,
                   *, ts, tr):
    first = pl.program_id(2) == 0

    @pl.when(first)
    def _():
        sg_ref[0:HIST, :] = jnp.zeros((HIST, sg_ref.shape[1]), F32)
        su_ref[0:HIST, :] = jnp.zeros((HIST, su_ref.shape[1]), F32)

    new_tile = first & (pl.program_id(1) == 0)
    wg_ref = _weight_tile(wg_ref, wgb_ref, new_tile)
    wu_ref = _weight_tile(wu_ref, wub_ref, new_tile)

    def conv(sc_ref, c_ref, r0):
        y = c_ref[FFN_CONV - 1:FFN_CONV, :] * sc_ref[HIST + r0:HIST + r0 + tr, :]
        for back in range(1, FFN_CONV):
            tap = FFN_CONV - 1 - back
            y = y + c_ref[tap:tap + 1, :] * sc_ref[HIST + r0 - back:HIST + r0 - back + tr, :]
        return y

    for r0 in range(0, ts, tr):
        x = x_ref[0, r0:r0 + tr, :]
        sg_ref[HIST + r0:HIST + r0 + tr, :] = _dot(x, wg_ref[...])
        g = conv(sg_ref, cg_ref, r0)
        g = g * _sigmoid(g)
        su_ref[HIST + r0:HIST + r0 + tr, :] = _dot(x, wu_ref[...])
        u = conv(su_ref, cu_ref, r0)
        o_ref[0, r0:r0 + tr, :] = (g * u).astype(o_ref.dtype)
    _carry_rows(sg_ref, ts)
    _carry_rows(su_ref, ts)


def _ffn_up(xb, w_up, conv_w, ts=1024, tn=512, tr=1024, layer=0):
    b, s, d = xb.shape
    ts = min(ts, s)
    nj = D_FF // tn
    cast_scratch = [] if w_up.dtype == BF16 else [pltpu.VMEM((d, tn), BF16), pltpu.VMEM((d, tn), BF16)]
    return pl.pallas_call(
        functools.partial(_ffn_up_kernel, ts=ts, tr=min(tr, ts)),
        grid=(nj, b, s // ts),
        in_specs=[pl.BlockSpec((1, ts, d), lambda j, bi, i: (bi, i, 0)),
                  _weight_spec(w_up, tn, layer, lambda j, bi, i: j),
                  _weight_spec(w_up, tn, layer, lambda j, bi, i: nj + j),
                  pl.BlockSpec((FFN_CONV, tn), lambda j, bi, i: (0, j)),
                  pl.BlockSpec((FFN_CONV, tn), lambda j, bi, i: (0, nj + j))],
        out_specs=pl.BlockSpec((1, ts, tn), lambda j, bi, i: (bi, i, j)),
        out_shape=jax.ShapeDtypeStruct((b, s, D_FF), BF16),
        scratch_shapes=[pltpu.VMEM((ts + HIST, tn), F32), pltpu.VMEM((ts + HIST, tn), F32)] + cast_scratch,
        compiler_params=_cparams("arbitrary", "arbitrary", "arbitrary"),
        name="ffn_up",
    )(xb, w_up, w_up, conv_w, conv_w)


def _pool_kernel(u_ref, w_ref, sc_ref, o_ref, st_ref, *, ts):
    i = pl.program_id(1)

    @pl.when(i == 0)
    def _():
        st_ref[0:2 * HIST, :] = jnp.zeros((2 * HIST, st_ref.shape[1]), F32)

    st_ref[2 * HIST:2 * HIST + ts, :] = u_ref[0].astype(F32)
    pos = (i * ts + 1 + lax.broadcasted_iota(jnp.int32, (ts, 1), 0)).astype(F32)
    base = 2 * HIST
    for g, win in enumerate(POOL_WINDOWS):
        cols = slice(g * POOL_GDIM, (g + 1) * POOL_GDIM)
        cur = st_ref[base:base + ts, cols]
        acc = cur
        for back in range(1, win):
            acc = acc + st_ref[base - back:base - back + ts, cols]
        mean = acc / jnp.minimum(pos, float(win))
        y = _dot(_bf(mean - cur), w_ref[g])
        o_ref[0, :, cols] = (y * sc_ref[:, cols]).astype(o_ref.dtype)
    st_ref[0:2 * HIST, :] = st_ref[ts:ts + 2 * HIST, :]


def _pool_mixer(h_main, pool_w, pool_scale, ts=512):
    b, s, _ = h_main.shape
    ts = min(ts, s)
    return pl.pallas_call(
        functools.partial(_pool_kernel, ts=ts),
        grid=(b, s // ts),
        in_specs=[pl.BlockSpec((1, ts, HALF), lambda bi, i: (bi, i, 0)),
                  pl.BlockSpec((len(POOL_WINDOWS), POOL_GDIM, POOL_GDIM), lambda bi, i: (0, 0, 0)),
                  pl.BlockSpec((1, HALF), lambda bi, i: (0, 0))],
        out_specs=pl.BlockSpec((1, ts, HALF), lambda bi, i: (bi, i, 0)),
        out_shape=jax.ShapeDtypeStruct((b, s, HALF), BF16),
        scratch_shapes=[pltpu.VMEM((ts + 2 * HIST, HALF), F32)],
        compiler_params=_cparams("arbitrary", "arbitrary"),
        name="pool_mixer",
    )(h_main, pool_w, pool_scale.reshape(1, HALF))


def _gdn_kernel(q_ref, k_ref, v_ref, z_ref, g_ref, alog_ref, dtb_ref, nw_ref, o_ref, st_ref, *, ts, hps):
    c = CHUNK
    h0 = pl.program_id(1)

    @pl.when(pl.program_id(2) == 0)
    def _():
        st_ref[...] = jnp.zeros_like(st_ref)

    q_all = q_ref[0]
    k_all = k_ref[0]
    v_all = v_ref[0]
    z_all = z_ref[0].astype(F32)

    logits = g_ref[0]
    lane = lax.broadcasted_iota(jnp.int32, logits.shape, 1)
    g_all = -jnp.exp(alog_ref[...]) * _softplus(logits + dtb_ref[...])
    sig_all = _sigmoid(logits)

    causal, strict = _tri_masks(c)
    tril = causal.astype(BF16)
    eye = jnp.where(causal & (~strict), 1.0, 0.0).astype(F32)
    lane_c = lax.broadcasted_iota(jnp.int32, (c, GDN_DK), 1)
    nw = nw_ref[...]
    chunk_rows = [slice(ci * c, (ci + 1) * c) for ci in range(ts // c)]
    nck = len(chunk_rows)

    q, k, v, betas, g_cols = [], [], [], [], []
    for hd in range(hps):
        hl = slice(hd * GDN_DK, (hd + 1) * GDN_DK)
        qh, kh = q_all[:, hl], k_all[:, hl]
        qh = qh * lax.rsqrt(jnp.sum(qh * qh, axis=-1, keepdims=True) + NORM_EPS) * (GDN_DK ** -0.5)
        kh = kh * lax.rsqrt(jnp.sum(kh * kh, axis=-1, keepdims=True) + NORM_EPS)
        head = h0 * hps + hd
        g_col = jnp.sum(jnp.where(lane == head, g_all, 0.0), axis=1, keepdims=True)
        b_col = jnp.sum(jnp.where(lane == head + GDN_HEADS, sig_all, 0.0), axis=1, keepdims=True)
        for r in chunk_rows:
            q.append(qh[r])
            k.append(kh[r])
            v.append(v_all[r, hl])
            betas.append(jnp.broadcast_to(b_col[r], (c, GDN_DK)))
            g_cols.append(jnp.broadcast_to(g_col[r], (c, GDN_DK)))
    nprob = len(q)
    rows = list(range(nprob))
    gcs = [_dot01(tril, gcol) for gcol in g_cols]
    diffs = []
    for gc in gcs:
        g1, g2, g3 = (p.astype(F32) for p in _split3(gc))
        lhs = jnp.where(lane_c == 0, g1, jnp.where(lane_c == 1, g2, jnp.where(lane_c == 2, g3,
              jnp.where(lane_c < 6, 1.0, 0.0))))
        rhs = jnp.where(lane_c < 3, 1.0, jnp.where(lane_c == 3, -g1, jnp.where(lane_c == 4, -g2,
              jnp.where(lane_c == 5, -g3, 0.0))))
        diffs.append(_dot_nt(_bf(lhs), _bf(rhs)))
    decays = [jnp.where(causal, jnp.exp(jnp.where(causal, d, 0.0)), 0.0) for d in diffs]
    kbs = [k[r] * bt for r, bt in zip(rows, betas)]
    kcbs = [_bf(k[r]) for r in rows]
    lmats = [jnp.where(strict, _dot_nt(_bf(kb), kcb) * dc, 0.0) for kb, kcb, dc in zip(kbs, kcbs, decays)]
    intras = [_bf(jnp.where(causal, _dot_nt(_bf(q[r]), kcb) * dc, 0.0)) for r, kcb, dc in zip(rows, kcbs, decays)]
    tmats = [_bf(t) for t in _unit_lower_inverses([-l for l in lmats], eye)]
    egs = [jnp.exp(gc) for gc in gcs]
    g_lasts = [gc[c - 1:c, :] for gc in gcs]
    uws = [_dot(t, jnp.concatenate([_bf(v[r] * bt), _bf(kb * eg)], axis=1))
           for t, r, bt, kb, eg in zip(tmats, rows, betas, kbs, egs)]
    us = [uw[:, :GDN_DK] for uw in uws]
    ws = [_bf(uw[:, GDN_DK:]) for uw in uws]
    q_decs = [_bf(q[r] * eg) for r, eg in zip(rows, egs)]
    k_decs = [_bf(k[r] * jnp.exp(gl - gc)) for r, gl, gc in zip(rows, g_lasts, gcs)]
    e_lasts = [jnp.exp(gl) for gl in g_lasts]
    mns = [_dot_tn(kd, jnp.concatenate([w, _bf(u)], axis=1)) for kd, w, u in zip(k_decs, ws, us)]
    mmats = [_bf(mn[:, :GDN_DK]) for mn in mns]
    nmats = [mn[:, GDN_DK:] for mn in mns]

    states = [st_ref[hd] for hd in range(hps)]
    sb_hist = [[None] * nck for _ in range(hps)]
    for ci in range(nck):
        for hd in range(hps):
            p = hd * nck + ci
            sb = _bf(states[hd])
            sb_hist[hd][ci] = sb
            states[hd] = states[hd] * e_lasts[p] - _dot(mmats[p], sb) + nmats[p]
    for hd in range(hps):
        st_ref[hd] = states[hd]
    for hd in range(hps):
        hl = slice(hd * GDN_DK, (hd + 1) * GDN_DK)
        for ci in range(nck):
            p = hd * nck + ci
            sb = sb_hist[hd][ci]
            v_new = _bf(us[p] - _dot(ws[p], sb))
            o = _dot(q_decs[p], sb) + _dot(intras[p], v_new)
            o = o * lax.rsqrt(jnp.mean(o * o, axis=-1, keepdims=True) + NORM_EPS) * nw
            zc = z_all[chunk_rows[ci], hl]
            o_ref[0, chunk_rows[ci], hl] = (o * (zc * _sigmoid(zc))).astype(o_ref.dtype)


def _gated_deltanet(qkv, z, logits, a_log, dt_bias, norm_w, ts=256, hps=8):
    b, s, _ = qkv.shape
    ts = min(ts, s)
    hd = GDN_DK
    gw = hps * hd
    ng = HALF // gw
    pad = lambda t: jnp.pad(t.astype(F32), (0, hd - t.shape[0])).reshape(1, hd)
    col = lambda grp: (lambda bi, h, i: (bi, i, grp * ng + h))
    return pl.pallas_call(
        functools.partial(_gdn_kernel, ts=ts, hps=hps),
        grid=(b, ng, s // ts),
        in_specs=[pl.BlockSpec((1, ts, gw), col(0)), pl.BlockSpec((1, ts, gw), col(1)),
                  pl.BlockSpec((1, ts, gw), col(2)), pl.BlockSpec((1, ts, gw), col(0)),
                  pl.BlockSpec((1, ts, hd), lambda bi, h, i: (bi, i, 0)),
                  pl.BlockSpec((1, hd), lambda bi, h, i: (0, 0)),
                  pl.BlockSpec((1, hd), lambda bi, h, i: (0, 0)),
                  pl.BlockSpec((1, hd), lambda bi, h, i: (0, 0))],
        out_specs=pl.BlockSpec((1, ts, gw), lambda bi, h, i: (bi, i, h)),
        out_shape=jax.ShapeDtypeStruct((b, s, HALF), BF16),
        scratch_shapes=[pltpu.VMEM((hps, hd, hd), F32)],
        compiler_params=_cparams("arbitrary", "arbitrary", "arbitrary"),
        name="gated_deltanet",
    )(qkv, qkv, qkv, z, logits, pad(a_log), pad(dt_bias), norm_w.astype(F32).reshape(1, hd))


def _rwkv_kernel(r_ref, k_ref, v_ref, l_ref, w2_ref, a2_ref, g2_ref,
                 w0_ref, a0_ref, kk_ref, ka_ref, rk_ref, lnw_ref, lnb_ref, o_ref, st_ref, *, ts, hps):
    c = CHUNK
    hs = RWKV_HEAD

    @pl.when(pl.program_id(2) == 0)
    def _():
        st_ref[...] = jnp.zeros_like(st_ref)

    r = r_ref[0]
    k = k_ref[0]
    v = v_ref[0]
    lo = l_ref[0]
    p0, p1, p2 = RWKV_LORA_PAD
    w_log = -_softplus(-(w0_ref[...] + _dot(lo[:, :p0], w2_ref[...]))) - 0.5
    lw = -jnp.exp(w_log)
    a = _sigmoid(a0_ref[...] + _dot(lo[:, p0:p0 + p1], a2_ref[...]))
    g = _dot(lo[:, p0 + p1:p0 + p1 + p2], g2_ref[...])
    kk_raw = k * kk_ref[...]
    km = k * (1.0 + (a - 1.0) * ka_ref[...])
    head_of_lane = lax.broadcasted_iota(jnp.int32, (ts, hps * hs), 1) // hs

    def per_head_sum(x):
        out = jnp.zeros_like(x)
        for hd in range(hps):
            sel = head_of_lane == hd
            out = jnp.where(sel, jnp.sum(jnp.where(sel, x, 0.0), axis=-1, keepdims=True), out)
        return out

    kk = kk_raw * lax.rsqrt(per_head_sum(kk_raw * kk_raw) + NORM_EPS)
    aa = -kk
    bb = kk * a
    bonus = per_head_sum(r * km * rk_ref[...]) * v

    causal, strict = _tri_masks(c)
    tril = causal.astype(BF16)
    eye = jnp.where(causal & (~strict), 1.0, 0.0).astype(F32)
    lnw = lnw_ref[...]
    lnb = lnb_ref[...]

    rows = [slice(ci * c, (ci + 1) * c) for ci in range(ts // c)]
    nck = len(rows)
    heads = [slice(hd * hs, (hd + 1) * hs) for hd in range(hps)]
    cums = [_dot01(tril, lw[rw]) for rw in rows]
    c_lasts = [cum[c - 1:c, :] for cum in cums]
    w_invs = [jnp.exp(-cum) for cum in cums]
    w_tails = [jnp.exp(cl - cum) for cl, cum in zip(c_lasts, cums)]
    a_ts = [_bf(aa[rw] * jnp.exp(cum - lw[rw])) for rw, cum in zip(rows, cums)]
    r_ts = [_bf(r[rw] * jnp.exp(cum)) for rw, cum in zip(rows, cums)]
    b_ts = [_bf(bb[rw] * wi) for rw, wi in zip(rows, w_invs)]
    k_ts = [_bf(km[rw] * wi) for rw, wi in zip(rows, w_invs)]
    b_hs = [_bf(bb[rw] * wt) for rw, wt in zip(rows, w_tails)]
    k_hs = [_bf(km[rw] * wt) for rw, wt in zip(rows, w_tails)]
    vcs = [_bf(v[rw]) for rw in rows]
    e_lasts = [jnp.exp(cl) for cl in c_lasts]
    prob = [(ci, cs) for ci in range(nck) for cs in heads]
    row2 = lax.broadcasted_iota(jnp.int32, (c, 2 * c), 0)
    col2 = lax.broadcasted_iota(jnp.int32, (c, 2 * c), 1) % c
    bks = [jnp.concatenate([b_ts[ci][:, cs], k_ts[ci][:, cs]], axis=0) for ci, cs in prob]
    a_abks = [jnp.where(row2 > col2, _dot_nt(a_ts[ci][:, cs], bk), 0.0) for bk, (ci, cs) in zip(bks, prob)]
    a_rbks = [_bf(jnp.where(row2 >= col2, _dot_nt(r_ts[ci][:, cs], bk), 0.0)) for bk, (ci, cs) in zip(bks, prob)]
    tmats = [_bf(t) for t in _unit_lower_inverses([m[:, :c] for m in a_abks], eye)]
    zero_c = jnp.zeros((c, hs), BF16)
    avs = [_bf(_dot(_bf(abk), jnp.concatenate([zero_c, vcs[ci][:, cs]], axis=0)))
           for abk, (ci, cs) in zip(a_abks, prob)]
    pmats = [_bf(_dot(t, a_ts[ci][:, cs])) for t, (ci, cs) in zip(tmats, prob)]
    qmats = [_dot(t, av) for t, av in zip(tmats, avs)]
    mmats = [_bf(_dot_tn(pm, b_hs[ci][:, cs])) for pm, (ci, cs) in zip(pmats, prob)]
    nmats = [_dot_tn(jnp.concatenate([_bf(qm), vcs[ci][:, cs]], axis=0),
                     jnp.concatenate([b_hs[ci][:, cs], k_hs[ci][:, cs]], axis=0))
             for qm, (ci, cs) in zip(qmats, prob)]

    states = [st_ref[hd] for hd in range(hps)]
    sb_hist = []
    for ci in range(nck):
        sbs = [_bf(s) for s in states]
        sb_hist.append(sbs)
        states = [states[hd] * e_lasts[ci][:, heads[hd]] + _dot(sbs[hd], mmats[hps * ci + hd]) + nmats[hps * ci + hd]
                  for hd in range(hps)]
    for hd in range(hps):
        st_ref[hd] = states[hd]
    ys = []
    for ci in range(nck):
        sbs = sb_hist[ci]
        ubs = [_bf(_dot_nt(pmats[hps * ci + hd], sbs[hd]) + qmats[hps * ci + hd]) for hd in range(hps)]
        ys.append([_dot_nt(r_ts[ci][:, heads[hd]], sbs[hd])
                   + _dot(a_rbks[hps * ci + hd], jnp.concatenate([ubs[hd], vcs[ci][:, heads[hd]]], axis=0))
                   for hd in range(hps)])
    for ci in range(nck):
        outs = []
        for y in ys[ci]:
            ym = jnp.mean(y, axis=-1, keepdims=True)
            yd = y - ym
            yv = jnp.mean(yd * yd, axis=-1, keepdims=True)
            outs.append(yd * lax.rsqrt(yv + RWKV_LNX_EPS))
        yn = jnp.concatenate(outs, axis=-1) * lnw + lnb
        o_ref[0, rows[ci], :] = ((yn + bonus[rows[ci]]) * g[rows[ci]]).astype(o_ref.dtype)


def _rwkv7(hr, hl, w2p, a2p, g2p, w0, a0, k_k, k_a, r_k, lnx_w, lnx_b, ts=128, hps=16):
    b, s, _ = hr.shape
    ts = min(ts, s)
    pw = hps * RWKV_HEAD
    npair = HALF // pw
    lw = sum(RWKV_LORA_PAD)
    col = lambda grp: (lambda bi, hp, i: (bi, i, grp * npair + hp))
    vec = lambda grp: pl.BlockSpec((1, pw), lambda bi, hp, i: (0, grp * npair + hp))
    row1 = lambda t: t.astype(F32).reshape(1, -1)
    return pl.pallas_call(
        functools.partial(_rwkv_kernel, ts=ts, hps=hps),
        grid=(b, npair, s // ts),
        in_specs=[pl.BlockSpec((1, ts, pw), col(0)), pl.BlockSpec((1, ts, pw), col(1)),
                  pl.BlockSpec((1, ts, pw), col(2)),
                  pl.BlockSpec((1, ts, lw), lambda bi, hp, i: (bi, i, 0)),
                  pl.BlockSpec((RWKV_LORA_PAD[0], pw), lambda bi, hp, i: (0, hp)),
                  pl.BlockSpec((RWKV_LORA_PAD[1], pw), lambda bi, hp, i: (0, hp)),
                  pl.BlockSpec((RWKV_LORA_PAD[2], pw), lambda bi, hp, i: (0, hp)),
                  vec(0), vec(0), vec(0), vec(0), vec(0), vec(0), vec(0)],
        out_specs=pl.BlockSpec((1, ts, pw), lambda bi, hp, i: (bi, i, hp)),
        out_shape=jax.ShapeDtypeStruct((b, s, HALF), BF16),
        scratch_shapes=[pltpu.VMEM((hps, RWKV_HEAD, RWKV_HEAD), F32)],
        compiler_params=_cparams("arbitrary", "arbitrary", "arbitrary"),
        name="rwkv7",
    )(hr, hr, hr, hl, w2p, a2p, g2p,
      row1(w0), row1(a0), row1(k_k), row1(k_a), row1(r_k), row1(lnx_w), row1(lnx_b))


def _fox_cumsum_kernel(f_ref, bf_ref, o_ref):
    x = f_ref[0] + bf_ref[...]
    ls = jnp.minimum(x, 0.0) - jnp.log(1.0 + jnp.exp(-jnp.abs(x)))
    nh, nr, nl = ls.shape
    li = lax.broadcasted_iota(jnp.int32, (nl, nl), 0)
    lj = lax.broadcasted_iota(jnp.int32, (nl, nl), 1)
    upper = (li <= lj).astype(BF16)
    ri = lax.broadcasted_iota(jnp.int32, (nr, nr), 0)
    rj = lax.broadcasted_iota(jnp.int32, (nr, nr), 1)
    below = (ri > rj).astype(BF16)
    for h in range(nh):
        within = _dot01_right(ls[h], upper)
        tot = jnp.broadcast_to(within[:, nl - 1:nl], (nr, nl))
        o_ref[0, h] = within + _dot01(below, tot)


def _dot01_right(x, m01):
    x1, x2, x3 = _split3(x)
    return _dot(x1, m01) + _dot(x2, m01) + _dot(x3, m01)


def _fox_cumsum(f_t, b_f):
    b, nh, s = f_t.shape
    nl = 128
    nr = s // nl
    out = pl.pallas_call(
        _fox_cumsum_kernel,
        grid=(b,),
        in_specs=[pl.BlockSpec((1, nh, nr, nl), lambda bi: (bi, 0, 0, 0)),
                  pl.BlockSpec((nh, 1, 1), lambda bi: (0, 0, 0))],
        out_specs=pl.BlockSpec((1, nh, nr, nl), lambda bi: (bi, 0, 0, 0)),
        out_shape=jax.ShapeDtypeStruct((b, nh, nr, nl), F32),
        compiler_params=_cparams("arbitrary"),
        name="fox_cumsum",
    )(f_t.reshape(b, nh, nr, nl), b_f.astype(F32).reshape(nh, 1, 1))
    return out.reshape(b, nh, 1, s)


def _fox_kernel(qi_ref, ki_ref, q_ref, k_ref, v_ref, cq_ref, ck_ref, o_ref, qs_ref, m_ref, acc_ref, *, tq, tk, tr):
    p = pl.program_id(2)
    qi = qi_ref[p]
    ki = ki_ref[p]
    last_k = ((qi + 1) * tq - 1) // tk
    log2e = 1.0 / math.log(2.0)

    @pl.when(ki == 0)
    def _():
        qs_ref[...] = (q_ref[0].astype(F32) * (FOX_HD ** -0.5 * log2e)).astype(BF16)
        m_ref[...] = jnp.full_like(m_ref, -jnp.inf)
        acc_ref[...] = jnp.zeros_like(acc_ref)

    def step(masked):
        bias = (cq_ref[0, 0, :, 0:1] - ck_ref[0, 0]) * log2e
        v_aug = jnp.concatenate([v_ref[0], jnp.ones((tk, FOX_HD), BF16)], axis=1)
        kb = k_ref[0]
        for r0 in range(0, tq, tr):
            rows = slice(r0, r0 + tr)
            s = _dot_nt(qs_ref[rows, :], kb) + bias
            if masked:
                qpos = qi * tq + r0 + lax.broadcasted_iota(jnp.int32, (tr, tk), 0)
                kpos = ki * tk + lax.broadcasted_iota(jnp.int32, (tr, tk), 1)
                s = jnp.where(kpos <= qpos, s, -jnp.inf)
            m_old = m_ref[rows, :]
            m_new = jnp.maximum(m_old, jnp.max(s, axis=-1, keepdims=True))
            corr = jnp.exp2(m_old - m_new)
            pexp = _bf(jnp.exp2(s - m_new))
            acc_ref[rows, :] = corr * acc_ref[rows, :] + _dot(pexp, v_aug)
            m_ref[rows, :] = m_new

    needs_mask = (ki + 1) * tk - 1 > qi * tq

    @pl.when(needs_mask)
    def _():
        step(True)

    @pl.when(jnp.logical_not(needs_mask))
    def _():
        step(False)

    @pl.when(ki == last_k)
    def _():
        acc = acc_ref[...]
        o_ref[0] = (acc[:, :FOX_HD] / acc[:, FOX_HD:FOX_HD + 1]).astype(o_ref.dtype)


def _fox_attention(hf, c, tq=2048, tk=2048, tr=128):
    b, s, _ = hf.shape
    tq = min(tq, s)
    tk = min(tk, s)
    nh = FOX_HEADS
    pairs = [(qi, ki) for qi in range(s // tq) for ki in range(((qi + 1) * tq - 1) // tk + 1)]
    qi_arr = jnp.asarray(np.array([p[0] for p in pairs], np.int32))
    ki_arr = jnp.asarray(np.array([p[1] for p in pairs], np.int32))
    grid_spec = pltpu.PrefetchScalarGridSpec(
        num_scalar_prefetch=2,
        grid=(b, nh, len(pairs)),
        in_specs=[pl.BlockSpec((1, tq, FOX_HD), lambda bi, h, p, qa, ka: (bi, qa[p], h)),
                  pl.BlockSpec((1, tk, FOX_HD), lambda bi, h, p, qa, ka: (bi, ka[p], nh + h)),
                  pl.BlockSpec((1, tk, FOX_HD), lambda bi, h, p, qa, ka: (bi, ka[p], 2 * nh + h)),
                  pl.BlockSpec((1, 1, 1, tq), lambda bi, h, p, qa, ka: (bi, h, 0, qa[p])),
                  pl.BlockSpec((1, 1, 1, tk), lambda bi, h, p, qa, ka: (bi, h, 0, ka[p]))],
        out_specs=pl.BlockSpec((1, tq, FOX_HD), lambda bi, h, p, qa, ka: (bi, qa[p], h)),
        scratch_shapes=[pltpu.VMEM((tq, FOX_HD), BF16), pltpu.VMEM((tq, 1), F32),
                        pltpu.VMEM((tq, 2 * FOX_HD), F32)])
    return pl.pallas_call(
        functools.partial(_fox_kernel, tq=tq, tk=tk, tr=min(tr, tq)),
        grid_spec=grid_spec,
        out_shape=jax.ShapeDtypeStruct((b, s, HALF), BF16),
        compiler_params=_cparams("arbitrary", "arbitrary", "arbitrary"),
        name="fox_attention",
    )(qi_arr, ki_arr, hf, hf, hf, c, c)


def _pad_cols(w, width):
    return jnp.pad(w, ((0, 0), (0, width - w.shape[1])))


def _pad_rows(w, height):
    return jnp.pad(w, ((0, height - w.shape[0]), (0, 0)))


def _even_mixer(x32, xb, w_in, layer, pool_w, pool_scale, conv_w, a_log, dt_bias, norm_w, w_out, ln_w, ln_b):
    b, s, d = x32.shape
    t = b * s
    n_main = 5 * HALF
    w_gate = _bf(_pad_cols(w_in[layer, :, n_main:], 128))
    xb2 = xb.reshape(t, d)
    u_pool = _mm(xb2, w_in, F32, col0=0, n=HALF, layer=layer).reshape(b, s, HALF)
    qkv = _mm_taps(xb, w_in, conv_w, F32, act="silu", col0=HALF, n=3 * HALF, layer=layer)
    z = _mm(xb2, w_in, BF16, col0=4 * HALF, n=HALF, layer=layer).reshape(b, s, HALF)
    logits = _mm(xb2, w_gate, F32, tn=128).reshape(b, s, 128)
    y_a = _pool_mixer(u_pool, _bf(pool_w), pool_scale)
    y_b = _gated_deltanet(qkv, z, logits, a_log, dt_bias, norm_w)
    wo = _bf(w_out)
    return _mm_res_ln([y_a.reshape(t, HALF), y_b.reshape(t, HALF)], [wo[:HALF], wo[HALF:]],
                      x32.reshape(t, d), ln_w, ln_b)


def _odd_mixer(x32, xb, w_in_all, layer, mu, w0, w2, a0, a2, g2, k_k, k_a, r_k, lnx_w, lnx_b, b_f, w_out, ln_w, ln_b):
    b, s, d = x32.shape
    t = b * s
    l0, l1, l2 = RWKV_LORA
    p0, p1, p2 = RWKV_LORA_PAD
    o_l = 3 * HALF
    o_f = o_l + l0 + l1 + l2
    w_tail = w_in_all[layer, :, o_l:]
    w_l = _bf(jnp.concatenate([_pad_cols(w_tail[:, :l0], p0),
                               _pad_cols(w_tail[:, l0:l0 + l1], p1),
                               _pad_cols(w_tail[:, l0 + l1:o_f - o_l], p2)], axis=1))
    w_fox = _bf(w_tail[:, o_f - o_l:o_f - o_l + 3 * HALF])
    w_fl = _bf(_pad_cols(w_tail[:, o_f - o_l + 3 * HALF:], 128))
    mu_l = jnp.concatenate([jnp.pad(mu[o_l:o_l + l0], (0, p0 - l0)),
                            jnp.pad(mu[o_l + l0:o_l + l0 + l1], (0, p1 - l1)),
                            jnp.pad(mu[o_l + l0 + l1:o_f], (0, p2 - l2))])
    xb2 = xb.reshape(t, d)
    lerp = lambda m: jnp.stack([m, 1.0 - m])
    hr = _mm_taps(xb, w_in_all, lerp(mu[:o_l]), F32, col0=0, n=o_l, layer=layer)
    hl = _mm_taps(xb, w_l, lerp(mu_l), BF16, act="lora", tn=p0 + p1 + p2)
    hf = _mm(xb2, w_fox, BF16).reshape(b, s, 3 * HALF)
    fl = _mm(xb2, w_fl, F32, tn=128).reshape(b, s, 128)
    y_c = _rwkv7(hr, hl, _bf(_pad_rows(w2, p0)), _bf(_pad_rows(a2, p1)), _bf(_pad_rows(g2, p2)),
                 w0, a0, k_k, k_a, r_k, lnx_w, lnx_b)
    c = _fox_cumsum(jnp.transpose(fl[:, :, :FOX_HEADS], (0, 2, 1)), b_f)
    y_d = _fox_attention(hf, c)
    wo = _bf(w_out)
    return _mm_res_ln([y_c.reshape(t, HALF), y_d.reshape(t, HALF)], [wo[:HALF], wo[HALF:]],
                      x32.reshape(t, d), ln_w, ln_b)


def _cross_attention(x32, xb, mem_b, w_q, w_kv, layer, w_o, ln_w, ln_b):
    t, d = x32.shape
    b, mlen, _ = mem_b.shape
    s = t // b
    q = _mm(xb, w_q, BF16, layer=layer).reshape(b, s, d)
    kv = _mm(mem_b.reshape(b * mlen, d), w_kv, BF16, layer=layer).reshape(b, mlen, 2 * d)
    o = _xattn(q, kv).reshape(t, d)
    return _mm_res_ln([o], [_bf(w_o)], x32, ln_w, ln_b)


def _conv_glu_ffn(x32, xb, b, w_up, layer, conv_w, w_down, ln_w, ln_b):
    t, d = x32.shape
    hmid = _ffn_up(xb.reshape(b, t // b, d), w_up, conv_w, layer=layer)
    return _mm_res_ln([hmid.reshape(t, D_FF)], [_bf(w_down)], x32, ln_w, ln_b, tk=D_FF // 2)


def kernel(x, mem, ev_w_in, pool_w, pool_scale, gdn_conv_w, gdn_a_log, gdn_dt_bias, gdn_norm_w, ev_w_out,
           od_w_in, rwkv_mu, rwkv_w0, rwkv_w2, rwkv_a0, rwkv_a2, rwkv_g2, rwkv_k_k, rwkv_k_a, rwkv_r_k,
           rwkv_lnx_w, rwkv_lnx_b, fox_b_f, od_w_out,
           ln_mix_w, ln_mix_b, xa_w_q, xa_w_kv, xa_w_o, ln_xa_w, ln_xa_b,
           ffn_w_up, ffn_conv_w, ffn_w_down, ln_ffn_w, ln_ffn_b):
    b, s, d = x.shape
    depth = ln_mix_w.shape[0]
    mem_b = _bf(mem)
    x32 = x
    xb = _bf(x)
    for i in range(depth):
        j = i // 2
        x3 = x32.reshape(b, s, d)
        xb3 = xb.reshape(b, s, d)
        if i % 2 == 0:
            x32, xb = _even_mixer(x3, xb3, ev_w_in, j, pool_w[j], pool_scale[j], gdn_conv_w[j], gdn_a_log[j],
                                  gdn_dt_bias[j], gdn_norm_w[j], ev_w_out[j], ln_mix_w[i], ln_mix_b[i])
        else:
            x32, xb = _odd_mixer(x3, xb3, od_w_in, j, rwkv_mu[j], rwkv_w0[j], rwkv_w2[j], rwkv_a0[j], rwkv_a2[j],
                                 rwkv_g2[j], rwkv_k_k[j], rwkv_k_a[j], rwkv_r_k[j].reshape(-1), rwkv_lnx_w[j],
                                 rwkv_lnx_b[j], fox_b_f[j], od_w_out[j], ln_mix_w[i], ln_mix_b[i])
        x32, xb = _cross_attention(x32, xb, mem_b, xa_w_q, xa_w_kv, i, xa_w_o[i], ln_xa_w[i], ln_xa_b[i])
        x32, xb = _conv_glu_ffn(x32, xb, b, ffn_w_up, i, ffn_conv_w[i], ffn_w_down[i], ln_ffn_w[i], ln_ffn_b[i])
    return x32.reshape(b, s, d)
```

```python
import functools
import math

import numpy as np
import jax
import jax.numpy as jnp
from jax import lax
from jax.experimental import pallas as pl
from jax.experimental.pallas import tpu as pltpu

F32 = jnp.float32
BF16 = jnp.bfloat16

D_MODEL = 2048
HALF = D_MODEL // 2
POOL_WINDOWS = (2, 4, 8, 16)
POOL_GDIM = HALF // len(POOL_WINDOWS)
GDN_HEADS = 8
GDN_DK = HALF // GDN_HEADS
GDN_CONV = 4
NORM_EPS = 1e-6
RWKV_HEAD = 64
RWKV_LNX_EPS = 64e-5
RWKV_LORA = (64, 64, 160)
RWKV_LORA_PAD = (128, 128, 256)
FOX_HEADS = 8
FOX_HD = HALF // FOX_HEADS
XA_HEADS = 4
XA_HD = D_MODEL // XA_HEADS
D_FF = 5632
FFN_CONV = 3
DEPTH = 2
DEEPNORM_ALPHA = float((2 * DEPTH) ** 0.25)
LN_EPS = 1e-5

CHUNK = 64
HIST = 8
VMEM_LIMIT_BYTES = 56 * 1024 * 1024


def _cparams(*sem):
    return pltpu.CompilerParams(dimension_semantics=sem, vmem_limit_bytes=VMEM_LIMIT_BYTES)


def _dot(a, b):
    return jnp.dot(a, b, preferred_element_type=F32)


def _dot_nt(a, b):
    return lax.dot_general(a, b, (((1,), (1,)), ((), ())), preferred_element_type=F32)


def _dot_tn(a, b):
    return lax.dot_general(a, b, (((0,), (0,)), ((), ())), preferred_element_type=F32)


def _bf(x):
    return x.astype(BF16)


def _split3(x):
    x1 = x.astype(BF16)
    r1 = x - x1.astype(F32)
    x2 = r1.astype(BF16)
    x3 = (r1 - x2.astype(F32)).astype(BF16)
    return x1, x2, x3


def _dot01(m01, x):
    x1, x2, x3 = _split3(x)
    return _dot(m01, x1) + _dot(m01, x2) + _dot(m01, x3)


def _unit_lower_inverses(n_mats, eye, fillers=()):
    c = eye.shape[0]
    levels = int(math.log2(c))
    fillers = list(fillers)
    xs = [eye + n for n in n_mats]
    nbs = [_bf(n) for n in n_mats]
    ps = [_dot(nb, nb) for nb in nbs]
    for level in range(1, levels):
        pbs = [_bf(p) for p in ps]
        if level == levels - 1:
            prods = [_dot(_bf(x), pb) for x, pb in zip(xs, pbs)]
        else:
            prods = [_dot(jnp.concatenate([pb, _bf(x)], axis=0), pb) for x, pb in zip(xs, pbs)]
        if fillers:
            fillers.pop(0)()
        if level == levels - 1:
            xs = [x + pr for x, pr in zip(xs, prods)]
        else:
            xs = [x + pr[c:] for x, pr in zip(xs, prods)]
            ps = [pr[:c] for pr in prods]
    for filler in fillers:
        filler()
    return xs


def _sigmoid(x):
    return 1.0 / (1.0 + jnp.exp(-x))


def _softplus(x):
    return jnp.maximum(x, 0.0) + jnp.log(1.0 + jnp.exp(-jnp.abs(x)))


def _tri_masks(c):
    row = lax.broadcasted_iota(jnp.int32, (c, c), 0)
    col = lax.broadcasted_iota(jnp.int32, (c, c), 1)
    return row >= col, row > col


def _stage_rows(x_ref, sc_ref, ts, first):
    @pl.when(first)
    def _():
        sc_ref[0:HIST, :] = jnp.zeros((HIST, sc_ref.shape[1]), F32)
    sc_ref[HIST:HIST + ts, :] = x_ref[0].astype(F32)


def _carry_rows(sc_ref, ts):
    sc_ref[0:HIST, :] = sc_ref[ts:ts + HIST, :]


def _weight_tile(w_ref, wb_ref, first):
    if wb_ref is None:
        return w_ref
    @pl.when(first)
    def _():
        wb_ref[...] = w_ref[...].astype(BF16)
    return wb_ref


def _weight_cols(w, n, col0, tn):
    n = w.shape[-1] - col0 if n is None else n
    tn = min(tn, n)
    assert n % tn == 0 and col0 % tn == 0, (n, col0, tn)
    scratch = [] if w.dtype == BF16 else [pltpu.VMEM((w.shape[-2], tn), BF16)]
    return n, tn, col0 // tn, scratch


def _weight_spec(w, tn, layer, col_of):
    k = w.shape[-2]
    if w.ndim == 2:
        return pl.BlockSpec((k, tn), lambda *g: (0, col_of(*g)))
    return pl.BlockSpec((None, k, tn), lambda *g: (layer, 0, col_of(*g)))


def _mm_kernel(a_ref, w_ref, o_ref, wb_ref=None):
    w = _weight_tile(w_ref, wb_ref, pl.program_id(1) == 0)
    o_ref[...] = _dot(a_ref[...], w[...]).astype(o_ref.dtype)


def _mm_cast_kernel(a_ref, w_ref, o_ref, ab_ref, wb_ref=None):
    w = _weight_tile(w_ref, wb_ref, pl.program_id(0) == 0)
    ab = a_ref[...].astype(BF16)
    ab_ref[...] = ab
    o_ref[...] = _dot(ab, w[...]).astype(o_ref.dtype)


def _mm_cast(a, w, out_dtype, tm=512, col0=0, n=None, layer=0):
    m, k = a.shape
    n, tn, joff, scratch = _weight_cols(w, n, col0, w.shape[-1])
    tm = min(tm, m)
    return pl.pallas_call(
        _mm_cast_kernel,
        grid=(m // tm,),
        in_specs=[pl.BlockSpec((tm, k), lambda i: (i, 0)),
                  _weight_spec(w, tn, layer, lambda i: joff)],
        out_specs=[pl.BlockSpec((tm, tn), lambda i: (i, 0)), pl.BlockSpec((tm, k), lambda i: (i, 0))],
        out_shape=[jax.ShapeDtypeStruct((m, n), out_dtype), jax.ShapeDtypeStruct((m, k), BF16)],
        scratch_shapes=scratch,
        compiler_params=_cparams("arbitrary"),
        name="mm_cast",
    )(a, w)


def _mm(a, w, out_dtype, tm=1024, tn=1024, col0=0, n=None, layer=0):
    m, k = a.shape
    n, tn, joff, scratch = _weight_cols(w, n, col0, tn)
    tm = min(tm, m)
    return pl.pallas_call(
        _mm_kernel,
        grid=(n // tn, m // tm),
        in_specs=[pl.BlockSpec((tm, k), lambda j, i: (i, 0)),
                  _weight_spec(w, tn, layer, lambda j, i: joff + j)],
        out_specs=pl.BlockSpec((tm, tn), lambda j, i: (i, j)),
        out_shape=jax.ShapeDtypeStruct((m, n), out_dtype),
        scratch_shapes=scratch,
        compiler_params=_cparams("arbitrary", "arbitrary"),
        name="mm",
    )(a, w)


def _mm_res_ln_kernel(*refs, n_in, tk_steps, tr):
    a_refs = refs[:n_in]
    w_refs = refs[n_in:2 * n_in]
    res_ref, lnw_ref, lnb_ref, o32_ref, o16_ref = refs[2 * n_in:2 * n_in + 5]
    acc_ref = refs[2 * n_in + 5] if tk_steps > 1 else None
    kk = pl.program_id(1)
    tm = res_ref.shape[0]

    def partial_sum(rows):
        tot = None
        for a_ref, w_ref in zip(a_refs, w_refs):
            d = _dot(a_ref[rows, :], w_ref[...])
            tot = d if tot is None else tot + d
        return tot

    if tk_steps > 1:
        @pl.when(kk == 0)
        def _():
            acc_ref[...] = partial_sum(slice(None))

        @pl.when((kk > 0) & (kk < tk_steps - 1))
        def _():
            acc_ref[...] += partial_sum(slice(None))

    @pl.when(kk == tk_steps - 1)
    def _():
        for r0 in range(0, tm, tr):
            rows = slice(r0, r0 + tr)
            y = DEEPNORM_ALPHA * res_ref[rows, :] + partial_sum(rows)
            if tk_steps > 1:
                y = y + acc_ref[rows, :]
            mu = jnp.mean(y, axis=-1, keepdims=True)
            d = y - mu
            var = jnp.mean(d * d, axis=-1, keepdims=True)
            out = d * lax.rsqrt(var + LN_EPS) * lnw_ref[...] + lnb_ref[...]
            o32_ref[rows, :] = out
            o16_ref[rows, :] = out.astype(BF16)


def _mm_res_ln(a_list, w_list, res, lnw, lnb, tm=512, tk=None, tr=128):
    m = res.shape[0]
    n = res.shape[1]
    k = a_list[0].shape[1]
    tm = min(tm, m)
    tk = k if tk is None else min(tk, k)
    tr = min(tr, tm)
    n_in = len(a_list)
    steps = k // tk
    in_specs = ([pl.BlockSpec((tm, tk), lambda i, kk: (i, kk)) for _ in a_list]
                + [pl.BlockSpec((tk, n), lambda i, kk: (kk, 0)) for _ in w_list]
                + [pl.BlockSpec((tm, n), lambda i, kk: (i, 0)),
                   pl.BlockSpec((1, n), lambda i, kk: (0, 0)),
                   pl.BlockSpec((1, n), lambda i, kk: (0, 0))])
    return pl.pallas_call(
        functools.partial(_mm_res_ln_kernel, n_in=n_in, tk_steps=steps, tr=tr),
        grid=(m // tm, steps),
        in_specs=in_specs,
        out_specs=[pl.BlockSpec((tm, n), lambda i, kk: (i, 0)),
                   pl.BlockSpec((tm, n), lambda i, kk: (i, 0))],
        out_shape=[jax.ShapeDtypeStruct((m, n), F32), jax.ShapeDtypeStruct((m, n), BF16)],
        scratch_shapes=[pltpu.VMEM((tm, n), F32)] if steps > 1 else [],
        compiler_params=_cparams("arbitrary", "arbitrary"),
        name="mm_res_ln",
    )(*a_list, *w_list, res, lnw.reshape(1, n), lnb.reshape(1, n))


def _xattn_kernel(q_ref, k_ref, v_ref, o_ref, *, tr):
    kb = k_ref[0]
    vb = v_ref[0]
    for r0 in range(0, q_ref.shape[1], tr):
        s = _dot_nt(q_ref[0, r0:r0 + tr, :], kb) * (XA_HD ** -0.5)
        m = jnp.max(s, axis=-1, keepdims=True)
        p = jnp.exp(s - m)
        l = jnp.sum(p, axis=-1, keepdims=True)
        o = _dot(_bf(p / l), vb)
        o_ref[0, r0:r0 + tr, :] = o.astype(o_ref.dtype)


def _xattn(q, kv, ts=2048):
    b, s, d = q.shape
    mlen = kv.shape[1]
    ts = min(ts, s)
    return pl.pallas_call(
        functools.partial(_xattn_kernel, tr=min(256, ts)),
        grid=(b, s // ts, XA_HEADS),
        in_specs=[pl.BlockSpec((1, ts, XA_HD), lambda bi, i, h: (bi, i, h)),
                  pl.BlockSpec((1, mlen, XA_HD), lambda bi, i, h: (bi, 0, h)),
                  pl.BlockSpec((1, mlen, XA_HD), lambda bi, i, h: (bi, 0, XA_HEADS + h))],
        out_specs=pl.BlockSpec((1, ts, XA_HD), lambda bi, i, h: (bi, i, h)),
        out_shape=jax.ShapeDtypeStruct((b, s, d), BF16),
        compiler_params=_cparams("arbitrary", "arbitrary", "arbitrary"),
        name="xattn",
    )(q, kv, kv)


def _mm_taps_kernel(x_ref, w_ref, t_ref, o_ref, sc_ref, wb_ref=None, *, ts, ntap, act):
    first = pl.program_id(2) == 0

    @pl.when(first)
    def _():
        sc_ref[0:HIST, :] = jnp.zeros((HIST, sc_ref.shape[1]), F32)

    w = _weight_tile(w_ref, wb_ref, first & (pl.program_id(1) == 0))
    sc_ref[HIST:HIST + ts, :] = _dot(x_ref[0], w[...])
    y = t_ref[ntap - 1:ntap, :] * sc_ref[HIST:HIST + ts, :]
    for back in range(1, ntap):
        tap = ntap - 1 - back
        y = y + t_ref[tap:tap + 1, :] * sc_ref[HIST - back:HIST - back + ts, :]
    if act == "silu":
        y = y * _sigmoid(y)
    elif act == "lora":
        p0, p1, _ = RWKV_LORA_PAD
        y = jnp.concatenate([jnp.tanh(y[:, :p0]), y[:, p0:p0 + p1], _sigmoid(y[:, p0 + p1:])], axis=1)
    o_ref[0] = y.astype(o_ref.dtype)
    _carry_rows(sc_ref, ts)


def _mm_taps(xb, w, taps, out_dtype, act=None, ts=1024, tn=1024, col0=0, n=None, layer=0):
    b, s, d = xb.shape
    n, tn, joff, cast_scratch = _weight_cols(w, n, col0, tn)
    ts = min(ts, s)
    ntap = taps.shape[0]
    return pl.pallas_call(
        functools.partial(_mm_taps_kernel, ts=ts, ntap=ntap, act=act),
        grid=(n // tn, b, s // ts),
        in_specs=[pl.BlockSpec((1, ts, d), lambda j, bi, i: (bi, i, 0)),
                  _weight_spec(w, tn, layer, lambda j, bi, i: joff + j),
                  pl.BlockSpec((ntap, tn), lambda j, bi, i: (0, j))],
        out_specs=pl.BlockSpec((1, ts, tn), lambda j, bi, i: (bi, i, j)),
        out_shape=jax.ShapeDtypeStruct((b, s, n), out_dtype),
        scratch_shapes=[pltpu.VMEM((ts + HIST, tn), F32)] + cast_scratch,
        compiler_params=_cparams("arbitrary", "arbitrary", "arbitrary"),
        name="mm_taps",
    )(xb, w, taps.astype(F32))


def _ffn_up_kernel(x_ref, wg_ref, wu_ref, cg_ref, cu_ref, o_ref, sg_ref, su_ref, wgb_ref=None, wub_ref=None,
                   *, ts, tr):
    first = pl.program_id(2) == 0

    @pl.when(first)
    def _():
        sg_ref[0:HIST, :] = jnp.zeros((HIST, sg_ref.shape[1]), F32)
        su_ref[0:HIST, :] = jnp.zeros((HIST, su_ref.shape[1]), F32)

    new_tile = first & (pl.program_id(1) == 0)
    wg_ref = _weight_tile(wg_ref, wgb_ref, new_tile)
    wu_ref = _weight_tile(wu_ref, wub_ref, new_tile)

    def conv(sc_ref, c_ref, r0):
        y = c_ref[FFN_CONV - 1:FFN_CONV, :] * sc_ref[HIST + r0:HIST + r0 + tr, :]
        for back in range(1, FFN_CONV):
            tap = FFN_CONV - 1 - back
            y = y + c_ref[tap:tap + 1, :] * sc_ref[HIST + r0 - back:HIST + r0 - back + tr, :]
        return y

    for r0 in range(0, ts, tr):
        x = x_ref[0, r0:r0 + tr, :]
        sg_ref[HIST + r0:HIST + r0 + tr, :] = _dot(x, wg_ref[...])
        g = conv(sg_ref, cg_ref, r0)
        g = g * _sigmoid(g)
        su_ref[HIST + r0:HIST + r0 + tr, :] = _dot(x, wu_ref[...])
        u = conv(su_ref, cu_ref, r0)
        o_ref[0, r0:r0 + tr, :] = (g * u).astype(o_ref.dtype)
    _carry_rows(sg_ref, ts)
    _carry_rows(su_ref, ts)


def _ffn_up(xb, w_up, conv_w, ts=1024, tn=512, tr=1024, layer=0):
    b, s, d = xb.shape
    ts = min(ts, s)
    nj = D_FF // tn
    cast_scratch = [] if w_up.dtype == BF16 else [pltpu.VMEM((d, tn), BF16), pltpu.VMEM((d, tn), BF16)]
    return pl.pallas_call(
        functools.partial(_ffn_up_kernel, ts=ts, tr=min(tr, ts)),
        grid=(nj, b, s // ts),
        in_specs=[pl.BlockSpec((1, ts, d), lambda j, bi, i: (bi, i, 0)),
                  _weight_spec(w_up, tn, layer, lambda j, bi, i: j),
                  _weight_spec(w_up, tn, layer, lambda j, bi, i: nj + j),
                  pl.BlockSpec((FFN_CONV, tn), lambda j, bi, i: (0, j)),
                  pl.BlockSpec((FFN_CONV, tn), lambda j, bi, i: (0, nj + j))],
        out_specs=pl.BlockSpec((1, ts, tn), lambda j, bi, i: (bi, i, j)),
        out_shape=jax.ShapeDtypeStruct((b, s, D_FF), BF16),
        scratch_shapes=[pltpu.VMEM((ts + HIST, tn), F32), pltpu.VMEM((ts + HIST, tn), F32)] + cast_scratch,
        compiler_params=_cparams("arbitrary", "arbitrary", "arbitrary"),
        name="ffn_up",
    )(xb, w_up, w_up, conv_w, conv_w)


def _pool_kernel(u_ref, w_ref, sc_ref, o_ref, st_ref, *, ts):
    i = pl.program_id(1)

    @pl.when(i == 0)
    def _():
        st_ref[0:2 * HIST, :] = jnp.zeros((2 * HIST, st_ref.shape[1]), F32)

    st_ref[2 * HIST:2 * HIST + ts, :] = u_ref[0].astype(F32)
    pos = (i * ts + 1 + lax.broadcasted_iota(jnp.int32, (ts, 1), 0)).astype(F32)
    base = 2 * HIST
    for g, win in enumerate(POOL_WINDOWS):
        cols = slice(g * POOL_GDIM, (g + 1) * POOL_GDIM)
        cur = st_ref[base:base + ts, cols]
        acc = cur
        for back in range(1, win):
            acc = acc + st_ref[base - back:base - back + ts, cols]
        mean = acc / jnp.minimum(pos, float(win))
        y = _dot(_bf(mean - cur), w_ref[g])
        o_ref[0, :, cols] = (y * sc_ref[:, cols]).astype(o_ref.dtype)
    st_ref[0:2 * HIST, :] = st_ref[ts:ts + 2 * HIST, :]


def _pool_mixer(h_main, pool_w, pool_scale, ts=512):
    b, s, _ = h_main.shape
    ts = min(ts, s)
    return pl.pallas_call(
        functools.partial(_pool_kernel, ts=ts),
        grid=(b, s // ts),
        in_specs=[pl.BlockSpec((1, ts, HALF), lambda bi, i: (bi, i, 0)),
                  pl.BlockSpec((len(POOL_WINDOWS), POOL_GDIM, POOL_GDIM), lambda bi, i: (0, 0, 0)),
                  pl.BlockSpec((1, HALF), lambda bi, i: (0, 0))],
        out_specs=pl.BlockSpec((1, ts, HALF), lambda bi, i: (bi, i, 0)),
        out_shape=jax.ShapeDtypeStruct((b, s, HALF), BF16),
        scratch_shapes=[pltpu.VMEM((ts + 2 * HIST, HALF), F32)],
        compiler_params=_cparams("arbitrary", "arbitrary"),
        name="pool_mixer",
    )(h_main, pool_w, pool_scale.reshape(1, HALF))


def _gdn_kernel(q_ref, k_ref, v_ref, z_ref, g_ref, alog_ref, dtb_ref, nw_ref, o_ref, st_ref, *, ts, hps):
    c = CHUNK
    h0 = pl.program_id(1)

    @pl.when(pl.program_id(2) == 0)
    def _():
        st_ref[...] = jnp.zeros_like(st_ref)

    q_all = q_ref[0]
    k_all = k_ref[0]
    v_all = v_ref[0]
    z_all = z_ref[0].astype(F32)

    logits = g_ref[0]
    lane = lax.broadcasted_iota(jnp.int32, logits.shape, 1)
    g_all = -jnp.exp(alog_ref[...]) * _softplus(logits + dtb_ref[...])
    sig_all = _sigmoid(logits)

    causal, strict = _tri_masks(c)
    tril = causal.astype(BF16)
    eye = jnp.where(causal & (~strict), 1.0, 0.0).astype(F32)
    lane_c = lax.broadcasted_iota(jnp.int32, (c, GDN_DK), 1)
    nw = nw_ref[...]
    chunk_rows = [slice(ci * c, (ci + 1) * c) for ci in range(ts // c)]
    nck = len(chunk_rows)

    q, k, v, betas, g_cols = [], [], [], [], []
    for hd in range(hps):
        hl = slice(hd * GDN_DK, (hd + 1) * GDN_DK)
        qh, kh = q_all[:, hl], k_all[:, hl]
        qh = qh * lax.rsqrt(jnp.sum(qh * qh, axis=-1, keepdims=True) + NORM_EPS) * (GDN_DK ** -0.5)
        kh = kh * lax.rsqrt(jnp.sum(kh * kh, axis=-1, keepdims=True) + NORM_EPS)
        head = h0 * hps + hd
        g_col = jnp.sum(jnp.where(lane == head, g_all, 0.0), axis=1, keepdims=True)
        b_col = jnp.sum(jnp.where(lane == head + GDN_HEADS, sig_all, 0.0), axis=1, keepdims=True)
        for r in chunk_rows:
            q.append(qh[r])
            k.append(kh[r])
            v.append(v_all[r, hl])
            betas.append(jnp.broadcast_to(b_col[r], (c, GDN_DK)))
            g_cols.append(jnp.broadcast_to(g_col[r], (c, GDN_DK)))
    nprob = len(q)
    rows = list(range(nprob))
    gcs = [_dot01(tril, gcol) for gcol in g_cols]
    diffs = []
    for gc in gcs:
        g1, g2, g3 = (p.astype(F32) for p in _split3(gc))
        lhs = jnp.where(lane_c == 0, g1, jnp.where(lane_c == 1, g2, jnp.where(lane_c == 2, g3,
              jnp.where(lane_c < 6, 1.0, 0.0))))
        rhs = jnp.where(lane_c < 3, 1.0, jnp.where(lane_c == 3, -g1, jnp.where(lane_c == 4, -g2,
              jnp.where(lane_c == 5, -g3, 0.0))))
        diffs.append(_dot_nt(_bf(lhs), _bf(rhs)))
    decays = [jnp.where(causal, jnp.exp(jnp.where(causal, d, 0.0)), 0.0) for d in diffs]
    kbs = [k[r] * bt for r, bt in zip(rows, betas)]
    kcbs = [_bf(k[r]) for r in rows]
    lmats = [jnp.where(strict, _dot_nt(_bf(kb), kcb) * dc, 0.0) for kb, kcb, dc in zip(kbs, kcbs, decays)]
    intras = [_bf(jnp.where(causal, _dot_nt(_bf(q[r]), kcb) * dc, 0.0)) for r, kcb, dc in zip(rows, kcbs, decays)]
    tmats = [_bf(t) for t in _unit_lower_inverses([-l for l in lmats], eye)]
    egs = [jnp.exp(gc) for gc in gcs]
    g_lasts = [gc[c - 1:c, :] for gc in gcs]
    uws = [_dot(t, jnp.concatenate([_bf(v[r] * bt), _bf(kb * eg)], axis=1))
           for t, r, bt, kb, eg in zip(tmats, rows, betas, kbs, egs)]
    us = [uw[:, :GDN_DK] for uw in uws]
    ws = [_bf(uw[:, GDN_DK:]) for uw in uws]
    q_decs = [_bf(q[r] * eg) for r, eg in zip(rows, egs)]
    k_decs = [_bf(k[r] * jnp.exp(gl - gc)) for r, gl, gc in zip(rows, g_lasts, gcs)]
    e_lasts = [jnp.exp(gl) for gl in g_lasts]
    mns = [_dot_tn(kd, jnp.concatenate([w, _bf(u)], axis=1)) for kd, w, u in zip(k_decs, ws, us)]
    mmats = [_bf(mn[:, :GDN_DK]) for mn in mns]
    nmats = [mn[:, GDN_DK:] for mn in mns]

    states = [st_ref[hd] for hd in range(hps)]
    sb_hist = [[None] * nck for _ in range(hps)]
    for ci in range(nck):
        for hd in range(hps):
            p = hd * nck + ci
            sb = _bf(states[hd])
            sb_hist[hd][ci] = sb
            states[hd] = states[hd] * e_lasts[p] - _dot(mmats[p], sb) + nmats[p]
    for hd in range(hps):
        st_ref[hd] = states[hd]
    for hd in range(hps):
        hl = slice(hd * GDN_DK, (hd + 1) * GDN_DK)
        for ci in range(nck):
            p = hd * nck + ci
            sb = sb_hist[hd][ci]
            v_new = _bf(us[p] - _dot(ws[p], sb))
            o = _dot(q_decs[p], sb) + _dot(intras[p], v_new)
            o = o * lax.rsqrt(jnp.mean(o * o, axis=-1, keepdims=True) + NORM_EPS) * nw
            zc = z_all[chunk_rows[ci], hl]
            o_ref[0, chunk_rows[ci], hl] = (o * (zc * _sigmoid(zc))).astype(o_ref.dtype)


def _gated_deltanet(qkv, z, logits, a_log, dt_bias, norm_w, ts=256, hps=8):
    b, s, _ = qkv.shape
    ts = min(ts, s)
    hd = GDN_DK
    gw = hps * hd
    ng = HALF // gw
    pad = lambda t: jnp.pad(t.astype(F32), (0, hd - t.shape[0])).reshape(1, hd)
    col = lambda grp: (lambda bi, h, i: (bi, i, grp * ng + h))
    return pl.pallas_call(
        functools.partial(_gdn_kernel, ts=ts, hps=hps),
        grid=(b, ng, s // ts),
        in_specs=[pl.BlockSpec((1, ts, gw), col(0)), pl.BlockSpec((1, ts, gw), col(1)),
                  pl.BlockSpec((1, ts, gw), col(2)), pl.BlockSpec((1, ts, gw), col(0)),
                  pl.BlockSpec((1, ts, hd), lambda bi, h, i: (bi, i, 0)),
                  pl.BlockSpec((1, hd), lambda bi, h, i: (0, 0)),
                  pl.BlockSpec((1, hd), lambda bi, h, i: (0, 0)),
                  pl.BlockSpec((1, hd), lambda bi, h, i: (0, 0))],
        out_specs=pl.BlockSpec((1, ts, gw), lambda bi, h, i: (bi, i, h)),
        out_shape=jax.ShapeDtypeStruct((b, s, HALF), BF16),
        scratch_shapes=[pltpu.VMEM((hps, hd, hd), F32)],
        compiler_params=_cparams("arbitrary", "arbitrary", "arbitrary"),
        name="gated_deltanet",
    )(qkv, qkv, qkv, z, logits, pad(a_log), pad(dt_bias), norm_w.astype(F32).reshape(1, hd))


def _rwkv_kernel(r_ref, k_ref, v_ref, l_ref, w2_ref, a2_ref, g2_ref,
                 w0_ref, a0_ref, kk_ref, ka_ref, rk_ref, lnw_ref, lnb_ref, o_ref, st_ref, *, ts, hps):
    c = CHUNK
    hs = RWKV_HEAD

    @pl.when(pl.program_id(2) == 0)
    def _():
        st_ref[...] = jnp.zeros_like(st_ref)

    r = r_ref[0]
    k = k_ref[0]
    v = v_ref[0]
    lo = l_ref[0]
    p0, p1, p2 = RWKV_LORA_PAD
    w_log = -_softplus(-(w0_ref[...] + _dot(lo[:, :p0], w2_ref[...]))) - 0.5
    lw = -jnp.exp(w_log)
    a = _sigmoid(a0_ref[...] + _dot(lo[:, p0:p0 + p1], a2_ref[...]))
    g = _dot(lo[:, p0 + p1:p0 + p1 + p2], g2_ref[...])
    kk_raw = k * kk_ref[...]
    km = k * (1.0 + (a - 1.0) * ka_ref[...])
    head_of_lane = lax.broadcasted_iota(jnp.int32, (ts, hps * hs), 1) // hs

    def per_head_sum(x):
        out = jnp.zeros_like(x)
        for hd in range(hps):
            sel = head_of_lane == hd
            out = jnp.where(sel, jnp.sum(jnp.where(sel, x, 0.0), axis=-1, keepdims=True), out)
        return out

    kk = kk_raw * lax.rsqrt(per_head_sum(kk_raw * kk_raw) + NORM_EPS)
    aa = -kk
    bb = kk * a
    bonus = per_head_sum(r * km * rk_ref[...]) * v

    causal, strict = _tri_masks(c)
    tril = causal.astype(BF16)
    eye = jnp.where(causal & (~strict), 1.0, 0.0).astype(F32)
    lnw = lnw_ref[...]
    lnb = lnb_ref[...]

    rows = [slice(ci * c, (ci + 1) * c) for ci in range(ts // c)]
    nck = len(rows)
    heads = [slice(hd * hs, (hd + 1) * hs) for hd in range(hps)]
    cums = [_dot01(tril, lw[rw]) for rw in rows]
    c_lasts = [cum[c - 1:c, :] for cum in cums]
    w_invs = [jnp.exp(-cum) for cum in cums]
    w_tails = [jnp.exp(cl - cum) for cl, cum in zip(c_lasts, cums)]
    a_ts = [_bf(aa[rw] * jnp.exp(cum - lw[rw])) for rw, cum in zip(rows, cums)]
    r_ts = [_bf(r[rw] * jnp.exp(cum)) for rw, cum in zip(rows, cums)]
    b_ts = [_bf(bb[rw] * wi) for rw, wi in zip(rows, w_invs)]
    k_ts = [_bf(km[rw] * wi) for rw, wi in zip(rows, w_invs)]
    b_hs = [_bf(bb[rw] * wt) for rw, wt in zip(rows, w_tails)]
    k_hs = [_bf(km[rw] * wt) for rw, wt in zip(rows, w_tails)]
    vcs = [_bf(v[rw]) for rw in rows]
    e_lasts = [jnp.exp(cl) for cl in c_lasts]
    prob = [(ci, cs) for ci in range(nck) for cs in heads]
    row2 = lax.broadcasted_iota(jnp.int32, (c, 2 * c), 0)
    col2 = lax.broadcasted_iota(jnp.int32, (c, 2 * c), 1) % c
    bks = [jnp.concatenate([b_ts[ci][:, cs], k_ts[ci][:, cs]], axis=0) for ci, cs in prob]
    a_abks = [jnp.where(row2 > col2, _dot_nt(a_ts[ci][:, cs], bk), 0.0) for bk, (ci, cs) in zip(bks, prob)]
    a_rbks = [_bf(jnp.where(row2 >= col2, _dot_nt(r_ts[ci][:, cs], bk), 0.0)) for bk, (ci, cs) in zip(bks, prob)]
    tmats = [_bf(t) for t in _unit_lower_inverses([m[:, :c] for m in a_abks], eye)]
    zero_c = jnp.zeros((c, hs), BF16)
    avs = [_bf(_dot(_bf(abk), jnp.concatenate([zero_c, vcs[ci][:, cs]], axis=0)))
           for abk, (ci, cs) in zip(a_abks, prob)]
    pmats = [_bf(_dot(t, a_ts[ci][:, cs])) for t, (ci, cs) in zip(tmats, prob)]
    qmats = [_dot(t, av) for t, av in zip(tmats, avs)]
    mmats = [_bf(_dot_tn(pm, b_hs[ci][:, cs])) for pm, (ci, cs) in zip(pmats, prob)]
    nmats = [_dot_tn(jnp.concatenate([_bf(qm), vcs[ci][:, cs]], axis=0),
                     jnp.concatenate([b_hs[ci][:, cs], k_hs[ci][:, cs]], axis=0))
             for qm, (ci, cs) in zip(qmats, prob)]

    states = [st_ref[hd] for hd in range(hps)]
    sb_hist = []
    for ci in range(nck):
        sbs = [_bf(s) for s in states]
        sb_hist.append(sbs)
        states = [states[hd] * e_lasts[ci][:, heads[hd]] + _dot(sbs[hd], mmats[hps * ci + hd]) + nmats[hps * ci + hd]
                  for hd in range(hps)]
    for hd in range(hps):
        st_ref[hd] = states[hd]
    ys = []
    for ci in range(nck):
        sbs = sb_hist[ci]
        ubs = [_bf(_dot_nt(pmats[hps * ci + hd], sbs[hd]) + qmats[hps * ci + hd]) for hd in range(hps)]
        ys.append([_dot_nt(r_ts[ci][:, heads[hd]], sbs[hd])
                   + _dot(a_rbks[hps * ci + hd], jnp.concatenate([ubs[hd], vcs[ci][:, heads[hd]]], axis=0))
                   for hd in range(hps)])
    for ci in range(nck):
        outs = []
        for y in ys[ci]:
            ym = jnp.mean(y, axis=-1, keepdims=True)
            yd = y - ym
            yv = jnp.mean(yd * yd, axis=-1, keepdims=True)
            outs.append(yd * lax.rsqrt(yv + RWKV_LNX_EPS))
        yn = jnp.concatenate(outs, axis=-1) * lnw + lnb
        o_ref[0, rows[ci], :] = ((yn + bonus[rows[ci]]) * g[rows[ci]]).astype(o_ref.dtype)


def _rwkv7(hr, hl, w2p, a2p, g2p, w0, a0, k_k, k_a, r_k, lnx_w, lnx_b, ts=128, hps=16):
    b, s, _ = hr.shape
    ts = min(ts, s)
    pw = hps * RWKV_HEAD
    npair = HALF // pw
    lw = sum(RWKV_LORA_PAD)
    col = lambda grp: (lambda bi, hp, i: (bi, i, grp * npair + hp))
    vec = lambda grp: pl.BlockSpec((1, pw), lambda bi, hp, i: (0, grp * npair + hp))
    row1 = lambda t: t.astype(F32).reshape(1, -1)
    return pl.pallas_call(
        functools.partial(_rwkv_kernel, ts=ts, hps=hps),
        grid=(b, npair, s // ts),
        in_specs=[pl.BlockSpec((1, ts, pw), col(0)), pl.BlockSpec((1, ts, pw), col(1)),
                  pl.BlockSpec((1, ts, pw), col(2)),
                  pl.BlockSpec((1, ts, lw), lambda bi, hp, i: (bi, i, 0)),
                  pl.BlockSpec((RWKV_LORA_PAD[0], pw), lambda bi, hp, i: (0, hp)),
                  pl.BlockSpec((RWKV_LORA_PAD[1], pw), lambda bi, hp, i: (0, hp)),
                  pl.BlockSpec((RWKV_LORA_PAD[2], pw), lambda bi, hp, i: (0, hp)),
                  vec(0), vec(0), vec(0), vec(0), vec(0), vec(0), vec(0)],
        out_specs=pl.BlockSpec((1, ts, pw), lambda bi, hp, i: (bi, i, hp)),
        out_shape=jax.ShapeDtypeStruct((b, s, HALF), BF16),
        scratch_shapes=[pltpu.VMEM((hps, RWKV_HEAD, RWKV_HEAD), F32)],
        compiler_params=_cparams("arbitrary", "arbitrary", "arbitrary"),
        name="rwkv7",
    )(hr, hr, hr, hl, w2p, a2p, g2p,
      row1(w0), row1(a0), row1(k_k), row1(k_a), row1(r_k), row1(lnx_w), row1(lnx_b))


def _fox_cumsum_kernel(f_ref, bf_ref, o_ref):
    x = f_ref[0] + bf_ref[...]
    ls = jnp.minimum(x, 0.0) - jnp.log(1.0 + jnp.exp(-jnp.abs(x)))
    nh, nr, nl = ls.shape
    li = lax.broadcasted_iota(jnp.int32, (nl, nl), 0)
    lj = lax.broadcasted_iota(jnp.int32, (nl, nl), 1)
    upper = (li <= lj).astype(BF16)
    ri = lax.broadcasted_iota(jnp.int32, (nr, nr), 0)
    rj = lax.broadcasted_iota(jnp.int32, (nr, nr), 1)
    below = (ri > rj).astype(BF16)
    for h in range(nh):
        within = _dot01_right(ls[h], upper)
        tot = jnp.broadcast_to(within[:, nl - 1:nl], (nr, nl))
        o_ref[0, h] = within + _dot01(below, tot)


def _dot01_right(x, m01):
    x1, x2, x3 = _split3(x)
    return _dot(x1, m01) + _dot(x2, m01) + _dot(x3, m01)


def _fox_cumsum(f_t, b_f):
    b, nh, s = f_t.shape
    nl = 128
    nr = s // nl
    out = pl.pallas_call(
        _fox_cumsum_kernel,
        grid=(b,),
        in_specs=[pl.BlockSpec((1, nh, nr, nl), lambda bi: (bi, 0, 0, 0)),
                  pl.BlockSpec((nh, 1, 1), lambda bi: (0, 0, 0))],
        out_specs=pl.BlockSpec((1, nh, nr, nl), lambda bi: (bi, 0, 0, 0)),
        out_shape=jax.ShapeDtypeStruct((b, nh, nr, nl), F32),
        compiler_params=_cparams("arbitrary"),
        name="fox_cumsum",
    )(f_t.reshape(b, nh, nr, nl), b_f.astype(F32).reshape(nh, 1, 1))
    return out.reshape(b, nh, 1, s)


def _fox_kernel(qi_ref, ki_ref, q_ref, k_ref, v_ref, cq_ref, ck_ref, o_ref, qs_ref, m_ref, acc_ref, *, tq, tk, tr):
    p = pl.program_id(2)
    qi = qi_ref[p]
    ki = ki_ref[p]
    last_k = ((qi + 1) * tq - 1) // tk
    log2e = 1.0 / math.log(2.0)

    @pl.when(ki == 0)
    def _():
        qs_ref[...] = (q_ref[0].astype(F32) * (FOX_HD ** -0.5 * log2e)).astype(BF16)
        m_ref[...] = jnp.full_like(m_ref, -jnp.inf)
        acc_ref[...] = jnp.zeros_like(acc_ref)

    def step(masked):
        bias = (cq_ref[0, 0, :, 0:1] - ck_ref[0, 0]) * log2e
        v_aug = jnp.concatenate([v_ref[0], jnp.ones((tk, FOX_HD), BF16)], axis=1)
        kb = k_ref[0]
        for r0 in range(0, tq, tr):
            rows = slice(r0, r0 + tr)
            s = _dot_nt(qs_ref[rows, :], kb) + bias
            if masked:
                qpos = qi * tq + r0 + lax.broadcasted_iota(jnp.int32, (tr, tk), 0)
                kpos = ki * tk + lax.broadcasted_iota(jnp.int32, (tr, tk), 1)
                s = jnp.where(kpos <= qpos, s, -jnp.inf)
            m_old = m_ref[rows, :]
            m_new = jnp.maximum(m_old, jnp.max(s, axis=-1, keepdims=True))
            corr = jnp.exp2(m_old - m_new)
            pexp = _bf(jnp.exp2(s - m_new))
            acc_ref[rows, :] = corr * acc_ref[rows, :] + _dot(pexp, v_aug)
            m_ref[rows, :] = m_new

    needs_mask = (ki + 1) * tk - 1 > qi * tq

    @pl.when(needs_mask)
    def _():
        step(True)

    @pl.when(jnp.logical_not(needs_mask))
    def _():
        step(False)

    @pl.when(ki == last_k)
    def _():
        acc = acc_ref[...]
        o_ref[0] = (acc[:, :FOX_HD] / acc[:, FOX_HD:FOX_HD + 1]).astype(o_ref.dtype)


def _fox_attention(hf, c, tq=2048, tk=2048, tr=128):
    b, s, _ = hf.shape
    tq = min(tq, s)
    tk = min(tk, s)
    nh = FOX_HEADS
    pairs = [(qi, ki) for qi in range(s // tq) for ki in range(((qi + 1) * tq - 1) // tk + 1)]
    qi_arr = jnp.asarray(np.array([p[0] for p in pairs], np.int32))
    ki_arr = jnp.asarray(np.array([p[1] for p in pairs], np.int32))
    grid_spec = pltpu.PrefetchScalarGridSpec(
        num_scalar_prefetch=2,
        grid=(b, nh, len(pairs)),
        in_specs=[pl.BlockSpec((1, tq, FOX_HD), lambda bi, h, p, qa, ka: (bi, qa[p], h)),
                  pl.BlockSpec((1, tk, FOX_HD), lambda bi, h, p, qa, ka: (bi, ka[p], nh + h)),
                  pl.BlockSpec((1, tk, FOX_HD), lambda bi, h, p, qa, ka: (bi, ka[p], 2 * nh + h)),
                  pl.BlockSpec((1, 1, 1, tq), lambda bi, h, p, qa, ka: (bi, h, 0, qa[p])),
                  pl.BlockSpec((1, 1, 1, tk), lambda bi, h, p, qa, ka: (bi, h, 0, ka[p]))],
        out_specs=pl.BlockSpec((1, tq, FOX_HD), lambda bi, h, p, qa, ka: (bi, qa[p], h)),
        scratch_shapes=[pltpu.VMEM((tq, FOX_HD), BF16), pltpu.VMEM((tq, 1), F32),
                        pltpu.VMEM((tq, 2 * FOX_HD), F32)])
    return pl.pallas_call(
        functools.partial(_fox_kernel, tq=tq, tk=tk, tr=min(tr, tq)),
        grid_spec=grid_spec,
        out_shape=jax.ShapeDtypeStruct((b, s, HALF), BF16),
        compiler_params=_cparams("arbitrary", "arbitrary", "arbitrary"),
        name="fox_attention",
    )(qi_arr, ki_arr, hf, hf, hf, c, c)


def _pad_cols(w, width):
    return jnp.pad(w, ((0, 0), (0, width - w.shape[1])))


def _pad_rows(w, height):
    return jnp.pad(w, ((0, height - w.shape[0]), (0, 0)))


def _even_mixer(x32, xb, w_in, layer, pool_w, pool_scale, conv_w, a_log, dt_bias, norm_w, w_out, ln_w, ln_b):
    b, s, d = x32.shape
    t = b * s
    n_main = 5 * HALF
    w_gate = _bf(_pad_cols(w_in[layer, :, n_main:], 128))
    if xb is None:
        u_pool, xb2 = _mm_cast(x32.reshape(t, d), w_in, F32, col0=0, n=HALF, layer=layer)
        xb = xb2.reshape(b, s, d)
    else:
        xb2 = xb.reshape(t, d)
        u_pool = _mm(xb2, w_in, F32, col0=0, n=HALF, layer=layer)
    u_pool = u_pool.reshape(b, s, HALF)
    qkv = _mm_taps(xb, w_in, conv_w, F32, act="silu", col0=HALF, n=3 * HALF, layer=layer)
    z = _mm(xb2, w_in, BF16, col0=4 * HALF, n=HALF, layer=layer).reshape(b, s, HALF)
    logits = _mm(xb2, w_gate, F32, tn=128).reshape(b, s, 128)
    y_a = _pool_mixer(u_pool, _bf(pool_w), pool_scale)
    y_b = _gated_deltanet(qkv, z, logits, a_log, dt_bias, norm_w)
    wo = _bf(w_out)
    return _mm_res_ln([y_a.reshape(t, HALF), y_b.reshape(t, HALF)], [wo[:HALF], wo[HALF:]],
                      x32.reshape(t, d), ln_w, ln_b)


def _odd_mixer(x32, xb, w_in_all, layer, mu, w0, w2, a0, a2, g2, k_k, k_a, r_k, lnx_w, lnx_b, b_f, w_out, ln_w, ln_b):
    b, s, d = x32.shape
    t = b * s
    l0, l1, l2 = RWKV_LORA
    p0, p1, p2 = RWKV_LORA_PAD
    o_l = 3 * HALF
    o_f = o_l + l0 + l1 + l2
    w_tail = w_in_all[layer, :, o_l:]
    w_l = _bf(jnp.concatenate([_pad_cols(w_tail[:, :l0], p0),
                               _pad_cols(w_tail[:, l0:l0 + l1], p1),
                               _pad_cols(w_tail[:, l0 + l1:o_f - o_l], p2)], axis=1))
    w_fox = _bf(w_tail[:, o_f - o_l:o_f - o_l + 3 * HALF])
    w_fl = _bf(_pad_cols(w_tail[:, o_f - o_l + 3 * HALF:], 128))
    mu_l = jnp.concatenate([jnp.pad(mu[o_l:o_l + l0], (0, p0 - l0)),
                            jnp.pad(mu[o_l + l0:o_l + l0 + l1], (0, p1 - l1)),
                            jnp.pad(mu[o_l + l0 + l1:o_f], (0, p2 - l2))])
    xb2 = xb.reshape(t, d)
    lerp = lambda m: jnp.stack([m, 1.0 - m])
    hr = _mm_taps(xb, w_in_all, lerp(mu[:o_l]), F32, col0=0, n=o_l, layer=layer)
    hl = _mm_taps(xb, w_l, lerp(mu_l), BF16, act="lora", tn=p0 + p1 + p2)
    hf = _mm(xb2, w_fox, BF16).reshape(b, s, 3 * HALF)
    fl = _mm(xb2, w_fl, F32, tn=128).reshape(b, s, 128)
    y_c = _rwkv7(hr, hl, _bf(_pad_rows(w2, p0)), _bf(_pad_rows(a2, p1)), _bf(_pad_rows(g2, p2)),
                 w0, a0, k_k, k_a, r_k, lnx_w, lnx_b)
    c = _fox_cumsum(jnp.transpose(fl[:, :, :FOX_HEADS], (0, 2, 1)), b_f)
    y_d = _fox_attention(hf, c)
    wo = _bf(w_out)
    return _mm_res_ln([y_c.reshape(t, HALF), y_d.reshape(t, HALF)], [wo[:HALF], wo[HALF:]],
                      x32.reshape(t, d), ln_w, ln_b)


def _cross_attention(x32, xb, mem_b, w_q, w_kv, layer, w_o, ln_w, ln_b):
    t, d = x32.shape
    b, mlen, _ = mem_b.shape
    s = t // b
    q = _mm(xb, w_q, BF16, layer=layer).reshape(b, s, d)
    kv = _mm(mem_b.reshape(b * mlen, d), w_kv, BF16, layer=layer).reshape(b, mlen, 2 * d)
    o = _xattn(q, kv).reshape(t, d)
    return _mm_res_ln([o], [_bf(w_o)], x32, ln_w, ln_b)


def _conv_glu_ffn(x32, xb, b, w_up, layer, conv_w, w_down, ln_w, ln_b):
    t, d = x32.shape
    hmid = _ffn_up(xb.reshape(b, t // b, d), w_up, conv_w, layer=layer)
    return _mm_res_ln([hmid.reshape(t, D_FF)], [_bf(w_down)], x32, ln_w, ln_b, tk=D_FF // 2)


def kernel(x, mem, ev_w_in, pool_w, pool_scale, gdn_conv_w, gdn_a_log, gdn_dt_bias, gdn_norm_w, ev_w_out,
           od_w_in, rwkv_mu, rwkv_w0, rwkv_w2, rwkv_a0, rwkv_a2, rwkv_g2, rwkv_k_k, rwkv_k_a, rwkv_r_k,
           rwkv_lnx_w, rwkv_lnx_b, fox_b_f, od_w_out,
           ln_mix_w, ln_mix_b, xa_w_q, xa_w_kv, xa_w_o, ln_xa_w, ln_xa_b,
           ffn_w_up, ffn_conv_w, ffn_w_down, ln_ffn_w, ln_ffn_b):
    b, s, d = x.shape
    depth = ln_mix_w.shape[0]
    mem_b = _bf(mem)
    x32 = x
    xb = None
    for i in range(depth):
        j = i // 2
        x3 = x32.reshape(b, s, d)
        xb3 = None if xb is None else xb.reshape(b, s, d)
        if i % 2 == 0:
            x32, xb = _even_mixer(x3, xb3, ev_w_in, j, pool_w[j], pool_scale[j], gdn_conv_w[j], gdn_a_log[j],
                                  gdn_dt_bias[j], gdn_norm_w[j], ev_w_out[j], ln_mix_w[i], ln_mix_b[i])
        else:
            x32, xb = _odd_mixer(x3, xb3, od_w_in, j, rwkv_mu[j], rwkv_w0[j], rwkv_w2[j], rwkv_a0[j], rwkv_a2[j],
                                 rwkv_g2[j], rwkv_k_k[j], rwkv_k_a[j], rwkv_r_k[j].reshape(-1), rwkv_lnx_w[j],
                                 rwkv_lnx_b[j], fox_b_f[j], od_w_out[j], ln_mix_w[i], ln_mix_b[i])
        x32, xb = _cross_attention(x32, xb, mem_b, xa_w_q, xa_w_kv, i, xa_w_o[i], ln_xa_w[i], ln_xa_b[i])
        x32, xb = _conv_glu_ffn(x32, xb, b, ffn_w_up, i, ffn_conv_w[i], ffn_w_down[i], ln_ffn_w[i], ln_ffn_b[i])
    return x32.reshape(b, s, d)
```

```python
import functools
import math

import numpy as np
import jax
import jax.numpy as jnp
from jax import lax
from jax.experimental import pallas as pl
from jax.experimental.pallas import tpu as pltpu

F32 = jnp.float32
BF16 = jnp.bfloat16

D_MODEL = 2048
HALF = D_MODEL // 2
POOL_WINDOWS = (2, 4, 8, 16)
POOL_GDIM = HALF // len(POOL_WINDOWS)
GDN_HEADS = 8
GDN_DK = HALF // GDN_HEADS
GDN_CONV = 4
NORM_EPS = 1e-6
RWKV_HEAD = 64
RWKV_LNX_EPS = 64e-5
RWKV_LORA = (64, 64, 160)
RWKV_LORA_PAD = (128, 128, 256)
FOX_HEADS = 8
FOX_HD = HALF // FOX_HEADS
XA_HEADS = 4
XA_HD = D_MODEL // XA_HEADS
D_FF = 5632
FFN_CONV = 3
DEPTH = 2
DEEPNORM_ALPHA = float((2 * DEPTH) ** 0.25)
LN_EPS = 1e-5

CHUNK = 64
HIST = 8
VMEM_LIMIT_BYTES = 56 * 1024 * 1024


def _cparams(*sem):
    return pltpu.CompilerParams(dimension_semantics=sem, vmem_limit_bytes=VMEM_LIMIT_BYTES)


def _dot(a, b):
    return jnp.dot(a, b, preferred_element_type=F32)


def _dot_nt(a, b):
    return lax.dot_general(a, b, (((1,), (1,)), ((), ())), preferred_element_type=F32)


def _dot_tn(a, b):
    return lax.dot_general(a, b, (((0,), (0,)), ((), ())), preferred_element_type=F32)


def _bf(x):
    return x.astype(BF16)


def _split3(x):
    x1 = x.astype(BF16)
    r1 = x - x1.astype(F32)
    x2 = r1.astype(BF16)
    x3 = (r1 - x2.astype(F32)).astype(BF16)
    return x1, x2, x3


def _dot01(m01, x):
    x1, x2, x3 = _split3(x)
    return _dot(m01, x1) + _dot(m01, x2) + _dot(m01, x3)


def _unit_lower_inverses(n_mats, eye, fillers=()):
    c = eye.shape[0]
    levels = int(math.log2(c))
    fillers = list(fillers)
    xs = [eye + n for n in n_mats]
    nbs = [_bf(n) for n in n_mats]
    ps = [_dot(nb, nb) for nb in nbs]
    for level in range(1, levels):
        pbs = [_bf(p) for p in ps]
        if level == levels - 1:
            prods = [_dot(_bf(x), pb) for x, pb in zip(xs, pbs)]
        else:
            prods = [_dot(jnp.concatenate([pb, _bf(x)], axis=0), pb) for x, pb in zip(xs, pbs)]
        if fillers:
            fillers.pop(0)()
        if level == levels - 1:
            xs = [x + pr for x, pr in zip(xs, prods)]
        else:
            xs = [x + pr[c:] for x, pr in zip(xs, prods)]
            ps = [pr[:c] for pr in prods]
    for filler in fillers:
        filler()
    return xs


def _sigmoid(x):
    return 1.0 / (1.0 + jnp.exp(-x))


def _softplus(x):
    return jnp.maximum(x, 0.0) + jnp.log(1.0 + jnp.exp(-jnp.abs(x)))


def _tri_masks(c):
    row = lax.broadcasted_iota(jnp.int32, (c, c), 0)
    col = lax.broadcasted_iota(jnp.int32, (c, c), 1)
    return row >= col, row > col


def _stage_rows(x_ref, sc_ref, ts, first):
    @pl.when(first)
    def _():
        sc_ref[0:HIST, :] = jnp.zeros((HIST, sc_ref.shape[1]), F32)
    sc_ref[HIST:HIST + ts, :] = x_ref[0].astype(F32)


def _carry_rows(sc_ref, ts):
    sc_ref[0:HIST, :] = sc_ref[ts:ts + HIST, :]


def _weight_tile(w_ref, wb_ref, first):
    if wb_ref is None:
        return w_ref
    @pl.when(first)
    def _():
        wb_ref[...] = w_ref[...].astype(BF16)
    return wb_ref


def _weight_cols(w, n, col0, tn):
    n = w.shape[-1] - col0 if n is None else n
    tn = min(tn, n)
    assert n % tn == 0 and col0 % tn == 0, (n, col0, tn)
    scratch = [] if w.dtype == BF16 else [pltpu.VMEM((w.shape[-2], tn), BF16)]
    return n, tn, col0 // tn, scratch


def _weight_spec(w, tn, layer, col_of):
    k = w.shape[-2]
    if w.ndim == 2:
        return pl.BlockSpec((k, tn), lambda *g: (0, col_of(*g)))
    return pl.BlockSpec((None, k, tn), lambda *g: (layer, 0, col_of(*g)))


def _mm_kernel(a_ref, w_ref, o_ref, wb_ref=None):
    w = _weight_tile(w_ref, wb_ref, pl.program_id(1) == 0)
    o_ref[...] = _dot(a_ref[...], w[...]).astype(o_ref.dtype)


def _mm_cast_kernel(a_ref, w_ref, o_ref, ab_ref, wb_ref=None):
    w = _weight_tile(w_ref, wb_ref, pl.program_id(0) == 0)
    ab = a_ref[...].astype(BF16)
    ab_ref[...] = ab
    o_ref[...] = _dot(ab, w[...]).astype(o_ref.dtype)


def _mm_cast(a, w, out_dtype, tm=512, col0=0, n=None, layer=0):
    m, k = a.shape
    n, tn, joff, scratch = _weight_cols(w, n, col0, w.shape[-1])
    tm = min(tm, m)
    return pl.pallas_call(
        _mm_cast_kernel,
        grid=(m // tm,),
        in_specs=[pl.BlockSpec((tm, k), lambda i: (i, 0)),
                  _weight_spec(w, tn, layer, lambda i: joff)],
        out_specs=[pl.BlockSpec((tm, tn), lambda i: (i, 0)), pl.BlockSpec((tm, k), lambda i: (i, 0))],
        out_shape=[jax.ShapeDtypeStruct((m, n), out_dtype), jax.ShapeDtypeStruct((m, k), BF16)],
        scratch_shapes=scratch,
        compiler_params=_cparams("arbitrary"),
        name="mm_cast",
    )(a, w)


def _mm(a, w, out_dtype, tm=1024, tn=1024, col0=0, n=None, layer=0):
    m, k = a.shape
    n, tn, joff, scratch = _weight_cols(w, n, col0, tn)
    tm = min(tm, m)
    return pl.pallas_call(
        _mm_kernel,
        grid=(n // tn, m // tm),
        in_specs=[pl.BlockSpec((tm, k), lambda j, i: (i, 0)),
                  _weight_spec(w, tn, layer, lambda j, i: joff + j)],
        out_specs=pl.BlockSpec((tm, tn), lambda j, i: (i, j)),
        out_shape=jax.ShapeDtypeStruct((m, n), out_dtype),
        scratch_shapes=scratch,
        compiler_params=_cparams("arbitrary", "arbitrary"),
        name="mm",
    )(a, w)


def _mm_res_ln_kernel(*refs, n_in, tk_steps, tr):
    a_refs = refs[:n_in]
    w_refs = refs[n_in:2 * n_in]
    res_ref, lnw_ref, lnb_ref, o32_ref, o16_ref = refs[2 * n_in:2 * n_in + 5]
    acc_ref = refs[2 * n_in + 5] if tk_steps > 1 else None
    kk = pl.program_id(1)
    tm = res_ref.shape[0]

    def partial_sum(rows):
        tot = None
        for a_ref, w_ref in zip(a_refs, w_refs):
            d = _dot(a_ref[rows, :], w_ref[...])
            tot = d if tot is None else tot + d
        return tot

    if tk_steps > 1:
        @pl.when(kk == 0)
        def _():
            acc_ref[...] = partial_sum(slice(None))

        @pl.when((kk > 0) & (kk < tk_steps - 1))
        def _():
            acc_ref[...] += partial_sum(slice(None))

    @pl.when(kk == tk_steps - 1)
    def _():
        for r0 in range(0, tm, tr):
            rows = slice(r0, r0 + tr)
            y = DEEPNORM_ALPHA * res_ref[rows, :] + partial_sum(rows)
            if tk_steps > 1:
                y = y + acc_ref[rows, :]
            mu = jnp.mean(y, axis=-1, keepdims=True)
            d = y - mu
            var = jnp.mean(d * d, axis=-1, keepdims=True)
            out = d * lax.rsqrt(var + LN_EPS) * lnw_ref[...] + lnb_ref[...]
            o32_ref[rows, :] = out
            o16_ref[rows, :] = out.astype(BF16)


def _mm_res_ln(a_list, w_list, res, lnw, lnb, tm=512, tk=None, tr=128):
    m = res.shape[0]
    n = res.shape[1]
    k = a_list[0].shape[1]
    tm = min(tm, m)
    tk = k if tk is None else min(tk, k)
    tr = min(tr, tm)
    n_in = len(a_list)
    steps = k // tk
    w_mode = {"pipeline_mode": pl.Buffered(1)} if steps == 1 else {}
    in_specs = ([pl.BlockSpec((tm, tk), lambda i, kk: (i, kk)) for _ in a_list]
                + [pl.BlockSpec((tk, n), lambda i, kk: (kk, 0), **w_mode) for _ in w_list]
                + [pl.BlockSpec((tm, n), lambda i, kk: (i, 0)),
                   pl.BlockSpec((1, n), lambda i, kk: (0, 0)),
                   pl.BlockSpec((1, n), lambda i, kk: (0, 0))])
    return pl.pallas_call(
        functools.partial(_mm_res_ln_kernel, n_in=n_in, tk_steps=steps, tr=tr),
        grid=(m // tm, steps),
        in_specs=in_specs,
        out_specs=[pl.BlockSpec((tm, n), lambda i, kk: (i, 0)),
                   pl.BlockSpec((tm, n), lambda i, kk: (i, 0))],
        out_shape=[jax.ShapeDtypeStruct((m, n), F32), jax.ShapeDtypeStruct((m, n), BF16)],
        scratch_shapes=[pltpu.VMEM((tm, n), F32)] if steps > 1 else [],
        compiler_params=_cparams("arbitrary", "arbitrary"),
        name="mm_res_ln",
    )(*a_list, *w_list, res, lnw.reshape(1, n), lnb.reshape(1, n))


def _xattn_kernel(q_ref, k_ref, v_ref, o_ref, *, tr):
    kb = k_ref[0]
    vb = v_ref[0]
    for r0 in range(0, q_ref.shape[1], tr):
        s = _dot_nt(q_ref[0, r0:r0 + tr, :], kb) * (XA_HD ** -0.5)
        m = jnp.max(s, axis=-1, keepdims=True)
        p = jnp.exp(s - m)
        l = jnp.sum(p, axis=-1, keepdims=True)
        o = _dot(_bf(p / l), vb)
        o_ref[0, r0:r0 + tr, :] = o.astype(o_ref.dtype)


def _xattn(q, kv, ts=2048):
    b, s, d = q.shape
    mlen = kv.shape[1]
    ts = min(ts, s)
    return pl.pallas_call(
        functools.partial(_xattn_kernel, tr=min(256, ts)),
        grid=(b, s // ts, XA_HEADS),
        in_specs=[pl.BlockSpec((1, ts, XA_HD), lambda bi, i, h: (bi, i, h)),
                  pl.BlockSpec((1, mlen, XA_HD), lambda bi, i, h: (bi, 0, h)),
                  pl.BlockSpec((1, mlen, XA_HD), lambda bi, i, h: (bi, 0, XA_HEADS + h))],
        out_specs=pl.BlockSpec((1, ts, XA_HD), lambda bi, i, h: (bi, i, h)),
        out_shape=jax.ShapeDtypeStruct((b, s, d), BF16),
        compiler_params=_cparams("arbitrary", "arbitrary", "arbitrary"),
        name="xattn",
    )(q, kv, kv)


def _mm_taps_kernel(x_ref, w_ref, t_ref, o_ref, sc_ref, wb_ref=None, *, ts, ntap, act):
    first = pl.program_id(2) == 0

    @pl.when(first)
    def _():
        sc_ref[0:HIST, :] = jnp.zeros((HIST, sc_ref.shape[1]), F32)

    w = _weight_tile(w_ref, wb_ref, first & (pl.program_id(1) == 0))
    sc_ref[HIST:HIST + ts, :] = _dot(x_ref[0], w[...])
    y = t_ref[ntap - 1:ntap, :] * sc_ref[HIST:HIST + ts, :]
    for back in range(1, ntap):
        tap = ntap - 1 - back
        y = y + t_ref[tap:tap + 1, :] * sc_ref[HIST - back:HIST - back + ts, :]
    if act == "silu":
        y = y * _sigmoid(y)
    elif act == "lora":
        p0, p1, _ = RWKV_LORA_PAD
        y = jnp.concatenate([jnp.tanh(y[:, :p0]), y[:, p0:p0 + p1], _sigmoid(y[:, p0 + p1:])], axis=1)
    o_ref[0] = y.astype(o_ref.dtype)
    _carry_rows(sc_ref, ts)


def _mm_taps(xb, w, taps, out_dtype, act=None, ts=1024, tn=1024, col0=0, n=None, layer=0):
    b, s, d = xb.shape
    n, tn, joff, cast_scratch = _weight_cols(w, n, col0, tn)
    ts = min(ts, s)
    ntap = taps.shape[0]
    return pl.pallas_call(
        functools.partial(_mm_taps_kernel, ts=ts, ntap=ntap, act=act),
        grid=(n // tn, b, s // ts),
        in_specs=[pl.BlockSpec((1, ts, d), lambda j, bi, i: (bi, i, 0)),
                  _weight_spec(w, tn, layer, lambda j, bi, i: joff + j),
                  pl.BlockSpec((ntap, tn), lambda j, bi, i: (0, j))],
        out_specs=pl.BlockSpec((1, ts, tn), lambda j, bi, i: (bi, i, j)),
        out_shape=jax.ShapeDtypeStruct((b, s, n), out_dtype),
        scratch_shapes=[pltpu.VMEM((ts + HIST, tn), F32)] + cast_scratch,
        compiler_params=_cparams("arbitrary", "arbitrary", "arbitrary"),
        name="mm_taps",
    )(xb, w, taps.astype(F32))


def _ffn_up_kernel(x_ref, wg_ref, wu_ref, cg_ref, cu_ref, o_ref, sg_ref, su_ref, wgb_ref=None, wub_ref=None,
                   *, ts, tr):
    first = pl.program_id(2) == 0

    @pl.when(first)
    def _():
        sg_ref[0:HIST, :] = jnp.zeros((HIST, sg_ref.shape[1]), F32)
        su_ref[0:HIST, :] = jnp.zeros((HIST, su_ref.shape[1]), F32)

    new_tile = first & (pl.program_id(1) == 0)
    wg_ref = _weight_tile(wg_ref, wgb_ref, new_tile)
    wu_ref = _weight_tile(wu_ref, wub_ref, new_tile)

    def conv(sc_ref, c_ref, r0):
        y = c_ref[FFN_CONV - 1:FFN_CONV, :] * sc_ref[HIST + r0:HIST + r0 + tr, :]
        for back in range(1, FFN_CONV):
            tap = FFN_CONV - 1 - back
            y = y + c_ref[tap:tap + 1, :] * sc_ref[HIST + r0 - back:HIST + r0 - back + tr, :]
        return y

    for r0 in range(0, ts, tr):
        x = x_ref[0, r0:r0 + tr, :]
        sg_ref[HIST + r0:HIST + r0 + tr, :] = _dot(x, wg_ref[...])
        g = conv(sg_ref, cg_ref, r0)
        g = g * _sigmoid(g)
        su_ref[HIST + r0:HIST + r0 + tr, :] = _dot(x, wu_ref[...])
        u = conv(su_ref, cu_ref, r0)
        o_ref[0, r0:r0 + tr, :] = (g * u).astype(o_ref.dtype)
    _carry_rows(sg_ref, ts)
    _carry_rows(su_ref, ts)


def _ffn_up(xb, w_up, conv_w, ts=1024, tn=512, tr=1024, layer=0):
    b, s, d = xb.shape
    ts = min(ts, s)
    nj = D_FF // tn
    cast_scratch = [] if w_up.dtype == BF16 else [pltpu.VMEM((d, tn), BF16), pltpu.VMEM((d, tn), BF16)]
    return pl.pallas_call(
        functools.partial(_ffn_up_kernel, ts=ts, tr=min(tr, ts)),
        grid=(nj, b, s // ts),
        in_specs=[pl.BlockSpec((1, ts, d), lambda j, bi, i: (bi, i, 0)),
                  _weight_spec(w_up, tn, layer, lambda j, bi, i: j),
                  _weight_spec(w_up, tn, layer, lambda j, bi, i: nj + j),
                  pl.BlockSpec((FFN_CONV, tn), lambda j, bi, i: (0, j)),
                  pl.BlockSpec((FFN_CONV, tn), lambda j, bi, i: (0, nj + j))],
        out_specs=pl.BlockSpec((1, ts, tn), lambda j, bi, i: (bi, i, j)),
        out_shape=jax.ShapeDtypeStruct((b, s, D_FF), BF16),
        scratch_shapes=[pltpu.VMEM((ts + HIST, tn), F32), pltpu.VMEM((ts + HIST, tn), F32)] + cast_scratch,
        compiler_params=_cparams("arbitrary", "arbitrary", "arbitrary"),
        name="ffn_up",
    )(xb, w_up, w_up, conv_w, conv_w)


def _pool_kernel(u_ref, w_ref, sc_ref, o_ref, st_ref, *, ts):
    i = pl.program_id(1)

    @pl.when(i == 0)
    def _():
        st_ref[0:2 * HIST, :] = jnp.zeros((2 * HIST, st_ref.shape[1]), F32)

    st_ref[2 * HIST:2 * HIST + ts, :] = u_ref[0].astype(F32)
    pos = (i * ts + 1 + lax.broadcasted_iota(jnp.int32, (ts, 1), 0)).astype(F32)
    base = 2 * HIST
    for g, win in enumerate(POOL_WINDOWS):
        cols = slice(g * POOL_GDIM, (g + 1) * POOL_GDIM)
        cur = st_ref[base:base + ts, cols]
        acc = cur
        for back in range(1, win):
            acc = acc + st_ref[base - back:base - back + ts, cols]
        mean = acc / jnp.minimum(pos, float(win))
        y = _dot(_bf(mean - cur), w_ref[g])
        o_ref[0, :, cols] = (y * sc_ref[:, cols]).astype(o_ref.dtype)
    st_ref[0:2 * HIST, :] = st_ref[ts:ts + 2 * HIST, :]


def _pool_mixer(h_main, pool_w, pool_scale, ts=512):
    b, s, _ = h_main.shape
    ts = min(ts, s)
    return pl.pallas_call(
        functools.partial(_pool_kernel, ts=ts),
        grid=(b, s // ts),
        in_specs=[pl.BlockSpec((1, ts, HALF), lambda bi, i: (bi, i, 0)),
                  pl.BlockSpec((len(POOL_WINDOWS), POOL_GDIM, POOL_GDIM), lambda bi, i: (0, 0, 0)),
                  pl.BlockSpec((1, HALF), lambda bi, i: (0, 0))],
        out_specs=pl.BlockSpec((1, ts, HALF), lambda bi, i: (bi, i, 0)),
        out_shape=jax.ShapeDtypeStruct((b, s, HALF), BF16),
        scratch_shapes=[pltpu.VMEM((ts + 2 * HIST, HALF), F32)],
        compiler_params=_cparams("arbitrary", "arbitrary"),
        name="pool_mixer",
    )(h_main, pool_w, pool_scale.reshape(1, HALF))


def _gdn_kernel(q_ref, k_ref, v_ref, z_ref, g_ref, alog_ref, dtb_ref, nw_ref, o_ref, st_ref, *, ts, hps):
    c = CHUNK
    h0 = pl.program_id(1)

    @pl.when(pl.program_id(2) == 0)
    def _():
        st_ref[...] = jnp.zeros_like(st_ref)

    q_all = q_ref[0]
    k_all = k_ref[0]
    v_all = v_ref[0]
    z_all = z_ref[0].astype(F32)

    logits = g_ref[0]
    lane = lax.broadcasted_iota(jnp.int32, logits.shape, 1)
    g_all = -jnp.exp(alog_ref[...]) * _softplus(logits + dtb_ref[...])
    sig_all = _sigmoid(logits)

    causal, strict = _tri_masks(c)
    tril = causal.astype(BF16)
    eye = jnp.where(causal & (~strict), 1.0, 0.0).astype(F32)
    lane_c = lax.broadcasted_iota(jnp.int32, (c, GDN_DK), 1)
    nw = nw_ref[...]
    chunk_rows = [slice(ci * c, (ci + 1) * c) for ci in range(ts // c)]
    nck = len(chunk_rows)

    q, k, v, betas, g_cols = [], [], [], [], []
    for hd in range(hps):
        hl = slice(hd * GDN_DK, (hd + 1) * GDN_DK)
        qh, kh = q_all[:, hl], k_all[:, hl]
        qh = qh * lax.rsqrt(jnp.sum(qh * qh, axis=-1, keepdims=True) + NORM_EPS) * (GDN_DK ** -0.5)
        kh = kh * lax.rsqrt(jnp.sum(kh * kh, axis=-1, keepdims=True) + NORM_EPS)
        head = h0 * hps + hd
        g_col = jnp.sum(jnp.where(lane == head, g_all, 0.0), axis=1, keepdims=True)
        b_col = jnp.sum(jnp.where(lane == head + GDN_HEADS, sig_all, 0.0), axis=1, keepdims=True)
        for r in chunk_rows:
            q.append(qh[r])
            k.append(kh[r])
            v.append(v_all[r, hl])
            betas.append(jnp.broadcast_to(b_col[r], (c, GDN_DK)))
            g_cols.append(jnp.broadcast_to(g_col[r], (c, GDN_DK)))
    nprob = len(q)
    rows = list(range(nprob))
    gcs = [_dot01(tril, gcol) for gcol in g_cols]
    diffs = []
    for gc in gcs:
        g1, g2, g3 = (p.astype(F32) for p in _split3(gc))
        lhs = jnp.where(lane_c == 0, g1, jnp.where(lane_c == 1, g2, jnp.where(lane_c == 2, g3,
              jnp.where(lane_c < 6, 1.0, 0.0))))
        rhs = jnp.where(lane_c < 3, 1.0, jnp.where(lane_c == 3, -g1, jnp.where(lane_c == 4, -g2,
              jnp.where(lane_c == 5, -g3, 0.0))))
        diffs.append(_dot_nt(_bf(lhs), _bf(rhs)))
    decays = [jnp.where(causal, jnp.exp(jnp.where(causal, d, 0.0)), 0.0) for d in diffs]
    kbs = [k[r] * bt for r, bt in zip(rows, betas)]
    kcbs = [_bf(k[r]) for r in rows]
    lmats = [jnp.where(strict, _dot_nt(_bf(kb), kcb) * dc, 0.0) for kb, kcb, dc in zip(kbs, kcbs, decays)]
    intras = [_bf(jnp.where(causal, _dot_nt(_bf(q[r]), kcb) * dc, 0.0)) for r, kcb, dc in zip(rows, kcbs, decays)]
    tmats = [_bf(t) for t in _unit_lower_inverses([-l for l in lmats], eye)]
    egs = [jnp.exp(gc) for gc in gcs]
    g_lasts = [gc[c - 1:c, :] for gc in gcs]
    uws = [_dot(t, jnp.concatenate([_bf(v[r] * bt), _bf(kb * eg)], axis=1))
           for t, r, bt, kb, eg in zip(tmats, rows, betas, kbs, egs)]
    us = [uw[:, :GDN_DK] for uw in uws]
    ws = [_bf(uw[:, GDN_DK:]) for uw in uws]
    q_decs = [_bf(q[r] * eg) for r, eg in zip(rows, egs)]
    k_decs = [_bf(k[r] * jnp.exp(gl - gc)) for r, gl, gc in zip(rows, g_lasts, gcs)]
    e_lasts = [jnp.exp(gl) for gl in g_lasts]
    mns = [_dot_tn(kd, jnp.concatenate([w, _bf(u)], axis=1)) for kd, w, u in zip(k_decs, ws, us)]
    mmats = [_bf(mn[:, :GDN_DK]) for mn in mns]
    nmats = [mn[:, GDN_DK:] for mn in mns]

    states = [st_ref[hd] for hd in range(hps)]
    sb_hist = [[None] * nck for _ in range(hps)]
    for ci in range(nck):
        for hd in range(hps):
            p = hd * nck + ci
            sb = _bf(states[hd])
            sb_hist[hd][ci] = sb
            states[hd] = states[hd] * e_lasts[p] - _dot(mmats[p], sb) + nmats[p]
    for hd in range(hps):
        st_ref[hd] = states[hd]
    for hd in range(hps):
        hl = slice(hd * GDN_DK, (hd + 1) * GDN_DK)
        for ci in range(nck):
            p = hd * nck + ci
            sb = sb_hist[hd][ci]
            v_new = _bf(us[p] - _dot(ws[p], sb))
            o = _dot(q_decs[p], sb) + _dot(intras[p], v_new)
            o = o * lax.rsqrt(jnp.mean(o * o, axis=-1, keepdims=True) + NORM_EPS) * nw
            zc = z_all[chunk_rows[ci], hl]
            o_ref[0, chunk_rows[ci], hl] = (o * (zc * _sigmoid(zc))).astype(o_ref.dtype)


def _gated_deltanet(qkv, z, logits, a_log, dt_bias, norm_w, ts=256, hps=8):
    b, s, _ = qkv.shape
    ts = min(ts, s)
    hd = GDN_DK
    gw = hps * hd
    ng = HALF // gw
    pad = lambda t: jnp.pad(t.astype(F32), (0, hd - t.shape[0])).reshape(1, hd)
    col = lambda grp: (lambda bi, h, i: (bi, i, grp * ng + h))
    return pl.pallas_call(
        functools.partial(_gdn_kernel, ts=ts, hps=hps),
        grid=(b, ng, s // ts),
        in_specs=[pl.BlockSpec((1, ts, gw), col(0)), pl.BlockSpec((1, ts, gw), col(1)),
                  pl.BlockSpec((1, ts, gw), col(2)), pl.BlockSpec((1, ts, gw), col(0)),
                  pl.BlockSpec((1, ts, hd), lambda bi, h, i: (bi, i, 0)),
                  pl.BlockSpec((1, hd), lambda bi, h, i: (0, 0)),
                  pl.BlockSpec((1, hd), lambda bi, h, i: (0, 0)),
                  pl.BlockSpec((1, hd), lambda bi, h, i: (0, 0))],
        out_specs=pl.BlockSpec((1, ts, gw), lambda bi, h, i: (bi, i, h)),
        out_shape=jax.ShapeDtypeStruct((b, s, HALF), BF16),
        scratch_shapes=[pltpu.VMEM((hps, hd, hd), F32)],
        compiler_params=_cparams("arbitrary", "arbitrary", "arbitrary"),
        name="gated_deltanet",
    )(qkv, qkv, qkv, z, logits, pad(a_log), pad(dt_bias), norm_w.astype(F32).reshape(1, hd))


def _rwkv_kernel(r_ref, k_ref, v_ref, l_ref, w2_ref, a2_ref, g2_ref,
                 w0_ref, a0_ref, kk_ref, ka_ref, rk_ref, lnw_ref, lnb_ref, o_ref, st_ref, *, ts, hps):
    c = CHUNK
    hs = RWKV_HEAD

    @pl.when(pl.program_id(2) == 0)
    def _():
        st_ref[...] = jnp.zeros_like(st_ref)

    r = r_ref[0]
    k = k_ref[0]
    v = v_ref[0]
    lo = l_ref[0]
    p0, p1, p2 = RWKV_LORA_PAD
    w_log = -_softplus(-(w0_ref[...] + _dot(lo[:, :p0], w2_ref[...]))) - 0.5
    lw = -jnp.exp(w_log)
    a = _sigmoid(a0_ref[...] + _dot(lo[:, p0:p0 + p1], a2_ref[...]))
    g = _dot(lo[:, p0 + p1:p0 + p1 + p2], g2_ref[...])
    kk_raw = k * kk_ref[...]
    km = k * (1.0 + (a - 1.0) * ka_ref[...])
    head_of_lane = lax.broadcasted_iota(jnp.int32, (ts, hps * hs), 1) // hs

    def per_head_sum(x):
        out = jnp.zeros_like(x)
        for hd in range(hps):
            sel = head_of_lane == hd
            out = jnp.where(sel, jnp.sum(jnp.where(sel, x, 0.0), axis=-1, keepdims=True), out)
        return out

    kk = kk_raw * lax.rsqrt(per_head_sum(kk_raw * kk_raw) + NORM_EPS)
    aa = -kk
    bb = kk * a
    bonus = per_head_sum(r * km * rk_ref[...]) * v

    causal, strict = _tri_masks(c)
    tril = causal.astype(BF16)
    eye = jnp.where(causal & (~strict), 1.0, 0.0).astype(F32)
    lnw = lnw_ref[...]
    lnb = lnb_ref[...]

    rows = [slice(ci * c, (ci + 1) * c) for ci in range(ts // c)]
    nck = len(rows)
    heads = [slice(hd * hs, (hd + 1) * hs) for hd in range(hps)]
    cums = [_dot01(tril, lw[rw]) for rw in rows]
    c_lasts = [cum[c - 1:c, :] for cum in cums]
    w_invs = [jnp.exp(-cum) for cum in cums]
    w_tails = [jnp.exp(cl - cum) for cl, cum in zip(c_lasts, cums)]
    a_ts = [_bf(aa[rw] * jnp.exp(cum - lw[rw])) for rw, cum in zip(rows, cums)]
    r_ts = [_bf(r[rw] * jnp.exp(cum)) for rw, cum in zip(rows, cums)]
    b_ts = [_bf(bb[rw] * wi) for rw, wi in zip(rows, w_invs)]
    k_ts = [_bf(km[rw] * wi) for rw, wi in zip(rows, w_invs)]
    b_hs = [_bf(bb[rw] * wt) for rw, wt in zip(rows, w_tails)]
    k_hs = [_bf(km[rw] * wt) for rw, wt in zip(rows, w_tails)]
    vcs = [_bf(v[rw]) for rw in rows]
    e_lasts = [jnp.exp(cl) for cl in c_lasts]
    prob = [(ci, cs) for ci in range(nck) for cs in heads]
    row2 = lax.broadcasted_iota(jnp.int32, (c, 2 * c), 0)
    col2 = lax.broadcasted_iota(jnp.int32, (c, 2 * c), 1) % c
    bks = [jnp.concatenate([b_ts[ci][:, cs], k_ts[ci][:, cs]], axis=0) for ci, cs in prob]
    a_abks = [jnp.where(row2 > col2, _dot_nt(a_ts[ci][:, cs], bk), 0.0) for bk, (ci, cs) in zip(bks, prob)]
    a_rbks = [_bf(jnp.where(row2 >= col2, _dot_nt(r_ts[ci][:, cs], bk), 0.0)) for bk, (ci, cs) in zip(bks, prob)]
    tmats = [_bf(t) for t in _unit_lower_inverses([m[:, :c] for m in a_abks], eye)]
    zero_c = jnp.zeros((c, hs), BF16)
    avs = [_bf(_dot(_bf(abk), jnp.concatenate([zero_c, vcs[ci][:, cs]], axis=0)))
           for abk, (ci, cs) in zip(a_abks, prob)]
    pmats = [_bf(_dot(t, a_ts[ci][:, cs])) for t, (ci, cs) in zip(tmats, prob)]
    qmats = [_dot(t, av) for t, av in zip(tmats, avs)]
    mmats = [_bf(_dot_tn(pm, b_hs[ci][:, cs])) for pm, (ci, cs) in zip(pmats, prob)]
    nmats = [_dot_tn(jnp.concatenate([_bf(qm), vcs[ci][:, cs]], axis=0),
                     jnp.concatenate([b_hs[ci][:, cs], k_hs[ci][:, cs]], axis=0))
             for qm, (ci, cs) in zip(qmats, prob)]

    states = [st_ref[hd] for hd in range(hps)]
    sb_hist = []
    for ci in range(nck):
        sbs = [_bf(s) for s in states]
        sb_hist.append(sbs)
        states = [states[hd] * e_lasts[ci][:, heads[hd]] + _dot(sbs[hd], mmats[hps * ci + hd]) + nmats[hps * ci + hd]
                  for hd in range(hps)]
    for hd in range(hps):
        st_ref[hd] = states[hd]
    ys = []
    for ci in range(nck):
        sbs = sb_hist[ci]
        ubs = [_bf(_dot_nt(pmats[hps * ci + hd], sbs[hd]) + qmats[hps * ci + hd]) for hd in range(hps)]
        ys.append([_dot_nt(r_ts[ci][:, heads[hd]], sbs[hd])
                   + _dot(a_rbks[hps * ci + hd], jnp.concatenate([ubs[hd], vcs[ci][:, heads[hd]]], axis=0))
                   for hd in range(hps)])
    for ci in range(nck):
        outs = []
        for y in ys[ci]:
            ym = jnp.mean(y, axis=-1, keepdims=True)
            yd = y - ym
            yv = jnp.mean(yd * yd, axis=-1, keepdims=True)
            outs.append(yd * lax.rsqrt(yv + RWKV_LNX_EPS))
        yn = jnp.concatenate(outs, axis=-1) * lnw + lnb
        o_ref[0, rows[ci], :] = ((yn + bonus[rows[ci]]) * g[rows[ci]]).astype(o_ref.dtype)


def _rwkv7(hr, hl, w2p, a2p, g2p, w0, a0, k_k, k_a, r_k, lnx_w, lnx_b, ts=128, hps=16):
    b, s, _ = hr.shape
    ts = min(ts, s)
    pw = hps * RWKV_HEAD
    npair = HALF // pw
    lw = sum(RWKV_LORA_PAD)
    col = lambda grp: (lambda bi, hp, i: (bi, i, grp * npair + hp))
    vec = lambda grp: pl.BlockSpec((1, pw), lambda bi, hp, i: (0, grp * npair + hp))
    row1 = lambda t: t.astype(F32).reshape(1, -1)
    return pl.pallas_call(
        functools.partial(_rwkv_kernel, ts=ts, hps=hps),
        grid=(b, npair, s // ts),
        in_specs=[pl.BlockSpec((1, ts, pw), col(0)), pl.BlockSpec((1, ts, pw), col(1)),
                  pl.BlockSpec((1, ts, pw), col(2)),
                  pl.BlockSpec((1, ts, lw), lambda bi, hp, i: (bi, i, 0)),
                  pl.BlockSpec((RWKV_LORA_PAD[0], pw), lambda bi, hp, i: (0, hp)),
                  pl.BlockSpec((RWKV_LORA_PAD[1], pw), lambda bi, hp, i: (0, hp)),
                  pl.BlockSpec((RWKV_LORA_PAD[2], pw), lambda bi, hp, i: (0, hp)),
                  vec(0), vec(0), vec(0), vec(0), vec(0), vec(0), vec(0)],
        out_specs=pl.BlockSpec((1, ts, pw), lambda bi, hp, i: (bi, i, hp)),
        out_shape=jax.ShapeDtypeStruct((b, s, HALF), BF16),
        scratch_shapes=[pltpu.VMEM((hps, RWKV_HEAD, RWKV_HEAD), F32)],
        compiler_params=_cparams("arbitrary", "arbitrary", "arbitrary"),
        name="rwkv7",
    )(hr, hr, hr, hl, w2p, a2p, g2p,
      row1(w0), row1(a0), row1(k_k), row1(k_a), row1(r_k), row1(lnx_w), row1(lnx_b))


def _fox_cumsum_kernel(f_ref, bf_ref, o_ref):
    x = f_ref[0] + bf_ref[...]
    ls = jnp.minimum(x, 0.0) - jnp.log(1.0 + jnp.exp(-jnp.abs(x)))
    nh, nr, nl = ls.shape
    li = lax.broadcasted_iota(jnp.int32, (nl, nl), 0)
    lj = lax.broadcasted_iota(jnp.int32, (nl, nl), 1)
    upper = (li <= lj).astype(BF16)
    ri = lax.broadcasted_iota(jnp.int32, (nr, nr), 0)
    rj = lax.broadcasted_iota(jnp.int32, (nr, nr), 1)
    below = (ri > rj).astype(BF16)
    for h in range(nh):
        within = _dot01_right(ls[h], upper)
        tot = jnp.broadcast_to(within[:, nl - 1:nl], (nr, nl))
        o_ref[0, h] = within + _dot01(below, tot)


def _dot01_right(x, m01):
    x1, x2, x3 = _split3(x)
    return _dot(x1, m01) + _dot(x2, m01) + _dot(x3, m01)


def _fox_cumsum(f_t, b_f):
    b, nh, s = f_t.shape
    nl = 128
    nr = s // nl
    out = pl.pallas_call(
        _fox_cumsum_kernel,
        grid=(b,),
        in_specs=[pl.BlockSpec((1, nh, nr, nl), lambda bi: (bi, 0, 0, 0)),
                  pl.BlockSpec((nh, 1, 1), lambda bi: (0, 0, 0))],
        out_specs=pl.BlockSpec((1, nh, nr, nl), lambda bi: (bi, 0, 0, 0)),
        out_shape=jax.ShapeDtypeStruct((b, nh, nr, nl), F32),
        compiler_params=_cparams("arbitrary"),
        name="fox_cumsum",
    )(f_t.reshape(b, nh, nr, nl), b_f.astype(F32).reshape(nh, 1, 1))
    return out.reshape(b, nh, 1, s)


def _fox_kernel(qi_ref, ki_ref, q_ref, k_ref, v_ref, cq_ref, ck_ref, o_ref, qs_ref, m_ref, acc_ref, *, tq, tk, tr):
    p = pl.program_id(2)
    qi = qi_ref[p]
    ki = ki_ref[p]
    last_k = ((qi + 1) * tq - 1) // tk
    log2e = 1.0 / math.log(2.0)

    @pl.when(ki == 0)
    def _():
        qs_ref[...] = (q_ref[0].astype(F32) * (FOX_HD ** -0.5 * log2e)).astype(BF16)
        m_ref[...] = jnp.full_like(m_ref, -jnp.inf)
        acc_ref[...] = jnp.zeros_like(acc_ref)

    def step(masked):
        bias = (cq_ref[0, 0, :, 0:1] - ck_ref[0, 0]) * log2e
        v_aug = jnp.concatenate([v_ref[0], jnp.ones((tk, FOX_HD), BF16)], axis=1)
        kb = k_ref[0]
        for r0 in range(0, tq, tr):
            rows = slice(r0, r0 + tr)
            s = _dot_nt(qs_ref[rows, :], kb) + bias
            if masked:
                qpos = qi * tq + r0 + lax.broadcasted_iota(jnp.int32, (tr, tk), 0)
                kpos = ki * tk + lax.broadcasted_iota(jnp.int32, (tr, tk), 1)
                s = jnp.where(kpos <= qpos, s, -jnp.inf)
            m_old = m_ref[rows, :]
            m_new = jnp.maximum(m_old, jnp.max(s, axis=-1, keepdims=True))
            corr = jnp.exp2(m_old - m_new)
            pexp = _bf(jnp.exp2(s - m_new))
            acc_ref[rows, :] = corr * acc_ref[rows, :] + _dot(pexp, v_aug)
            m_ref[rows, :] = m_new

    needs_mask = (ki + 1) * tk - 1 > qi * tq

    @pl.when(needs_mask)
    def _():
        step(True)

    @pl.when(jnp.logical_not(needs_mask))
    def _():
        step(False)

    @pl.when(ki == last_k)
    def _():
        acc = acc_ref[...]
        o_ref[0] = (acc[:, :FOX_HD] / acc[:, FOX_HD:FOX_HD + 1]).astype(o_ref.dtype)


def _fox_attention(hf, c, tq=2048, tk=2048, tr=128):
    b, s, _ = hf.shape
    tq = min(tq, s)
    tk = min(tk, s)
    nh = FOX_HEADS
    pairs = [(qi, ki) for qi in range(s // tq) for ki in range(((qi + 1) * tq - 1) // tk + 1)]
    qi_arr = jnp.asarray(np.array([p[0] for p in pairs], np.int32))
    ki_arr = jnp.asarray(np.array([p[1] for p in pairs], np.int32))
    grid_spec = pltpu.PrefetchScalarGridSpec(
        num_scalar_prefetch=2,
        grid=(b, nh, len(pairs)),
        in_specs=[pl.BlockSpec((1, tq, FOX_HD), lambda bi, h, p, qa, ka: (bi, qa[p], h)),
                  pl.BlockSpec((1, tk, FOX_HD), lambda bi, h, p, qa, ka: (bi, ka[p], nh + h)),
                  pl.BlockSpec((1, tk, FOX_HD), lambda bi, h, p, qa, ka: (bi, ka[p], 2 * nh + h)),
                  pl.BlockSpec((1, 1, 1, tq), lambda bi, h, p, qa, ka: (bi, h, 0, qa[p])),
                  pl.BlockSpec((1, 1, 1, tk), lambda bi, h, p, qa, ka: (bi, h, 0, ka[p]))],
        out_specs=pl.BlockSpec((1, tq, FOX_HD), lambda bi, h, p, qa, ka: (bi, qa[p], h)),
        scratch_shapes=[pltpu.VMEM((tq, FOX_HD), BF16), pltpu.VMEM((tq, 1), F32),
                        pltpu.VMEM((tq, 2 * FOX_HD), F32)])
    return pl.pallas_call(
        functools.partial(_fox_kernel, tq=tq, tk=tk, tr=min(tr, tq)),
        grid_spec=grid_spec,
        out_shape=jax.ShapeDtypeStruct((b, s, HALF), BF16),
        compiler_params=_cparams("arbitrary", "arbitrary", "arbitrary"),
        name="fox_attention",
    )(qi_arr, ki_arr, hf, hf, hf, c, c)


def _pad_cols(w, width):
    return jnp.pad(w, ((0, 0), (0, width - w.shape[1])))


def _pad_rows(w, height):
    return jnp.pad(w, ((0, height - w.shape[0]), (0, 0)))


def _even_mixer(x32, xb, w_in, layer, pool_w, pool_scale, conv_w, a_log, dt_bias, norm_w, w_out, ln_w, ln_b):
    b, s, d = x32.shape
    t = b * s
    n_main = 5 * HALF
    w_gate = _bf(_pad_cols(w_in[layer, :, n_main:], 128))
    if xb is None:
        u_pool, xb2 = _mm_cast(x32.reshape(t, d), w_in, F32, col0=0, n=HALF, layer=layer)
        xb = xb2.reshape(b, s, d)
    else:
        xb2 = xb.reshape(t, d)
        u_pool = _mm(xb2, w_in, F32, col0=0, n=HALF, layer=layer)
    u_pool = u_pool.reshape(b, s, HALF)
    qkv = _mm_taps(xb, w_in, conv_w, F32, act="silu", col0=HALF, n=3 * HALF, layer=layer)
    z = _mm(xb2, w_in, BF16, col0=4 * HALF, n=HALF, layer=layer).reshape(b, s, HALF)
    logits = _mm(xb2, w_gate, F32, tn=128).reshape(b, s, 128)
    y_a = _pool_mixer(u_pool, _bf(pool_w), pool_scale)
    y_b = _gated_deltanet(qkv, z, logits, a_log, dt_bias, norm_w)
    wo = _bf(w_out)
    return _mm_res_ln([y_a.reshape(t, HALF), y_b.reshape(t, HALF)], [wo[:HALF], wo[HALF:]],
                      x32.reshape(t, d), ln_w, ln_b)


def _odd_mixer(x32, xb, w_in_all, layer, mu, w0, w2, a0, a2, g2, k_k, k_a, r_k, lnx_w, lnx_b, b_f, w_out, ln_w, ln_b):
    b, s, d = x32.shape
    t = b * s
    l0, l1, l2 = RWKV_LORA
    p0, p1, p2 = RWKV_LORA_PAD
    o_l = 3 * HALF
    o_f = o_l + l0 + l1 + l2
    w_tail = w_in_all[layer, :, o_l:]
    w_l = _bf(jnp.concatenate([_pad_cols(w_tail[:, :l0], p0),
                               _pad_cols(w_tail[:, l0:l0 + l1], p1),
                               _pad_cols(w_tail[:, l0 + l1:o_f - o_l], p2)], axis=1))
    w_fox = _bf(w_tail[:, o_f - o_l:o_f - o_l + 3 * HALF])
    w_fl = _bf(_pad_cols(w_tail[:, o_f - o_l + 3 * HALF:], 128))
    mu_l = jnp.concatenate([jnp.pad(mu[o_l:o_l + l0], (0, p0 - l0)),
                            jnp.pad(mu[o_l + l0:o_l + l0 + l1], (0, p1 - l1)),
                            jnp.pad(mu[o_l + l0 + l1:o_f], (0, p2 - l2))])
    xb2 = xb.reshape(t, d)
    lerp = lambda m: jnp.stack([m, 1.0 - m])
    hr = _mm_taps(xb, w_in_all, lerp(mu[:o_l]), F32, col0=0, n=o_l, layer=layer)
    hl = _mm_taps(xb, w_l, lerp(mu_l), BF16, act="lora", tn=p0 + p1 + p2)
    hf = _mm(xb2, w_fox, BF16).reshape(b, s, 3 * HALF)
    fl = _mm(xb2, w_fl, F32, tn=128).reshape(b, s, 128)
    y_c = _rwkv7(hr, hl, _bf(_pad_rows(w2, p0)), _bf(_pad_rows(a2, p1)), _bf(_pad_rows(g2, p2)),
                 w0, a0, k_k, k_a, r_k, lnx_w, lnx_b)
    c = _fox_cumsum(jnp.transpose(fl[:, :, :FOX_HEADS], (0, 2, 1)), b_f)
    y_d = _fox_attention(hf, c)
    wo = _bf(w_out)
    return _mm_res_ln([y_c.reshape(t, HALF), y_d.reshape(t, HALF)], [wo[:HALF], wo[HALF:]],
                      x32.reshape(t, d), ln_w, ln_b)


def _cross_attention(x32, xb, mem_b, w_q, w_kv, layer, w_o, ln_w, ln_b):
    t, d = x32.shape
    b, mlen, _ = mem_b.shape
    s = t // b
    q = _mm(xb, w_q, BF16, layer=layer).reshape(b, s, d)
    kv = _mm(mem_b.reshape(b * mlen, d), w_kv, BF16, layer=layer).reshape(b, mlen, 2 * d)
    o = _xattn(q, kv).reshape(t, d)
    return _mm_res_ln([o], [_bf(w_o)], x32, ln_w, ln_b)


def _conv_glu_ffn(x32, xb, b, w_up, layer, conv_w, w_down, ln_w, ln_b):
    t, d = x32.shape
    hmid = _ffn_up(xb.reshape(b, t // b, d), w_up, conv_w, layer=layer)
    return _mm_res_ln([hmid.reshape(t, D_FF)], [_bf(w_down)], x32, ln_w, ln_b, tm=256)


def kernel(x, mem, ev_w_in, pool_w, pool_scale, gdn_conv_w, gdn_a_log, gdn_dt_bias, gdn_norm_w, ev_w_out,
           od_w_in, rwkv_mu, rwkv_w0, rwkv_w2, rwkv_a0, rwkv_a2, rwkv_g2, rwkv_k_k, rwkv_k_a, rwkv_r_k,
           rwkv_lnx_w, rwkv_lnx_b, fox_b_f, od_w_out,
           ln_mix_w, ln_mix_b, xa_w_q, xa_w_kv, xa_w_o, ln_xa_w, ln_xa_b,
           ffn_w_up, ffn_conv_w, ffn_w_down, ln_ffn_w, ln_ffn_b):
    b, s, d = x.shape
    depth = ln_mix_w.shape[0]
    mem_b = _bf(mem)
    x32 = x
    xb = None
    for i in range(depth):
        j = i // 2
        x3 = x32.reshape(b, s, d)
        xb3 = None if xb is None else xb.reshape(b, s, d)
        if i % 2 == 0:
            x32, xb = _even_mixer(x3, xb3, ev_w_in, j, pool_w[j], pool_scale[j], gdn_conv_w[j], gdn_a_log[j],
                                  gdn_dt_bias[j], gdn_norm_w[j], ev_w_out[j], ln_mix_w[i], ln_mix_b[i])
        else:
            x32, xb = _odd_mixer(x3, xb3, od_w_in, j, rwkv_mu[j], rwkv_w0[j], rwkv_w2[j], rwkv_a0[j], rwkv_a2[j],
                                 rwkv_g2[j], rwkv_k_k[j], rwkv_k_a[j], rwkv_r_k[j].reshape(-1), rwkv_lnx_w[j],
                                 rwkv_lnx_b[j], fox_b_f[j], od_w_out[j], ln_mix_w[i], ln_mix_b[i])
        x32, xb = _cross_attention(x32, xb, mem_b, xa_w_q, xa_w_kv, i, xa_w_o[i], ln_xa_w[i], ln_xa_b[i])
        x32, xb = _conv_glu_ffn(x32, xb, b, ffn_w_up, i, ffn_conv_w[i], ffn_w_down[i], ln_ffn_w[i], ln_ffn_b[i])
    return x32.reshape(b, s, d)
```

```python
import functools
import math

import numpy as np
import jax
import jax.numpy as jnp
from jax import lax
from jax.experimental import pallas as pl
from jax.experimental.pallas import tpu as pltpu

F32 = jnp.float32
BF16 = jnp.bfloat16

D_MODEL = 2048
HALF = D_MODEL // 2
POOL_WINDOWS = (2, 4, 8, 16)
POOL_GDIM = HALF // len(POOL_WINDOWS)
GDN_HEADS = 8
GDN_DK = HALF // GDN_HEADS
GDN_CONV = 4
NORM_EPS = 1e-6
RWKV_HEAD = 64
RWKV_LNX_EPS = 64e-5
RWKV_LORA = (64, 64, 160)
RWKV_LORA_PAD = (128, 128, 256)
FOX_HEADS = 8
FOX_HD = HALF // FOX_HEADS
XA_HEADS = 4
XA_HD = D_MODEL // XA_HEADS
D_FF = 5632
FFN_CONV = 3
DEPTH = 2
DEEPNORM_ALPHA = float((2 * DEPTH) ** 0.25)
LN_EPS = 1e-5

CHUNK = 64
HIST = 8
VMEM_LIMIT_BYTES = 56 * 1024 * 1024


def _cparams(*sem):
    return pltpu.CompilerParams(dimension_semantics=sem, vmem_limit_bytes=VMEM_LIMIT_BYTES)


def _dot(a, b):
    return jnp.dot(a, b, preferred_element_type=F32)


def _dot_nt(a, b):
    return lax.dot_general(a, b, (((1,), (1,)), ((), ())), preferred_element_type=F32)


def _dot_tn(a, b):
    return lax.dot_general(a, b, (((0,), (0,)), ((), ())), preferred_element_type=F32)


def _bf(x):
    return x.astype(BF16)


def _split3(x):
    x1 = x.astype(BF16)
    r1 = x - x1.astype(F32)
    x2 = r1.astype(BF16)
    x3 = (r1 - x2.astype(F32)).astype(BF16)
    return x1, x2, x3


def _dot01(m01, x):
    x1, x2, x3 = _split3(x)
    return _dot(m01, x1) + _dot(m01, x2) + _dot(m01, x3)


def _unit_lower_inverses(n_mats, eye, fillers=()):
    c = eye.shape[0]
    levels = int(math.log2(c))
    fillers = list(fillers)
    xs = [eye + n for n in n_mats]
    nbs = [_bf(n) for n in n_mats]
    ps = [_dot(nb, nb) for nb in nbs]
    for level in range(1, levels):
        pbs = [_bf(p) for p in ps]
        if level == levels - 1:
            prods = [_dot(_bf(x), pb) for x, pb in zip(xs, pbs)]
        else:
            prods = [_dot(jnp.concatenate([pb, _bf(x)], axis=0), pb) for x, pb in zip(xs, pbs)]
        if fillers:
            fillers.pop(0)()
        if level == levels - 1:
            xs = [x + pr for x, pr in zip(xs, prods)]
        else:
            xs = [x + pr[c:] for x, pr in zip(xs, prods)]
            ps = [pr[:c] for pr in prods]
    for filler in fillers:
        filler()
    return xs


def _sigmoid(x):
    return 1.0 / (1.0 + jnp.exp(-x))


def _softplus(x):
    return jnp.maximum(x, 0.0) + jnp.log(1.0 + jnp.exp(-jnp.abs(x)))


def _tri_masks(c):
    row = lax.broadcasted_iota(jnp.int32, (c, c), 0)
    col = lax.broadcasted_iota(jnp.int32, (c, c), 1)
    return row >= col, row > col


def _stage_rows(x_ref, sc_ref, ts, first):
    @pl.when(first)
    def _():
        sc_ref[0:HIST, :] = jnp.zeros((HIST, sc_ref.shape[1]), F32)
    sc_ref[HIST:HIST + ts, :] = x_ref[0].astype(F32)


def _carry_rows(sc_ref, ts):
    sc_ref[0:HIST, :] = sc_ref[ts:ts + HIST, :]


def _weight_tile(w_ref, wb_ref, first):
    if wb_ref is None:
        return w_ref
    @pl.when(first)
    def _():
        wb_ref[...] = w_ref[...].astype(BF16)
    return wb_ref


def _weight_cols(w, n, col0, tn):
    n = w.shape[-1] - col0 if n is None else n
    tn = min(tn, n)
    assert n % tn == 0 and col0 % tn == 0, (n, col0, tn)
    scratch = [] if w.dtype == BF16 else [pltpu.VMEM((w.shape[-2], tn), BF16)]
    return n, tn, col0 // tn, scratch


def _weight_spec(w, tn, layer, col_of):
    k = w.shape[-2]
    if w.ndim == 2:
        return pl.BlockSpec((k, tn), lambda *g: (0, col_of(*g)))
    return pl.BlockSpec((None, k, tn), lambda *g: (layer, 0, col_of(*g)))


def _mm_kernel(a_ref, w_ref, o_ref, wb_ref=None):
    w = _weight_tile(w_ref, wb_ref, pl.program_id(1) == 0)
    o_ref[...] = _dot(a_ref[...], w[...]).astype(o_ref.dtype)


def _mm_cast_kernel(a_ref, w_ref, o_ref, ab_ref, wb_ref=None):
    w = _weight_tile(w_ref, wb_ref, pl.program_id(0) == 0)
    ab = a_ref[...].astype(BF16)
    ab_ref[...] = ab
    o_ref[...] = _dot(ab, w[...]).astype(o_ref.dtype)


def _mm_cast(a, w, out_dtype, tm=512, col0=0, n=None, layer=0):
    m, k = a.shape
    n, tn, joff, scratch = _weight_cols(w, n, col0, w.shape[-1])
    tm = min(tm, m)
    return pl.pallas_call(
        _mm_cast_kernel,
        grid=(m // tm,),
        in_specs=[pl.BlockSpec((tm, k), lambda i: (i, 0)),
                  _weight_spec(w, tn, layer, lambda i: joff)],
        out_specs=[pl.BlockSpec((tm, tn), lambda i: (i, 0)), pl.BlockSpec((tm, k), lambda i: (i, 0))],
        out_shape=[jax.ShapeDtypeStruct((m, n), out_dtype), jax.ShapeDtypeStruct((m, k), BF16)],
        scratch_shapes=scratch,
        compiler_params=_cparams("arbitrary"),
        name="mm_cast",
    )(a, w)


def _mm(a, w, out_dtype, tm=1024, tn=1024, col0=0, n=None, layer=0):
    m, k = a.shape
    n, tn, joff, scratch = _weight_cols(w, n, col0, tn)
    tm = min(tm, m)
    return pl.pallas_call(
        _mm_kernel,
        grid=(n // tn, m // tm),
        in_specs=[pl.BlockSpec((tm, k), lambda j, i: (i, 0)),
                  _weight_spec(w, tn, layer, lambda j, i: joff + j)],
        out_specs=pl.BlockSpec((tm, tn), lambda j, i: (i, j)),
        out_shape=jax.ShapeDtypeStruct((m, n), out_dtype),
        scratch_shapes=scratch,
        compiler_params=_cparams("arbitrary", "arbitrary"),
        name="mm",
    )(a, w)


def _mm_res_ln_kernel(*refs, n_in, tk_steps, tr):
    a_refs = refs[:n_in]
    w_refs = refs[n_in:2 * n_in]
    res_ref, lnw_ref, lnb_ref, o32_ref, o16_ref = refs[2 * n_in:2 * n_in + 5]
    acc_ref = refs[2 * n_in + 5] if tk_steps > 1 else None
    kk = pl.program_id(1)
    tm = res_ref.shape[0]

    def partial_sum(rows):
        tot = None
        for a_ref, w_ref in zip(a_refs, w_refs):
            d = _dot(a_ref[rows, :], w_ref[...])
            tot = d if tot is None else tot + d
        return tot

    if tk_steps > 1:
        @pl.when(kk == 0)
        def _():
            acc_ref[...] = partial_sum(slice(None))

        @pl.when((kk > 0) & (kk < tk_steps - 1))
        def _():
            acc_ref[...] += partial_sum(slice(None))

    @pl.when(kk == tk_steps - 1)
    def _():
        for r0 in range(0, tm, tr):
            rows = slice(r0, r0 + tr)
            y = DEEPNORM_ALPHA * res_ref[rows, :] + partial_sum(rows)
            if tk_steps > 1:
                y = y + acc_ref[rows, :]
            mu = jnp.mean(y, axis=-1, keepdims=True)
            d = y - mu
            var = jnp.mean(d * d, axis=-1, keepdims=True)
            out = d * lax.rsqrt(var + LN_EPS) * lnw_ref[...] + lnb_ref[...]
            o32_ref[rows, :] = out
            o16_ref[rows, :] = out.astype(BF16)


def _mm_res_ln(a_list, w_list, res, lnw, lnb, tm=512, tk=None, tr=128):
    m = res.shape[0]
    n = res.shape[1]
    k = a_list[0].shape[1]
    tm = min(tm, m)
    tk = k if tk is None else min(tk, k)
    tr = min(tr, tm)
    n_in = len(a_list)
    steps = k // tk
    w_mode = {"pipeline_mode": pl.Buffered(1)} if steps == 1 else {}
    in_specs = ([pl.BlockSpec((tm, tk), lambda i, kk: (i, kk)) for _ in a_list]
                + [pl.BlockSpec((tk, n), lambda i, kk: (kk, 0), **w_mode) for _ in w_list]
                + [pl.BlockSpec((tm, n), lambda i, kk: (i, 0)),
                   pl.BlockSpec((1, n), lambda i, kk: (0, 0)),
                   pl.BlockSpec((1, n), lambda i, kk: (0, 0))])
    return pl.pallas_call(
        functools.partial(_mm_res_ln_kernel, n_in=n_in, tk_steps=steps, tr=tr),
        grid=(m // tm, steps),
        in_specs=in_specs,
        out_specs=[pl.BlockSpec((tm, n), lambda i, kk: (i, 0)),
                   pl.BlockSpec((tm, n), lambda i, kk: (i, 0))],
        out_shape=[jax.ShapeDtypeStruct((m, n), F32), jax.ShapeDtypeStruct((m, n), BF16)],
        scratch_shapes=[pltpu.VMEM((tm, n), F32)] if steps > 1 else [],
        compiler_params=_cparams("arbitrary", "arbitrary"),
        name="mm_res_ln",
    )(*a_list, *w_list, res, lnw.reshape(1, n), lnb.reshape(1, n))


def _xattn_kernel(q_ref, k_ref, v_ref, o_ref, *, tr):
    kb = k_ref[0]
    vb = v_ref[0]
    for r0 in range(0, q_ref.shape[1], tr):
        s = _dot_nt(q_ref[0, r0:r0 + tr, :], kb) * (XA_HD ** -0.5)
        m = jnp.max(s, axis=-1, keepdims=True)
        p = jnp.exp(s - m)
        l = jnp.sum(p, axis=-1, keepdims=True)
        o = _dot(_bf(p / l), vb)
        o_ref[0, r0:r0 + tr, :] = o.astype(o_ref.dtype)


def _xattn(q, kv, ts=2048):
    b, s, d = q.shape
    mlen = kv.shape[1]
    ts = min(ts, s)
    return pl.pallas_call(
        functools.partial(_xattn_kernel, tr=min(256, ts)),
        grid=(b, s // ts, XA_HEADS),
        in_specs=[pl.BlockSpec((1, ts, XA_HD), lambda bi, i, h: (bi, i, h)),
                  pl.BlockSpec((1, mlen, XA_HD), lambda bi, i, h: (bi, 0, h)),
                  pl.BlockSpec((1, mlen, XA_HD), lambda bi, i, h: (bi, 0, XA_HEADS + h))],
        out_specs=pl.BlockSpec((1, ts, XA_HD), lambda bi, i, h: (bi, i, h)),
        out_shape=jax.ShapeDtypeStruct((b, s, d), BF16),
        compiler_params=_cparams("arbitrary", "arbitrary", "arbitrary"),
        name="xattn",
    )(q, kv, kv)


def _mm_taps_kernel(x_ref, w_ref, t_ref, o_ref, sc_ref, wb_ref=None, *, ts, ntap, act):
    first = pl.program_id(2) == 0

    @pl.when(first)
    def _():
        sc_ref[0:HIST, :] = jnp.zeros((HIST, sc_ref.shape[1]), F32)

    w = _weight_tile(w_ref, wb_ref, first & (pl.program_id(1) == 0))
    sc_ref[HIST:HIST + ts, :] = _dot(x_ref[0], w[...])
    y = t_ref[ntap - 1:ntap, :] * sc_ref[HIST:HIST + ts, :]
    for back in range(1, ntap):
        tap = ntap - 1 - back
        y = y + t_ref[tap:tap + 1, :] * sc_ref[HIST - back:HIST - back + ts, :]
    if act == "silu":
        y = y * _sigmoid(y)
    elif act == "lora":
        p0, p1, _ = RWKV_LORA_PAD
        y = jnp.concatenate([jnp.tanh(y[:, :p0]), y[:, p0:p0 + p1], _sigmoid(y[:, p0 + p1:])], axis=1)
    o_ref[0] = y.astype(o_ref.dtype)
    _carry_rows(sc_ref, ts)


def _mm_taps(xb, w, taps, out_dtype, act=None, ts=1024, tn=1024, col0=0, n=None, layer=0):
    b, s, d = xb.shape
    n, tn, joff, cast_scratch = _weight_cols(w, n, col0, tn)
    ts = min(ts, s)
    ntap = taps.shape[0]
    return pl.pallas_call(
        functools.partial(_mm_taps_kernel, ts=ts, ntap=ntap, act=act),
        grid=(n // tn, b, s // ts),
        in_specs=[pl.BlockSpec((1, ts, d), lambda j, bi, i: (bi, i, 0)),
                  _weight_spec(w, tn, layer, lambda j, bi, i: joff + j),
                  pl.BlockSpec((ntap, tn), lambda j, bi, i: (0, j))],
        out_specs=pl.BlockSpec((1, ts, tn), lambda j, bi, i: (bi, i, j)),
        out_shape=jax.ShapeDtypeStruct((b, s, n), out_dtype),
        scratch_shapes=[pltpu.VMEM((ts + HIST, tn), F32)] + cast_scratch,
        compiler_params=_cparams("arbitrary", "arbitrary", "arbitrary"),
        name="mm_taps",
    )(xb, w, taps.astype(F32))


def _ffn_up_kernel(x_ref, wg_ref, wu_ref, cg_ref, cu_ref, o_ref, sg_ref, su_ref, wgb_ref=None, wub_ref=None,
                   *, ts, tr):
    first = pl.program_id(2) == 0

    @pl.when(first)
    def _():
        sg_ref[0:HIST, :] = jnp.zeros((HIST, sg_ref.shape[1]), F32)
        su_ref[0:HIST, :] = jnp.zeros((HIST, su_ref.shape[1]), F32)

    new_tile = first & (pl.program_id(1) == 0)
    wg_ref = _weight_tile(wg_ref, wgb_ref, new_tile)
    wu_ref = _weight_tile(wu_ref, wub_ref, new_tile)

    def conv(sc_ref, c_ref, r0):
        y = c_ref[FFN_CONV - 1:FFN_CONV, :] * sc_ref[HIST + r0:HIST + r0 + tr, :]
        for back in range(1, FFN_CONV):
            tap = FFN_CONV - 1 - back
            y = y + c_ref[tap:tap + 1, :] * sc_ref[HIST + r0 - back:HIST + r0 - back + tr, :]
        return y

    for r0 in range(0, ts, tr):
        x = x_ref[0, r0:r0 + tr, :]
        sg_ref[HIST + r0:HIST + r0 + tr, :] = _dot(x, wg_ref[...])
        g = conv(sg_ref, cg_ref, r0)
        g = g * _sigmoid(g)
        su_ref[HIST + r0:HIST + r0 + tr, :] = _dot(x, wu_ref[...])
        u = conv(su_ref, cu_ref, r0)
        o_ref[0, r0:r0 + tr, :] = (g * u).astype(o_ref.dtype)
    _carry_rows(sg_ref, ts)
    _carry_rows(su_ref, ts)


def _ffn_up(xb, w_up, conv_w, ts=1024, tn=512, tr=1024, layer=0):
    b, s, d = xb.shape
    ts = min(ts, s)
    nj = D_FF // tn
    cast_scratch = [] if w_up.dtype == BF16 else [pltpu.VMEM((d, tn), BF16), pltpu.VMEM((d, tn), BF16)]
    return pl.pallas_call(
        functools.partial(_ffn_up_kernel, ts=ts, tr=min(tr, ts)),
        grid=(nj, b, s // ts),
        in_specs=[pl.BlockSpec((1, ts, d), lambda j, bi, i: (bi, i, 0)),
                  _weight_spec(w_up, tn, layer, lambda j, bi, i: j),
                  _weight_spec(w_up, tn, layer, lambda j, bi, i: nj + j),
                  pl.BlockSpec((FFN_CONV, tn), lambda j, bi, i: (0, j)),
                  pl.BlockSpec((FFN_CONV, tn), lambda j, bi, i: (0, nj + j))],
        out_specs=pl.BlockSpec((1, ts, tn), lambda j, bi, i: (bi, i, j)),
        out_shape=jax.ShapeDtypeStruct((b, s, D_FF), BF16),
        scratch_shapes=[pltpu.VMEM((ts + HIST, tn), F32), pltpu.VMEM((ts + HIST, tn), F32)] + cast_scratch,
        compiler_params=_cparams("arbitrary", "arbitrary", "arbitrary"),
        name="ffn_up",
    )(xb, w_up, w_up, conv_w, conv_w)


def _pool_kernel(u_ref, w_ref, sc_ref, o_ref, st_ref, *, ts):
    i = pl.program_id(1)

    @pl.when(i == 0)
    def _():
        st_ref[0:2 * HIST, :] = jnp.zeros((2 * HIST, st_ref.shape[1]), F32)

    st_ref[2 * HIST:2 * HIST + ts, :] = u_ref[0].astype(F32)
    pos = (i * ts + 1 + lax.broadcasted_iota(jnp.int32, (ts, 1), 0)).astype(F32)
    base = 2 * HIST
    for g, win in enumerate(POOL_WINDOWS):
        cols = slice(g * POOL_GDIM, (g + 1) * POOL_GDIM)
        cur = st_ref[base:base + ts, cols]
        acc = cur
        for back in range(1, win):
            acc = acc + st_ref[base - back:base - back + ts, cols]
        mean = acc / jnp.minimum(pos, float(win))
        y = _dot(_bf(mean - cur), w_ref[g])
        o_ref[0, :, cols] = (y * sc_ref[:, cols]).astype(o_ref.dtype)
    st_ref[0:2 * HIST, :] = st_ref[ts:ts + 2 * HIST, :]


def _pool_mixer(h_main, pool_w, pool_scale, ts=512):
    b, s, _ = h_main.shape
    ts = min(ts, s)
    return pl.pallas_call(
        functools.partial(_pool_kernel, ts=ts),
        grid=(b, s // ts),
        in_specs=[pl.BlockSpec((1, ts, HALF), lambda bi, i: (bi, i, 0)),
                  pl.BlockSpec((len(POOL_WINDOWS), POOL_GDIM, POOL_GDIM), lambda bi, i: (0, 0, 0)),
                  pl.BlockSpec((1, HALF), lambda bi, i: (0, 0))],
        out_specs=pl.BlockSpec((1, ts, HALF), lambda bi, i: (bi, i, 0)),
        out_shape=jax.ShapeDtypeStruct((b, s, HALF), BF16),
        scratch_shapes=[pltpu.VMEM((ts + 2 * HIST, HALF), F32)],
        compiler_params=_cparams("arbitrary", "arbitrary"),
        name="pool_mixer",
    )(h_main, pool_w, pool_scale.reshape(1, HALF))


def _gdn_kernel(q_ref, k_ref, v_ref, z_ref, g_ref, alog_ref, dtb_ref, nw_ref, o_ref, st_ref, *, ts, hps):
    c = CHUNK
    h0 = pl.program_id(1)

    @pl.when(pl.program_id(2) == 0)
    def _():
        st_ref[...] = jnp.zeros_like(st_ref)

    q_all = q_ref[0]
    k_all = k_ref[0]
    v_all = v_ref[0]
    z_all = z_ref[0].astype(F32)

    logits = g_ref[0]
    lane = lax.broadcasted_iota(jnp.int32, logits.shape, 1)
    g_all = -jnp.exp(alog_ref[...]) * _softplus(logits + dtb_ref[...])
    sig_all = _sigmoid(logits)
    trow = lax.broadcasted_iota(jnp.int32, (ts, ts), 0)
    tcol = lax.broadcasted_iota(jnp.int32, (ts, ts), 1)
    tril_chunks = ((trow >= tcol) & (trow // c == tcol // c)).astype(BF16)
    gc_all = _dot01(tril_chunks, g_all)

    causal, strict = _tri_masks(c)
    eye = jnp.where(causal & (~strict), 1.0, 0.0).astype(F32)
    lane_c = lax.broadcasted_iota(jnp.int32, (c, GDN_DK), 1)
    nw = nw_ref[...]
    chunk_rows = [slice(ci * c, (ci + 1) * c) for ci in range(ts // c)]
    nck = len(chunk_rows)

    q, k, v, betas, g_cols = [], [], [], [], []
    for hd in range(hps):
        hl = slice(hd * GDN_DK, (hd + 1) * GDN_DK)
        qh, kh = q_all[:, hl], k_all[:, hl]
        qh = qh * lax.rsqrt(jnp.sum(qh * qh, axis=-1, keepdims=True) + NORM_EPS) * (GDN_DK ** -0.5)
        kh = kh * lax.rsqrt(jnp.sum(kh * kh, axis=-1, keepdims=True) + NORM_EPS)
        head = h0 * hps + hd
        g_col = jnp.sum(jnp.where(lane == head, gc_all, 0.0), axis=1, keepdims=True)
        b_col = jnp.sum(jnp.where(lane == head + GDN_HEADS, sig_all, 0.0), axis=1, keepdims=True)
        for r in chunk_rows:
            q.append(qh[r])
            k.append(kh[r])
            v.append(v_all[r, hl])
            betas.append(jnp.broadcast_to(b_col[r], (c, GDN_DK)))
            g_cols.append(jnp.broadcast_to(g_col[r], (c, GDN_DK)))
    nprob = len(q)
    rows = list(range(nprob))
    gcs = g_cols
    diffs = []
    for gc in gcs:
        g1, g2, g3 = (p.astype(F32) for p in _split3(gc))
        lhs = jnp.where(lane_c == 0, g1, jnp.where(lane_c == 1, g2, jnp.where(lane_c == 2, g3,
              jnp.where(lane_c < 6, 1.0, 0.0))))
        rhs = jnp.where(lane_c < 3, 1.0, jnp.where(lane_c == 3, -g1, jnp.where(lane_c == 4, -g2,
              jnp.where(lane_c == 5, -g3, 0.0))))
        diffs.append(_dot_nt(_bf(lhs), _bf(rhs)))
    decays = [jnp.where(causal, jnp.exp(jnp.where(causal, d, 0.0)), 0.0) for d in diffs]
    kbs = [k[r] * bt for r, bt in zip(rows, betas)]
    kcbs = [_bf(k[r]) for r in rows]
    lmats = [jnp.where(strict, _dot_nt(_bf(kb), kcb) * dc, 0.0) for kb, kcb, dc in zip(kbs, kcbs, decays)]
    intras = [_bf(jnp.where(causal, _dot_nt(_bf(q[r]), kcb) * dc, 0.0)) for r, kcb, dc in zip(rows, kcbs, decays)]
    tmats = [_bf(t) for t in _unit_lower_inverses([-l for l in lmats], eye)]
    egs = [jnp.exp(gc) for gc in gcs]
    g_lasts = [gc[c - 1:c, :] for gc in gcs]
    uws = [_dot(t, jnp.concatenate([_bf(v[r] * bt), _bf(kb * eg)], axis=1))
           for t, r, bt, kb, eg in zip(tmats, rows, betas, kbs, egs)]
    us = [uw[:, :GDN_DK] for uw in uws]
    ws = [_bf(uw[:, GDN_DK:]) for uw in uws]
    q_decs = [_bf(q[r] * eg) for r, eg in zip(rows, egs)]
    k_decs = [_bf(k[r] * jnp.exp(gl - gc)) for r, gl, gc in zip(rows, g_lasts, gcs)]
    e_lasts = [jnp.exp(gl) for gl in g_lasts]
    mns = [_dot_tn(kd, jnp.concatenate([w, _bf(u)], axis=1)) for kd, w, u in zip(k_decs, ws, us)]
    mmats = [_bf(mn[:, :GDN_DK]) for mn in mns]
    nmats = [mn[:, GDN_DK:] for mn in mns]

    states = [st_ref[hd] for hd in range(hps)]
    sb_hist = [[None] * nck for _ in range(hps)]
    for ci in range(nck):
        for hd in range(hps):
            p = hd * nck + ci
            sb = _bf(states[hd])
            sb_hist[hd][ci] = sb
            states[hd] = states[hd] * e_lasts[p] - _dot(mmats[p], sb) + nmats[p]
    for hd in range(hps):
        st_ref[hd] = states[hd]
    for hd in range(hps):
        hl = slice(hd * GDN_DK, (hd + 1) * GDN_DK)
        for ci in range(nck):
            p = hd * nck + ci
            sb = sb_hist[hd][ci]
            v_new = _bf(us[p] - _dot(ws[p], sb))
            o = _dot(q_decs[p], sb) + _dot(intras[p], v_new)
            o = o * lax.rsqrt(jnp.mean(o * o, axis=-1, keepdims=True) + NORM_EPS) * nw
            zc = z_all[chunk_rows[ci], hl]
            o_ref[0, chunk_rows[ci], hl] = (o * (zc * _sigmoid(zc))).astype(o_ref.dtype)


def _gated_deltanet(qkv, z, logits, a_log, dt_bias, norm_w, ts=256, hps=8):
    b, s, _ = qkv.shape
    ts = min(ts, s)
    hd = GDN_DK
    gw = hps * hd
    ng = HALF // gw
    pad = lambda t: jnp.pad(t.astype(F32), (0, hd - t.shape[0])).reshape(1, hd)
    col = lambda grp: (lambda bi, h, i: (bi, i, grp * ng + h))
    return pl.pallas_call(
        functools.partial(_gdn_kernel, ts=ts, hps=hps),
        grid=(b, ng, s // ts),
        in_specs=[pl.BlockSpec((1, ts, gw), col(0)), pl.BlockSpec((1, ts, gw), col(1)),
                  pl.BlockSpec((1, ts, gw), col(2)), pl.BlockSpec((1, ts, gw), col(0)),
                  pl.BlockSpec((1, ts, hd), lambda bi, h, i: (bi, i, 0)),
                  pl.BlockSpec((1, hd), lambda bi, h, i: (0, 0)),
                  pl.BlockSpec((1, hd), lambda bi, h, i: (0, 0)),
                  pl.BlockSpec((1, hd), lambda bi, h, i: (0, 0))],
        out_specs=pl.BlockSpec((1, ts, gw), lambda bi, h, i: (bi, i, h)),
        out_shape=jax.ShapeDtypeStruct((b, s, HALF), BF16),
        scratch_shapes=[pltpu.VMEM((hps, hd, hd), F32)],
        compiler_params=_cparams("arbitrary", "arbitrary", "arbitrary"),
        name="gated_deltanet",
    )(qkv, qkv, qkv, z, logits, pad(a_log), pad(dt_bias), norm_w.astype(F32).reshape(1, hd))


def _rwkv_kernel(r_ref, k_ref, v_ref, l_ref, w2_ref, a2_ref, g2_ref,
                 w0_ref, a0_ref, kk_ref, ka_ref, rk_ref, lnw_ref, lnb_ref, o_ref, st_ref, *, ts, hps):
    c = CHUNK
    hs = RWKV_HEAD

    @pl.when(pl.program_id(2) == 0)
    def _():
        st_ref[...] = jnp.zeros_like(st_ref)

    r = r_ref[0]
    k = k_ref[0]
    v = v_ref[0]
    lo = l_ref[0]
    p0, p1, p2 = RWKV_LORA_PAD
    w_log = -_softplus(-(w0_ref[...] + _dot(lo[:, :p0], w2_ref[...]))) - 0.5
    lw = -jnp.exp(w_log)
    a = _sigmoid(a0_ref[...] + _dot(lo[:, p0:p0 + p1], a2_ref[...]))
    g = _dot(lo[:, p0 + p1:p0 + p1 + p2], g2_ref[...])
    kk_raw = k * kk_ref[...]
    km = k * (1.0 + (a - 1.0) * ka_ref[...])
    head_of_lane = lax.broadcasted_iota(jnp.int32, (ts, hps * hs), 1) // hs

    def per_head_sum(x):
        out = jnp.zeros_like(x)
        for hd in range(hps):
            sel = head_of_lane == hd
            out = jnp.where(sel, jnp.sum(jnp.where(sel, x, 0.0), axis=-1, keepdims=True), out)
        return out

    kk = kk_raw * lax.rsqrt(per_head_sum(kk_raw * kk_raw) + NORM_EPS)
    aa = -kk
    bb = kk * a
    bonus = per_head_sum(r * km * rk_ref[...]) * v

    causal, strict = _tri_masks(c)
    tril = causal.astype(BF16)
    eye = jnp.where(causal & (~strict), 1.0, 0.0).astype(F32)
    lnw = lnw_ref[...]
    lnb = lnb_ref[...]

    rows = [slice(ci * c, (ci + 1) * c) for ci in range(ts // c)]
    nck = len(rows)
    heads = [slice(hd * hs, (hd + 1) * hs) for hd in range(hps)]
    cums = [_dot01(tril, lw[rw]) for rw in rows]
    c_lasts = [cum[c - 1:c, :] for cum in cums]
    w_invs = [jnp.exp(-cum) for cum in cums]
    w_tails = [jnp.exp(cl - cum) for cl, cum in zip(c_lasts, cums)]
    a_ts = [_bf(aa[rw] * jnp.exp(cum - lw[rw])) for rw, cum in zip(rows, cums)]
    r_ts = [_bf(r[rw] * jnp.exp(cum)) for rw, cum in zip(rows, cums)]
    b_ts = [_bf(bb[rw] * wi) for rw, wi in zip(rows, w_invs)]
    k_ts = [_bf(km[rw] * wi) for rw, wi in zip(rows, w_invs)]
    b_hs = [_bf(bb[rw] * wt) for rw, wt in zip(rows, w_tails)]
    k_hs = [_bf(km[rw] * wt) for rw, wt in zip(rows, w_tails)]
    vcs = [_bf(v[rw]) for rw in rows]
    e_lasts = [jnp.exp(cl) for cl in c_lasts]
    prob = [(ci, cs) for ci in range(nck) for cs in heads]
    row2 = lax.broadcasted_iota(jnp.int32, (c, 2 * c), 0)
    col2 = lax.broadcasted_iota(jnp.int32, (c, 2 * c), 1) % c
    bks = [jnp.concatenate([b_ts[ci][:, cs], k_ts[ci][:, cs]], axis=0) for ci, cs in prob]
    a_abks = [jnp.where(row2 > col2, _dot_nt(a_ts[ci][:, cs], bk), 0.0) for bk, (ci, cs) in zip(bks, prob)]
    a_rbks = [_bf(jnp.where(row2 >= col2, _dot_nt(r_ts[ci][:, cs], bk), 0.0)) for bk, (ci, cs) in zip(bks, prob)]
    tmats = [_bf(t) for t in _unit_lower_inverses([m[:, :c] for m in a_abks], eye)]
    zero_c = jnp.zeros((c, hs), BF16)
    avs = [_bf(_dot(_bf(abk), jnp.concatenate([zero_c, vcs[ci][:, cs]], axis=0)))
           for abk, (ci, cs) in zip(a_abks, prob)]
    pmats = [_bf(_dot(t, a_ts[ci][:, cs])) for t, (ci, cs) in zip(tmats, prob)]
    qmats = [_dot(t, av) for t, av in zip(tmats, avs)]
    mmats = [_bf(_dot_tn(pm, b_hs[ci][:, cs])) for pm, (ci, cs) in zip(pmats, prob)]
    nmats = [_dot_tn(jnp.concatenate([_bf(qm), vcs[ci][:, cs]], axis=0),
                     jnp.concatenate([b_hs[ci][:, cs], k_hs[ci][:, cs]], axis=0))
             for qm, (ci, cs) in zip(qmats, prob)]

    states = [st_ref[hd] for hd in range(hps)]
    sb_hist = []
    for ci in range(nck):
        sbs = [_bf(s) for s in states]
        sb_hist.append(sbs)
        states = [states[hd] * e_lasts[ci][:, heads[hd]] + _dot(sbs[hd], mmats[hps * ci + hd]) + nmats[hps * ci + hd]
                  for hd in range(hps)]
    for hd in range(hps):
        st_ref[hd] = states[hd]
    ys = []
    for ci in range(nck):
        sbs = sb_hist[ci]
        ubs = [_bf(_dot_nt(pmats[hps * ci + hd], sbs[hd]) + qmats[hps * ci + hd]) for hd in range(hps)]
        ys.append([_dot_nt(r_ts[ci][:, heads[hd]], sbs[hd])
                   + _dot(a_rbks[hps * ci + hd], jnp.concatenate([ubs[hd], vcs[ci][:, heads[hd]]], axis=0))
                   for hd in range(hps)])
    for ci in range(nck):
        outs = []
        for y in ys[ci]:
            ym = jnp.mean(y, axis=-1, keepdims=True)
            yd = y - ym
            yv = jnp.mean(yd * yd, axis=-1, keepdims=True)
            outs.append(yd * lax.rsqrt(yv + RWKV_LNX_EPS))
        yn = jnp.concatenate(outs, axis=-1) * lnw + lnb
        o_ref[0, rows[ci], :] = ((yn + bonus[rows[ci]]) * g[rows[ci]]).astype(o_ref.dtype)


def _rwkv7(hr, hl, w2p, a2p, g2p, w0, a0, k_k, k_a, r_k, lnx_w, lnx_b, ts=256, hps=16):
    b, s, _ = hr.shape
    ts = min(ts, s)
    pw = hps * RWKV_HEAD
    npair = HALF // pw
    lw = sum(RWKV_LORA_PAD)
    col = lambda grp: (lambda bi, hp, i: (bi, i, grp * npair + hp))
    vec = lambda grp: pl.BlockSpec((1, pw), lambda bi, hp, i: (0, grp * npair + hp))
    row1 = lambda t: t.astype(F32).reshape(1, -1)
    return pl.pallas_call(
        functools.partial(_rwkv_kernel, ts=ts, hps=hps),
        grid=(b, npair, s // ts),
        in_specs=[pl.BlockSpec((1, ts, pw), col(0)), pl.BlockSpec((1, ts, pw), col(1)),
                  pl.BlockSpec((1, ts, pw), col(2)),
                  pl.BlockSpec((1, ts, lw), lambda bi, hp, i: (bi, i, 0)),
                  pl.BlockSpec((RWKV_LORA_PAD[0], pw), lambda bi, hp, i: (0, hp)),
                  pl.BlockSpec((RWKV_LORA_PAD[1], pw), lambda bi, hp, i: (0, hp)),
                  pl.BlockSpec((RWKV_LORA_PAD[2], pw), lambda bi, hp, i: (0, hp)),
                  vec(0), vec(0), vec(0), vec(0), vec(0), vec(0), vec(0)],
        out_specs=pl.BlockSpec((1, ts, pw), lambda bi, hp, i: (bi, i, hp)),
        out_shape=jax.ShapeDtypeStruct((b, s, HALF), BF16),
        scratch_shapes=[pltpu.VMEM((hps, RWKV_HEAD, RWKV_HEAD), F32)],
        compiler_params=_cparams("arbitrary", "arbitrary", "arbitrary"),
        name="rwkv7",
    )(hr, hr, hr, hl, w2p, a2p, g2p,
      row1(w0), row1(a0), row1(k_k), row1(k_a), row1(r_k), row1(lnx_w), row1(lnx_b))


def _fox_cumsum_kernel(f_ref, bf_ref, o_ref):
    x = f_ref[0] + bf_ref[...]
    ls = jnp.minimum(x, 0.0) - jnp.log(1.0 + jnp.exp(-jnp.abs(x)))
    nh, nr, nl = ls.shape
    li = lax.broadcasted_iota(jnp.int32, (nl, nl), 0)
    lj = lax.broadcasted_iota(jnp.int32, (nl, nl), 1)
    upper = (li <= lj).astype(BF16)
    ri = lax.broadcasted_iota(jnp.int32, (nr, nr), 0)
    rj = lax.broadcasted_iota(jnp.int32, (nr, nr), 1)
    below = (ri > rj).astype(BF16)
    for h in range(nh):
        within = _dot01_right(ls[h], upper)
        tot = jnp.broadcast_to(within[:, nl - 1:nl], (nr, nl))
        o_ref[0, h] = within + _dot01(below, tot)


def _dot01_right(x, m01):
    x1, x2, x3 = _split3(x)
    return _dot(x1, m01) + _dot(x2, m01) + _dot(x3, m01)


def _fox_cumsum(f_t, b_f):
    b, nh, s = f_t.shape
    nl = 128
    nr = s // nl
    out = pl.pallas_call(
        _fox_cumsum_kernel,
        grid=(b,),
        in_specs=[pl.BlockSpec((1, nh, nr, nl), lambda bi: (bi, 0, 0, 0)),
                  pl.BlockSpec((nh, 1, 1), lambda bi: (0, 0, 0))],
        out_specs=pl.BlockSpec((1, nh, nr, nl), lambda bi: (bi, 0, 0, 0)),
        out_shape=jax.ShapeDtypeStruct((b, nh, nr, nl), F32),
        compiler_params=_cparams("arbitrary"),
        name="fox_cumsum",
    )(f_t.reshape(b, nh, nr, nl), b_f.astype(F32).reshape(nh, 1, 1))
    return out.reshape(b, nh, 1, s)


def _fox_kernel(qi_ref, ki_ref, q_ref, k_ref, v_ref, cq_ref, ck_ref, o_ref, qs_ref, m_ref, acc_ref, *, tq, tk, tr):
    p = pl.program_id(2)
    qi = qi_ref[p]
    ki = ki_ref[p]
    last_k = ((qi + 1) * tq - 1) // tk
    log2e = 1.0 / math.log(2.0)

    @pl.when(ki == 0)
    def _():
        qs_ref[...] = (q_ref[0].astype(F32) * (FOX_HD ** -0.5 * log2e)).astype(BF16)
        m_ref[...] = jnp.full_like(m_ref, -jnp.inf)
        acc_ref[...] = jnp.zeros_like(acc_ref)

    def step(masked):
        bias = (cq_ref[0, 0, :, 0:1] - ck_ref[0, 0]) * log2e
        v_aug = jnp.concatenate([v_ref[0], jnp.ones((tk, FOX_HD), BF16)], axis=1)
        kb = k_ref[0]
        for r0 in range(0, tq, tr):
            rows = slice(r0, r0 + tr)
            s = _dot_nt(qs_ref[rows, :], kb) + bias
            if masked:
                qpos = qi * tq + r0 + lax.broadcasted_iota(jnp.int32, (tr, tk), 0)
                kpos = ki * tk + lax.broadcasted_iota(jnp.int32, (tr, tk), 1)
                s = jnp.where(kpos <= qpos, s, -jnp.inf)
            m_old = m_ref[rows, :]
            m_new = jnp.maximum(m_old, jnp.max(s, axis=-1, keepdims=True))
            corr = jnp.exp2(m_old - m_new)
            pexp = _bf(jnp.exp2(s - m_new))
            acc_ref[rows, :] = corr * acc_ref[rows, :] + _dot(pexp, v_aug)
            m_ref[rows, :] = m_new

    needs_mask = (ki + 1) * tk - 1 > qi * tq

    @pl.when(needs_mask)
    def _():
        step(True)

    @pl.when(jnp.logical_not(needs_mask))
    def _():
        step(False)

    @pl.when(ki == last_k)
    def _():
        acc = acc_ref[...]
        o_ref[0] = (acc[:, :FOX_HD] / acc[:, FOX_HD:FOX_HD + 1]).astype(o_ref.dtype)


def _fox_attention(hf, c, tq=2048, tk=2048, tr=128):
    b, s, _ = hf.shape
    tq = min(tq, s)
    tk = min(tk, s)
    nh = FOX_HEADS
    pairs = [(qi, ki) for qi in range(s // tq) for ki in range(((qi + 1) * tq - 1) // tk + 1)]
    qi_arr = jnp.asarray(np.array([p[0] for p in pairs], np.int32))
    ki_arr = jnp.asarray(np.array([p[1] for p in pairs], np.int32))
    grid_spec = pltpu.PrefetchScalarGridSpec(
        num_scalar_prefetch=2,
        grid=(b, nh, len(pairs)),
        in_specs=[pl.BlockSpec((1, tq, FOX_HD), lambda bi, h, p, qa, ka: (bi, qa[p], h)),
                  pl.BlockSpec((1, tk, FOX_HD), lambda bi, h, p, qa, ka: (bi, ka[p], nh + h)),
                  pl.BlockSpec((1, tk, FOX_HD), lambda bi, h, p, qa, ka: (bi, ka[p], 2 * nh + h)),
                  pl.BlockSpec((1, 1, 1, tq), lambda bi, h, p, qa, ka: (bi, h, 0, qa[p])),
                  pl.BlockSpec((1, 1, 1, tk), lambda bi, h, p, qa, ka: (bi, h, 0, ka[p]))],
        out_specs=pl.BlockSpec((1, tq, FOX_HD), lambda bi, h, p, qa, ka: (bi, qa[p], h)),
        scratch_shapes=[pltpu.VMEM((tq, FOX_HD), BF16), pltpu.VMEM((tq, 1), F32),
                        pltpu.VMEM((tq, 2 * FOX_HD), F32)])
    return pl.pallas_call(
        functools.partial(_fox_kernel, tq=tq, tk=tk, tr=min(tr, tq)),
        grid_spec=grid_spec,
        out_shape=jax.ShapeDtypeStruct((b, s, HALF), BF16),
        compiler_params=_cparams("arbitrary", "arbitrary", "arbitrary"),
        name="fox_attention",
    )(qi_arr, ki_arr, hf, hf, hf, c, c)


def _pad_cols(w, width):
    return jnp.pad(w, ((0, 0), (0, width - w.shape[1])))


def _pad_rows(w, height):
    return jnp.pad(w, ((0, height - w.shape[0]), (0, 0)))


def _even_mixer(x32, xb, w_in, layer, pool_w, pool_scale, conv_w, a_log, dt_bias, norm_w, w_out, ln_w, ln_b):
    b, s, d = x32.shape
    t = b * s
    n_main = 5 * HALF
    w_gate = _bf(_pad_cols(w_in[layer, :, n_main:], 128))
    if xb is None:
        u_pool, xb2 = _mm_cast(x32.reshape(t, d), w_in, F32, col0=0, n=HALF, layer=layer)
        xb = xb2.reshape(b, s, d)
    else:
        xb2 = xb.reshape(t, d)
        u_pool = _mm(xb2, w_in, F32, col0=0, n=HALF, layer=layer)
    u_pool = u_pool.reshape(b, s, HALF)
    qkv = _mm_taps(xb, w_in, conv_w, F32, act="silu", col0=HALF, n=3 * HALF, layer=layer)
    z = _mm(xb2, w_in, BF16, col0=4 * HALF, n=HALF, layer=layer).reshape(b, s, HALF)
    logits = _mm(xb2, w_gate, F32, tn=128).reshape(b, s, 128)
    y_a = _pool_mixer(u_pool, _bf(pool_w), pool_scale)
    y_b = _gated_deltanet(qkv, z, logits, a_log, dt_bias, norm_w)
    wo = _bf(w_out)
    return _mm_res_ln([y_a.reshape(t, HALF), y_b.reshape(t, HALF)], [wo[:HALF], wo[HALF:]],
                      x32.reshape(t, d), ln_w, ln_b)


def _odd_mixer(x32, xb, w_in_all, layer, mu, w0, w2, a0, a2, g2, k_k, k_a, r_k, lnx_w, lnx_b, b_f, w_out, ln_w, ln_b):
    b, s, d = x32.shape
    t = b * s
    l0, l1, l2 = RWKV_LORA
    p0, p1, p2 = RWKV_LORA_PAD
    o_l = 3 * HALF
    o_f = o_l + l0 + l1 + l2
    w_tail = w_in_all[layer, :, o_l:]
    w_l = _bf(jnp.concatenate([_pad_cols(w_tail[:, :l0], p0),
                               _pad_cols(w_tail[:, l0:l0 + l1], p1),
                               _pad_cols(w_tail[:, l0 + l1:o_f - o_l], p2)], axis=1))
    w_fox = _bf(w_tail[:, o_f - o_l:o_f - o_l + 3 * HALF])
    w_fl = _bf(_pad_cols(w_tail[:, o_f - o_l + 3 * HALF:], 128))
    mu_l = jnp.concatenate([jnp.pad(mu[o_l:o_l + l0], (0, p0 - l0)),
                            jnp.pad(mu[o_l + l0:o_l + l0 + l1], (0, p1 - l1)),
                            jnp.pad(mu[o_l + l0 + l1:o_f], (0, p2 - l2))])
    xb2 = xb.reshape(t, d)
    lerp = lambda m: jnp.stack([m, 1.0 - m])
    hr = _mm_taps(xb, w_in_all, lerp(mu[:o_l]), F32, col0=0, n=o_l, layer=layer)
    hl = _mm_taps(xb, w_l, lerp(mu_l), BF16, act="lora", tn=p0 + p1 + p2)
    hf = _mm(xb2, w_fox, BF16).reshape(b, s, 3 * HALF)
    fl = _mm(xb2, w_fl, F32, tn=128).reshape(b, s, 128)
    y_c = _rwkv7(hr, hl, _bf(_pad_rows(w2, p0)), _bf(_pad_rows(a2, p1)), _bf(_pad_rows(g2, p2)),
                 w0, a0, k_k, k_a, r_k, lnx_w, lnx_b)
    c = _fox_cumsum(jnp.transpose(fl[:, :, :FOX_HEADS], (0, 2, 1)), b_f)
    y_d = _fox_attention(hf, c)
    wo = _bf(w_out)
    return _mm_res_ln([y_c.reshape(t, HALF), y_d.reshape(t, HALF)], [wo[:HALF], wo[HALF:]],
                      x32.reshape(t, d), ln_w, ln_b)


def _cross_attention(x32, xb, mem_b, w_q, w_kv, layer, w_o, ln_w, ln_b):
    t, d = x32.shape
    b, mlen, _ = mem_b.shape
    s = t // b
    q = _mm(xb, w_q, BF16, layer=layer).reshape(b, s, d)
    kv = _mm(mem_b.reshape(b * mlen, d), w_kv, BF16, layer=layer).reshape(b, mlen, 2 * d)
    o = _xattn(q, kv).reshape(t, d)
    return _mm_res_ln([o], [_bf(w_o)], x32, ln_w, ln_b)


def _conv_glu_ffn(x32, xb, b, w_up, layer, conv_w, w_down, ln_w, ln_b):
    t, d = x32.shape
    hmid = _ffn_up(xb.reshape(b, t // b, d), w_up, conv_w, layer=layer)
    return _mm_res_ln([hmid.reshape(t, D_FF)], [_bf(w_down)], x32, ln_w, ln_b, tm=256)


def kernel(x, mem, ev_w_in, pool_w, pool_scale, gdn_conv_w, gdn_a_log, gdn_dt_bias, gdn_norm_w, ev_w_out,
           od_w_in, rwkv_mu, rwkv_w0, rwkv_w2, rwkv_a0, rwkv_a2, rwkv_g2, rwkv_k_k, rwkv_k_a, rwkv_r_k,
           rwkv_lnx_w, rwkv_lnx_b, fox_b_f, od_w_out,
           ln_mix_w, ln_mix_b, xa_w_q, xa_w_kv, xa_w_o, ln_xa_w, ln_xa_b,
           ffn_w_up, ffn_conv_w, ffn_w_down, ln_ffn_w, ln_ffn_b):
    b, s, d = x.shape
    depth = ln_mix_w.shape[0]
    mem_b = _bf(mem)
    x32 = x
    xb = None
    for i in range(depth):
        j = i // 2
        x3 = x32.reshape(b, s, d)
        xb3 = None if xb is None else xb.reshape(b, s, d)
        if i % 2 == 0:
            x32, xb = _even_mixer(x3, xb3, ev_w_in, j, pool_w[j], pool_scale[j], gdn_conv_w[j], gdn_a_log[j],
                                  gdn_dt_bias[j], gdn_norm_w[j], ev_w_out[j], ln_mix_w[i], ln_mix_b[i])
        else:
            x32, xb = _odd_mixer(x3, xb3, od_w_in, j, rwkv_mu[j], rwkv_w0[j], rwkv_w2[j], rwkv_a0[j], rwkv_a2[j],
                                 rwkv_g2[j], rwkv_k_k[j], rwkv_k_a[j], rwkv_r_k[j].reshape(-1), rwkv_lnx_w[j],
                                 rwkv_lnx_b[j], fox_b_f[j], od_w_out[j], ln_mix_w[i], ln_mix_b[i])
        x32, xb = _cross_attention(x32, xb, mem_b, xa_w_q, xa_w_kv, i, xa_w_o[i], ln_xa_w[i], ln_xa_b[i])
        x32, xb = _conv_glu_ffn(x32, xb, b, ffn_w_up, i, ffn_conv_w[i], ffn_w_down[i], ln_ffn_w[i], ln_ffn_b[i])
    return x32.reshape(b, s, d)
```

```python
import functools
import math

import numpy as np
import jax
import jax.numpy as jnp
from jax import lax
from jax.experimental import pallas as pl
from jax.experimental.pallas import tpu as pltpu

F32 = jnp.float32
BF16 = jnp.bfloat16

D_MODEL = 2048
HALF = D_MODEL // 2
POOL_WINDOWS = (2, 4, 8, 16)
POOL_GDIM = HALF // len(POOL_WINDOWS)
GDN_HEADS = 8
GDN_DK = HALF // GDN_HEADS
GDN_CONV = 4
NORM_EPS = 1e-6
RWKV_HEAD = 64
RWKV_LNX_EPS = 64e-5
RWKV_LORA = (64, 64, 160)
RWKV_LORA_PAD = (128, 128, 256)
FOX_HEADS = 8
FOX_HD = HALF // FOX_HEADS
XA_HEADS = 4
XA_HD = D_MODEL // XA_HEADS
D_FF = 5632
FFN_CONV = 3
DEPTH = 2
DEEPNORM_ALPHA = float((2 * DEPTH) ** 0.25)
LN_EPS = 1e-5

CHUNK = 64
HIST = 8
VMEM_LIMIT_BYTES = 56 * 1024 * 1024


def _cparams(*sem):
    return pltpu.CompilerParams(dimension_semantics=sem, vmem_limit_bytes=VMEM_LIMIT_BYTES)


def _dot(a, b):
    return jnp.dot(a, b, preferred_element_type=F32)


def _dot_nt(a, b):
    return lax.dot_general(a, b, (((1,), (1,)), ((), ())), preferred_element_type=F32)


def _dot_tn(a, b):
    return lax.dot_general(a, b, (((0,), (0,)), ((), ())), preferred_element_type=F32)


def _bf(x):
    return x.astype(BF16)


def _split3(x):
    x1 = x.astype(BF16)
    r1 = x - x1.astype(F32)
    x2 = r1.astype(BF16)
    x3 = (r1 - x2.astype(F32)).astype(BF16)
    return x1, x2, x3


def _dot01(m01, x):
    x1, x2, x3 = _split3(x)
    return _dot(m01, x1) + _dot(m01, x2) + _dot(m01, x3)


def _unit_lower_inverses(n_mats, eye, fillers=()):
    c = eye.shape[0]
    levels = int(math.log2(c))
    fillers = list(fillers)
    xs = [eye + n for n in n_mats]
    nbs = [_bf(n) for n in n_mats]
    ps = [_dot(nb, nb) for nb in nbs]
    for level in range(1, levels):
        pbs = [_bf(p) for p in ps]
        if level == levels - 1:
            prods = [_dot(_bf(x), pb) for x, pb in zip(xs, pbs)]
        else:
            prods = [_dot(jnp.concatenate([pb, _bf(x)], axis=0), pb) for x, pb in zip(xs, pbs)]
        if fillers:
            fillers.pop(0)()
        if level == levels - 1:
            xs = [x + pr for x, pr in zip(xs, prods)]
        else:
            xs = [x + pr[c:] for x, pr in zip(xs, prods)]
            ps = [pr[:c] for pr in prods]
    for filler in fillers:
        filler()
    return xs


def _sigmoid(x):
    return 1.0 / (1.0 + jnp.exp(-x))


def _softplus(x):
    return jnp.maximum(x, 0.0) + jnp.log(1.0 + jnp.exp(-jnp.abs(x)))


def _tri_masks(c):
    row = lax.broadcasted_iota(jnp.int32, (c, c), 0)
    col = lax.broadcasted_iota(jnp.int32, (c, c), 1)
    return row >= col, row > col


def _stage_rows(x_ref, sc_ref, ts, first):
    @pl.when(first)
    def _():
        sc_ref[0:HIST, :] = jnp.zeros((HIST, sc_ref.shape[1]), F32)
    sc_ref[HIST:HIST + ts, :] = x_ref[0].astype(F32)


def _carry_rows(sc_ref, ts):
    sc_ref[0:HIST, :] = sc_ref[ts:ts + HIST, :]


def _weight_tile(w_ref, wb_ref, first):
    if wb_ref is None:
        return w_ref
    @pl.when(first)
    def _():
        wb_ref[...] = w_ref[...].astype(BF16)
    return wb_ref


def _weight_cols(w, n, col0, tn):
    n = w.shape[-1] - col0 if n is None else n
    tn = min(tn, n)
    assert n % tn == 0 and col0 % tn == 0, (n, col0, tn)
    scratch = [] if w.dtype == BF16 else [pltpu.VMEM((w.shape[-2], tn), BF16)]
    return n, tn, col0 // tn, scratch


def _weight_spec(w, tn, layer, col_of):
    k = w.shape[-2]
    if w.ndim == 2:
        return pl.BlockSpec((k, tn), lambda *g: (0, col_of(*g)))
    return pl.BlockSpec((None, k, tn), lambda *g: (layer, 0, col_of(*g)))


def _mm_kernel(a_ref, w_ref, o_ref, wb_ref=None):
    w = _weight_tile(w_ref, wb_ref, pl.program_id(1) == 0)
    o_ref[...] = _dot(a_ref[...], w[...]).astype(o_ref.dtype)


def _mm_cast_kernel(a_ref, w_ref, o_ref, ab_ref, wb_ref=None):
    w = _weight_tile(w_ref, wb_ref, pl.program_id(0) == 0)
    ab = a_ref[...].astype(BF16)
    ab_ref[...] = ab
    o_ref[...] = _dot(ab, w[...]).astype(o_ref.dtype)


def _mm_cast(a, w, out_dtype, tm=512, col0=0, n=None, layer=0):
    m, k = a.shape
    n, tn, joff, scratch = _weight_cols(w, n, col0, w.shape[-1])
    tm = min(tm, m)
    return pl.pallas_call(
        _mm_cast_kernel,
        grid=(m // tm,),
        in_specs=[pl.BlockSpec((tm, k), lambda i: (i, 0)),
                  _weight_spec(w, tn, layer, lambda i: joff)],
        out_specs=[pl.BlockSpec((tm, tn), lambda i: (i, 0)), pl.BlockSpec((tm, k), lambda i: (i, 0))],
        out_shape=[jax.ShapeDtypeStruct((m, n), out_dtype), jax.ShapeDtypeStruct((m, k), BF16)],
        scratch_shapes=scratch,
        compiler_params=_cparams("arbitrary"),
        name="mm_cast",
    )(a, w)


def _mm(a, w, out_dtype, tm=1024, tn=1024, col0=0, n=None, layer=0):
    m, k = a.shape
    n, tn, joff, scratch = _weight_cols(w, n, col0, tn)
    tm = min(tm, m)
    return pl.pallas_call(
        _mm_kernel,
        grid=(n // tn, m // tm),
        in_specs=[pl.BlockSpec((tm, k), lambda j, i: (i, 0)),
                  _weight_spec(w, tn, layer, lambda j, i: joff + j)],
        out_specs=pl.BlockSpec((tm, tn), lambda j, i: (i, j)),
        out_shape=jax.ShapeDtypeStruct((m, n), out_dtype),
        scratch_shapes=scratch,
        compiler_params=_cparams("arbitrary", "arbitrary"),
        name="mm",
    )(a, w)


def _mm_res_ln_kernel(*refs, n_in, tk_steps, tr):
    a_refs = refs[:n_in]
    w_refs = refs[n_in:2 * n_in]
    res_ref, lnw_ref, lnb_ref, o32_ref, o16_ref = refs[2 * n_in:2 * n_in + 5]
    acc_ref = refs[2 * n_in + 5] if tk_steps > 1 else None
    kk = pl.program_id(1)
    tm = res_ref.shape[0]

    def partial_sum(rows):
        tot = None
        for a_ref, w_ref in zip(a_refs, w_refs):
            d = _dot(a_ref[rows, :], w_ref[...])
            tot = d if tot is None else tot + d
        return tot

    if tk_steps > 1:
        @pl.when(kk == 0)
        def _():
            acc_ref[...] = partial_sum(slice(None))

        @pl.when((kk > 0) & (kk < tk_steps - 1))
        def _():
            acc_ref[...] += partial_sum(slice(None))

    @pl.when(kk == tk_steps - 1)
    def _():
        for r0 in range(0, tm, tr):
            rows = slice(r0, r0 + tr)
            y = DEEPNORM_ALPHA * res_ref[rows, :] + partial_sum(rows)
            if tk_steps > 1:
                y = y + acc_ref[rows, :]
            mu = jnp.mean(y, axis=-1, keepdims=True)
            d = y - mu
            var = jnp.mean(d * d, axis=-1, keepdims=True)
            out = d * lax.rsqrt(var + LN_EPS) * lnw_ref[...] + lnb_ref[...]
            o32_ref[rows, :] = out
            o16_ref[rows, :] = out.astype(BF16)


def _mm_res_ln(a_list, w, res, lnw, lnb, layer=0, tm=512, tk=None, tr=128):
    m = res.shape[0]
    n = res.shape[1]
    k = a_list[0].shape[1]
    tm = min(tm, m)
    tk = k if tk is None else min(tk, k)
    tr = min(tr, tm)
    n_in = len(a_list)
    steps = k // tk
    w_mode = {"pipeline_mode": pl.Buffered(1)} if steps == 1 else {}
    w_spec = lambda part: pl.BlockSpec((None, tk, n), lambda i, kk: (layer, part * steps + kk, 0), **w_mode)
    in_specs = ([pl.BlockSpec((tm, tk), lambda i, kk: (i, kk)) for _ in a_list]
                + [w_spec(part) for part in range(n_in)]
                + [pl.BlockSpec((tm, n), lambda i, kk: (i, 0)),
                   pl.BlockSpec((1, n), lambda i, kk: (0, 0)),
                   pl.BlockSpec((1, n), lambda i, kk: (0, 0))])
    return pl.pallas_call(
        functools.partial(_mm_res_ln_kernel, n_in=n_in, tk_steps=steps, tr=tr),
        grid=(m // tm, steps),
        in_specs=in_specs,
        out_specs=[pl.BlockSpec((tm, n), lambda i, kk: (i, 0)),
                   pl.BlockSpec((tm, n), lambda i, kk: (i, 0))],
        out_shape=[jax.ShapeDtypeStruct((m, n), F32), jax.ShapeDtypeStruct((m, n), BF16)],
        scratch_shapes=[pltpu.VMEM((tm, n), F32)] if steps > 1 else [],
        compiler_params=_cparams("arbitrary", "arbitrary"),
        name="mm_res_ln",
    )(*a_list, *([w] * n_in), res, lnw.reshape(1, n), lnb.reshape(1, n))


def _xattn_kernel(q_ref, k_ref, v_ref, o_ref, *, tr):
    kb = k_ref[0]
    vb = v_ref[0]
    for r0 in range(0, q_ref.shape[1], tr):
        s = _dot_nt(q_ref[0, r0:r0 + tr, :], kb) * (XA_HD ** -0.5)
        m = jnp.max(s, axis=-1, keepdims=True)
        p = jnp.exp(s - m)
        l = jnp.sum(p, axis=-1, keepdims=True)
        o = _dot(_bf(p / l), vb)
        o_ref[0, r0:r0 + tr, :] = o.astype(o_ref.dtype)


def _xattn(q, kv, ts=2048):
    b, s, d = q.shape
    mlen = kv.shape[1]
    ts = min(ts, s)
    return pl.pallas_call(
        functools.partial(_xattn_kernel, tr=min(256, ts)),
        grid=(b, s // ts, XA_HEADS),
        in_specs=[pl.BlockSpec((1, ts, XA_HD), lambda bi, i, h: (bi, i, h)),
                  pl.BlockSpec((1, mlen, XA_HD), lambda bi, i, h: (bi, 0, h)),
                  pl.BlockSpec((1, mlen, XA_HD), lambda bi, i, h: (bi, 0, XA_HEADS + h))],
        out_specs=pl.BlockSpec((1, ts, XA_HD), lambda bi, i, h: (bi, i, h)),
        out_shape=jax.ShapeDtypeStruct((b, s, d), BF16),
        compiler_params=_cparams("arbitrary", "arbitrary", "arbitrary"),
        name="xattn",
    )(q, kv, kv)


def _mm_taps_kernel(x_ref, w_ref, t_ref, o_ref, sc_ref, wb_ref=None, *, ts, ntap, act):
    first = pl.program_id(2) == 0

    @pl.when(first)
    def _():
        sc_ref[0:HIST, :] = jnp.zeros((HIST, sc_ref.shape[1]), F32)

    w = _weight_tile(w_ref, wb_ref, first & (pl.program_id(1) == 0))
    sc_ref[HIST:HIST + ts, :] = _dot(x_ref[0], w[...])
    y = t_ref[ntap - 1:ntap, :] * sc_ref[HIST:HIST + ts, :]
    for back in range(1, ntap):
        tap = ntap - 1 - back
        y = y + t_ref[tap:tap + 1, :] * sc_ref[HIST - back:HIST - back + ts, :]
    if act == "silu":
        y = y * _sigmoid(y)
    elif act == "lora":
        p0, p1, _ = RWKV_LORA_PAD
        y = jnp.concatenate([jnp.tanh(y[:, :p0]), y[:, p0:p0 + p1], _sigmoid(y[:, p0 + p1:])], axis=1)
    o_ref[0] = y.astype(o_ref.dtype)
    _carry_rows(sc_ref, ts)


def _mm_taps(xb, w, taps, out_dtype, act=None, ts=1024, tn=1024, col0=0, n=None, layer=0):
    b, s, d = xb.shape
    n, tn, joff, cast_scratch = _weight_cols(w, n, col0, tn)
    ts = min(ts, s)
    ntap = taps.shape[0]
    return pl.pallas_call(
        functools.partial(_mm_taps_kernel, ts=ts, ntap=ntap, act=act),
        grid=(n // tn, b, s // ts),
        in_specs=[pl.BlockSpec((1, ts, d), lambda j, bi, i: (bi, i, 0)),
                  _weight_spec(w, tn, layer, lambda j, bi, i: joff + j),
                  pl.BlockSpec((ntap, tn), lambda j, bi, i: (0, j))],
        out_specs=pl.BlockSpec((1, ts, tn), lambda j, bi, i: (bi, i, j)),
        out_shape=jax.ShapeDtypeStruct((b, s, n), out_dtype),
        scratch_shapes=[pltpu.VMEM((ts + HIST, tn), F32)] + cast_scratch,
        compiler_params=_cparams("arbitrary", "arbitrary", "arbitrary"),
        name="mm_taps",
    )(xb, w, taps.astype(F32))


def _ffn_up_kernel(x_ref, wg_ref, wu_ref, cg_ref, cu_ref, o_ref, sg_ref, su_ref, wgb_ref=None, wub_ref=None,
                   *, ts, tr):
    first = pl.program_id(2) == 0

    @pl.when(first)
    def _():
        sg_ref[0:HIST, :] = jnp.zeros((HIST, sg_ref.shape[1]), F32)
        su_ref[0:HIST, :] = jnp.zeros((HIST, su_ref.shape[1]), F32)

    new_tile = first & (pl.program_id(1) == 0)
    wg_ref = _weight_tile(wg_ref, wgb_ref, new_tile)
    wu_ref = _weight_tile(wu_ref, wub_ref, new_tile)

    def conv(sc_ref, c_ref, r0):
        y = c_ref[FFN_CONV - 1:FFN_CONV, :] * sc_ref[HIST + r0:HIST + r0 + tr, :]
        for back in range(1, FFN_CONV):
            tap = FFN_CONV - 1 - back
            y = y + c_ref[tap:tap + 1, :] * sc_ref[HIST + r0 - back:HIST + r0 - back + tr, :]
        return y

    for r0 in range(0, ts, tr):
        x = x_ref[0, r0:r0 + tr, :]
        sg_ref[HIST + r0:HIST + r0 + tr, :] = _dot(x, wg_ref[...])
        g = conv(sg_ref, cg_ref, r0)
        g = g * _sigmoid(g)
        su_ref[HIST + r0:HIST + r0 + tr, :] = _dot(x, wu_ref[...])
        u = conv(su_ref, cu_ref, r0)
        o_ref[0, r0:r0 + tr, :] = (g * u).astype(o_ref.dtype)
    _carry_rows(sg_ref, ts)
    _carry_rows(su_ref, ts)


def _ffn_up(xb, w_up, conv_w, ts=1024, tn=512, tr=1024, layer=0):
    b, s, d = xb.shape
    ts = min(ts, s)
    nj = D_FF // tn
    cast_scratch = [] if w_up.dtype == BF16 else [pltpu.VMEM((d, tn), BF16), pltpu.VMEM((d, tn), BF16)]
    return pl.pallas_call(
        functools.partial(_ffn_up_kernel, ts=ts, tr=min(tr, ts)),
        grid=(nj, b, s // ts),
        in_specs=[pl.BlockSpec((1, ts, d), lambda j, bi, i: (bi, i, 0)),
                  _weight_spec(w_up, tn, layer, lambda j, bi, i: j),
                  _weight_spec(w_up, tn, layer, lambda j, bi, i: nj + j),
                  pl.BlockSpec((FFN_CONV, tn), lambda j, bi, i: (0, j)),
                  pl.BlockSpec((FFN_CONV, tn), lambda j, bi, i: (0, nj + j))],
        out_specs=pl.BlockSpec((1, ts, tn), lambda j, bi, i: (bi, i, j)),
        out_shape=jax.ShapeDtypeStruct((b, s, D_FF), BF16),
        scratch_shapes=[pltpu.VMEM((ts + HIST, tn), F32), pltpu.VMEM((ts + HIST, tn), F32)] + cast_scratch,
        compiler_params=_cparams("arbitrary", "arbitrary", "arbitrary"),
        name="ffn_up",
    )(xb, w_up, w_up, conv_w, conv_w)


def _pool_kernel(u_ref, w_ref, sc_ref, o_ref, st_ref, *, ts):
    i = pl.program_id(1)

    @pl.when(i == 0)
    def _():
        st_ref[0:2 * HIST, :] = jnp.zeros((2 * HIST, st_ref.shape[1]), F32)

    st_ref[2 * HIST:2 * HIST + ts, :] = u_ref[0].astype(F32)
    pos = (i * ts + 1 + lax.broadcasted_iota(jnp.int32, (ts, 1), 0)).astype(F32)
    base = 2 * HIST
    for g, win in enumerate(POOL_WINDOWS):
        cols = slice(g * POOL_GDIM, (g + 1) * POOL_GDIM)
        cur = st_ref[base:base + ts, cols]
        acc = cur
        for back in range(1, win):
            acc = acc + st_ref[base - back:base - back + ts, cols]
        mean = acc / jnp.minimum(pos, float(win))
        y = _dot(_bf(mean - cur), w_ref[g])
        o_ref[0, :, cols] = (y * sc_ref[:, cols]).astype(o_ref.dtype)
    st_ref[0:2 * HIST, :] = st_ref[ts:ts + 2 * HIST, :]


def _pool_mixer(h_main, pool_w, pool_scale, ts=512):
    b, s, _ = h_main.shape
    ts = min(ts, s)
    return pl.pallas_call(
        functools.partial(_pool_kernel, ts=ts),
        grid=(b, s // ts),
        in_specs=[pl.BlockSpec((1, ts, HALF), lambda bi, i: (bi, i, 0)),
                  pl.BlockSpec((len(POOL_WINDOWS), POOL_GDIM, POOL_GDIM), lambda bi, i: (0, 0, 0)),
                  pl.BlockSpec((1, HALF), lambda bi, i: (0, 0))],
        out_specs=pl.BlockSpec((1, ts, HALF), lambda bi, i: (bi, i, 0)),
        out_shape=jax.ShapeDtypeStruct((b, s, HALF), BF16),
        scratch_shapes=[pltpu.VMEM((ts + 2 * HIST, HALF), F32)],
        compiler_params=_cparams("arbitrary", "arbitrary"),
        name="pool_mixer",
    )(h_main, pool_w, pool_scale.reshape(1, HALF))


def _gdn_kernel(q_ref, k_ref, v_ref, z_ref, g_ref, alog_ref, dtb_ref, nw_ref, o_ref, st_ref, *, ts, hps):
    c = CHUNK
    h0 = pl.program_id(1)

    @pl.when(pl.program_id(2) == 0)
    def _():
        st_ref[...] = jnp.zeros_like(st_ref)

    q_all = q_ref[0]
    k_all = k_ref[0]
    v_all = v_ref[0]
    z_all = z_ref[0].astype(F32)

    logits = g_ref[0]
    lane = lax.broadcasted_iota(jnp.int32, logits.shape, 1)
    g_all = -jnp.exp(alog_ref[...]) * _softplus(logits + dtb_ref[...])
    sig_all = _sigmoid(logits)
    trow = lax.broadcasted_iota(jnp.int32, (ts, ts), 0)
    tcol = lax.broadcasted_iota(jnp.int32, (ts, ts), 1)
    tril_chunks = ((trow >= tcol) & (trow // c == tcol // c)).astype(BF16)
    gc_all = _dot01(tril_chunks, g_all)

    causal, strict = _tri_masks(c)
    eye = jnp.where(causal & (~strict), 1.0, 0.0).astype(F32)
    lane_c = lax.broadcasted_iota(jnp.int32, (c, GDN_DK), 1)
    nw = nw_ref[...]
    chunk_rows = [slice(ci * c, (ci + 1) * c) for ci in range(ts // c)]
    nck = len(chunk_rows)

    q, k, v, betas, g_cols = [], [], [], [], []
    for hd in range(hps):
        hl = slice(hd * GDN_DK, (hd + 1) * GDN_DK)
        qh, kh = q_all[:, hl], k_all[:, hl]
        qh = qh * lax.rsqrt(jnp.sum(qh * qh, axis=-1, keepdims=True) + NORM_EPS) * (GDN_DK ** -0.5)
        kh = kh * lax.rsqrt(jnp.sum(kh * kh, axis=-1, keepdims=True) + NORM_EPS)
        head = h0 * hps + hd
        g_col = jnp.sum(jnp.where(lane == head, gc_all, 0.0), axis=1, keepdims=True)
        b_col = jnp.sum(jnp.where(lane == head + GDN_HEADS, sig_all, 0.0), axis=1, keepdims=True)
        for r in chunk_rows:
            q.append(qh[r])
            k.append(kh[r])
            v.append(v_all[r, hl])
            betas.append(jnp.broadcast_to(b_col[r], (c, GDN_DK)))
            g_cols.append(jnp.broadcast_to(g_col[r], (c, GDN_DK)))
    nprob = len(q)
    rows = list(range(nprob))
    gcs = g_cols
    diffs = []
    for gc in gcs:
        g1, g2, g3 = (p.astype(F32) for p in _split3(gc))
        lhs = jnp.where(lane_c == 0, g1, jnp.where(lane_c == 1, g2, jnp.where(lane_c == 2, g3,
              jnp.where(lane_c < 6, 1.0, 0.0))))
        rhs = jnp.where(lane_c < 3, 1.0, jnp.where(lane_c == 3, -g1, jnp.where(lane_c == 4, -g2,
              jnp.where(lane_c == 5, -g3, 0.0))))
        diffs.append(_dot_nt(_bf(lhs), _bf(rhs)))
    decays = [jnp.where(causal, jnp.exp(jnp.where(causal, d, 0.0)), 0.0) for d in diffs]
    kbs = [k[r] * bt for r, bt in zip(rows, betas)]
    kcbs = [_bf(k[r]) for r in rows]
    lmats = [jnp.where(strict, _dot_nt(_bf(kb), kcb) * dc, 0.0) for kb, kcb, dc in zip(kbs, kcbs, decays)]
    intras = [_bf(jnp.where(causal, _dot_nt(_bf(q[r]), kcb) * dc, 0.0)) for r, kcb, dc in zip(rows, kcbs, decays)]
    tmats = [_bf(t) for t in _unit_lower_inverses([-l for l in lmats], eye)]
    egs = [jnp.exp(gc) for gc in gcs]
    g_lasts = [gc[c - 1:c, :] for gc in gcs]
    uws = [_dot(t, jnp.concatenate([_bf(v[r] * bt), _bf(kb * eg)], axis=1))
           for t, r, bt, kb, eg in zip(tmats, rows, betas, kbs, egs)]
    us = [uw[:, :GDN_DK] for uw in uws]
    ws = [_bf(uw[:, GDN_DK:]) for uw in uws]
    q_decs = [_bf(q[r] * eg) for r, eg in zip(rows, egs)]
    k_decs = [_bf(k[r] * jnp.exp(gl - gc)) for r, gl, gc in zip(rows, g_lasts, gcs)]
    e_lasts = [jnp.exp(gl) for gl in g_lasts]
    mns = [_dot_tn(kd, jnp.concatenate([w, _bf(u)], axis=1)) for kd, w, u in zip(k_decs, ws, us)]
    mmats = [_bf(mn[:, :GDN_DK]) for mn in mns]
    nmats = [mn[:, GDN_DK:] for mn in mns]

    states = [st_ref[hd] for hd in range(hps)]
    sb_hist = [[None] * nck for _ in range(hps)]
    for ci in range(nck):
        for hd in range(hps):
            p = hd * nck + ci
            sb = _bf(states[hd])
            sb_hist[hd][ci] = sb
            states[hd] = states[hd] * e_lasts[p] - _dot(mmats[p], sb) + nmats[p]
    for hd in range(hps):
        st_ref[hd] = states[hd]
    for hd in range(hps):
        hl = slice(hd * GDN_DK, (hd + 1) * GDN_DK)
        for ci in range(nck):
            p = hd * nck + ci
            sb = sb_hist[hd][ci]
            v_new = _bf(us[p] - _dot(ws[p], sb))
            o = _dot(q_decs[p], sb) + _dot(intras[p], v_new)
            o = o * lax.rsqrt(jnp.mean(o * o, axis=-1, keepdims=True) + NORM_EPS) * nw
            zc = z_all[chunk_rows[ci], hl]
            o_ref[0, chunk_rows[ci], hl] = (o * (zc * _sigmoid(zc))).astype(o_ref.dtype)


def _gated_deltanet(qkv, z, logits, a_log, dt_bias, norm_w, ts=256, hps=8):
    b, s, _ = qkv.shape
    ts = min(ts, s)
    hd = GDN_DK
    gw = hps * hd
    ng = HALF // gw
    pad = lambda t: jnp.pad(t.astype(F32), (0, hd - t.shape[0])).reshape(1, hd)
    col = lambda grp: (lambda bi, h, i: (bi, i, grp * ng + h))
    return pl.pallas_call(
        functools.partial(_gdn_kernel, ts=ts, hps=hps),
        grid=(b, ng, s // ts),
        in_specs=[pl.BlockSpec((1, ts, gw), col(0)), pl.BlockSpec((1, ts, gw), col(1)),
                  pl.BlockSpec((1, ts, gw), col(2)), pl.BlockSpec((1, ts, gw), col(0)),
                  pl.BlockSpec((1, ts, hd), lambda bi, h, i: (bi, i, 0)),
                  pl.BlockSpec((1, hd), lambda bi, h, i: (0, 0)),
                  pl.BlockSpec((1, hd), lambda bi, h, i: (0, 0)),
                  pl.BlockSpec((1, hd), lambda bi, h, i: (0, 0))],
        out_specs=pl.BlockSpec((1, ts, gw), lambda bi, h, i: (bi, i, h)),
        out_shape=jax.ShapeDtypeStruct((b, s, HALF), BF16),
        scratch_shapes=[pltpu.VMEM((hps, hd, hd), F32)],
        compiler_params=_cparams("arbitrary", "arbitrary", "arbitrary"),
        name="gated_deltanet",
    )(qkv, qkv, qkv, z, logits, pad(a_log), pad(dt_bias), norm_w.astype(F32).reshape(1, hd))


def _rwkv_kernel(r_ref, k_ref, v_ref, l_ref, w2_ref, a2_ref, g2_ref,
                 w0_ref, a0_ref, kk_ref, ka_ref, rk_ref, lnw_ref, lnb_ref, o_ref, st_ref, *, ts, hps):
    c = CHUNK
    hs = RWKV_HEAD

    @pl.when(pl.program_id(2) == 0)
    def _():
        st_ref[...] = jnp.zeros_like(st_ref)

    r = r_ref[0]
    k = k_ref[0]
    v = v_ref[0]
    lo = l_ref[0]
    p0, p1, p2 = RWKV_LORA_PAD
    w_log = -_softplus(-(w0_ref[...] + _dot(lo[:, :p0], w2_ref[...]))) - 0.5
    lw = -jnp.exp(w_log)
    a = _sigmoid(a0_ref[...] + _dot(lo[:, p0:p0 + p1], a2_ref[...]))
    g = _dot(lo[:, p0 + p1:p0 + p1 + p2], g2_ref[...])
    kk_raw = k * kk_ref[...]
    km = k * (1.0 + (a - 1.0) * ka_ref[...])
    head_of_lane = lax.broadcasted_iota(jnp.int32, (ts, hps * hs), 1) // hs

    def per_head_sum(x):
        out = jnp.zeros_like(x)
        for hd in range(hps):
            sel = head_of_lane == hd
            out = jnp.where(sel, jnp.sum(jnp.where(sel, x, 0.0), axis=-1, keepdims=True), out)
        return out

    kk = kk_raw * lax.rsqrt(per_head_sum(kk_raw * kk_raw) + NORM_EPS)
    aa = -kk
    bb = kk * a
    bonus = per_head_sum(r * km * rk_ref[...]) * v

    causal, strict = _tri_masks(c)
    tril = causal.astype(BF16)
    eye = jnp.where(causal & (~strict), 1.0, 0.0).astype(F32)
    lnw = lnw_ref[...]
    lnb = lnb_ref[...]

    rows = [slice(ci * c, (ci + 1) * c) for ci in range(ts // c)]
    nck = len(rows)
    heads = [slice(hd * hs, (hd + 1) * hs) for hd in range(hps)]
    cums = [_dot01(tril, lw[rw]) for rw in rows]
    c_lasts = [cum[c - 1:c, :] for cum in cums]
    w_invs = [jnp.exp(-cum) for cum in cums]
    w_tails = [jnp.exp(cl - cum) for cl, cum in zip(c_lasts, cums)]
    a_ts = [_bf(aa[rw] * jnp.exp(cum - lw[rw])) for rw, cum in zip(rows, cums)]
    r_ts = [_bf(r[rw] * jnp.exp(cum)) for rw, cum in zip(rows, cums)]
    b_ts = [_bf(bb[rw] * wi) for rw, wi in zip(rows, w_invs)]
    k_ts = [_bf(km[rw] * wi) for rw, wi in zip(rows, w_invs)]
    b_hs = [_bf(bb[rw] * wt) for rw, wt in zip(rows, w_tails)]
    k_hs = [_bf(km[rw] * wt) for rw, wt in zip(rows, w_tails)]
    vcs = [_bf(v[rw]) for rw in rows]
    e_lasts = [jnp.exp(cl) for cl in c_lasts]
    prob = [(ci, cs) for ci in range(nck) for cs in heads]
    row2 = lax.broadcasted_iota(jnp.int32, (c, 2 * c), 0)
    col2 = lax.broadcasted_iota(jnp.int32, (c, 2 * c), 1) % c
    bks = [jnp.concatenate([b_ts[ci][:, cs], k_ts[ci][:, cs]], axis=0) for ci, cs in prob]
    a_abks = [jnp.where(row2 > col2, _dot_nt(a_ts[ci][:, cs], bk), 0.0) for bk, (ci, cs) in zip(bks, prob)]
    a_rbks = [_bf(jnp.where(row2 >= col2, _dot_nt(r_ts[ci][:, cs], bk), 0.0)) for bk, (ci, cs) in zip(bks, prob)]
    tmats = [_bf(t) for t in _unit_lower_inverses([m[:, :c] for m in a_abks], eye)]
    zero_c = jnp.zeros((c, hs), BF16)
    avs = [_bf(_dot(_bf(abk), jnp.concatenate([zero_c, vcs[ci][:, cs]], axis=0)))
           for abk, (ci, cs) in zip(a_abks, prob)]
    pmats = [_bf(_dot(t, a_ts[ci][:, cs])) for t, (ci, cs) in zip(tmats, prob)]
    qmats = [_dot(t, av) for t, av in zip(tmats, avs)]
    mmats = [_bf(_dot_tn(pm, b_hs[ci][:, cs])) for pm, (ci, cs) in zip(pmats, prob)]
    nmats = [_dot_tn(jnp.concatenate([_bf(qm), vcs[ci][:, cs]], axis=0),
                     jnp.concatenate([b_hs[ci][:, cs], k_hs[ci][:, cs]], axis=0))
             for qm, (ci, cs) in zip(qmats, prob)]

    states = [st_ref[hd] for hd in range(hps)]
    sb_hist = []
    for ci in range(nck):
        sbs = [_bf(s) for s in states]
        sb_hist.append(sbs)
        states = [states[hd] * e_lasts[ci][:, heads[hd]] + _dot(sbs[hd], mmats[hps * ci + hd]) + nmats[hps * ci + hd]
                  for hd in range(hps)]
    for hd in range(hps):
        st_ref[hd] = states[hd]
    ys = []
    for ci in range(nck):
        sbs = sb_hist[ci]
        ubs = [_bf(_dot_nt(pmats[hps * ci + hd], sbs[hd]) + qmats[hps * ci + hd]) for hd in range(hps)]
        ys.append([_dot_nt(r_ts[ci][:, heads[hd]], sbs[hd])
                   + _dot(a_rbks[hps * ci + hd], jnp.concatenate([ubs[hd], vcs[ci][:, heads[hd]]], axis=0))
                   for hd in range(hps)])
    for ci in range(nck):
        outs = []
        for y in ys[ci]:
            ym = jnp.mean(y, axis=-1, keepdims=True)
            yd = y - ym
            yv = jnp.mean(yd * yd, axis=-1, keepdims=True)
            outs.append(yd * lax.rsqrt(yv + RWKV_LNX_EPS))
        yn = jnp.concatenate(outs, axis=-1) * lnw + lnb
        o_ref[0, rows[ci], :] = ((yn + bonus[rows[ci]]) * g[rows[ci]]).astype(o_ref.dtype)


def _rwkv7(hr, hl, w2p, a2p, g2p, w0, a0, k_k, k_a, r_k, lnx_w, lnx_b, ts=256, hps=16):
    b, s, _ = hr.shape
    ts = min(ts, s)
    pw = hps * RWKV_HEAD
    npair = HALF // pw
    lw = sum(RWKV_LORA_PAD)
    col = lambda grp: (lambda bi, hp, i: (bi, i, grp * npair + hp))
    vec = lambda grp: pl.BlockSpec((1, pw), lambda bi, hp, i: (0, grp * npair + hp))
    row1 = lambda t: t.astype(F32).reshape(1, -1)
    return pl.pallas_call(
        functools.partial(_rwkv_kernel, ts=ts, hps=hps),
        grid=(b, npair, s // ts),
        in_specs=[pl.BlockSpec((1, ts, pw), col(0)), pl.BlockSpec((1, ts, pw), col(1)),
                  pl.BlockSpec((1, ts, pw), col(2)),
                  pl.BlockSpec((1, ts, lw), lambda bi, hp, i: (bi, i, 0)),
                  pl.BlockSpec((RWKV_LORA_PAD[0], pw), lambda bi, hp, i: (0, hp)),
                  pl.BlockSpec((RWKV_LORA_PAD[1], pw), lambda bi, hp, i: (0, hp)),
                  pl.BlockSpec((RWKV_LORA_PAD[2], pw), lambda bi, hp, i: (0, hp)),
                  vec(0), vec(0), vec(0), vec(0), vec(0), vec(0), vec(0)],
        out_specs=pl.BlockSpec((1, ts, pw), lambda bi, hp, i: (bi, i, hp)),
        out_shape=jax.ShapeDtypeStruct((b, s, HALF), BF16),
        scratch_shapes=[pltpu.VMEM((hps, RWKV_HEAD, RWKV_HEAD), F32)],
        compiler_params=_cparams("arbitrary", "arbitrary", "arbitrary"),
        name="rwkv7",
    )(hr, hr, hr, hl, w2p, a2p, g2p,
      row1(w0), row1(a0), row1(k_k), row1(k_a), row1(r_k), row1(lnx_w), row1(lnx_b))


def _fox_cumsum_kernel(f_ref, bf_ref, o_ref):
    x = f_ref[0] + bf_ref[...]
    ls = jnp.minimum(x, 0.0) - jnp.log(1.0 + jnp.exp(-jnp.abs(x)))
    nh, nr, nl = ls.shape
    li = lax.broadcasted_iota(jnp.int32, (nl, nl), 0)
    lj = lax.broadcasted_iota(jnp.int32, (nl, nl), 1)
    upper = (li <= lj).astype(BF16)
    ri = lax.broadcasted_iota(jnp.int32, (nr, nr), 0)
    rj = lax.broadcasted_iota(jnp.int32, (nr, nr), 1)
    below = (ri > rj).astype(BF16)
    for h in range(nh):
        within = _dot01_right(ls[h], upper)
        tot = jnp.broadcast_to(within[:, nl - 1:nl], (nr, nl))
        o_ref[0, h] = within + _dot01(below, tot)


def _dot01_right(x, m01):
    x1, x2, x3 = _split3(x)
    return _dot(x1, m01) + _dot(x2, m01) + _dot(x3, m01)


def _fox_cumsum(f_t, b_f):
    b, nh, s = f_t.shape
    nl = 128
    nr = s // nl
    out = pl.pallas_call(
        _fox_cumsum_kernel,
        grid=(b,),
        in_specs=[pl.BlockSpec((1, nh, nr, nl), lambda bi: (bi, 0, 0, 0)),
                  pl.BlockSpec((nh, 1, 1), lambda bi: (0, 0, 0))],
        out_specs=pl.BlockSpec((1, nh, nr, nl), lambda bi: (bi, 0, 0, 0)),
        out_shape=jax.ShapeDtypeStruct((b, nh, nr, nl), F32),
        compiler_params=_cparams("arbitrary"),
        name="fox_cumsum",
    )(f_t.reshape(b, nh, nr, nl), b_f.astype(F32).reshape(nh, 1, 1))
    return out.reshape(b, nh, 1, s)


def _fox_kernel(qi_ref, ki_ref, q_ref, k_ref, v_ref, cq_ref, ck_ref, o_ref, qs_ref, m_ref, acc_ref, *, tq, tk, tr):
    p = pl.program_id(2)
    qi = qi_ref[p]
    ki = ki_ref[p]
    last_k = ((qi + 1) * tq - 1) // tk
    log2e = 1.0 / math.log(2.0)

    @pl.when(ki == 0)
    def _():
        qs_ref[...] = (q_ref[0].astype(F32) * (FOX_HD ** -0.5 * log2e)).astype(BF16)
        m_ref[...] = jnp.full_like(m_ref, -jnp.inf)
        acc_ref[...] = jnp.zeros_like(acc_ref)

    def step(masked):
        bias = (cq_ref[0, 0, :, 0:1] - ck_ref[0, 0]) * log2e
        v_aug = jnp.concatenate([v_ref[0], jnp.ones((tk, FOX_HD), BF16)], axis=1)
        kb = k_ref[0]
        for r0 in range(0, tq, tr):
            rows = slice(r0, r0 + tr)
            s = _dot_nt(qs_ref[rows, :], kb) + bias
            if masked:
                qpos = qi * tq + r0 + lax.broadcasted_iota(jnp.int32, (tr, tk), 0)
                kpos = ki * tk + lax.broadcasted_iota(jnp.int32, (tr, tk), 1)
                s = jnp.where(kpos <= qpos, s, -jnp.inf)
            m_old = m_ref[rows, :]
            m_new = jnp.maximum(m_old, jnp.max(s, axis=-1, keepdims=True))
            corr = jnp.exp2(m_old - m_new)
            pexp = _bf(jnp.exp2(s - m_new))
            acc_ref[rows, :] = corr * acc_ref[rows, :] + _dot(pexp, v_aug)
            m_ref[rows, :] = m_new

    needs_mask = (ki + 1) * tk - 1 > qi * tq

    @pl.when(needs_mask)
    def _():
        step(True)

    @pl.when(jnp.logical_not(needs_mask))
    def _():
        step(False)

    @pl.when(ki == last_k)
    def _():
        acc = acc_ref[...]
        o_ref[0] = (acc[:, :FOX_HD] / acc[:, FOX_HD:FOX_HD + 1]).astype(o_ref.dtype)


def _fox_attention(hf, c, tq=2048, tk=2048, tr=128):
    b, s, _ = hf.shape
    tq = min(tq, s)
    tk = min(tk, s)
    nh = FOX_HEADS
    pairs = [(qi, ki) for qi in range(s // tq) for ki in range(((qi + 1) * tq - 1) // tk + 1)]
    qi_arr = jnp.asarray(np.array([p[0] for p in pairs], np.int32))
    ki_arr = jnp.asarray(np.array([p[1] for p in pairs], np.int32))
    grid_spec = pltpu.PrefetchScalarGridSpec(
        num_scalar_prefetch=2,
        grid=(b, nh, len(pairs)),
        in_specs=[pl.BlockSpec((1, tq, FOX_HD), lambda bi, h, p, qa, ka: (bi, qa[p], h)),
                  pl.BlockSpec((1, tk, FOX_HD), lambda bi, h, p, qa, ka: (bi, ka[p], nh + h)),
                  pl.BlockSpec((1, tk, FOX_HD), lambda bi, h, p, qa, ka: (bi, ka[p], 2 * nh + h)),
                  pl.BlockSpec((1, 1, 1, tq), lambda bi, h, p, qa, ka: (bi, h, 0, qa[p])),
                  pl.BlockSpec((1, 1, 1, tk), lambda bi, h, p, qa, ka: (bi, h, 0, ka[p]))],
        out_specs=pl.BlockSpec((1, tq, FOX_HD), lambda bi, h, p, qa, ka: (bi, qa[p], h)),
        scratch_shapes=[pltpu.VMEM((tq, FOX_HD), BF16), pltpu.VMEM((tq, 1), F32),
                        pltpu.VMEM((tq, 2 * FOX_HD), F32)])
    return pl.pallas_call(
        functools.partial(_fox_kernel, tq=tq, tk=tk, tr=min(tr, tq)),
        grid_spec=grid_spec,
        out_shape=jax.ShapeDtypeStruct((b, s, HALF), BF16),
        compiler_params=_cparams("arbitrary", "arbitrary", "arbitrary"),
        name="fox_attention",
    )(qi_arr, ki_arr, hf, hf, hf, c, c)


def _pad_cols(w, width):
    return jnp.pad(w, ((0, 0), (0, width - w.shape[1])))


def _pad_rows(w, height):
    return jnp.pad(w, ((0, height - w.shape[0]), (0, 0)))


def _even_mixer(x32, xb, w_in, layer, pool_w, pool_scale, conv_w, a_log, dt_bias, norm_w, w_out, ln_w, ln_b):
    b, s, d = x32.shape
    t = b * s
    n_main = 5 * HALF
    w_gate = _bf(_pad_cols(w_in[layer, :, n_main:], 128))
    if xb is None:
        u_pool, xb2 = _mm_cast(x32.reshape(t, d), w_in, F32, col0=0, n=HALF, layer=layer)
        xb = xb2.reshape(b, s, d)
    else:
        xb2 = xb.reshape(t, d)
        u_pool = _mm(xb2, w_in, F32, col0=0, n=HALF, layer=layer)
    u_pool = u_pool.reshape(b, s, HALF)
    qkv = _mm_taps(xb, w_in, conv_w, F32, act="silu", col0=HALF, n=3 * HALF, layer=layer)
    z = _mm(xb2, w_in, BF16, col0=4 * HALF, n=HALF, layer=layer).reshape(b, s, HALF)
    logits = _mm(xb2, w_gate, F32, tn=128).reshape(b, s, 128)
    y_a = _pool_mixer(u_pool, _bf(pool_w), pool_scale)
    y_b = _gated_deltanet(qkv, z, logits, a_log, dt_bias, norm_w)
    return _mm_res_ln([y_a.reshape(t, HALF), y_b.reshape(t, HALF)], w_out, x32.reshape(t, d), ln_w, ln_b,
                      layer=layer)


def _odd_mixer(x32, xb, w_in_all, layer, mu, w0, w2, a0, a2, g2, k_k, k_a, r_k, lnx_w, lnx_b, b_f, w_out, ln_w, ln_b):
    b, s, d = x32.shape
    t = b * s
    l0, l1, l2 = RWKV_LORA
    p0, p1, p2 = RWKV_LORA_PAD
    o_l = 3 * HALF
    o_f = o_l + l0 + l1 + l2
    w_tail = w_in_all[layer, :, o_l:]
    w_l = _bf(jnp.concatenate([_pad_cols(w_tail[:, :l0], p0),
                               _pad_cols(w_tail[:, l0:l0 + l1], p1),
                               _pad_cols(w_tail[:, l0 + l1:o_f - o_l], p2)], axis=1))
    w_fox = _bf(w_tail[:, o_f - o_l:o_f - o_l + 3 * HALF])
    w_fl = _bf(_pad_cols(w_tail[:, o_f - o_l + 3 * HALF:], 128))
    mu_l = jnp.concatenate([jnp.pad(mu[o_l:o_l + l0], (0, p0 - l0)),
                            jnp.pad(mu[o_l + l0:o_l + l0 + l1], (0, p1 - l1)),
                            jnp.pad(mu[o_l + l0 + l1:o_f], (0, p2 - l2))])
    xb2 = xb.reshape(t, d)
    lerp = lambda m: jnp.stack([m, 1.0 - m])
    hr = _mm_taps(xb, w_in_all, lerp(mu[:o_l]), F32, col0=0, n=o_l, layer=layer)
    hl = _mm_taps(xb, w_l, lerp(mu_l), BF16, act="lora", tn=p0 + p1 + p2)
    hf = _mm(xb2, w_fox, BF16).reshape(b, s, 3 * HALF)
    fl = _mm(xb2, w_fl, F32, tn=128).reshape(b, s, 128)
    y_c = _rwkv7(hr, hl, _bf(_pad_rows(w2, p0)), _bf(_pad_rows(a2, p1)), _bf(_pad_rows(g2, p2)),
                 w0, a0, k_k, k_a, r_k, lnx_w, lnx_b)
    c = _fox_cumsum(jnp.transpose(fl[:, :, :FOX_HEADS], (0, 2, 1)), b_f)
    y_d = _fox_attention(hf, c)
    return _mm_res_ln([y_c.reshape(t, HALF), y_d.reshape(t, HALF)], w_out, x32.reshape(t, d), ln_w, ln_b,
                      layer=layer)


def _cross_attention(x32, xb, mem_b, w_q, w_kv, layer, w_o, ln_w, ln_b):
    t, d = x32.shape
    b, mlen, _ = mem_b.shape
    s = t // b
    q = _mm(xb, w_q, BF16, layer=layer).reshape(b, s, d)
    kv = _mm(mem_b.reshape(b * mlen, d), w_kv, BF16, layer=layer).reshape(b, mlen, 2 * d)
    o = _xattn(q, kv).reshape(t, d)
    return _mm_res_ln([o], w_o, x32, ln_w, ln_b, layer=layer)


def _conv_glu_ffn(x32, xb, b, w_up, layer, conv_w, w_down, ln_w, ln_b):
    t, d = x32.shape
    hmid = _ffn_up(xb.reshape(b, t // b, d), w_up, conv_w, layer=layer)
    return _mm_res_ln([hmid.reshape(t, D_FF)], w_down, x32, ln_w, ln_b, layer=layer, tm=256)


def kernel(x, mem, ev_w_in, pool_w, pool_scale, gdn_conv_w, gdn_a_log, gdn_dt_bias, gdn_norm_w, ev_w_out,
           od_w_in, rwkv_mu, rwkv_w0, rwkv_w2, rwkv_a0, rwkv_a2, rwkv_g2, rwkv_k_k, rwkv_k_a, rwkv_r_k,
           rwkv_lnx_w, rwkv_lnx_b, fox_b_f, od_w_out,
           ln_mix_w, ln_mix_b, xa_w_q, xa_w_kv, xa_w_o, ln_xa_w, ln_xa_b,
           ffn_w_up, ffn_conv_w, ffn_w_down, ln_ffn_w, ln_ffn_b):
    b, s, d = x.shape
    depth = ln_mix_w.shape[0]
    mem_b = _bf(mem)
    ev_wo, od_wo, xa_wo, ffn_wd = _bf(ev_w_out), _bf(od_w_out), _bf(xa_w_o), _bf(ffn_w_down)
    x32 = x
    xb = None
    for i in range(depth):
        j = i // 2
        x3 = x32.reshape(b, s, d)
        xb3 = None if xb is None else xb.reshape(b, s, d)
        if i % 2 == 0:
            x32, xb = _even_mixer(x3, xb3, ev_w_in, j, pool_w[j], pool_scale[j], gdn_conv_w[j], gdn_a_log[j],
                                  gdn_dt_bias[j], gdn_norm_w[j], ev_wo, ln_mix_w[i], ln_mix_b[i])
        else:
            x32, xb = _odd_mixer(x3, xb3, od_w_in, j, rwkv_mu[j], rwkv_w0[j], rwkv_w2[j], rwkv_a0[j], rwkv_a2[j],
                                 rwkv_g2[j], rwkv_k_k[j], rwkv_k_a[j], rwkv_r_k[j].reshape(-1), rwkv_lnx_w[j],
                                 rwkv_lnx_b[j], fox_b_f[j], od_wo, ln_mix_w[i], ln_mix_b[i])
        x32, xb = _cross_attention(x32, xb, mem_b, xa_w_q, xa_w_kv, i, xa_wo, ln_xa_w[i], ln_xa_b[i])
        x32, xb = _conv_glu_ffn(x32, xb, b, ffn_w_up, i, ffn_conv_w[i], ffn_wd, ln_ffn_w[i], ln_ffn_b[i])
    return x32.reshape(b, s, d)
```

```python
import functools
import math

import numpy as np
import jax
import jax.numpy as jnp
from jax import lax
from jax.experimental import pallas as pl
from jax.experimental.pallas import tpu as pltpu

F32 = jnp.float32
BF16 = jnp.bfloat16

D_MODEL = 2048
HALF = D_MODEL // 2
POOL_WINDOWS = (2, 4, 8, 16)
POOL_GDIM = HALF // len(POOL_WINDOWS)
GDN_HEADS = 8
GDN_DK = HALF // GDN_HEADS
GDN_CONV = 4
NORM_EPS = 1e-6
RWKV_HEAD = 64
RWKV_LNX_EPS = 64e-5
RWKV_LORA = (64, 64, 160)
RWKV_LORA_PAD = (128, 128, 256)
FOX_HEADS = 8
FOX_HD = HALF // FOX_HEADS
XA_HEADS = 4
XA_HD = D_MODEL // XA_HEADS
D_FF = 5632
FFN_CONV = 3
DEPTH = 2
DEEPNORM_ALPHA = float((2 * DEPTH) ** 0.25)
LN_EPS = 1e-5

CHUNK = 64
HIST = 8
VMEM_LIMIT_BYTES = 56 * 1024 * 1024


def _cparams(*sem):
    return pltpu.CompilerParams(dimension_semantics=sem, vmem_limit_bytes=VMEM_LIMIT_BYTES)


def _dot(a, b):
    return jnp.dot(a, b, preferred_element_type=F32)


def _dot_nt(a, b):
    return lax.dot_general(a, b, (((1,), (1,)), ((), ())), preferred_element_type=F32)


def _dot_tn(a, b):
    return lax.dot_general(a, b, (((0,), (0,)), ((), ())), preferred_element_type=F32)


def _bf(x):
    return x.astype(BF16)


def _split3(x):
    x1 = x.astype(BF16)
    r1 = x - x1.astype(F32)
    x2 = r1.astype(BF16)
    x3 = (r1 - x2.astype(F32)).astype(BF16)
    return x1, x2, x3


def _dot01(m01, x):
    x1, x2, x3 = _split3(x)
    return _dot(m01, x1) + _dot(m01, x2) + _dot(m01, x3)


def _unit_lower_inverses(n_mats, eye, fillers=()):
    c = eye.shape[0]
    levels = int(math.log2(c))
    fillers = list(fillers)
    xs = [eye + n for n in n_mats]
    nbs = [_bf(n) for n in n_mats]
    ps = [_dot(nb, nb) for nb in nbs]
    for level in range(1, levels):
        pbs = [_bf(p) for p in ps]
        if level == levels - 1:
            prods = [_dot(_bf(x), pb) for x, pb in zip(xs, pbs)]
        else:
            prods = [_dot(jnp.concatenate([pb, _bf(x)], axis=0), pb) for x, pb in zip(xs, pbs)]
        if fillers:
            fillers.pop(0)()
        if level == levels - 1:
            xs = [x + pr for x, pr in zip(xs, prods)]
        else:
            xs = [x + pr[c:] for x, pr in zip(xs, prods)]
            ps = [pr[:c] for pr in prods]
    for filler in fillers:
        filler()
    return xs


def _sigmoid(x):
    return 1.0 / (1.0 + jnp.exp(-x))


def _softplus(x):
    return jnp.maximum(x, 0.0) + jnp.log(1.0 + jnp.exp(-jnp.abs(x)))


def _tri_masks(c):
    row = lax.broadcasted_iota(jnp.int32, (c, c), 0)
    col = lax.broadcasted_iota(jnp.int32, (c, c), 1)
    return row >= col, row > col


def _stage_rows(x_ref, sc_ref, ts, first):
    @pl.when(first)
    def _():
        sc_ref[0:HIST, :] = jnp.zeros((HIST, sc_ref.shape[1]), F32)
    sc_ref[HIST:HIST + ts, :] = x_ref[0].astype(F32)


def _carry_rows(sc_ref, ts):
    sc_ref[0:HIST, :] = sc_ref[ts:ts + HIST, :]


def _weight_tile(w_ref, wb_ref, first):
    if wb_ref is None:
        return w_ref
    @pl.when(first)
    def _():
        wb_ref[...] = w_ref[...].astype(BF16)
    return wb_ref


def _weight_cols(w, n, col0, tn):
    n = w.shape[-1] - col0 if n is None else n
    tn = min(tn, n)
    assert n % tn == 0 and col0 % tn == 0, (n, col0, tn)
    scratch = [] if w.dtype == BF16 else [pltpu.VMEM((w.shape[-2], tn), BF16)]
    return n, tn, col0 // tn, scratch


def _weight_spec(w, tn, layer, col_of):
    k = w.shape[-2]
    if w.ndim == 2:
        return pl.BlockSpec((k, tn), lambda *g: (0, col_of(*g)))
    return pl.BlockSpec((None, k, tn), lambda *g: (layer, 0, col_of(*g)))


def _mm_kernel(a_ref, w_ref, o_ref, wb_ref=None):
    w = _weight_tile(w_ref, wb_ref, pl.program_id(1) == 0)
    o_ref[...] = _dot(a_ref[...], w[...]).astype(o_ref.dtype)


def _mm_cast_kernel(a_ref, w_ref, o_ref, ab_ref, wb_ref=None):
    w = _weight_tile(w_ref, wb_ref, pl.program_id(0) == 0)
    ab = a_ref[...].astype(BF16)
    ab_ref[...] = ab
    o_ref[...] = _dot(ab, w[...]).astype(o_ref.dtype)


def _mm_cast(a, w, out_dtype, tm=512, col0=0, n=None, layer=0):
    m, k = a.shape
    n, tn, joff, scratch = _weight_cols(w, n, col0, w.shape[-1])
    tm = min(tm, m)
    return pl.pallas_call(
        _mm_cast_kernel,
        grid=(m // tm,),
        in_specs=[pl.BlockSpec((tm, k), lambda i: (i, 0)),
                  _weight_spec(w, tn, layer, lambda i: joff)],
        out_specs=[pl.BlockSpec((tm, tn), lambda i: (i, 0)), pl.BlockSpec((tm, k), lambda i: (i, 0))],
        out_shape=[jax.ShapeDtypeStruct((m, n), out_dtype), jax.ShapeDtypeStruct((m, k), BF16)],
        scratch_shapes=scratch,
        compiler_params=_cparams("arbitrary"),
        name="mm_cast",
    )(a, w)


def _mm(a, w, out_dtype, tm=1024, tn=1024, col0=0, n=None, layer=0):
    m, k = a.shape
    n, tn, joff, scratch = _weight_cols(w, n, col0, tn)
    tm = min(tm, m)
    return pl.pallas_call(
        _mm_kernel,
        grid=(n // tn, m // tm),
        in_specs=[pl.BlockSpec((tm, k), lambda j, i: (i, 0)),
                  _weight_spec(w, tn, layer, lambda j, i: joff + j)],
        out_specs=pl.BlockSpec((tm, tn), lambda j, i: (i, j)),
        out_shape=jax.ShapeDtypeStruct((m, n), out_dtype),
        scratch_shapes=scratch,
        compiler_params=_cparams("arbitrary", "arbitrary"),
        name="mm",
    )(a, w)


def _mm_res_ln_kernel(*refs, n_in, tk_steps, tr):
    a_refs = refs[:n_in]
    w_refs = refs[n_in:2 * n_in]
    res_ref, lnw_ref, lnb_ref, o32_ref, o16_ref = refs[2 * n_in:2 * n_in + 5]
    acc_ref = refs[2 * n_in + 5] if tk_steps > 1 else None
    kk = pl.program_id(1)
    tm = res_ref.shape[0]

    def partial_sum(rows):
        tot = None
        for a_ref, w_ref in zip(a_refs, w_refs):
            d = _dot(a_ref[rows, :], w_ref[...])
            tot = d if tot is None else tot + d
        return tot

    if tk_steps > 1:
        @pl.when(kk == 0)
        def _():
            acc_ref[...] = partial_sum(slice(None))

        @pl.when((kk > 0) & (kk < tk_steps - 1))
        def _():
            acc_ref[...] += partial_sum(slice(None))

    @pl.when(kk == tk_steps - 1)
    def _():
        for r0 in range(0, tm, tr):
            rows = slice(r0, r0 + tr)
            y = DEEPNORM_ALPHA * res_ref[rows, :] + partial_sum(rows)
            if tk_steps > 1:
                y = y + acc_ref[rows, :]
            mu = jnp.mean(y, axis=-1, keepdims=True)
            d = y - mu
            var = jnp.mean(d * d, axis=-1, keepdims=True)
            out = d * lax.rsqrt(var + LN_EPS) * lnw_ref[...] + lnb_ref[...]
            o32_ref[rows, :] = out
            o16_ref[rows, :] = out.astype(BF16)


def _mm_res_ln(a_list, w, res, lnw, lnb, layer=0, tm=512, tk=None, tr=128):
    m = res.shape[0]
    n = res.shape[1]
    k = a_list[0].shape[1]
    tm = min(tm, m)
    tk = k if tk is None else min(tk, k)
    tr = min(tr, tm)
    n_in = len(a_list)
    steps = k // tk
    w_mode = {"pipeline_mode": pl.Buffered(1)} if steps == 1 else {}
    w_spec = lambda part: pl.BlockSpec((None, tk, n), lambda i, kk: (layer, part * steps + kk, 0), **w_mode)
    in_specs = ([pl.BlockSpec((tm, tk), lambda i, kk: (i, kk)) for _ in a_list]
                + [w_spec(part) for part in range(n_in)]
                + [pl.BlockSpec((tm, n), lambda i, kk: (i, 0)),
                   pl.BlockSpec((1, n), lambda i, kk: (0, 0)),
                   pl.BlockSpec((1, n), lambda i, kk: (0, 0))])
    return pl.pallas_call(
        functools.partial(_mm_res_ln_kernel, n_in=n_in, tk_steps=steps, tr=tr),
        grid=(m // tm, steps),
        in_specs=in_specs,
        out_specs=[pl.BlockSpec((tm, n), lambda i, kk: (i, 0)),
                   pl.BlockSpec((tm, n), lambda i, kk: (i, 0))],
        out_shape=[jax.ShapeDtypeStruct((m, n), F32), jax.ShapeDtypeStruct((m, n), BF16)],
        scratch_shapes=[pltpu.VMEM((tm, n), F32)] if steps > 1 else [],
        compiler_params=_cparams("arbitrary", "arbitrary"),
        name="mm_res_ln",
    )(*a_list, *([w] * n_in), res, lnw.reshape(1, n), lnb.reshape(1, n))


def _fold_qk_kernel(wq_ref, k_ref, g_ref):
    g = _dot_nt(wq_ref[...].astype(BF16), k_ref[0]) * (XA_HD ** -0.5)
    g_ref[0] = g.astype(g_ref.dtype)


def _fold_qk(w_q, layer, kv):
    b, mlen, _ = kv.shape
    d = w_q.shape[-1]
    return pl.pallas_call(
        _fold_qk_kernel,
        grid=(b, XA_HEADS),
        in_specs=[pl.BlockSpec((None, d, XA_HD), lambda bi, h: (layer, 0, h)),
                  pl.BlockSpec((1, mlen, XA_HD), lambda bi, h: (bi, 0, h))],
        out_specs=pl.BlockSpec((1, d, mlen), lambda bi, h: (bi, 0, h)),
        out_shape=jax.ShapeDtypeStruct((b, d, XA_HEADS * mlen), BF16),
        compiler_params=_cparams("arbitrary", "arbitrary"),
        name="fold_qk",
    )(w_q, kv)


def _fold_vo_kernel(v_ref, wo_ref, o_ref):
    o_ref[0] = _dot(v_ref[0], wo_ref[...]).astype(o_ref.dtype)


def _fold_vo(kv, w_o, layer):
    b, mlen, _ = kv.shape
    d = w_o.shape[-1]
    return pl.pallas_call(
        _fold_vo_kernel,
        grid=(b, XA_HEADS),
        in_specs=[pl.BlockSpec((1, mlen, XA_HD), lambda bi, h: (bi, 0, XA_HEADS + h)),
                  pl.BlockSpec((None, XA_HD, d), lambda bi, h: (layer, h, 0))],
        out_specs=pl.BlockSpec((1, mlen, d), lambda bi, h: (bi, h, 0)),
        out_shape=jax.ShapeDtypeStruct((b, XA_HEADS * mlen, d), BF16),
        compiler_params=_cparams("arbitrary", "arbitrary"),
        name="fold_vo",
    )(kv, w_o)


def _xattn_res_ln_kernel(x_ref, g_ref, vo_ref, res_ref, lnw_ref, lnb_ref, o32_ref, o16_ref, *, tr, mlen):
    tm = x_ref.shape[0]
    for r0 in range(0, tm, tr):
        rows = slice(r0, r0 + tr)
        s = _dot(x_ref[rows, :], g_ref[...])
        probs = []
        for h in range(XA_HEADS):
            sh = s[:, h * mlen:(h + 1) * mlen]
            e = jnp.exp(sh - jnp.max(sh, axis=-1, keepdims=True))
            probs.append(_bf(e / jnp.sum(e, axis=-1, keepdims=True)))
        y = DEEPNORM_ALPHA * res_ref[rows, :] + _dot(jnp.concatenate(probs, axis=1), vo_ref[...])
        mu = jnp.mean(y, axis=-1, keepdims=True)
        d = y - mu
        var = jnp.mean(d * d, axis=-1, keepdims=True)
        out = d * lax.rsqrt(var + LN_EPS) * lnw_ref[...] + lnb_ref[...]
        o32_ref[rows, :] = out
        o16_ref[rows, :] = out.astype(BF16)


def _xattn_res_ln(xb, g, vo, res, lnw, lnb, tm=512, tr=128):
    m, d = res.shape
    b, _, hm = g.shape
    tm = min(tm, m // b)
    tiles_per_batch = m // b // tm
    return pl.pallas_call(
        functools.partial(_xattn_res_ln_kernel, tr=min(tr, tm), mlen=hm // XA_HEADS),
        grid=(m // tm,),
        in_specs=[pl.BlockSpec((tm, d), lambda i: (i, 0)),
                  pl.BlockSpec((None, d, hm), lambda i: (i // tiles_per_batch, 0, 0)),
                  pl.BlockSpec((None, hm, d), lambda i: (i // tiles_per_batch, 0, 0)),
                  pl.BlockSpec((tm, d), lambda i: (i, 0)),
                  pl.BlockSpec((1, d), lambda i: (0, 0)),
                  pl.BlockSpec((1, d), lambda i: (0, 0))],
        out_specs=[pl.BlockSpec((tm, d), lambda i: (i, 0)), pl.BlockSpec((tm, d), lambda i: (i, 0))],
        out_shape=[jax.ShapeDtypeStruct((m, d), F32), jax.ShapeDtypeStruct((m, d), BF16)],
        compiler_params=_cparams("arbitrary"),
        name="xattn_res_ln",
    )(xb, g, vo, res, lnw.reshape(1, d), lnb.reshape(1, d))


def _mm_taps_kernel(x_ref, w_ref, t_ref, o_ref, sc_ref, wb_ref=None, *, ts, ntap, act):
    first = pl.program_id(2) == 0

    @pl.when(first)
    def _():
        sc_ref[0:HIST, :] = jnp.zeros((HIST, sc_ref.shape[1]), F32)

    w = _weight_tile(w_ref, wb_ref, first & (pl.program_id(1) == 0))
    sc_ref[HIST:HIST + ts, :] = _dot(x_ref[0], w[...])
    y = t_ref[ntap - 1:ntap, :] * sc_ref[HIST:HIST + ts, :]
    for back in range(1, ntap):
        tap = ntap - 1 - back
        y = y + t_ref[tap:tap + 1, :] * sc_ref[HIST - back:HIST - back + ts, :]
    if act == "silu":
        y = y * _sigmoid(y)
    elif act == "lora":
        p0, p1, _ = RWKV_LORA_PAD
        y = jnp.concatenate([jnp.tanh(y[:, :p0]), y[:, p0:p0 + p1], _sigmoid(y[:, p0 + p1:])], axis=1)
    o_ref[0] = y.astype(o_ref.dtype)
    _carry_rows(sc_ref, ts)


def _mm_taps(xb, w, taps, out_dtype, act=None, ts=1024, tn=1024, col0=0, n=None, layer=0):
    b, s, d = xb.shape
    n, tn, joff, cast_scratch = _weight_cols(w, n, col0, tn)
    ts = min(ts, s)
    ntap = taps.shape[0]
    return pl.pallas_call(
        functools.partial(_mm_taps_kernel, ts=ts, ntap=ntap, act=act),
        grid=(n // tn, b, s // ts),
        in_specs=[pl.BlockSpec((1, ts, d), lambda j, bi, i: (bi, i, 0)),
                  _weight_spec(w, tn, layer, lambda j, bi, i: joff + j),
                  pl.BlockSpec((ntap, tn), lambda j, bi, i: (0, j))],
        out_specs=pl.BlockSpec((1, ts, tn), lambda j, bi, i: (bi, i, j)),
        out_shape=jax.ShapeDtypeStruct((b, s, n), out_dtype),
        scratch_shapes=[pltpu.VMEM((ts + HIST, tn), F32)] + cast_scratch,
        compiler_params=_cparams("arbitrary", "arbitrary", "arbitrary"),
        name="mm_taps",
    )(xb, w, taps.astype(F32))


def _ffn_up_kernel(x_ref, wg_ref, wu_ref, cg_ref, cu_ref, o_ref, sg_ref, su_ref, wgb_ref=None, wub_ref=None,
                   *, ts, tr):
    first = pl.program_id(2) == 0

    @pl.when(first)
    def _():
        sg_ref[0:HIST, :] = jnp.zeros((HIST, sg_ref.shape[1]), F32)
        su_ref[0:HIST, :] = jnp.zeros((HIST, su_ref.shape[1]), F32)

    new_tile = first & (pl.program_id(1) == 0)
    wg_ref = _weight_tile(wg_ref, wgb_ref, new_tile)
    wu_ref = _weight_tile(wu_ref, wub_ref, new_tile)

    def conv(sc_ref, c_ref, r0):
        y = c_ref[FFN_CONV - 1:FFN_CONV, :] * sc_ref[HIST + r0:HIST + r0 + tr, :]
        for back in range(1, FFN_CONV):
            tap = FFN_CONV - 1 - back
            y = y + c_ref[tap:tap + 1, :] * sc_ref[HIST + r0 - back:HIST + r0 - back + tr, :]
        return y

    for r0 in range(0, ts, tr):
        x = x_ref[0, r0:r0 + tr, :]
        sg_ref[HIST + r0:HIST + r0 + tr, :] = _dot(x, wg_ref[...])
        g = conv(sg_ref, cg_ref, r0)
        g = g * _sigmoid(g)
        su_ref[HIST + r0:HIST + r0 + tr, :] = _dot(x, wu_ref[...])
        u = conv(su_ref, cu_ref, r0)
        o_ref[0, r0:r0 + tr, :] = (g * u).astype(o_ref.dtype)
    _carry_rows(sg_ref, ts)
    _carry_rows(su_ref, ts)


def _ffn_up(xb, w_up, conv_w, ts=1024, tn=512, tr=1024, layer=0):
    b, s, d = xb.shape
    ts = min(ts, s)
    nj = D_FF // tn
    cast_scratch = [] if w_up.dtype == BF16 else [pltpu.VMEM((d, tn), BF16), pltpu.VMEM((d, tn), BF16)]
    return pl.pallas_call(
        functools.partial(_ffn_up_kernel, ts=ts, tr=min(tr, ts)),
        grid=(nj, b, s // ts),
        in_specs=[pl.BlockSpec((1, ts, d), lambda j, bi, i: (bi, i, 0)),
                  _weight_spec(w_up, tn, layer, lambda j, bi, i: j),
                  _weight_spec(w_up, tn, layer, lambda j, bi, i: nj + j),
                  pl.BlockSpec((FFN_CONV, tn), lambda j, bi, i: (0, j)),
                  pl.BlockSpec((FFN_CONV, tn), lambda j, bi, i: (0, nj + j))],
        out_specs=pl.BlockSpec((1, ts, tn), lambda j, bi, i: (bi, i, j)),
        out_shape=jax.ShapeDtypeStruct((b, s, D_FF), BF16),
        scratch_shapes=[pltpu.VMEM((ts + HIST, tn), F32), pltpu.VMEM((ts + HIST, tn), F32)] + cast_scratch,
        compiler_params=_cparams("arbitrary", "arbitrary", "arbitrary"),
        name="ffn_up",
    )(xb, w_up, w_up, conv_w, conv_w)


def _pool_kernel(u_ref, w_ref, sc_ref, o_ref, st_ref, *, ts):
    i = pl.program_id(1)

    @pl.when(i == 0)
    def _():
        st_ref[0:2 * HIST, :] = jnp.zeros((2 * HIST, st_ref.shape[1]), F32)

    st_ref[2 * HIST:2 * HIST + ts, :] = u_ref[0].astype(F32)
    pos = (i * ts + 1 + lax.broadcasted_iota(jnp.int32, (ts, 1), 0)).astype(F32)
    base = 2 * HIST
    for g, win in enumerate(POOL_WINDOWS):
        cols = slice(g * POOL_GDIM, (g + 1) * POOL_GDIM)
        cur = st_ref[base:base + ts, cols]
        acc = cur
        for back in range(1, win):
            acc = acc + st_ref[base - back:base - back + ts, cols]
        mean = acc / jnp.minimum(pos, float(win))
        y = _dot(_bf(mean - cur), w_ref[g])
        o_ref[0, :, cols] = (y * sc_ref[:, cols]).astype(o_ref.dtype)
    st_ref[0:2 * HIST, :] = st_ref[ts:ts + 2 * HIST, :]


def _pool_mixer(h_main, pool_w, pool_scale, ts=512):
    b, s, _ = h_main.shape
    ts = min(ts, s)
    return pl.pallas_call(
        functools.partial(_pool_kernel, ts=ts),
        grid=(b, s // ts),
        in_specs=[pl.BlockSpec((1, ts, HALF), lambda bi, i: (bi, i, 0)),
                  pl.BlockSpec((len(POOL_WINDOWS), POOL_GDIM, POOL_GDIM), lambda bi, i: (0, 0, 0)),
                  pl.BlockSpec((1, HALF), lambda bi, i: (0, 0))],
        out_specs=pl.BlockSpec((1, ts, HALF), lambda bi, i: (bi, i, 0)),
        out_shape=jax.ShapeDtypeStruct((b, s, HALF), BF16),
        scratch_shapes=[pltpu.VMEM((ts + 2 * HIST, HALF), F32)],
        compiler_params=_cparams("arbitrary", "arbitrary"),
        name="pool_mixer",
    )(h_main, pool_w, pool_scale.reshape(1, HALF))


def _gdn_kernel(q_ref, k_ref, v_ref, z_ref, g_ref, alog_ref, dtb_ref, nw_ref, o_ref, st_ref, *, ts, hps):
    c = CHUNK
    h0 = pl.program_id(1)

    @pl.when(pl.program_id(2) == 0)
    def _():
        st_ref[...] = jnp.zeros_like(st_ref)

    q_all = q_ref[0]
    k_all = k_ref[0]
    v_all = v_ref[0]
    z_all = z_ref[0].astype(F32)

    logits = g_ref[0]
    lane = lax.broadcasted_iota(jnp.int32, logits.shape, 1)
    g_all = -jnp.exp(alog_ref[...]) * _softplus(logits + dtb_ref[...])
    sig_all = _sigmoid(logits)
    trow = lax.broadcasted_iota(jnp.int32, (ts, ts), 0)
    tcol = lax.broadcasted_iota(jnp.int32, (ts, ts), 1)
    tril_chunks = ((trow >= tcol) & (trow // c == tcol // c)).astype(BF16)
    gc_all = _dot01(tril_chunks, g_all)

    causal, strict = _tri_masks(c)
    eye = jnp.where(causal & (~strict), 1.0, 0.0).astype(F32)
    lane_c = lax.broadcasted_iota(jnp.int32, (c, GDN_DK), 1)
    nw = nw_ref[...]
    chunk_rows = [slice(ci * c, (ci + 1) * c) for ci in range(ts // c)]
    nck = len(chunk_rows)

    q, k, v, betas, g_cols = [], [], [], [], []
    for hd in range(hps):
        hl = slice(hd * GDN_DK, (hd + 1) * GDN_DK)
        qh, kh = q_all[:, hl], k_all[:, hl]
        qh = qh * lax.rsqrt(jnp.sum(qh * qh, axis=-1, keepdims=True) + NORM_EPS) * (GDN_DK ** -0.5)
        kh = kh * lax.rsqrt(jnp.sum(kh * kh, axis=-1, keepdims=True) + NORM_EPS)
        head = h0 * hps + hd
        g_col = jnp.sum(jnp.where(lane == head, gc_all, 0.0), axis=1, keepdims=True)
        b_col = jnp.sum(jnp.where(lane == head + GDN_HEADS, sig_all, 0.0), axis=1, keepdims=True)
        for r in chunk_rows:
            q.append(qh[r])
            k.append(kh[r])
            v.append(v_all[r, hl])
            betas.append(jnp.broadcast_to(b_col[r], (c, GDN_DK)))
            g_cols.append(jnp.broadcast_to(g_col[r], (c, GDN_DK)))
    nprob = len(q)
    rows = list(range(nprob))
    gcs = g_cols
    diffs = []
    for gc in gcs:
        g1, g2, g3 = (p.astype(F32) for p in _split3(gc))
        lhs = jnp.where(lane_c == 0, g1, jnp.where(lane_c == 1, g2, jnp.where(lane_c == 2, g3,
              jnp.where(lane_c < 6, 1.0, 0.0))))
        rhs = jnp.where(lane_c < 3, 1.0, jnp.where(lane_c == 3, -g1, jnp.where(lane_c == 4, -g2,
              jnp.where(lane_c == 5, -g3, 0.0))))
        diffs.append(_dot_nt(_bf(lhs), _bf(rhs)))
    decays = [jnp.where(causal, jnp.exp(jnp.where(causal, d, 0.0)), 0.0) for d in diffs]
    kbs = [k[r] * bt for r, bt in zip(rows, betas)]
    kcbs = [_bf(k[r]) for r in rows]
    lmats = [jnp.where(strict, _dot_nt(_bf(kb), kcb) * dc, 0.0) for kb, kcb, dc in zip(kbs, kcbs, decays)]
    intras = [_bf(jnp.where(causal, _dot_nt(_bf(q[r]), kcb) * dc, 0.0)) for r, kcb, dc in zip(rows, kcbs, decays)]
    tmats = [_bf(t) for t in _unit_lower_inverses([-l for l in lmats], eye)]
    egs = [jnp.exp(gc) for gc in gcs]
    g_lasts = [gc[c - 1:c, :] for gc in gcs]
    uws = [_dot(t, jnp.concatenate([_bf(v[r] * bt), _bf(kb * eg)], axis=1))
           for t, r, bt, kb, eg in zip(tmats, rows, betas, kbs, egs)]
    us = [uw[:, :GDN_DK] for uw in uws]
    ws = [_bf(uw[:, GDN_DK:]) for uw in uws]
    q_decs = [_bf(q[r] * eg) for r, eg in zip(rows, egs)]
    k_decs = [_bf(k[r] * jnp.exp(gl - gc)) for r, gl, gc in zip(rows, g_lasts, gcs)]
    e_lasts = [jnp.exp(gl) for gl in g_lasts]
    mns = [_dot_tn(kd, jnp.concatenate([w, _bf(u)], axis=1)) for kd, w, u in zip(k_decs, ws, us)]
    mmats = [_bf(mn[:, :GDN_DK]) for mn in mns]
    nmats = [mn[:, GDN_DK:] for mn in mns]

    states = [st_ref[hd] for hd in range(hps)]
    sb_hist = [[None] * nck for _ in range(hps)]
    for ci in range(nck):
        for hd in range(hps):
            p = hd * nck + ci
            sb = _bf(states[hd])
            sb_hist[hd][ci] = sb
            states[hd] = states[hd] * e_lasts[p] - _dot(mmats[p], sb) + nmats[p]
    for hd in range(hps):
        st_ref[hd] = states[hd]
    for hd in range(hps):
        hl = slice(hd * GDN_DK, (hd + 1) * GDN_DK)
        for ci in range(nck):
            p = hd * nck + ci
            sb = sb_hist[hd][ci]
            v_new = _bf(us[p] - _dot(ws[p], sb))
            o = _dot(q_decs[p], sb) + _dot(intras[p], v_new)
            o = o * lax.rsqrt(jnp.mean(o * o, axis=-1, keepdims=True) + NORM_EPS) * nw
            zc = z_all[chunk_rows[ci], hl]
            o_ref[0, chunk_rows[ci], hl] = (o * (zc * _sigmoid(zc))).astype(o_ref.dtype)


def _gated_deltanet(qkv, z, logits, a_log, dt_bias, norm_w, ts=256, hps=8):
    b, s, _ = qkv.shape
    ts = min(ts, s)
    hd = GDN_DK
    gw = hps * hd
    ng = HALF // gw
    pad = lambda t: jnp.pad(t.astype(F32), (0, hd - t.shape[0])).reshape(1, hd)
    col = lambda grp: (lambda bi, h, i: (bi, i, grp * ng + h))
    return pl.pallas_call(
        functools.partial(_gdn_kernel, ts=ts, hps=hps),
        grid=(b, ng, s // ts),
        in_specs=[pl.BlockSpec((1, ts, gw), col(0)), pl.BlockSpec((1, ts, gw), col(1)),
                  pl.BlockSpec((1, ts, gw), col(2)), pl.BlockSpec((1, ts, gw), col(0)),
                  pl.BlockSpec((1, ts, hd), lambda bi, h, i: (bi, i, 0)),
                  pl.BlockSpec((1, hd), lambda bi, h, i: (0, 0)),
                  pl.BlockSpec((1, hd), lambda bi, h, i: (0, 0)),
                  pl.BlockSpec((1, hd), lambda bi, h, i: (0, 0))],
        out_specs=pl.BlockSpec((1, ts, gw), lambda bi, h, i: (bi, i, h)),
        out_shape=jax.ShapeDtypeStruct((b, s, HALF), BF16),
        scratch_shapes=[pltpu.VMEM((hps, hd, hd), F32)],
        compiler_params=_cparams("arbitrary", "arbitrary", "arbitrary"),
        name="gated_deltanet",
    )(qkv, qkv, qkv, z, logits, pad(a_log), pad(dt_bias), norm_w.astype(F32).reshape(1, hd))


def _rwkv_kernel(r_ref, k_ref, v_ref, l_ref, w2_ref, a2_ref, g2_ref,
                 w0_ref, a0_ref, kk_ref, ka_ref, rk_ref, lnw_ref, lnb_ref, o_ref, st_ref, *, ts, hps):
    c = CHUNK
    hs = RWKV_HEAD

    @pl.when(pl.program_id(2) == 0)
    def _():
        st_ref[...] = jnp.zeros_like(st_ref)

    r = r_ref[0]
    k = k_ref[0]
    v = v_ref[0]
    lo = l_ref[0]
    p0, p1, p2 = RWKV_LORA_PAD
    w_log = -_softplus(-(w0_ref[...] + _dot(lo[:, :p0], w2_ref[...]))) - 0.5
    lw = -jnp.exp(w_log)
    a = _sigmoid(a0_ref[...] + _dot(lo[:, p0:p0 + p1], a2_ref[...]))
    g = _dot(lo[:, p0 + p1:p0 + p1 + p2], g2_ref[...])
    kk_raw = k * kk_ref[...]
    km = k * (1.0 + (a - 1.0) * ka_ref[...])
    head_of_lane = lax.broadcasted_iota(jnp.int32, (ts, hps * hs), 1) // hs

    def per_head_sum(x):
        out = jnp.zeros_like(x)
        for hd in range(hps):
            sel = head_of_lane == hd
            out = jnp.where(sel, jnp.sum(jnp.where(sel, x, 0.0), axis=-1, keepdims=True), out)
        return out

    kk = kk_raw * lax.rsqrt(per_head_sum(kk_raw * kk_raw) + NORM_EPS)
    aa = -kk
    bb = kk * a
    bonus = per_head_sum(r * km * rk_ref[...]) * v

    causal, strict = _tri_masks(c)
    tril = causal.astype(BF16)
    eye = jnp.where(causal & (~strict), 1.0, 0.0).astype(F32)
    lnw = lnw_ref[...]
    lnb = lnb_ref[...]

    rows = [slice(ci * c, (ci + 1) * c) for ci in range(ts // c)]
    nck = len(rows)
    heads = [slice(hd * hs, (hd + 1) * hs) for hd in range(hps)]
    cums = [_dot01(tril, lw[rw]) for rw in rows]
    c_lasts = [cum[c - 1:c, :] for cum in cums]
    w_invs = [jnp.exp(-cum) for cum in cums]
    w_tails = [jnp.exp(cl - cum) for cl, cum in zip(c_lasts, cums)]
    a_ts = [_bf(aa[rw] * jnp.exp(cum - lw[rw])) for rw, cum in zip(rows, cums)]
    r_ts = [_bf(r[rw] * jnp.exp(cum)) for rw, cum in zip(rows, cums)]
    b_ts = [_bf(bb[rw] * wi) for rw, wi in zip(rows, w_invs)]
    k_ts = [_bf(km[rw] * wi) for rw, wi in zip(rows, w_invs)]
    b_hs = [_bf(bb[rw] * wt) for rw, wt in zip(rows, w_tails)]
    k_hs = [_bf(km[rw] * wt) for rw, wt in zip(rows, w_tails)]
    vcs = [_bf(v[rw]) for rw in rows]
    e_lasts = [jnp.exp(cl) for cl in c_lasts]
    prob = [(ci, cs) for ci in range(nck) for cs in heads]
    row2 = lax.broadcasted_iota(jnp.int32, (c, 2 * c), 0)
    col2 = lax.broadcasted_iota(jnp.int32, (c, 2 * c), 1) % c
    bks = [jnp.concatenate([b_ts[ci][:, cs], k_ts[ci][:, cs]], axis=0) for ci, cs in prob]
    a_abks = [jnp.where(row2 > col2, _dot_nt(a_ts[ci][:, cs], bk), 0.0) for bk, (ci, cs) in zip(bks, prob)]
    a_rbks = [_bf(jnp.where(row2 >= col2, _dot_nt(r_ts[ci][:, cs], bk), 0.0)) for bk, (ci, cs) in zip(bks, prob)]
    tmats = [_bf(t) for t in _unit_lower_inverses([m[:, :c] for m in a_abks], eye)]
    zero_c = jnp.zeros((c, hs), BF16)
    avs = [_bf(_dot(_bf(abk), jnp.concatenate([zero_c, vcs[ci][:, cs]], axis=0)))
           for abk, (ci, cs) in zip(a_abks, prob)]
    pmats = [_bf(_dot(t, a_ts[ci][:, cs])) for t, (ci, cs) in zip(tmats, prob)]
    qmats = [_dot(t, av) for t, av in zip(tmats, avs)]
    mmats = [_bf(_dot_tn(pm, b_hs[ci][:, cs])) for pm, (ci, cs) in zip(pmats, prob)]
    nmats = [_dot_tn(jnp.concatenate([_bf(qm), vcs[ci][:, cs]], axis=0),
                     jnp.concatenate([b_hs[ci][:, cs], k_hs[ci][:, cs]], axis=0))
             for qm, (ci, cs) in zip(qmats, prob)]

    states = [st_ref[hd] for hd in range(hps)]
    sb_hist = []
    for ci in range(nck):
        sbs = [_bf(s) for s in states]
        sb_hist.append(sbs)
        states = [states[hd] * e_lasts[ci][:, heads[hd]] + _dot(sbs[hd], mmats[hps * ci + hd]) + nmats[hps * ci + hd]
                  for hd in range(hps)]
    for hd in range(hps):
        st_ref[hd] = states[hd]
    ys = []
    for ci in range(nck):
        sbs = sb_hist[ci]
        ubs = [_bf(_dot_nt(pmats[hps * ci + hd], sbs[hd]) + qmats[hps * ci + hd]) for hd in range(hps)]
        ys.append([_dot_nt(r_ts[ci][:, heads[hd]], sbs[hd])
                   + _dot(a_rbks[hps * ci + hd], jnp.concatenate([ubs[hd], vcs[ci][:, heads[hd]]], axis=0))
                   for hd in range(hps)])
    for ci in range(nck):
        outs = []
        for y in ys[ci]:
            ym = jnp.mean(y, axis=-1, keepdims=True)
            yd = y - ym
            yv = jnp.mean(yd * yd, axis=-1, keepdims=True)
            outs.append(yd * lax.rsqrt(yv + RWKV_LNX_EPS))
        yn = jnp.concatenate(outs, axis=-1) * lnw + lnb
        o_ref[0, rows[ci], :] = ((yn + bonus[rows[ci]]) * g[rows[ci]]).astype(o_ref.dtype)


def _rwkv7(hr, hl, w2p, a2p, g2p, w0, a0, k_k, k_a, r_k, lnx_w, lnx_b, ts=256, hps=16):
    b, s, _ = hr.shape
    ts = min(ts, s)
    pw = hps * RWKV_HEAD
    npair = HALF // pw
    lw = sum(RWKV_LORA_PAD)
    col = lambda grp: (lambda bi, hp, i: (bi, i, grp * npair + hp))
    vec = lambda grp: pl.BlockSpec((1, pw), lambda bi, hp, i: (0, grp * npair + hp))
    row1 = lambda t: t.astype(F32).reshape(1, -1)
    return pl.pallas_call(
        functools.partial(_rwkv_kernel, ts=ts, hps=hps),
        grid=(b, npair, s // ts),
        in_specs=[pl.BlockSpec((1, ts, pw), col(0)), pl.BlockSpec((1, ts, pw), col(1)),
                  pl.BlockSpec((1, ts, pw), col(2)),
                  pl.BlockSpec((1, ts, lw), lambda bi, hp, i: (bi, i, 0)),
                  pl.BlockSpec((RWKV_LORA_PAD[0], pw), lambda bi, hp, i: (0, hp)),
                  pl.BlockSpec((RWKV_LORA_PAD[1], pw), lambda bi, hp, i: (0, hp)),
                  pl.BlockSpec((RWKV_LORA_PAD[2], pw), lambda bi, hp, i: (0, hp)),
                  vec(0), vec(0), vec(0), vec(0), vec(0), vec(0), vec(0)],
        out_specs=pl.BlockSpec((1, ts, pw), lambda bi, hp, i: (bi, i, hp)),
        out_shape=jax.ShapeDtypeStruct((b, s, HALF), BF16),
        scratch_shapes=[pltpu.VMEM((hps, RWKV_HEAD, RWKV_HEAD), F32)],
        compiler_params=_cparams("arbitrary", "arbitrary", "arbitrary"),
        name="rwkv7",
    )(hr, hr, hr, hl, w2p, a2p, g2p,
      row1(w0), row1(a0), row1(k_k), row1(k_a), row1(r_k), row1(lnx_w), row1(lnx_b))


def _fox_cumsum_kernel(f_ref, bf_ref, o_ref):
    x = f_ref[0] + bf_ref[...]
    ls = jnp.minimum(x, 0.0) - jnp.log(1.0 + jnp.exp(-jnp.abs(x)))
    nh, nr, nl = ls.shape
    li = lax.broadcasted_iota(jnp.int32, (nl, nl), 0)
    lj = lax.broadcasted_iota(jnp.int32, (nl, nl), 1)
    upper = (li <= lj).astype(BF16)
    ri = lax.broadcasted_iota(jnp.int32, (nr, nr), 0)
    rj = lax.broadcasted_iota(jnp.int32, (nr, nr), 1)
    below = (ri > rj).astype(BF16)
    for h in range(nh):
        within = _dot01_right(ls[h], upper)
        tot = jnp.broadcast_to(within[:, nl - 1:nl], (nr, nl))
        o_ref[0, h] = within + _dot01(below, tot)


def _dot01_right(x, m01):
    x1, x2, x3 = _split3(x)
    return _dot(x1, m01) + _dot(x2, m01) + _dot(x3, m01)


def _fox_cumsum(f_t, b_f):
    b, nh, s = f_t.shape
    nl = 128
    nr = s // nl
    out = pl.pallas_call(
        _fox_cumsum_kernel,
        grid=(b,),
        in_specs=[pl.BlockSpec((1, nh, nr, nl), lambda bi: (bi, 0, 0, 0)),
                  pl.BlockSpec((nh, 1, 1), lambda bi: (0, 0, 0))],
        out_specs=pl.BlockSpec((1, nh, nr, nl), lambda bi: (bi, 0, 0, 0)),
        out_shape=jax.ShapeDtypeStruct((b, nh, nr, nl), F32),
        compiler_params=_cparams("arbitrary"),
        name="fox_cumsum",
    )(f_t.reshape(b, nh, nr, nl), b_f.astype(F32).reshape(nh, 1, 1))
    return out.reshape(b, nh, 1, s)


def _fox_kernel(qi_ref, ki_ref, q_ref, k_ref, v_ref, cq_ref, ck_ref, o_ref, qs_ref, m_ref, acc_ref, *, tq, tk, tr):
    p = pl.program_id(2)
    qi = qi_ref[p]
    ki = ki_ref[p]
    last_k = ((qi + 1) * tq - 1) // tk
    log2e = 1.0 / math.log(2.0)

    @pl.when(ki == 0)
    def _():
        qs_ref[...] = (q_ref[0].astype(F32) * (FOX_HD ** -0.5 * log2e)).astype(BF16)
        m_ref[...] = jnp.full_like(m_ref, -jnp.inf)
        acc_ref[...] = jnp.zeros_like(acc_ref)

    def step(masked):
        bias = (cq_ref[0, 0, :, 0:1] - ck_ref[0, 0]) * log2e
        v_aug = jnp.concatenate([v_ref[0], jnp.ones((tk, FOX_HD), BF16)], axis=1)
        kb = k_ref[0]
        for r0 in range(0, tq, tr):
            rows = slice(r0, r0 + tr)
            s = _dot_nt(qs_ref[rows, :], kb) + bias
            if masked:
                qpos = qi * tq + r0 + lax.broadcasted_iota(jnp.int32, (tr, tk), 0)
                kpos = ki * tk + lax.broadcasted_iota(jnp.int32, (tr, tk), 1)
                s = jnp.where(kpos <= qpos, s, -jnp.inf)
            m_old = m_ref[rows, :]
            m_new = jnp.maximum(m_old, jnp.max(s, axis=-1, keepdims=True))
            corr = jnp.exp2(m_old - m_new)
            pexp = _bf(jnp.exp2(s - m_new))
            acc_ref[rows, :] = corr * acc_ref[rows, :] + _dot(pexp, v_aug)
            m_ref[rows, :] = m_new

    needs_mask = (ki + 1) * tk - 1 > qi * tq

    @pl.when(needs_mask)
    def _():
        step(True)

    @pl.when(jnp.logical_not(needs_mask))
    def _():
        step(False)

    @pl.when(ki == last_k)
    def _():
        acc = acc_ref[...]
        o_ref[0] = (acc[:, :FOX_HD] / acc[:, FOX_HD:FOX_HD + 1]).astype(o_ref.dtype)


def _fox_attention(hf, c, tq=2048, tk=2048, tr=128):
    b, s, _ = hf.shape
    tq = min(tq, s)
    tk = min(tk, s)
    nh = FOX_HEADS
    pairs = [(qi, ki) for qi in range(s // tq) for ki in range(((qi + 1) * tq - 1) // tk + 1)]
    qi_arr = jnp.asarray(np.array([p[0] for p in pairs], np.int32))
    ki_arr = jnp.asarray(np.array([p[1] for p in pairs], np.int32))
    grid_spec = pltpu.PrefetchScalarGridSpec(
        num_scalar_prefetch=2,
        grid=(b, nh, len(pairs)),
        in_specs=[pl.BlockSpec((1, tq, FOX_HD), lambda bi, h, p, qa, ka: (bi, qa[p], h)),
                  pl.BlockSpec((1, tk, FOX_HD), lambda bi, h, p, qa, ka: (bi, ka[p], nh + h)),
                  pl.BlockSpec((1, tk, FOX_HD), lambda bi, h, p, qa, ka: (bi, ka[p], 2 * nh + h)),
                  pl.BlockSpec((1, 1, 1, tq), lambda bi, h, p, qa, ka: (bi, h, 0, qa[p])),
                  pl.BlockSpec((1, 1, 1, tk), lambda bi, h, p, qa, ka: (bi, h, 0, ka[p]))],
        out_specs=pl.BlockSpec((1, tq, FOX_HD), lambda bi, h, p, qa, ka: (bi, qa[p], h)),
        scratch_shapes=[pltpu.VMEM((tq, FOX_HD), BF16), pltpu.VMEM((tq, 1), F32),
                        pltpu.VMEM((tq, 2 * FOX_HD), F32)])
    return pl.pallas_call(
        functools.partial(_fox_kernel, tq=tq, tk=tk, tr=min(tr, tq)),
        grid_spec=grid_spec,
        out_shape=jax.ShapeDtypeStruct((b, s, HALF), BF16),
        compiler_params=_cparams("arbitrary", "arbitrary", "arbitrary"),
        name="fox_attention",
    )(qi_arr, ki_arr, hf, hf, hf, c, c)


def _pad_cols(w, width):
    return jnp.pad(w, ((0, 0), (0, width - w.shape[1])))


def _pad_rows(w, height):
    return jnp.pad(w, ((0, height - w.shape[0]), (0, 0)))


def _even_mixer(x32, xb, w_in, layer, pool_w, pool_scale, conv_w, a_log, dt_bias, norm_w, w_out, ln_w, ln_b):
    b, s, d = x32.shape
    t = b * s
    n_main = 5 * HALF
    w_gate = _bf(_pad_cols(w_in[layer, :, n_main:], 128))
    if xb is None:
        u_pool, xb2 = _mm_cast(x32.reshape(t, d), w_in, F32, col0=0, n=HALF, layer=layer)
        xb = xb2.reshape(b, s, d)
    else:
        xb2 = xb.reshape(t, d)
        u_pool = _mm(xb2, w_in, F32, col0=0, n=HALF, layer=layer)
    u_pool = u_pool.reshape(b, s, HALF)
    qkv = _mm_taps(xb, w_in, conv_w, F32, act="silu", col0=HALF, n=3 * HALF, layer=layer)
    z = _mm(xb2, w_in, BF16, col0=4 * HALF, n=HALF, layer=layer).reshape(b, s, HALF)
    logits = _mm(xb2, w_gate, F32, tn=128).reshape(b, s, 128)
    y_a = _pool_mixer(u_pool, _bf(pool_w), pool_scale)
    y_b = _gated_deltanet(qkv, z, logits, a_log, dt_bias, norm_w)
    return _mm_res_ln([y_a.reshape(t, HALF), y_b.reshape(t, HALF)], w_out, x32.reshape(t, d), ln_w, ln_b,
                      layer=layer)


def _odd_mixer(x32, xb, w_in_all, layer, mu, w0, w2, a0, a2, g2, k_k, k_a, r_k, lnx_w, lnx_b, b_f, w_out, ln_w, ln_b):
    b, s, d = x32.shape
    t = b * s
    l0, l1, l2 = RWKV_LORA
    p0, p1, p2 = RWKV_LORA_PAD
    o_l = 3 * HALF
    o_f = o_l + l0 + l1 + l2
    w_tail = w_in_all[layer, :, o_l:]
    w_l = _bf(jnp.concatenate([_pad_cols(w_tail[:, :l0], p0),
                               _pad_cols(w_tail[:, l0:l0 + l1], p1),
                               _pad_cols(w_tail[:, l0 + l1:o_f - o_l], p2)], axis=1))
    w_fox = _bf(w_tail[:, o_f - o_l:o_f - o_l + 3 * HALF])
    w_fl = _bf(_pad_cols(w_tail[:, o_f - o_l + 3 * HALF:], 128))
    mu_l = jnp.concatenate([jnp.pad(mu[o_l:o_l + l0], (0, p0 - l0)),
                            jnp.pad(mu[o_l + l0:o_l + l0 + l1], (0, p1 - l1)),
                            jnp.pad(mu[o_l + l0 + l1:o_f], (0, p2 - l2))])
    xb2 = xb.reshape(t, d)
    lerp = lambda m: jnp.stack([m, 1.0 - m])
    hr = _mm_taps(xb, w_in_all, lerp(mu[:o_l]), F32, col0=0, n=o_l, layer=layer)
    hl = _mm_taps(xb, w_l, lerp(mu_l), BF16, act="lora", tn=p0 + p1 + p2)
    hf = _mm(xb2, w_fox, BF16).reshape(b, s, 3 * HALF)
    fl = _mm(xb2, w_fl, F32, tn=128).reshape(b, s, 128)
    y_c = _rwkv7(hr, hl, _bf(_pad_rows(w2, p0)), _bf(_pad_rows(a2, p1)), _bf(_pad_rows(g2, p2)),
                 w0, a0, k_k, k_a, r_k, lnx_w, lnx_b)
    c = _fox_cumsum(jnp.transpose(fl[:, :, :FOX_HEADS], (0, 2, 1)), b_f)
    y_d = _fox_attention(hf, c)
    return _mm_res_ln([y_c.reshape(t, HALF), y_d.reshape(t, HALF)], w_out, x32.reshape(t, d), ln_w, ln_b,
                      layer=layer)


def _cross_attention(x32, xb, mem_b, w_q, w_kv, layer, w_o, ln_w, ln_b):
    t, d = x32.shape
    b, mlen, _ = mem_b.shape
    s = t // b
    kv = _mm(mem_b.reshape(b * mlen, d), w_kv, BF16, layer=layer).reshape(b, mlen, 2 * d)
    g = _fold_qk(w_q, layer, kv)
    vo = _fold_vo(kv, w_o, layer)
    return _xattn_res_ln(xb, g, vo, x32, ln_w, ln_b)


def _conv_glu_ffn(x32, xb, b, w_up, layer, conv_w, w_down, ln_w, ln_b):
    t, d = x32.shape
    hmid = _ffn_up(xb.reshape(b, t // b, d), w_up, conv_w, layer=layer)
    return _mm_res_ln([hmid.reshape(t, D_FF)], w_down, x32, ln_w, ln_b, layer=layer, tm=256)


def kernel(x, mem, ev_w_in, pool_w, pool_scale, gdn_conv_w, gdn_a_log, gdn_dt_bias, gdn_norm_w, ev_w_out,
           od_w_in, rwkv_mu, rwkv_w0, rwkv_w2, rwkv_a0, rwkv_a2, rwkv_g2, rwkv_k_k, rwkv_k_a, rwkv_r_k,
           rwkv_lnx_w, rwkv_lnx_b, fox_b_f, od_w_out,
           ln_mix_w, ln_mix_b, xa_w_q, xa_w_kv, xa_w_o, ln_xa_w, ln_xa_b,
           ffn_w_up, ffn_conv_w, ffn_w_down, ln_ffn_w, ln_ffn_b):
    b, s, d = x.shape
    depth = ln_mix_w.shape[0]
    mem_b = _bf(mem)
    ev_wo, od_wo, xa_wo, ffn_wd = _bf(ev_w_out), _bf(od_w_out), _bf(xa_w_o), _bf(ffn_w_down)
    x32 = x
    xb = None
    for i in range(depth):
        j = i // 2
        x3 = x32.reshape(b, s, d)
        xb3 = None if xb is None else xb.reshape(b, s, d)
        if i % 2 == 0:
            x32, xb = _even_mixer(x3, xb3, ev_w_in, j, pool_w[j], pool_scale[j], gdn_conv_w[j], gdn_a_log[j],
                                  gdn_dt_bias[j], gdn_norm_w[j], ev_wo, ln_mix_w[i], ln_mix_b[i])
        else:
            x32, xb = _odd_mixer(x3, xb3, od_w_in, j, rwkv_mu[j], rwkv_w0[j], rwkv_w2[j], rwkv_a0[j], rwkv_a2[j],
                                 rwkv_g2[j], rwkv_k_k[j], rwkv_k_a[j], rwkv_r_k[j].reshape(-1), rwkv_lnx_w[j],
                                 rwkv_lnx_b[j], fox_b_f[j], od_wo, ln_mix_w[i], ln_mix_b[i])
        x32, xb = _cross_attention(x32, xb, mem_b, xa_w_q, xa_w_kv, i, xa_wo, ln_xa_w[i], ln_xa_b[i])
        x32, xb = _conv_glu_ffn(x32, xb, b, ffn_w_up, i, ffn_conv_w[i], ffn_wd, ln_ffn_w[i], ln_ffn_b[i])
    return x32.reshape(b, s, d)
```

```python
import functools
import math

import numpy as np
import jax
import jax.numpy as jnp
from jax import lax
from jax.experimental import pallas as pl
from jax.experimental.pallas import tpu as pltpu

F32 = jnp.float32
BF16 = jnp.bfloat16

D_MODEL = 2048
HALF = D_MODEL // 2
POOL_WINDOWS = (2, 4, 8, 16)
POOL_GDIM = HALF // len(POOL_WINDOWS)
GDN_HEADS = 8
GDN_DK = HALF // GDN_HEADS
GDN_CONV = 4
NORM_EPS = 1e-6
RWKV_HEAD = 64
RWKV_LNX_EPS = 64e-5
RWKV_LORA = (64, 64, 160)
RWKV_LORA_PAD = (128, 128, 256)
FOX_HEADS = 8
FOX_HD = HALF // FOX_HEADS
XA_HEADS = 4
XA_HD = D_MODEL // XA_HEADS
D_FF = 5632
FFN_CONV = 3
DEPTH = 2
DEEPNORM_ALPHA = float((2 * DEPTH) ** 0.25)
LN_EPS = 1e-5

CHUNK = 64
HIST = 8
VMEM_LIMIT_BYTES = 56 * 1024 * 1024


def _cparams(*sem):
    return pltpu.CompilerParams(dimension_semantics=sem, vmem_limit_bytes=VMEM_LIMIT_BYTES)


def _dot(a, b):
    return jnp.dot(a, b, preferred_element_type=F32)


def _dot_nt(a, b):
    return lax.dot_general(a, b, (((1,), (1,)), ((), ())), preferred_element_type=F32)


def _dot_tn(a, b):
    return lax.dot_general(a, b, (((0,), (0,)), ((), ())), preferred_element_type=F32)


def _bf(x):
    return x.astype(BF16)


def _split3(x):
    x1 = x.astype(BF16)
    r1 = x - x1.astype(F32)
    x2 = r1.astype(BF16)
    x3 = (r1 - x2.astype(F32)).astype(BF16)
    return x1, x2, x3


def _dot01(m01, x):
    x1, x2, x3 = _split3(x)
    return _dot(m01, x1) + _dot(m01, x2) + _dot(m01, x3)


def _unit_lower_inverses(n_mats, eye, fillers=()):
    c = eye.shape[0]
    levels = int(math.log2(c))
    fillers = list(fillers)
    xs = [eye + n for n in n_mats]
    nbs = [_bf(n) for n in n_mats]
    ps = [_dot(nb, nb) for nb in nbs]
    for level in range(1, levels):
        pbs = [_bf(p) for p in ps]
        if level == levels - 1:
            prods = [_dot(_bf(x), pb) for x, pb in zip(xs, pbs)]
        else:
            prods = [_dot(jnp.concatenate([pb, _bf(x)], axis=0), pb) for x, pb in zip(xs, pbs)]
        if fillers:
            fillers.pop(0)()
        if level == levels - 1:
            xs = [x + pr for x, pr in zip(xs, prods)]
        else:
            xs = [x + pr[c:] for x, pr in zip(xs, prods)]
            ps = [pr[:c] for pr in prods]
    for filler in fillers:
        filler()
    return xs


def _sigmoid(x):
    return 1.0 / (1.0 + jnp.exp(-x))


def _softplus(x):
    return jnp.maximum(x, 0.0) + jnp.log(1.0 + jnp.exp(-jnp.abs(x)))


def _tri_masks(c):
    row = lax.broadcasted_iota(jnp.int32, (c, c), 0)
    col = lax.broadcasted_iota(jnp.int32, (c, c), 1)
    return row >= col, row > col


def _stage_rows(x_ref, sc_ref, ts, first):
    @pl.when(first)
    def _():
        sc_ref[0:HIST, :] = jnp.zeros((HIST, sc_ref.shape[1]), F32)
    sc_ref[HIST:HIST + ts, :] = x_ref[0].astype(F32)


def _carry_rows(sc_ref, ts):
    sc_ref[0:HIST, :] = sc_ref[ts:ts + HIST, :]


def _weight_tile(w_ref, wb_ref, first):
    if wb_ref is None:
        return w_ref
    @pl.when(first)
    def _():
        wb_ref[...] = w_ref[...].astype(BF16)
    return wb_ref


def _weight_cols(w, n, col0, tn):
    n = w.shape[-1] - col0 if n is None else n
    tn = min(tn, n)
    assert n % tn == 0 and col0 % tn == 0, (n, col0, tn)
    scratch = [] if w.dtype == BF16 else [pltpu.VMEM((w.shape[-2], tn), BF16)]
    return n, tn, col0 // tn, scratch


def _weight_spec(w, tn, layer, col_of):
    k = w.shape[-2]
    if w.ndim == 2:
        return pl.BlockSpec((k, tn), lambda *g: (0, col_of(*g)))
    return pl.BlockSpec((None, k, tn), lambda *g: (layer, 0, col_of(*g)))


def _mm_kernel(a_ref, w_ref, o_ref, wb_ref=None):
    w = _weight_tile(w_ref, wb_ref, pl.program_id(1) == 0)
    o_ref[...] = _dot(a_ref[...], w[...]).astype(o_ref.dtype)


def _mm_cast_kernel(a_ref, w_ref, o_ref, ab_ref, wb_ref=None):
    w = _weight_tile(w_ref, wb_ref, pl.program_id(0) == 0)
    ab = a_ref[...].astype(BF16)
    ab_ref[...] = ab
    o_ref[...] = _dot(ab, w[...]).astype(o_ref.dtype)


def _mm_cast(a, w, out_dtype, tm=512, col0=0, n=None, layer=0):
    m, k = a.shape
    n, tn, joff, scratch = _weight_cols(w, n, col0, w.shape[-1])
    tm = min(tm, m)
    return pl.pallas_call(
        _mm_cast_kernel,
        grid=(m // tm,),
        in_specs=[pl.BlockSpec((tm, k), lambda i: (i, 0)),
                  _weight_spec(w, tn, layer, lambda i: joff)],
        out_specs=[pl.BlockSpec((tm, tn), lambda i: (i, 0)), pl.BlockSpec((tm, k), lambda i: (i, 0))],
        out_shape=[jax.ShapeDtypeStruct((m, n), out_dtype), jax.ShapeDtypeStruct((m, k), BF16)],
        scratch_shapes=scratch,
        compiler_params=_cparams("arbitrary"),
        name="mm_cast",
    )(a, w)


def _mm(a, w, out_dtype, tm=1024, tn=1024, col0=0, n=None, layer=0):
    m, k = a.shape
    n, tn, joff, scratch = _weight_cols(w, n, col0, tn)
    tm = min(tm, m)
    return pl.pallas_call(
        _mm_kernel,
        grid=(n // tn, m // tm),
        in_specs=[pl.BlockSpec((tm, k), lambda j, i: (i, 0)),
                  _weight_spec(w, tn, layer, lambda j, i: joff + j)],
        out_specs=pl.BlockSpec((tm, tn), lambda j, i: (i, j)),
        out_shape=jax.ShapeDtypeStruct((m, n), out_dtype),
        scratch_shapes=scratch,
        compiler_params=_cparams("arbitrary", "arbitrary"),
        name="mm",
    )(a, w)


def _mm_res_ln_kernel(*refs, n_in, tk_steps, tr):
    a_refs = refs[:n_in]
    w_refs = refs[n_in:2 * n_in]
    res_ref, lnw_ref, lnb_ref, o32_ref, o16_ref = refs[2 * n_in:2 * n_in + 5]
    acc_ref = refs[2 * n_in + 5] if tk_steps > 1 else None
    kk = pl.program_id(1)
    tm = res_ref.shape[0]

    def partial_sum(rows):
        tot = None
        for a_ref, w_ref in zip(a_refs, w_refs):
            d = _dot(a_ref[rows, :], w_ref[...])
            tot = d if tot is None else tot + d
        return tot

    if tk_steps > 1:
        @pl.when(kk == 0)
        def _():
            acc_ref[...] = partial_sum(slice(None))

        @pl.when((kk > 0) & (kk < tk_steps - 1))
        def _():
            acc_ref[...] += partial_sum(slice(None))

    @pl.when(kk == tk_steps - 1)
    def _():
        for r0 in range(0, tm, tr):
            rows = slice(r0, r0 + tr)
            y = DEEPNORM_ALPHA * res_ref[rows, :] + partial_sum(rows)
            if tk_steps > 1:
                y = y + acc_ref[rows, :]
            mu = jnp.mean(y, axis=-1, keepdims=True)
            d = y - mu
            var = jnp.mean(d * d, axis=-1, keepdims=True)
            out = d * lax.rsqrt(var + LN_EPS) * lnw_ref[...] + lnb_ref[...]
            o32_ref[rows, :] = out
            o16_ref[rows, :] = out.astype(BF16)


def _mm_res_ln(a_list, w, res, lnw, lnb, layer=0, tm=512, tk=None, tr=128):
    m = res.shape[0]
    n = res.shape[1]
    k = a_list[0].shape[1]
    tm = min(tm, m)
    tk = k if tk is None else min(tk, k)
    tr = min(tr, tm)
    n_in = len(a_list)
    steps = k // tk
    w_mode = {"pipeline_mode": pl.Buffered(1)} if steps == 1 else {}
    w_spec = lambda part: pl.BlockSpec((None, tk, n), lambda i, kk: (layer, part * steps + kk, 0), **w_mode)
    in_specs = ([pl.BlockSpec((tm, tk), lambda i, kk: (i, kk)) for _ in a_list]
                + [w_spec(part) for part in range(n_in)]
                + [pl.BlockSpec((tm, n), lambda i, kk: (i, 0)),
                   pl.BlockSpec((1, n), lambda i, kk: (0, 0)),
                   pl.BlockSpec((1, n), lambda i, kk: (0, 0))])
    return pl.pallas_call(
        functools.partial(_mm_res_ln_kernel, n_in=n_in, tk_steps=steps, tr=tr),
        grid=(m // tm, steps),
        in_specs=in_specs,
        out_specs=[pl.BlockSpec((tm, n), lambda i, kk: (i, 0)),
                   pl.BlockSpec((tm, n), lambda i, kk: (i, 0))],
        out_shape=[jax.ShapeDtypeStruct((m, n), F32), jax.ShapeDtypeStruct((m, n), BF16)],
        scratch_shapes=[pltpu.VMEM((tm, n), F32)] if steps > 1 else [],
        compiler_params=_cparams("arbitrary", "arbitrary"),
        name="mm_res_ln",
    )(*a_list, *([w] * n_in), res, lnw.reshape(1, n), lnb.reshape(1, n))


def _fold_qk_kernel(wq_ref, k_ref, g_ref):
    g = _dot_nt(wq_ref[...].astype(BF16), k_ref[0]) * (XA_HD ** -0.5)
    g_ref[0] = g.astype(g_ref.dtype)


def _fold_qk(w_q, layer, kv):
    b, mlen, _ = kv.shape
    d = w_q.shape[-1]
    return pl.pallas_call(
        _fold_qk_kernel,
        grid=(b, XA_HEADS),
        in_specs=[pl.BlockSpec((None, d, XA_HD), lambda bi, h: (layer, 0, h)),
                  pl.BlockSpec((1, mlen, XA_HD), lambda bi, h: (bi, 0, h))],
        out_specs=pl.BlockSpec((1, d, mlen), lambda bi, h: (bi, 0, h)),
        out_shape=jax.ShapeDtypeStruct((b, d, XA_HEADS * mlen), BF16),
        compiler_params=_cparams("arbitrary", "arbitrary"),
        name="fold_qk",
    )(w_q, kv)


def _fold_vo_kernel(v_ref, wo_ref, o_ref):
    o_ref[0] = _dot(v_ref[0], wo_ref[...]).astype(o_ref.dtype)


def _fold_vo(kv, w_o, layer):
    b, mlen, _ = kv.shape
    d = w_o.shape[-1]
    return pl.pallas_call(
        _fold_vo_kernel,
        grid=(b, XA_HEADS),
        in_specs=[pl.BlockSpec((1, mlen, XA_HD), lambda bi, h: (bi, 0, XA_HEADS + h)),
                  pl.BlockSpec((None, XA_HD, d), lambda bi, h: (layer, h, 0))],
        out_specs=pl.BlockSpec((1, mlen, d), lambda bi, h: (bi, h, 0)),
        out_shape=jax.ShapeDtypeStruct((b, XA_HEADS * mlen, d), BF16),
        compiler_params=_cparams("arbitrary", "arbitrary"),
        name="fold_vo",
    )(kv, w_o)


def _xattn_res_ln_kernel(x_ref, g_ref, vo_ref, res_ref, lnw_ref, lnb_ref, o32_ref, o16_ref, *, tr, mlen):
    tm = x_ref.shape[0]
    for r0 in range(0, tm, tr):
        rows = slice(r0, r0 + tr)
        s = _dot(x_ref[rows, :], g_ref[...])
        probs = []
        for h in range(XA_HEADS):
            sh = s[:, h * mlen:(h + 1) * mlen]
            e = jnp.exp(sh - jnp.max(sh, axis=-1, keepdims=True))
            probs.append(_bf(e / jnp.sum(e, axis=-1, keepdims=True)))
        y = DEEPNORM_ALPHA * res_ref[rows, :] + _dot(jnp.concatenate(probs, axis=1), vo_ref[...])
        mu = jnp.mean(y, axis=-1, keepdims=True)
        d = y - mu
        var = jnp.mean(d * d, axis=-1, keepdims=True)
        out = d * lax.rsqrt(var + LN_EPS) * lnw_ref[...] + lnb_ref[...]
        o32_ref[rows, :] = out
        o16_ref[rows, :] = out.astype(BF16)


def _xattn_res_ln(xb, g, vo, res, lnw, lnb, tm=512, tr=256):
    m, d = res.shape
    b, _, hm = g.shape
    tm = min(tm, m // b)
    tiles_per_batch = m // b // tm
    return pl.pallas_call(
        functools.partial(_xattn_res_ln_kernel, tr=min(tr, tm), mlen=hm // XA_HEADS),
        grid=(m // tm,),
        in_specs=[pl.BlockSpec((tm, d), lambda i: (i, 0)),
                  pl.BlockSpec((None, d, hm), lambda i: (i // tiles_per_batch, 0, 0)),
                  pl.BlockSpec((None, hm, d), lambda i: (i // tiles_per_batch, 0, 0)),
                  pl.BlockSpec((tm, d), lambda i: (i, 0)),
                  pl.BlockSpec((1, d), lambda i: (0, 0)),
                  pl.BlockSpec((1, d), lambda i: (0, 0))],
        out_specs=[pl.BlockSpec((tm, d), lambda i: (i, 0)), pl.BlockSpec((tm, d), lambda i: (i, 0))],
        out_shape=[jax.ShapeDtypeStruct((m, d), F32), jax.ShapeDtypeStruct((m, d), BF16)],
        compiler_params=_cparams("arbitrary"),
        name="xattn_res_ln",
    )(xb, g, vo, res, lnw.reshape(1, d), lnb.reshape(1, d))


def _mm_taps_kernel(x_ref, w_ref, t_ref, o_ref, sc_ref, wb_ref=None, *, ts, ntap, act):
    first = pl.program_id(2) == 0

    @pl.when(first)
    def _():
        sc_ref[0:HIST, :] = jnp.zeros((HIST, sc_ref.shape[1]), F32)

    w = _weight_tile(w_ref, wb_ref, first & (pl.program_id(1) == 0))
    sc_ref[HIST:HIST + ts, :] = _dot(x_ref[0], w[...])
    y = t_ref[ntap - 1:ntap, :] * sc_ref[HIST:HIST + ts, :]
    for back in range(1, ntap):
        tap = ntap - 1 - back
        y = y + t_ref[tap:tap + 1, :] * sc_ref[HIST - back:HIST - back + ts, :]
    if act == "silu":
        y = y * _sigmoid(y)
    elif act == "lora":
        p0, p1, _ = RWKV_LORA_PAD
        y = jnp.concatenate([jnp.tanh(y[:, :p0]), y[:, p0:p0 + p1], _sigmoid(y[:, p0 + p1:])], axis=1)
    o_ref[0] = y.astype(o_ref.dtype)
    _carry_rows(sc_ref, ts)


def _mm_taps(xb, w, taps, out_dtype, act=None, ts=1024, tn=1024, col0=0, n=None, layer=0):
    b, s, d = xb.shape
    n, tn, joff, cast_scratch = _weight_cols(w, n, col0, tn)
    ts = min(ts, s)
    ntap = taps.shape[0]
    return pl.pallas_call(
        functools.partial(_mm_taps_kernel, ts=ts, ntap=ntap, act=act),
        grid=(n // tn, b, s // ts),
        in_specs=[pl.BlockSpec((1, ts, d), lambda j, bi, i: (bi, i, 0)),
                  _weight_spec(w, tn, layer, lambda j, bi, i: joff + j),
                  pl.BlockSpec((ntap, tn), lambda j, bi, i: (0, j))],
        out_specs=pl.BlockSpec((1, ts, tn), lambda j, bi, i: (bi, i, j)),
        out_shape=jax.ShapeDtypeStruct((b, s, n), out_dtype),
        scratch_shapes=[pltpu.VMEM((ts + HIST, tn), F32)] + cast_scratch,
        compiler_params=_cparams("arbitrary", "arbitrary", "arbitrary"),
        name="mm_taps",
    )(xb, w, taps.astype(F32))


def _ffn_up_kernel(x_ref, wg_ref, wu_ref, cg_ref, cu_ref, o_ref, sg_ref, su_ref, wgb_ref=None, wub_ref=None,
                   *, ts, tr):
    first = pl.program_id(2) == 0

    @pl.when(first)
    def _():
        sg_ref[0:HIST, :] = jnp.zeros((HIST, sg_ref.shape[1]), F32)
        su_ref[0:HIST, :] = jnp.zeros((HIST, su_ref.shape[1]), F32)

    new_tile = first & (pl.program_id(1) == 0)
    wg_ref = _weight_tile(wg_ref, wgb_ref, new_tile)
    wu_ref = _weight_tile(wu_ref, wub_ref, new_tile)

    def conv(sc_ref, c_ref, r0):
        y = c_ref[FFN_CONV - 1:FFN_CONV, :] * sc_ref[HIST + r0:HIST + r0 + tr, :]
        for back in range(1, FFN_CONV):
            tap = FFN_CONV - 1 - back
            y = y + c_ref[tap:tap + 1, :] * sc_ref[HIST + r0 - back:HIST + r0 - back + tr, :]
        return y

    for r0 in range(0, ts, tr):
        x = x_ref[0, r0:r0 + tr, :]
        sg_ref[HIST + r0:HIST + r0 + tr, :] = _dot(x, wg_ref[...])
        g = conv(sg_ref, cg_ref, r0)
        g = g * _sigmoid(g)
        su_ref[HIST + r0:HIST + r0 + tr, :] = _dot(x, wu_ref[...])
        u = conv(su_ref, cu_ref, r0)
        o_ref[0, r0:r0 + tr, :] = (g * u).astype(o_ref.dtype)
    _carry_rows(sg_ref, ts)
    _carry_rows(su_ref, ts)


def _ffn_up(xb, w_up, conv_w, ts=1024, tn=512, tr=1024, layer=0):
    b, s, d = xb.shape
    ts = min(ts, s)
    nj = D_FF // tn
    cast_scratch = [] if w_up.dtype == BF16 else [pltpu.VMEM((d, tn), BF16), pltpu.VMEM((d, tn), BF16)]
    return pl.pallas_call(
        functools.partial(_ffn_up_kernel, ts=ts, tr=min(tr, ts)),
        grid=(nj, b, s // ts),
        in_specs=[pl.BlockSpec((1, ts, d), lambda j, bi, i: (bi, i, 0)),
                  _weight_spec(w_up, tn, layer, lambda j, bi, i: j),
                  _weight_spec(w_up, tn, layer, lambda j, bi, i: nj + j),
                  pl.BlockSpec((FFN_CONV, tn), lambda j, bi, i: (0, j)),
                  pl.BlockSpec((FFN_CONV, tn), lambda j, bi, i: (0, nj + j))],
        out_specs=pl.BlockSpec((1, ts, tn), lambda j, bi, i: (bi, i, j)),
        out_shape=jax.ShapeDtypeStruct((b, s, D_FF), BF16),
        scratch_shapes=[pltpu.VMEM((ts + HIST, tn), F32), pltpu.VMEM((ts + HIST, tn), F32)] + cast_scratch,
        compiler_params=_cparams("arbitrary", "arbitrary", "arbitrary"),
        name="ffn_up",
    )(xb, w_up, w_up, conv_w, conv_w)


def _pool_kernel(u_ref, w_ref, sc_ref, o_ref, st_ref, *, ts):
    i = pl.program_id(1)

    @pl.when(i == 0)
    def _():
        st_ref[0:2 * HIST, :] = jnp.zeros((2 * HIST, st_ref.shape[1]), F32)

    st_ref[2 * HIST:2 * HIST + ts, :] = u_ref[0].astype(F32)
    pos = (i * ts + 1 + lax.broadcasted_iota(jnp.int32, (ts, 1), 0)).astype(F32)
    base = 2 * HIST
    for g, win in enumerate(POOL_WINDOWS):
        cols = slice(g * POOL_GDIM, (g + 1) * POOL_GDIM)
        cur = st_ref[base:base + ts, cols]
        acc = cur
        for back in range(1, win):
            acc = acc + st_ref[base - back:base - back + ts, cols]
        mean = acc / jnp.minimum(pos, float(win))
        y = _dot(_bf(mean - cur), w_ref[g])
        o_ref[0, :, cols] = (y * sc_ref[:, cols]).astype(o_ref.dtype)
    st_ref[0:2 * HIST, :] = st_ref[ts:ts + 2 * HIST, :]


def _pool_mixer(h_main, pool_w, pool_scale, ts=512):
    b, s, _ = h_main.shape
    ts = min(ts, s)
    return pl.pallas_call(
        functools.partial(_pool_kernel, ts=ts),
        grid=(b, s // ts),
        in_specs=[pl.BlockSpec((1, ts, HALF), lambda bi, i: (bi, i, 0)),
                  pl.BlockSpec((len(POOL_WINDOWS), POOL_GDIM, POOL_GDIM), lambda bi, i: (0, 0, 0)),
                  pl.BlockSpec((1, HALF), lambda bi, i: (0, 0))],
        out_specs=pl.BlockSpec((1, ts, HALF), lambda bi, i: (bi, i, 0)),
        out_shape=jax.ShapeDtypeStruct((b, s, HALF), BF16),
        scratch_shapes=[pltpu.VMEM((ts + 2 * HIST, HALF), F32)],
        compiler_params=_cparams("arbitrary", "arbitrary"),
        name="pool_mixer",
    )(h_main, pool_w, pool_scale.reshape(1, HALF))


def _gdn_kernel(q_ref, k_ref, v_ref, z_ref, g_ref, alog_ref, dtb_ref, nw_ref, o_ref, st_ref, *, ts, hps):
    c = CHUNK
    h0 = pl.program_id(1)

    @pl.when(pl.program_id(2) == 0)
    def _():
        st_ref[...] = jnp.zeros_like(st_ref)

    q_all = q_ref[0]
    k_all = k_ref[0]
    v_all = v_ref[0]
    z_all = z_ref[0].astype(F32)

    logits = g_ref[0]
    lane = lax.broadcasted_iota(jnp.int32, logits.shape, 1)
    g_all = -jnp.exp(alog_ref[...]) * _softplus(logits + dtb_ref[...])
    sig_all = _sigmoid(logits)
    trow = lax.broadcasted_iota(jnp.int32, (ts, ts), 0)
    tcol = lax.broadcasted_iota(jnp.int32, (ts, ts), 1)
    tril_chunks = ((trow >= tcol) & (trow // c == tcol // c)).astype(BF16)
    gc_all = _dot01(tril_chunks, g_all)

    causal, strict = _tri_masks(c)
    eye = jnp.where(causal & (~strict), 1.0, 0.0).astype(F32)
    lane_c = lax.broadcasted_iota(jnp.int32, (c, GDN_DK), 1)
    nw = nw_ref[...]
    chunk_rows = [slice(ci * c, (ci + 1) * c) for ci in range(ts // c)]
    nck = len(chunk_rows)

    q, k, v, betas, g_cols = [], [], [], [], []
    for hd in range(hps):
        hl = slice(hd * GDN_DK, (hd + 1) * GDN_DK)
        qh, kh = q_all[:, hl], k_all[:, hl]
        qh = qh * lax.rsqrt(jnp.sum(qh * qh, axis=-1, keepdims=True) + NORM_EPS) * (GDN_DK ** -0.5)
        kh = kh * lax.rsqrt(jnp.sum(kh * kh, axis=-1, keepdims=True) + NORM_EPS)
        head = h0 * hps + hd
        g_col = jnp.sum(jnp.where(lane == head, gc_all, 0.0), axis=1, keepdims=True)
        b_col = jnp.sum(jnp.where(lane == head + GDN_HEADS, sig_all, 0.0), axis=1, keepdims=True)
        for r in chunk_rows:
            q.append(qh[r])
            k.append(kh[r])
            v.append(v_all[r, hl])
            betas.append(jnp.broadcast_to(b_col[r], (c, GDN_DK)))
            g_cols.append(jnp.broadcast_to(g_col[r], (c, GDN_DK)))
    nprob = len(q)
    rows = list(range(nprob))
    gcs = g_cols
    diffs = []
    for gc in gcs:
        g1, g2, g3 = (p.astype(F32) for p in _split3(gc))
        lhs = jnp.where(lane_c == 0, g1, jnp.where(lane_c == 1, g2, jnp.where(lane_c == 2, g3,
              jnp.where(lane_c < 6, 1.0, 0.0))))
        rhs = jnp.where(lane_c < 3, 1.0, jnp.where(lane_c == 3, -g1, jnp.where(lane_c == 4, -g2,
              jnp.where(lane_c == 5, -g3, 0.0))))
        diffs.append(_dot_nt(_bf(lhs), _bf(rhs)))
    decays = [jnp.where(causal, jnp.exp(jnp.where(causal, d, 0.0)), 0.0) for d in diffs]
    kbs = [k[r] * bt for r, bt in zip(rows, betas)]
    kcbs = [_bf(k[r]) for r in rows]
    lmats = [jnp.where(strict, _dot_nt(_bf(kb), kcb) * dc, 0.0) for kb, kcb, dc in zip(kbs, kcbs, decays)]
    intras = [_bf(jnp.where(causal, _dot_nt(_bf(q[r]), kcb) * dc, 0.0)) for r, kcb, dc in zip(rows, kcbs, decays)]
    tmats = [_bf(t) for t in _unit_lower_inverses([-l for l in lmats], eye)]
    egs = [jnp.exp(gc) for gc in gcs]
    g_lasts = [gc[c - 1:c, :] for gc in gcs]
    uws = [_dot(t, jnp.concatenate([_bf(v[r] * bt), _bf(kb * eg)], axis=1))
           for t, r, bt, kb, eg in zip(tmats, rows, betas, kbs, egs)]
    us = [uw[:, :GDN_DK] for uw in uws]
    ws = [_bf(uw[:, GDN_DK:]) for uw in uws]
    q_decs = [_bf(q[r] * eg) for r, eg in zip(rows, egs)]
    k_decs = [_bf(k[r] * jnp.exp(gl - gc)) for r, gl, gc in zip(rows, g_lasts, gcs)]
    e_lasts = [jnp.exp(gl) for gl in g_lasts]
    mns = [_dot_tn(kd, jnp.concatenate([w, _bf(u)], axis=1)) for kd, w, u in zip(k_decs, ws, us)]
    mmats = [_bf(mn[:, :GDN_DK]) for mn in mns]
    nmats = [mn[:, GDN_DK:] for mn in mns]

    states = [st_ref[hd] for hd in range(hps)]
    sb_hist = [[None] * nck for _ in range(hps)]
    for ci in range(nck):
        for hd in range(hps):
            p = hd * nck + ci
            sb = _bf(states[hd])
            sb_hist[hd][ci] = sb
            states[hd] = states[hd] * e_lasts[p] - _dot(mmats[p], sb) + nmats[p]
    for hd in range(hps):
        st_ref[hd] = states[hd]
    for hd in range(hps):
        hl = slice(hd * GDN_DK, (hd + 1) * GDN_DK)
        for ci in range(nck):
            p = hd * nck + ci
            sb = sb_hist[hd][ci]
            v_new = _bf(us[p] - _dot(ws[p], sb))
            o = _dot(q_decs[p], sb) + _dot(intras[p], v_new)
            o = o * lax.rsqrt(jnp.mean(o * o, axis=-1, keepdims=True) + NORM_EPS) * nw
            zc = z_all[chunk_rows[ci], hl]
            o_ref[0, chunk_rows[ci], hl] = (o * (zc * _sigmoid(zc))).astype(o_ref.dtype)


def _gated_deltanet(qkv, z, logits, a_log, dt_bias, norm_w, ts=256, hps=8):
    b, s, _ = qkv.shape
    ts = min(ts, s)
    hd = GDN_DK
    gw = hps * hd
    ng = HALF // gw
    pad = lambda t: jnp.pad(t.astype(F32), (0, hd - t.shape[0])).reshape(1, hd)
    col = lambda grp: (lambda bi, h, i: (bi, i, grp * ng + h))
    return pl.pallas_call(
        functools.partial(_gdn_kernel, ts=ts, hps=hps),
        grid=(b, ng, s // ts),
        in_specs=[pl.BlockSpec((1, ts, gw), col(0)), pl.BlockSpec((1, ts, gw), col(1)),
                  pl.BlockSpec((1, ts, gw), col(2)), pl.BlockSpec((1, ts, gw), col(0)),
                  pl.BlockSpec((1, ts, hd), lambda bi, h, i: (bi, i, 0)),
                  pl.BlockSpec((1, hd), lambda bi, h, i: (0, 0)),
                  pl.BlockSpec((1, hd), lambda bi, h, i: (0, 0)),
                  pl.BlockSpec((1, hd), lambda bi, h, i: (0, 0))],
        out_specs=pl.BlockSpec((1, ts, gw), lambda bi, h, i: (bi, i, h)),
        out_shape=jax.ShapeDtypeStruct((b, s, HALF), BF16),
        scratch_shapes=[pltpu.VMEM((hps, hd, hd), F32)],
        compiler_params=_cparams("arbitrary", "arbitrary", "arbitrary"),
        name="gated_deltanet",
    )(qkv, qkv, qkv, z, logits, pad(a_log), pad(dt_bias), norm_w.astype(F32).reshape(1, hd))


def _rwkv_kernel(r_ref, k_ref, v_ref, l_ref, w2_ref, a2_ref, g2_ref,
                 w0_ref, a0_ref, kk_ref, ka_ref, rk_ref, lnw_ref, lnb_ref, o_ref, st_ref, *, ts, hps):
    c = CHUNK
    hs = RWKV_HEAD

    @pl.when(pl.program_id(2) == 0)
    def _():
        st_ref[...] = jnp.zeros_like(st_ref)

    r = r_ref[0]
    k = k_ref[0]
    v = v_ref[0]
    lo = l_ref[0]
    p0, p1, p2 = RWKV_LORA_PAD
    w_log = -_softplus(-(w0_ref[...] + _dot(lo[:, :p0], w2_ref[...]))) - 0.5
    lw = -jnp.exp(w_log)
    a = _sigmoid(a0_ref[...] + _dot(lo[:, p0:p0 + p1], a2_ref[...]))
    g = _dot(lo[:, p0 + p1:p0 + p1 + p2], g2_ref[...])
    kk_raw = k * kk_ref[...]
    km = k * (1.0 + (a - 1.0) * ka_ref[...])
    head_of_lane = lax.broadcasted_iota(jnp.int32, (ts, hps * hs), 1) // hs

    def per_head_sum(x):
        out = jnp.zeros_like(x)
        for hd in range(hps):
            sel = head_of_lane == hd
            out = jnp.where(sel, jnp.sum(jnp.where(sel, x, 0.0), axis=-1, keepdims=True), out)
        return out

    kk = kk_raw * lax.rsqrt(per_head_sum(kk_raw * kk_raw) + NORM_EPS)
    aa = -kk
    bb = kk * a
    bonus = per_head_sum(r * km * rk_ref[...]) * v

    causal, strict = _tri_masks(c)
    tril = causal.astype(BF16)
    eye = jnp.where(causal & (~strict), 1.0, 0.0).astype(F32)
    lnw = lnw_ref[...]
    lnb = lnb_ref[...]

    rows = [slice(ci * c, (ci + 1) * c) for ci in range(ts // c)]
    nck = len(rows)
    heads = [slice(hd * hs, (hd + 1) * hs) for hd in range(hps)]
    cums = [_dot01(tril, lw[rw]) for rw in rows]
    c_lasts = [cum[c - 1:c, :] for cum in cums]
    w_invs = [jnp.exp(-cum) for cum in cums]
    w_tails = [jnp.exp(cl - cum) for cl, cum in zip(c_lasts, cums)]
    a_ts = [_bf(aa[rw] * jnp.exp(cum - lw[rw])) for rw, cum in zip(rows, cums)]
    r_ts = [_bf(r[rw] * jnp.exp(cum)) for rw, cum in zip(rows, cums)]
    b_ts = [_bf(bb[rw] * wi) for rw, wi in zip(rows, w_invs)]
    k_ts = [_bf(km[rw] * wi) for rw, wi in zip(rows, w_invs)]
    b_hs = [_bf(bb[rw] * wt) for rw, wt in zip(rows, w_tails)]
    k_hs = [_bf(km[rw] * wt) for rw, wt in zip(rows, w_tails)]
    vcs = [_bf(v[rw]) for rw in rows]
    e_lasts = [jnp.exp(cl) for cl in c_lasts]
    prob = [(ci, cs) for ci in range(nck) for cs in heads]
    row2 = lax.broadcasted_iota(jnp.int32, (c, 2 * c), 0)
    col2 = lax.broadcasted_iota(jnp.int32, (c, 2 * c), 1) % c
    bks = [jnp.concatenate([b_ts[ci][:, cs], k_ts[ci][:, cs]], axis=0) for ci, cs in prob]
    a_abks = [jnp.where(row2 > col2, _dot_nt(a_ts[ci][:, cs], bk), 0.0) for bk, (ci, cs) in zip(bks, prob)]
    a_rbks = [_bf(jnp.where(row2 >= col2, _dot_nt(r_ts[ci][:, cs], bk), 0.0)) for bk, (ci, cs) in zip(bks, prob)]
    tmats = [_bf(t) for t in _unit_lower_inverses([m[:, :c] for m in a_abks], eye)]
    zero_c = jnp.zeros((c, hs), BF16)
    avs = [_bf(_dot(_bf(abk), jnp.concatenate([zero_c, vcs[ci][:, cs]], axis=0)))
           for abk, (ci, cs) in zip(a_abks, prob)]
    pmats = [_bf(_dot(t, a_ts[ci][:, cs])) for t, (ci, cs) in zip(tmats, prob)]
    qmats = [_dot(t, av) for t, av in zip(tmats, avs)]
    mmats = [_bf(_dot_tn(pm, b_hs[ci][:, cs])) for pm, (ci, cs) in zip(pmats, prob)]
    nmats = [_dot_tn(jnp.concatenate([_bf(qm), vcs[ci][:, cs]], axis=0),
                     jnp.concatenate([b_hs[ci][:, cs], k_hs[ci][:, cs]], axis=0))
             for qm, (ci, cs) in zip(qmats, prob)]

    states = [st_ref[hd] for hd in range(hps)]
    sb_hist = []
    for ci in range(nck):
        sbs = [_bf(s) for s in states]
        sb_hist.append(sbs)
        states = [states[hd] * e_lasts[ci][:, heads[hd]] + _dot(sbs[hd], mmats[hps * ci + hd]) + nmats[hps * ci + hd]
                  for hd in range(hps)]
    for hd in range(hps):
        st_ref[hd] = states[hd]
    ys = []
    for ci in range(nck):
        sbs = sb_hist[ci]
        ubs = [_bf(_dot_nt(pmats[hps * ci + hd], sbs[hd]) + qmats[hps * ci + hd]) for hd in range(hps)]
        ys.append([_dot_nt(r_ts[ci][:, heads[hd]], sbs[hd])
                   + _dot(a_rbks[hps * ci + hd], jnp.concatenate([ubs[hd], vcs[ci][:, heads[hd]]], axis=0))
                   for hd in range(hps)])
    for ci in range(nck):
        outs = []
        for y in ys[ci]:
            ym = jnp.mean(y, axis=-1, keepdims=True)
            yd = y - ym
            yv = jnp.mean(yd * yd, axis=-1, keepdims=True)
            outs.append(yd * lax.rsqrt(yv + RWKV_LNX_EPS))
        yn = jnp.concatenate(outs, axis=-1) * lnw + lnb
        o_ref[0, rows[ci], :] = ((yn + bonus[rows[ci]]) * g[rows[ci]]).astype(o_ref.dtype)


def _rwkv7(hr, hl, w2p, a2p, g2p, w0, a0, k_k, k_a, r_k, lnx_w, lnx_b, ts=256, hps=16):
    b, s, _ = hr.shape
    ts = min(ts, s)
    pw = hps * RWKV_HEAD
    npair = HALF // pw
    lw = sum(RWKV_LORA_PAD)
    col = lambda grp: (lambda bi, hp, i: (bi, i, grp * npair + hp))
    vec = lambda grp: pl.BlockSpec((1, pw), lambda bi, hp, i: (0, grp * npair + hp))
    row1 = lambda t: t.astype(F32).reshape(1, -1)
    return pl.pallas_call(
        functools.partial(_rwkv_kernel, ts=ts, hps=hps),
        grid=(b, npair, s // ts),
        in_specs=[pl.BlockSpec((1, ts, pw), col(0)), pl.BlockSpec((1, ts, pw), col(1)),
                  pl.BlockSpec((1, ts, pw), col(2)),
                  pl.BlockSpec((1, ts, lw), lambda bi, hp, i: (bi, i, 0)),
                  pl.BlockSpec((RWKV_LORA_PAD[0], pw), lambda bi, hp, i: (0, hp)),
                  pl.BlockSpec((RWKV_LORA_PAD[1], pw), lambda bi, hp, i: (0, hp)),
                  pl.BlockSpec((RWKV_LORA_PAD[2], pw), lambda bi, hp, i: (0, hp)),
                  vec(0), vec(0), vec(0), vec(0), vec(0), vec(0), vec(0)],
        out_specs=pl.BlockSpec((1, ts, pw), lambda bi, hp, i: (bi, i, hp)),
        out_shape=jax.ShapeDtypeStruct((b, s, HALF), BF16),
        scratch_shapes=[pltpu.VMEM((hps, RWKV_HEAD, RWKV_HEAD), F32)],
        compiler_params=_cparams("arbitrary", "arbitrary", "arbitrary"),
        name="rwkv7",
    )(hr, hr, hr, hl, w2p, a2p, g2p,
      row1(w0), row1(a0), row1(k_k), row1(k_a), row1(r_k), row1(lnx_w), row1(lnx_b))


def _fox_cumsum_kernel(f_ref, bf_ref, o_ref):
    x = f_ref[0] + bf_ref[...]
    ls = jnp.minimum(x, 0.0) - jnp.log(1.0 + jnp.exp(-jnp.abs(x)))
    nh, nr, nl = ls.shape
    li = lax.broadcasted_iota(jnp.int32, (nl, nl), 0)
    lj = lax.broadcasted_iota(jnp.int32, (nl, nl), 1)
    upper = (li <= lj).astype(BF16)
    ri = lax.broadcasted_iota(jnp.int32, (nr, nr), 0)
    rj = lax.broadcasted_iota(jnp.int32, (nr, nr), 1)
    below = (ri > rj).astype(BF16)
    for h in range(nh):
        within = _dot01_right(ls[h], upper)
        tot = jnp.broadcast_to(within[:, nl - 1:nl], (nr, nl))
        o_ref[0, h] = within + _dot01(below, tot)


def _dot01_right(x, m01):
    x1, x2, x3 = _split3(x)
    return _dot(x1, m01) + _dot(x2, m01) + _dot(x3, m01)


def _fox_cumsum(f_t, b_f):
    b, nh, s = f_t.shape
    nl = 128
    nr = s // nl
    out = pl.pallas_call(
        _fox_cumsum_kernel,
        grid=(b,),
        in_specs=[pl.BlockSpec((1, nh, nr, nl), lambda bi: (bi, 0, 0, 0)),
                  pl.BlockSpec((nh, 1, 1), lambda bi: (0, 0, 0))],
        out_specs=pl.BlockSpec((1, nh, nr, nl), lambda bi: (bi, 0, 0, 0)),
        out_shape=jax.ShapeDtypeStruct((b, nh, nr, nl), F32),
        compiler_params=_cparams("arbitrary"),
        name="fox_cumsum",
    )(f_t.reshape(b, nh, nr, nl), b_f.astype(F32).reshape(nh, 1, 1))
    return out.reshape(b, nh, 1, s)


def _fox_kernel(qi_ref, ki_ref, q_ref, k_ref, v_ref, cq_ref, ck_ref, o_ref, qs_ref, m_ref, acc_ref, *, tq, tk, tr):
    p = pl.program_id(2)
    qi = qi_ref[p]
    ki = ki_ref[p]
    last_k = ((qi + 1) * tq - 1) // tk
    log2e = 1.0 / math.log(2.0)

    @pl.when(ki == 0)
    def _():
        qs_ref[...] = (q_ref[0].astype(F32) * (FOX_HD ** -0.5 * log2e)).astype(BF16)
        m_ref[...] = jnp.full_like(m_ref, -jnp.inf)
        acc_ref[...] = jnp.zeros_like(acc_ref)

    def step(masked):
        bias = (cq_ref[0, 0, :, 0:1] - ck_ref[0, 0]) * log2e
        v_aug = jnp.concatenate([v_ref[0], jnp.ones((tk, FOX_HD), BF16)], axis=1)
        kb = k_ref[0]
        for r0 in range(0, tq, tr):
            rows = slice(r0, r0 + tr)
            s = _dot_nt(qs_ref[rows, :], kb) + bias
            if masked:
                qpos = qi * tq + r0 + lax.broadcasted_iota(jnp.int32, (tr, tk), 0)
                kpos = ki * tk + lax.broadcasted_iota(jnp.int32, (tr, tk), 1)
                s = jnp.where(kpos <= qpos, s, -jnp.inf)
            m_old = m_ref[rows, :]
            m_new = jnp.maximum(m_old, jnp.max(s, axis=-1, keepdims=True))
            corr = jnp.exp2(m_old - m_new)
            pexp = _bf(jnp.exp2(s - m_new))
            acc_ref[rows, :] = corr * acc_ref[rows, :] + _dot(pexp, v_aug)
            m_ref[rows, :] = m_new

    needs_mask = (ki + 1) * tk - 1 > qi * tq

    @pl.when(needs_mask)
    def _():
        step(True)

    @pl.when(jnp.logical_not(needs_mask))
    def _():
        step(False)

    @pl.when(ki == last_k)
    def _():
        acc = acc_ref[...]
        o_ref[0] = (acc[:, :FOX_HD] / acc[:, FOX_HD:FOX_HD + 1]).astype(o_ref.dtype)


def _fox_attention(hf, c, tq=2048, tk=2048, tr=128):
    b, s, _ = hf.shape
    tq = min(tq, s)
    tk = min(tk, s)
    nh = FOX_HEADS
    pairs = [(qi, ki) for qi in range(s // tq) for ki in range(((qi + 1) * tq - 1) // tk + 1)]
    qi_arr = jnp.asarray(np.array([p[0] for p in pairs], np.int32))
    ki_arr = jnp.asarray(np.array([p[1] for p in pairs], np.int32))
    grid_spec = pltpu.PrefetchScalarGridSpec(
        num_scalar_prefetch=2,
        grid=(b, nh, len(pairs)),
        in_specs=[pl.BlockSpec((1, tq, FOX_HD), lambda bi, h, p, qa, ka: (bi, qa[p], h)),
                  pl.BlockSpec((1, tk, FOX_HD), lambda bi, h, p, qa, ka: (bi, ka[p], nh + h)),
                  pl.BlockSpec((1, tk, FOX_HD), lambda bi, h, p, qa, ka: (bi, ka[p], 2 * nh + h)),
                  pl.BlockSpec((1, 1, 1, tq), lambda bi, h, p, qa, ka: (bi, h, 0, qa[p])),
                  pl.BlockSpec((1, 1, 1, tk), lambda bi, h, p, qa, ka: (bi, h, 0, ka[p]))],
        out_specs=pl.BlockSpec((1, tq, FOX_HD), lambda bi, h, p, qa, ka: (bi, qa[p], h)),
        scratch_shapes=[pltpu.VMEM((tq, FOX_HD), BF16), pltpu.VMEM((tq, 1), F32),
                        pltpu.VMEM((tq, 2 * FOX_HD), F32)])
    return pl.pallas_call(
        functools.partial(_fox_kernel, tq=tq, tk=tk, tr=min(tr, tq)),
        grid_spec=grid_spec,
        out_shape=jax.ShapeDtypeStruct((b, s, HALF), BF16),
        compiler_params=_cparams("arbitrary", "arbitrary", "arbitrary"),
        name="fox_attention",
    )(qi_arr, ki_arr, hf, hf, hf, c, c)


def _pad_cols(w, width):
    return jnp.pad(w, ((0, 0), (0, width - w.shape[1])))


def _pad_rows(w, height):
    return jnp.pad(w, ((0, height - w.shape[0]), (0, 0)))


def _even_mixer(x32, xb, w_in, layer, pool_w, pool_scale, conv_w, a_log, dt_bias, norm_w, w_out, ln_w, ln_b):
    b, s, d = x32.shape
    t = b * s
    n_main = 5 * HALF
    w_gate = _bf(_pad_cols(w_in[layer, :, n_main:], 128))
    if xb is None:
        u_pool, xb2 = _mm_cast(x32.reshape(t, d), w_in, F32, col0=0, n=HALF, layer=layer)
        xb = xb2.reshape(b, s, d)
    else:
        xb2 = xb.reshape(t, d)
        u_pool = _mm(xb2, w_in, F32, col0=0, n=HALF, layer=layer)
    u_pool = u_pool.reshape(b, s, HALF)
    qkv = _mm_taps(xb, w_in, conv_w, F32, act="silu", col0=HALF, n=3 * HALF, layer=layer)
    z = _mm(xb2, w_in, BF16, col0=4 * HALF, n=HALF, layer=layer).reshape(b, s, HALF)
    logits = _mm(xb2, w_gate, F32, tn=128).reshape(b, s, 128)
    y_a = _pool_mixer(u_pool, _bf(pool_w), pool_scale)
    y_b = _gated_deltanet(qkv, z, logits, a_log, dt_bias, norm_w)
    return _mm_res_ln([y_a.reshape(t, HALF), y_b.reshape(t, HALF)], w_out, x32.reshape(t, d), ln_w, ln_b,
                      layer=layer)


def _odd_mixer(x32, xb, w_in_all, layer, mu, w0, w2, a0, a2, g2, k_k, k_a, r_k, lnx_w, lnx_b, b_f, w_out, ln_w, ln_b):
    b, s, d = x32.shape
    t = b * s
    l0, l1, l2 = RWKV_LORA
    p0, p1, p2 = RWKV_LORA_PAD
    o_l = 3 * HALF
    o_f = o_l + l0 + l1 + l2
    w_tail = w_in_all[layer, :, o_l:]
    w_l = _bf(jnp.concatenate([_pad_cols(w_tail[:, :l0], p0),
                               _pad_cols(w_tail[:, l0:l0 + l1], p1),
                               _pad_cols(w_tail[:, l0 + l1:o_f - o_l], p2)], axis=1))
    w_fox = _bf(w_tail[:, o_f - o_l:o_f - o_l + 3 * HALF])
    w_fl = _bf(_pad_cols(w_tail[:, o_f - o_l + 3 * HALF:], 128))
    mu_l = jnp.concatenate([jnp.pad(mu[o_l:o_l + l0], (0, p0 - l0)),
                            jnp.pad(mu[o_l + l0:o_l + l0 + l1], (0, p1 - l1)),
                            jnp.pad(mu[o_l + l0 + l1:o_f], (0, p2 - l2))])
    xb2 = xb.reshape(t, d)
    lerp = lambda m: jnp.stack([m, 1.0 - m])
    hr = _mm_taps(xb, w_in_all, lerp(mu[:o_l]), F32, col0=0, n=o_l, layer=layer)
    hl = _mm_taps(xb, w_l, lerp(mu_l), BF16, act="lora", tn=p0 + p1 + p2)
    hf = _mm(xb2, w_fox, BF16).reshape(b, s, 3 * HALF)
    fl = _mm(xb2, w_fl, F32, tn=128).reshape(b, s, 128)
    y_c = _rwkv7(hr, hl, _bf(_pad_rows(w2, p0)), _bf(_pad_rows(a2, p1)), _bf(_pad_rows(g2, p2)),
                 w0, a0, k_k, k_a, r_k, lnx_w, lnx_b)
    c = _fox_cumsum(jnp.transpose(fl[:, :, :FOX_HEADS], (0, 2, 1)), b_f)
    y_d = _fox_attention(hf, c)
    return _mm_res_ln([y_c.reshape(t, HALF), y_d.reshape(t, HALF)], w_out, x32.reshape(t, d), ln_w, ln_b,
                      layer=layer)


def _cross_attention(x32, xb, mem_b, w_q, w_kv, layer, w_o, ln_w, ln_b):
    d = x32.shape[1]
    b, mlen, _ = mem_b.shape
    kv = _mm(mem_b.reshape(b * mlen, d), w_kv, BF16, layer=layer).reshape(b, mlen, 2 * d)
    g = _fold_qk(w_q, layer, kv)
    vo = _fold_vo(kv, w_o, layer)
    return _xattn_res_ln(xb, g, vo, x32, ln_w, ln_b)


def _conv_glu_ffn(x32, xb, b, w_up, layer, conv_w, w_down, ln_w, ln_b):
    t, d = x32.shape
    hmid = _ffn_up(xb.reshape(b, t // b, d), w_up, conv_w, layer=layer)
    return _mm_res_ln([hmid.reshape(t, D_FF)], w_down, x32, ln_w, ln_b, layer=layer, tm=256)


def kernel(x, mem, ev_w_in, pool_w, pool_scale, gdn_conv_w, gdn_a_log, gdn_dt_bias, gdn_norm_w, ev_w_out,
           od_w_in, rwkv_mu, rwkv_w0, rwkv_w2, rwkv_a0, rwkv_a2, rwkv_g2, rwkv_k_k, rwkv_k_a, rwkv_r_k,
           rwkv_lnx_w, rwkv_lnx_b, fox_b_f, od_w_out,
           ln_mix_w, ln_mix_b, xa_w_q, xa_w_kv, xa_w_o, ln_xa_w, ln_xa_b,
           ffn_w_up, ffn_conv_w, ffn_w_down, ln_ffn_w, ln_ffn_b):
    b, s, d = x.shape
    depth = ln_mix_w.shape[0]
    mem_b = _bf(mem)
    ev_wo, od_wo, xa_wo, ffn_wd = _bf(ev_w_out), _bf(od_w_out), _bf(xa_w_o), _bf(ffn_w_down)
    x32 = x
    xb = None
    for i in range(depth):
        j = i // 2
        x3 = x32.reshape(b, s, d)
        xb3 = None if xb is None else xb.reshape(b, s, d)
        if i % 2 == 0:
            x32, xb = _even_mixer(x3, xb3, ev_w_in, j, pool_w[j], pool_scale[j], gdn_conv_w[j], gdn_a_log[j],
                                  gdn_dt_bias[j], gdn_norm_w[j], ev_wo, ln_mix_w[i], ln_mix_b[i])
        else:
            x32, xb = _odd_mixer(x3, xb3, od_w_in, j, rwkv_mu[j], rwkv_w0[j], rwkv_w2[j], rwkv_a0[j], rwkv_a2[j],
                                 rwkv_g2[j], rwkv_k_k[j], rwkv_k_a[j], rwkv_r_k[j].reshape(-1), rwkv_lnx_w[j],
                                 rwkv_lnx_b[j], fox_b_f[j], od_wo, ln_mix_w[i], ln_mix_b[i])
        x32, xb = _cross_attention(x32, xb, mem_b, xa_w_q, xa_w_kv, i, xa_wo, ln_xa_w[i], ln_xa_b[i])
        x32, xb = _conv_glu_ffn(x32, xb, b, ffn_w_up, i, ffn_conv_w[i], ffn_wd, ln_ffn_w[i], ln_ffn_b[i])
    return x32.reshape(b, s, d)
```
